```python
import math
import jax, jax.numpy as jnp
from jax import lax
import numpy as np

D_MODEL = 1024
BATCH = 16
SEQ = 4096
DEPTH = 4

DN_ALPHA = (2.0 * DEPTH) ** 0.25
DN_BETA = (8.0 * DEPTH) ** -0.25
NORM_EPS = 1e-5
N_MOD = 9
D_FF = ((8 * D_MODEL // 3 + 127) // 128) * 128

SSD_HEAD_DIM = 64
SSD_HEADS = D_MODEL // SSD_HEAD_DIM
SSD_D_INNER = SSD_HEADS * SSD_HEAD_DIM
SSD_GROUPS = 2
SSD_STATE = 128
SSD_CONV = 4
SSD_CHUNK = 128
SSD_CONV_DIM = SSD_D_INNER + 2 * SSD_GROUPS * SSD_STATE
SSD_IN = SSD_D_INNER + SSD_CONV_DIM + SSD_HEADS

POOL_WINDOWS = (2, 4, 8, 16)
POOL_GROUPS = len(POOL_WINDOWS)
POOL_GROUP_DIM = D_MODEL // 8
POOL_DIM = POOL_GROUPS * POOL_GROUP_DIM

EVEN_IN = SSD_IN + POOL_DIM
EVEN_MIX = SSD_D_INNER + POOL_DIM

CONF_DIM = D_MODEL // 2
CONF_KERNEL = 31
LRU_DIM = D_MODEL
LRU_HEADS = 8
LRU_HEAD_DIM = LRU_DIM // LRU_HEADS
LRU_CONV = 4
LRU_C = 8.0

ODD_IN = 2 * CONF_DIM + 2 * LRU_DIM
ODD_MIX = CONF_DIM + LRU_DIM

N_EVEN = (DEPTH + 1) // 2
N_ODD = DEPTH // 2

kernel_name = "hybrid_ssd_pool_conformer_rglru_trunk"


def layer_norm(x, g, b):
    xf = x.astype(jnp.float32)
    mu = jnp.mean(xf, axis=-1, keepdims=True)
    xc = xf - mu
    var = jnp.mean(xc * xc, axis=-1, keepdims=True)
    return (xc * lax.rsqrt(var + NORM_EPS) * g.astype(jnp.float32) + b.astype(jnp.float32)).astype(x.dtype)


def rms_norm(x, g):
    xf = x.astype(jnp.float32)
    ms = jnp.mean(xf * xf, axis=-1, keepdims=True)
    return xf * lax.rsqrt(ms + NORM_EPS) * g.astype(jnp.float32)


def causal_dwconv(x, w, b):
    k, ch = w.shape
    y = lax.conv_general_dilated(
        x, w[:, None, :].astype(x.dtype), window_strides=(1,), padding=[(k - 1, 0)],
        dimension_numbers=("NWC", "WIO", "NWC"), feature_group_count=ch)
    return y + b.astype(x.dtype)


def modulate(h, shift, scale):
    return h * (1.0 + scale[:, None, :]) + shift[:, None, :]


def post_norm(x, y, g, b):
    return layer_norm(DN_ALPHA * x + y, g, b)


def swiglu(h, w_in, w_out):
    gu = jnp.einsum("btd,df->btf", h, w_in)
    gate, up = jnp.split(gu, 2, axis=-1)
    return jnp.einsum("btf,fd->btd", jax.nn.silu(gate) * up, w_out)


def ssd_mixer(zxbcdt, conv_w, conv_b, dt_bias, a_log, d_skip, norm_g):
    bsz, t_len, _ = zxbcdt.shape
    z = zxbcdt[..., :SSD_D_INNER]
    xbc = zxbcdt[..., SSD_D_INNER:SSD_D_INNER + SSD_CONV_DIM]
    dt = zxbcdt[..., SSD_D_INNER + SSD_CONV_DIM:]
    xbc = jax.nn.silu(causal_dwconv(xbc, conv_w, conv_b)).astype(jnp.float32)
    g_n = SSD_GROUPS * SSD_STATE
    nc = t_len // SSD_CHUNK
    r_h = SSD_HEADS // SSD_GROUPS
    x = xbc[..., :SSD_D_INNER].reshape(bsz, nc, SSD_CHUNK, SSD_GROUPS, r_h, SSD_HEAD_DIM)
    bm = xbc[..., SSD_D_INNER:SSD_D_INNER + g_n].reshape(bsz, nc, SSD_CHUNK, SSD_GROUPS, SSD_STATE)
    cm = xbc[..., SSD_D_INNER + g_n:].reshape(bsz, nc, SSD_CHUNK, SSD_GROUPS, SSD_STATE)
    dt = jax.nn.softplus(dt.astype(jnp.float32) + dt_bias.astype(jnp.float32))
    dt = dt.reshape(bsz, nc, SSD_CHUNK, SSD_GROUPS, r_h)
    a = -jnp.exp(a_log.astype(jnp.float32)).reshape(SSD_GROUPS, r_h)
    a_cum = jnp.cumsum(dt * a, axis=2)
    xdt = x * dt[..., None]
    seg = a_cum[:, :, :, None] - a_cum[:, :, None]
    causal = jnp.tril(jnp.ones((SSD_CHUNK, SSD_CHUNK), dtype=bool))[:, :, None, None]
    l_mat = jnp.exp(jnp.where(causal, seg, -jnp.inf))
    cb = jnp.einsum("bclgn,bcsgn->bclsg", cm, bm)
    y_diag = jnp.einsum("bclsg,bclsgr,bcsgrp->bclgrp", cb, l_mat, xdt)
    decay_s = jnp.exp(a_cum[:, :, -1:] - a_cum)
    states = jnp.einsum("bcsgn,bcsgr,bcsgrp->bcgrpn", bm, decay_s, xdt)
    chunk_decay = jnp.exp(a_cum[:, :, -1])

    def step(h, inp):
        s_c, dec = inp
        return h * dec[..., None, None] + s_c, h

    h0 = jnp.zeros((bsz, SSD_GROUPS, r_h, SSD_HEAD_DIM, SSD_STATE), jnp.float32)
    _, h_prev = lax.scan(step, h0, (jnp.moveaxis(states, 1, 0), jnp.moveaxis(chunk_decay, 1, 0)))
    h_prev = jnp.moveaxis(h_prev, 0, 1)
    y_off = jnp.einsum("bclgn,bcgrpn,bclgr->bclgrp", cm, h_prev, jnp.exp(a_cum))
    y = y_diag + y_off + x * d_skip.astype(jnp.float32).reshape(SSD_GROUPS, r_h)[:, :, None]
    y = y.reshape(bsz, t_len, SSD_D_INNER) * jax.nn.silu(z.astype(jnp.float32))
    return rms_norm(y, norm_g).astype(zxbcdt.dtype)


def pool_mixer(u, w_grp, scale):
    bsz, t_len, _ = u.shape
    uf = u.astype(jnp.float32).reshape(bsz, t_len, POOL_GROUPS, POOL_GROUP_DIM)
    cs = jnp.cumsum(uf, axis=1)
    pos = jnp.arange(1, t_len + 1, dtype=jnp.float32)
    outs = []
    for g, w in enumerate(POOL_WINDOWS):
        c_g = cs[:, :, g]
        lo = jnp.pad(c_g[:, :t_len - w], ((0, 0), (w, 0), (0, 0)))
        cnt = jnp.minimum(pos, float(w))[None, :, None]
        outs.append((c_g - lo) / cnt - uf[:, :, g])
    pooled = jnp.stack(outs, axis=2)
    mixed = jnp.einsum("btgc,gcd->btgd", pooled, w_grp.astype(jnp.float32))
    return (mixed.reshape(bsz, t_len, POOL_DIM) * scale.astype(jnp.float32)).astype(u.dtype)


def conformer_conv(v, gate, dw_w, dw_b, ln_g, ln_b):
    h = v * jax.nn.sigmoid(gate)
    h = causal_dwconv(h, dw_w, dw_b)
    h = layer_norm(h, ln_g, ln_b)
    return jax.nn.silu(h)


def _lin_combine(e1, e2):
    a1, b1 = e1
    a2, b2 = e2
    return a1 * a2, a2 * b1 + b2


def rglru_mixer(xr, gate, conv_w, conv_b, wa, ba, wx, bx, lam):
    bsz, t_len, _ = xr.shape
    xr = causal_dwconv(xr, conv_w, conv_b)
    xh = xr.reshape(bsz, t_len, LRU_HEADS, LRU_HEAD_DIM)
    r = jax.nn.sigmoid(jnp.einsum("bthi,hij->bthj", xh, wa).reshape(bsz, t_len, LRU_DIM) + ba)
    i = jax.nn.sigmoid(jnp.einsum("bthi,hij->bthj", xh, wx).reshape(bsz, t_len, LRU_DIM) + bx)
    log_a = -LRU_C * r.astype(jnp.float32) * jax.nn.softplus(-lam.astype(jnp.float32))
    a = jnp.exp(log_a)
    b = jnp.sqrt(-jnp.expm1(2.0 * log_a)) * (i * xr).astype(jnp.float32)
    _, h = lax.associative_scan(_lin_combine, (a, b), axis=1)
    return h.astype(xr.dtype) * jax.nn.gelu(gate)


def _fwd_setup_inputs(seed: int = 0) -> dict:
    key = jax.random.key(seed)
    ks = iter(jax.random.split(key, 48))

    def nrm(shape, scale):
        return jax.random.normal(next(ks), shape, jnp.float32) * scale

    def unif(shape, lo, hi):
        return jax.random.uniform(next(ks), shape, jnp.float32, minval=lo, maxval=hi)

    x = nrm((BATCH, SEQ, D_MODEL), 1.0)
    c = nrm((BATCH, D_MODEL), 1.0)
    ada_w = nrm((DEPTH, D_MODEL, N_MOD * D_MODEL), 0.1 * D_MODEL ** -0.5)
    ada_b = nrm((DEPTH, N_MOD * D_MODEL), 0.01)
    ln_g = 1.0 + nrm((DEPTH, 3, D_MODEL), 0.01)
    ln_b = nrm((DEPTH, 3, D_MODEL), 0.01)
    ffn_w_in = nrm((DEPTH, 2, D_MODEL, 2 * D_FF), D_MODEL ** -0.5)
    ffn_w_out = nrm((DEPTH, 2, D_FF, D_MODEL), DN_BETA * D_FF ** -0.5)
    ev_w_in = nrm((N_EVEN, D_MODEL, EVEN_IN), D_MODEL ** -0.5)
    ssd_conv_w = nrm((N_EVEN, SSD_CONV, SSD_CONV_DIM), SSD_CONV ** -0.5)
    ssd_conv_b = nrm((N_EVEN, SSD_CONV_DIM), 0.01)
    dt0 = jnp.exp(unif((N_EVEN, SSD_HEADS), math.log(1e-3), math.log(1e-1)))
    ssd_dt_bias = dt0 + jnp.log(-jnp.expm1(-dt0))
    ssd_a_log = jnp.log(unif((N_EVEN, SSD_HEADS), 1.0, 16.0))
    ssd_d = 1.0 + nrm((N_EVEN, SSD_HEADS), 0.01)
    ssd_norm_g = 1.0 + nrm((N_EVEN, SSD_D_INNER), 0.01)
    pool_w = nrm((N_EVEN, POOL_GROUPS, POOL_GROUP_DIM, POOL_GROUP_DIM), POOL_GROUP_DIM ** -0.5)
    pool_scale = 1.0 + nrm((N_EVEN, POOL_DIM), 0.01)
    ev_w_out = nrm((N_EVEN, EVEN_MIX, D_MODEL), DN_BETA * EVEN_MIX ** -0.5)
    od_w_in = nrm((N_ODD, D_MODEL, ODD_IN), D_MODEL ** -0.5)
    conf_dw_w = nrm((N_ODD, CONF_KERNEL, CONF_DIM), CONF_KERNEL ** -0.5)
    conf_dw_b = nrm((N_ODD, CONF_DIM), 0.01)
    conf_ln_g = 1.0 + nrm((N_ODD, CONF_DIM), 0.01)
    conf_ln_b = nrm((N_ODD, CONF_DIM), 0.01)
    lru_conv_w = nrm((N_ODD, LRU_CONV, LRU_DIM), LRU_CONV ** -0.5)
    lru_conv_b = nrm((N_ODD, LRU_DIM), 0.01)
    lru_wa = nrm((N_ODD, LRU_HEADS, LRU_HEAD_DIM, LRU_HEAD_DIM), LRU_HEAD_DIM ** -0.5)
    lru_ba = nrm((N_ODD, LRU_DIM), 0.01)
    lru_wx = nrm((N_ODD, LRU_HEADS, LRU_HEAD_DIM, LRU_HEAD_DIM), LRU_HEAD_DIM ** -0.5)
    lru_bx = nrm((N_ODD, LRU_DIM), 0.01)
    a_c = unif((N_ODD, LRU_DIM), 0.9, 0.999)
    a_base = a_c ** (1.0 / LRU_C)
    lru_lambda = jnp.log(a_base) - jnp.log1p(-a_base)
    od_w_out = nrm((N_ODD, ODD_MIX, D_MODEL), DN_BETA * ODD_MIX ** -0.5)
    return {
        "x": x, "c": c, "ada_w": ada_w, "ada_b": ada_b, "ln_g": ln_g, "ln_b": ln_b,
        "ffn_w_in": ffn_w_in, "ffn_w_out": ffn_w_out,
        "ev_w_in": ev_w_in, "ssd_conv_w": ssd_conv_w, "ssd_conv_b": ssd_conv_b,
        "ssd_dt_bias": ssd_dt_bias, "ssd_a_log": ssd_a_log, "ssd_d": ssd_d, "ssd_norm_g": ssd_norm_g,
        "pool_w": pool_w, "pool_scale": pool_scale, "ev_w_out": ev_w_out,
        "od_w_in": od_w_in, "conf_dw_w": conf_dw_w, "conf_dw_b": conf_dw_b,
        "conf_ln_g": conf_ln_g, "conf_ln_b": conf_ln_b,
        "lru_conv_w": lru_conv_w, "lru_conv_b": lru_conv_b, "lru_wa": lru_wa, "lru_ba": lru_ba,
        "lru_wx": lru_wx, "lru_bx": lru_bx, "lru_lambda": lru_lambda, "od_w_out": od_w_out,
    }


def _fwd_reference(x, c, ada_w, ada_b, ln_g, ln_b, ffn_w_in, ffn_w_out,
              ev_w_in, ssd_conv_w, ssd_conv_b, ssd_dt_bias, ssd_a_log, ssd_d, ssd_norm_g,
              pool_w, pool_scale, ev_w_out,
              od_w_in, conf_dw_w, conf_dw_b, conf_ln_g, conf_ln_b,
              lru_conv_w, lru_conv_b, lru_wa, lru_ba, lru_wx, lru_bx, lru_lambda, od_w_out):
    cond = jax.nn.silu(c)
    for layer in range(DEPTH):
        mod = cond @ ada_w[layer] + ada_b[layer]
        sh1, sc1, g1, sh2, sc2, g2, sh3, sc3, g3 = jnp.split(mod, N_MOD, axis=-1)
        y = swiglu(modulate(x, sh1, sc1), ffn_w_in[layer, 0], ffn_w_out[layer, 0])
        x = post_norm(x, 0.5 * (1.0 + g1[:, None, :]) * y, ln_g[layer, 0], ln_b[layer, 0])
        h = modulate(x, sh2, sc2)
        if layer % 2 == 0:
            e = layer // 2
            proj = jnp.einsum("btd,de->bte", h, ev_w_in[e])
            y_a = ssd_mixer(proj[..., :SSD_IN], ssd_conv_w[e], ssd_conv_b[e], ssd_dt_bias[e],
                            ssd_a_log[e], ssd_d[e], ssd_norm_g[e])
            y_b = pool_mixer(proj[..., SSD_IN:], pool_w[e], pool_scale[e])
            y = jnp.einsum("bte,ed->btd", jnp.concatenate([y_a, y_b], axis=-1), ev_w_out[e])
        else:
            o = layer // 2
            proj = jnp.einsum("btd,de->bte", h, od_w_in[o])
            v = proj[..., :CONF_DIM]
            gt = proj[..., CONF_DIM:2 * CONF_DIM]
            xr = proj[..., 2 * CONF_DIM:2 * CONF_DIM + LRU_DIM]
            gr = proj[..., 2 * CONF_DIM + LRU_DIM:]
            y_c = conformer_conv(v, gt, conf_dw_w[o], conf_dw_b[o], conf_ln_g[o], conf_ln_b[o])
            y_d = rglru_mixer(xr, gr, lru_conv_w[o], lru_conv_b[o], lru_wa[o], lru_ba[o],
                              lru_wx[o], lru_bx[o], lru_lambda[o])
            y = jnp.einsum("bte,ed->btd", jnp.concatenate([y_c, y_d], axis=-1), od_w_out[o])
        x = post_norm(x, (1.0 + g2[:, None, :]) * y, ln_g[layer, 1], ln_b[layer, 1])
        y = swiglu(modulate(x, sh3, sc3), ffn_w_in[layer, 1], ffn_w_out[layer, 1])
        x = post_norm(x, 0.5 * (1.0 + g3[:, None, :]) * y, ln_g[layer, 2], ln_b[layer, 2])
    return x


import jax as _jax
import jax.numpy as _jnp

TWIN_FORMAT = 'train_step'
FWD_PARAMS = ['x', 'c', 'ada_w', 'ada_b', 'ln_g', 'ln_b', 'ffn_w_in', 'ffn_w_out', 'ev_w_in', 'ssd_conv_w', 'ssd_conv_b', 'ssd_dt_bias', 'ssd_a_log', 'ssd_d', 'ssd_norm_g', 'pool_w', 'pool_scale', 'ev_w_out', 'od_w_in', 'conf_dw_w', 'conf_dw_b', 'conf_ln_g', 'conf_ln_b', 'lru_conv_w', 'lru_conv_b', 'lru_wa', 'lru_ba', 'lru_wx', 'lru_bx', 'lru_lambda', 'od_w_out']
TWIN_WEIGHTS = ['ada_w', 'ada_b', 'ln_g', 'ln_b', 'ffn_w_in', 'ffn_w_out', 'ev_w_in', 'ssd_conv_w', 'ssd_conv_b', 'ssd_dt_bias', 'ssd_a_log', 'ssd_d', 'ssd_norm_g', 'pool_w', 'pool_scale', 'ev_w_out', 'od_w_in', 'conf_dw_w', 'conf_dw_b', 'conf_ln_g', 'conf_ln_b', 'lru_conv_w', 'lru_conv_b', 'lru_wa', 'lru_ba', 'lru_wx', 'lru_bx', 'lru_lambda', 'od_w_out']
TWIN_DIFF_INPUT = 'x'
TWIN_INPUTS = ['x', 'c', 'ada_w', 'ada_b', 'ln_g', 'ln_b', 'ffn_w_in', 'ffn_w_out', 'ev_w_in', 'ssd_conv_w', 'ssd_conv_b', 'ssd_dt_bias', 'ssd_a_log', 'ssd_d', 'ssd_norm_g', 'pool_w', 'pool_scale', 'ev_w_out', 'od_w_in', 'conf_dw_w', 'conf_dw_b', 'conf_ln_g', 'conf_ln_b', 'lru_conv_w', 'lru_conv_b', 'lru_wa', 'lru_ba', 'lru_wx', 'lru_bx', 'lru_lambda', 'od_w_out', 'loss_target', 'm_ada_w', 'm_ada_b', 'm_ln_g', 'm_ln_b', 'm_ffn_w_in', 'm_ffn_w_out', 'm_ev_w_in', 'm_ssd_conv_w', 'm_ssd_conv_b', 'm_ssd_dt_bias', 'm_ssd_a_log', 'm_ssd_d', 'm_ssd_norm_g', 'm_pool_w', 'm_pool_scale', 'm_ev_w_out', 'm_od_w_in', 'm_conf_dw_w', 'm_conf_dw_b', 'm_conf_ln_g', 'm_conf_ln_b', 'm_lru_conv_w', 'm_lru_conv_b', 'm_lru_wa', 'm_lru_ba', 'm_lru_wx', 'm_lru_bx', 'm_lru_lambda', 'm_od_w_out', 'v_ada_w', 'v_ada_b', 'v_ln_g', 'v_ln_b', 'v_ffn_w_in', 'v_ffn_w_out', 'v_ev_w_in', 'v_ssd_conv_w', 'v_ssd_conv_b', 'v_ssd_dt_bias', 'v_ssd_a_log', 'v_ssd_d', 'v_ssd_norm_g', 'v_pool_w', 'v_pool_scale', 'v_ev_w_out', 'v_od_w_in', 'v_conf_dw_w', 'v_conf_dw_b', 'v_conf_ln_g', 'v_conf_ln_b', 'v_lru_conv_w', 'v_lru_conv_b', 'v_lru_wa', 'v_lru_ba', 'v_lru_wx', 'v_lru_bx', 'v_lru_lambda', 'v_od_w_out']
TWIN_OUTPUTS = ['loss', 'grad_x', 'grad_ada_w', 'grad_ada_b', 'grad_ln_g', 'grad_ln_b', 'grad_ffn_w_in', 'grad_ffn_w_out', 'grad_ev_w_in', 'grad_ssd_conv_w', 'grad_ssd_conv_b', 'grad_ssd_dt_bias', 'grad_ssd_a_log', 'grad_ssd_d', 'grad_ssd_norm_g', 'grad_pool_w', 'grad_pool_scale', 'grad_ev_w_out', 'grad_od_w_in', 'grad_conf_dw_w', 'grad_conf_dw_b', 'grad_conf_ln_g', 'grad_conf_ln_b', 'grad_lru_conv_w', 'grad_lru_conv_b', 'grad_lru_wa', 'grad_lru_ba', 'grad_lru_wx', 'grad_lru_bx', 'grad_lru_lambda', 'grad_od_w_out', 'delta_ada_w', 'delta_ada_b', 'delta_ln_g', 'delta_ln_b', 'delta_ffn_w_in', 'delta_ffn_w_out', 'delta_ev_w_in', 'delta_ssd_conv_w', 'delta_ssd_conv_b', 'delta_ssd_dt_bias', 'delta_ssd_a_log', 'delta_ssd_d', 'delta_ssd_norm_g', 'delta_pool_w', 'delta_pool_scale', 'delta_ev_w_out', 'delta_od_w_in', 'delta_conf_dw_w', 'delta_conf_dw_b', 'delta_conf_ln_g', 'delta_conf_ln_b', 'delta_lru_conv_w', 'delta_lru_conv_b', 'delta_lru_wa', 'delta_lru_ba', 'delta_lru_wx', 'delta_lru_bx', 'delta_lru_lambda', 'delta_od_w_out', 'new_m_ada_w', 'new_m_ada_b', 'new_m_ln_g', 'new_m_ln_b', 'new_m_ffn_w_in', 'new_m_ffn_w_out', 'new_m_ev_w_in', 'new_m_ssd_conv_w', 'new_m_ssd_conv_b', 'new_m_ssd_dt_bias', 'new_m_ssd_a_log', 'new_m_ssd_d', 'new_m_ssd_norm_g', 'new_m_pool_w', 'new_m_pool_scale', 'new_m_ev_w_out', 'new_m_od_w_in', 'new_m_conf_dw_w', 'new_m_conf_dw_b', 'new_m_conf_ln_g', 'new_m_conf_ln_b', 'new_m_lru_conv_w', 'new_m_lru_conv_b', 'new_m_lru_wa', 'new_m_lru_ba', 'new_m_lru_wx', 'new_m_lru_bx', 'new_m_lru_lambda', 'new_m_od_w_out', 'new_v_ada_w', 'new_v_ada_b', 'new_v_ln_g', 'new_v_ln_b', 'new_v_ffn_w_in', 'new_v_ffn_w_out', 'new_v_ev_w_in', 'new_v_ssd_conv_w', 'new_v_ssd_conv_b', 'new_v_ssd_dt_bias', 'new_v_ssd_a_log', 'new_v_ssd_d', 'new_v_ssd_norm_g', 'new_v_pool_w', 'new_v_pool_scale', 'new_v_ev_w_out', 'new_v_od_w_in', 'new_v_conf_dw_w', 'new_v_conf_dw_b', 'new_v_conf_ln_g', 'new_v_conf_ln_b', 'new_v_lru_conv_w', 'new_v_lru_conv_b', 'new_v_lru_wa', 'new_v_lru_ba', 'new_v_lru_wx', 'new_v_lru_bx', 'new_v_lru_lambda', 'new_v_od_w_out']
TWIN_LEAF_KINDS = {'loss': 'loss', 'grad_x': 'grad_x', 'grad_ada_w': 'grad_w', 'grad_ada_b': 'grad_w', 'grad_ln_g': 'grad_w', 'grad_ln_b': 'grad_w', 'grad_ffn_w_in': 'grad_w', 'grad_ffn_w_out': 'grad_w', 'grad_ev_w_in': 'grad_w', 'grad_ssd_conv_w': 'grad_w', 'grad_ssd_conv_b': 'grad_w', 'grad_ssd_dt_bias': 'grad_w', 'grad_ssd_a_log': 'grad_w', 'grad_ssd_d': 'grad_w', 'grad_ssd_norm_g': 'grad_w', 'grad_pool_w': 'grad_w', 'grad_pool_scale': 'grad_w', 'grad_ev_w_out': 'grad_w', 'grad_od_w_in': 'grad_w', 'grad_conf_dw_w': 'grad_w', 'grad_conf_dw_b': 'grad_w', 'grad_conf_ln_g': 'grad_w', 'grad_conf_ln_b': 'grad_w', 'grad_lru_conv_w': 'grad_w', 'grad_lru_conv_b': 'grad_w', 'grad_lru_wa': 'grad_w', 'grad_lru_ba': 'grad_w', 'grad_lru_wx': 'grad_w', 'grad_lru_bx': 'grad_w', 'grad_lru_lambda': 'grad_w', 'grad_od_w_out': 'grad_w', 'delta_ada_w': 'delta_w', 'delta_ada_b': 'delta_w', 'delta_ln_g': 'delta_w', 'delta_ln_b': 'delta_w', 'delta_ffn_w_in': 'delta_w', 'delta_ffn_w_out': 'delta_w', 'delta_ev_w_in': 'delta_w', 'delta_ssd_conv_w': 'delta_w', 'delta_ssd_conv_b': 'delta_w', 'delta_ssd_dt_bias': 'delta_w', 'delta_ssd_a_log': 'delta_w', 'delta_ssd_d': 'delta_w', 'delta_ssd_norm_g': 'delta_w', 'delta_pool_w': 'delta_w', 'delta_pool_scale': 'delta_w', 'delta_ev_w_out': 'delta_w', 'delta_od_w_in': 'delta_w', 'delta_conf_dw_w': 'delta_w', 'delta_conf_dw_b': 'delta_w', 'delta_conf_ln_g': 'delta_w', 'delta_conf_ln_b': 'delta_w', 'delta_lru_conv_w': 'delta_w', 'delta_lru_conv_b': 'delta_w', 'delta_lru_wa': 'delta_w', 'delta_lru_ba': 'delta_w', 'delta_lru_wx': 'delta_w', 'delta_lru_bx': 'delta_w', 'delta_lru_lambda': 'delta_w', 'delta_od_w_out': 'delta_w', 'new_m_ada_w': 'new_m', 'new_m_ada_b': 'new_m', 'new_m_ln_g': 'new_m', 'new_m_ln_b': 'new_m', 'new_m_ffn_w_in': 'new_m', 'new_m_ffn_w_out': 'new_m', 'new_m_ev_w_in': 'new_m', 'new_m_ssd_conv_w': 'new_m', 'new_m_ssd_conv_b': 'new_m', 'new_m_ssd_dt_bias': 'new_m', 'new_m_ssd_a_log': 'new_m', 'new_m_ssd_d': 'new_m', 'new_m_ssd_norm_g': 'new_m', 'new_m_pool_w': 'new_m', 'new_m_pool_scale': 'new_m', 'new_m_ev_w_out': 'new_m', 'new_m_od_w_in': 'new_m', 'new_m_conf_dw_w': 'new_m', 'new_m_conf_dw_b': 'new_m', 'new_m_conf_ln_g': 'new_m', 'new_m_conf_ln_b': 'new_m', 'new_m_lru_conv_w': 'new_m', 'new_m_lru_conv_b': 'new_m', 'new_m_lru_wa': 'new_m', 'new_m_lru_ba': 'new_m', 'new_m_lru_wx': 'new_m', 'new_m_lru_bx': 'new_m', 'new_m_lru_lambda': 'new_m', 'new_m_od_w_out': 'new_m', 'new_v_ada_w': 'new_v', 'new_v_ada_b': 'new_v', 'new_v_ln_g': 'new_v', 'new_v_ln_b': 'new_v', 'new_v_ffn_w_in': 'new_v', 'new_v_ffn_w_out': 'new_v', 'new_v_ev_w_in': 'new_v', 'new_v_ssd_conv_w': 'new_v', 'new_v_ssd_conv_b': 'new_v', 'new_v_ssd_dt_bias': 'new_v', 'new_v_ssd_a_log': 'new_v', 'new_v_ssd_d': 'new_v', 'new_v_ssd_norm_g': 'new_v', 'new_v_pool_w': 'new_v', 'new_v_pool_scale': 'new_v', 'new_v_ev_w_out': 'new_v', 'new_v_od_w_in': 'new_v', 'new_v_conf_dw_w': 'new_v', 'new_v_conf_dw_b': 'new_v', 'new_v_conf_ln_g': 'new_v', 'new_v_conf_ln_b': 'new_v', 'new_v_lru_conv_w': 'new_v', 'new_v_lru_conv_b': 'new_v', 'new_v_lru_wa': 'new_v', 'new_v_lru_ba': 'new_v', 'new_v_lru_wx': 'new_v', 'new_v_lru_bx': 'new_v', 'new_v_lru_lambda': 'new_v', 'new_v_od_w_out': 'new_v'}


def _forward(args):
    return _fwd_reference(*[args[k] for k in FWD_PARAMS])


def _output_shape():
    out = _jax.eval_shape(lambda: _forward(_fwd_setup_inputs(0)))
    return out.shape, out.dtype

N_MICROBATCH = 1
ADAM_LR = 0.001
ADAM_B1 = 0.9
ADAM_B2 = 0.999
ADAM_EPS = 1e-08
ADAM_WD = 0.01
ADAM_STEP = 10
PER_EXAMPLE_BATCH_AXIS = {'x': 0, 'c': 0, 'loss_target': 0}
SHARED_INPUTS = []
_WEIGHT_DTYPES = {'ada_w': _jnp.float32, 'ada_b': _jnp.float32, 'ln_g': _jnp.float32, 'ln_b': _jnp.float32, 'ffn_w_in': _jnp.float32, 'ffn_w_out': _jnp.float32, 'ev_w_in': _jnp.float32, 'ssd_conv_w': _jnp.float32, 'ssd_conv_b': _jnp.float32, 'ssd_dt_bias': _jnp.float32, 'ssd_a_log': _jnp.float32, 'ssd_d': _jnp.float32, 'ssd_norm_g': _jnp.float32, 'pool_w': _jnp.float32, 'pool_scale': _jnp.float32, 'ev_w_out': _jnp.float32, 'od_w_in': _jnp.float32, 'conf_dw_w': _jnp.float32, 'conf_dw_b': _jnp.float32, 'conf_ln_g': _jnp.float32, 'conf_ln_b': _jnp.float32, 'lru_conv_w': _jnp.float32, 'lru_conv_b': _jnp.float32, 'lru_wa': _jnp.float32, 'lru_ba': _jnp.float32, 'lru_wx': _jnp.float32, 'lru_bx': _jnp.float32, 'lru_lambda': _jnp.float32, 'od_w_out': _jnp.float32}
MOMENT_SCALE = {'ada_w': 5.161096e-02, 'ada_b': 8.908052e-02, 'ln_g': 1.839779e+01, 'ln_b': 1.111857e+00, 'ffn_w_in': 1.190074e-02, 'ffn_w_out': 4.626160e-02, 'ev_w_in': 4.866631e-02, 'ssd_conv_w': 4.693789e-02, 'ssd_conv_b': 6.881125e-02, 'ssd_dt_bias': 1.100938e-01, 'ssd_a_log': 1.620756e-01, 'ssd_d': 2.338197e-01, 'ssd_norm_g': 5.432756e-02, 'pool_w': 4.518607e-02, 'pool_scale': 4.593751e-02, 'ev_w_out': 1.482245e-01, 'od_w_in': 3.107286e-02, 'conf_dw_w': 3.239983e-02, 'conf_dw_b': 7.365194e-02, 'conf_ln_g': 4.070312e-02, 'conf_ln_b': 4.253106e-02, 'lru_conv_w': 3.663114e-02, 'lru_conv_b': 3.288037e-01, 'lru_wa': 9.562692e-03, 'lru_ba': 9.826631e-03, 'lru_wx': 1.742647e-02, 'lru_bx': 1.480173e-02, 'lru_lambda': 2.040171e-02, 'od_w_out': 9.997182e-02}


def _to_microbatches(a, axis):
    t = _jnp.moveaxis(a, axis, 0)
    t = t.reshape((N_MICROBATCH, t.shape[0] // N_MICROBATCH) + t.shape[1:])
    return _jnp.moveaxis(t, 1, axis + 1)


def setup_inputs(seed: int = 0) -> dict:
    inp = _fwd_setup_inputs(seed)
    key = _jax.random.fold_in(_jax.random.key(seed), 7919)
    shape, _ = _output_shape()
    out = dict(inp)
    out["loss_target"] = _jax.random.normal(_jax.random.fold_in(key, 0), shape, _jnp.float32)
    for i, name in enumerate(TWIN_WEIGHTS):
        w = inp[name].astype(_jnp.float32)
        if MOMENT_SCALE is None:
            s = _jnp.sqrt(_jnp.mean(_jnp.square(w)) + 1e-30)
        else:
            s = MOMENT_SCALE[name]
        km, kv = _jax.random.split(_jax.random.fold_in(key, i + 1))
        out[name] = w
        out["m_" + name] = s * _jax.random.normal(km, w.shape, _jnp.float32)
        out["v_" + name] = (s * s) * _jax.random.uniform(kv, w.shape, _jnp.float32, 0.5, 1.5)
    if N_MICROBATCH > 1:
        for name, axis in PER_EXAMPLE_BATCH_AXIS.items():
            out[name] = _to_microbatches(out[name], axis)
    return {'x': out['x'], 'c': out['c'], 'ada_w': out['ada_w'], 'ada_b': out['ada_b'], 'ln_g': out['ln_g'], 'ln_b': out['ln_b'], 'ffn_w_in': out['ffn_w_in'], 'ffn_w_out': out['ffn_w_out'], 'ev_w_in': out['ev_w_in'], 'ssd_conv_w': out['ssd_conv_w'], 'ssd_conv_b': out['ssd_conv_b'], 'ssd_dt_bias': out['ssd_dt_bias'], 'ssd_a_log': out['ssd_a_log'], 'ssd_d': out['ssd_d'], 'ssd_norm_g': out['ssd_norm_g'], 'pool_w': out['pool_w'], 'pool_scale': out['pool_scale'], 'ev_w_out': out['ev_w_out'], 'od_w_in': out['od_w_in'], 'conf_dw_w': out['conf_dw_w'], 'conf_dw_b': out['conf_dw_b'], 'conf_ln_g': out['conf_ln_g'], 'conf_ln_b': out['conf_ln_b'], 'lru_conv_w': out['lru_conv_w'], 'lru_conv_b': out['lru_conv_b'], 'lru_wa': out['lru_wa'], 'lru_ba': out['lru_ba'], 'lru_wx': out['lru_wx'], 'lru_bx': out['lru_bx'], 'lru_lambda': out['lru_lambda'], 'od_w_out': out['od_w_out'], 'loss_target': out['loss_target'], 'm_ada_w': out['m_ada_w'], 'm_ada_b': out['m_ada_b'], 'm_ln_g': out['m_ln_g'], 'm_ln_b': out['m_ln_b'], 'm_ffn_w_in': out['m_ffn_w_in'], 'm_ffn_w_out': out['m_ffn_w_out'], 'm_ev_w_in': out['m_ev_w_in'], 'm_ssd_conv_w': out['m_ssd_conv_w'], 'm_ssd_conv_b': out['m_ssd_conv_b'], 'm_ssd_dt_bias': out['m_ssd_dt_bias'], 'm_ssd_a_log': out['m_ssd_a_log'], 'm_ssd_d': out['m_ssd_d'], 'm_ssd_norm_g': out['m_ssd_norm_g'], 'm_pool_w': out['m_pool_w'], 'm_pool_scale': out['m_pool_scale'], 'm_ev_w_out': out['m_ev_w_out'], 'm_od_w_in': out['m_od_w_in'], 'm_conf_dw_w': out['m_conf_dw_w'], 'm_conf_dw_b': out['m_conf_dw_b'], 'm_conf_ln_g': out['m_conf_ln_g'], 'm_conf_ln_b': out['m_conf_ln_b'], 'm_lru_conv_w': out['m_lru_conv_w'], 'm_lru_conv_b': out['m_lru_conv_b'], 'm_lru_wa': out['m_lru_wa'], 'm_lru_ba': out['m_lru_ba'], 'm_lru_wx': out['m_lru_wx'], 'm_lru_bx': out['m_lru_bx'], 'm_lru_lambda': out['m_lru_lambda'], 'm_od_w_out': out['m_od_w_out'], 'v_ada_w': out['v_ada_w'], 'v_ada_b': out['v_ada_b'], 'v_ln_g': out['v_ln_g'], 'v_ln_b': out['v_ln_b'], 'v_ffn_w_in': out['v_ffn_w_in'], 'v_ffn_w_out': out['v_ffn_w_out'], 'v_ev_w_in': out['v_ev_w_in'], 'v_ssd_conv_w': out['v_ssd_conv_w'], 'v_ssd_conv_b': out['v_ssd_conv_b'], 'v_ssd_dt_bias': out['v_ssd_dt_bias'], 'v_ssd_a_log': out['v_ssd_a_log'], 'v_ssd_d': out['v_ssd_d'], 'v_ssd_norm_g': out['v_ssd_norm_g'], 'v_pool_w': out['v_pool_w'], 'v_pool_scale': out['v_pool_scale'], 'v_ev_w_out': out['v_ev_w_out'], 'v_od_w_in': out['v_od_w_in'], 'v_conf_dw_w': out['v_conf_dw_w'], 'v_conf_dw_b': out['v_conf_dw_b'], 'v_conf_ln_g': out['v_conf_ln_g'], 'v_conf_ln_b': out['v_conf_ln_b'], 'v_lru_conv_w': out['v_lru_conv_w'], 'v_lru_conv_b': out['v_lru_conv_b'], 'v_lru_wa': out['v_lru_wa'], 'v_lru_ba': out['v_lru_ba'], 'v_lru_wx': out['v_lru_wx'], 'v_lru_bx': out['v_lru_bx'], 'v_lru_lambda': out['v_lru_lambda'], 'v_od_w_out': out['v_od_w_out']}


def _loss(weights, diff, rest, loss_target):
    with _jax.named_scope("forward"):
        args = {**rest, TWIN_DIFF_INPUT: diff, **{k: w.astype(_WEIGHT_DTYPES[k]) for k, w in weights.items()}}
        y = _forward(args)
    with _jax.named_scope("loss_head"):
        err = _jnp.square(y.astype(_jnp.float32) - loss_target)
        return 0.5 * _jnp.sum(_jnp.mean(err, axis=-1)) if err.ndim else 0.5 * err


def _adamw(w, g, m, v):
    m = ADAM_B1 * m + (1.0 - ADAM_B1) * g
    v = ADAM_B2 * v + (1.0 - ADAM_B2) * _jnp.square(g)
    m_hat = m / (1.0 - ADAM_B1 ** ADAM_STEP)
    v_hat = v / (1.0 - ADAM_B2 ** ADAM_STEP)
    delta = -ADAM_LR * (m_hat / (_jnp.sqrt(v_hat) + ADAM_EPS) + ADAM_WD * w)
    return delta, m, v


def reference(x, c, ada_w, ada_b, ln_g, ln_b, ffn_w_in, ffn_w_out, ev_w_in, ssd_conv_w, ssd_conv_b, ssd_dt_bias, ssd_a_log, ssd_d, ssd_norm_g, pool_w, pool_scale, ev_w_out, od_w_in, conf_dw_w, conf_dw_b, conf_ln_g, conf_ln_b, lru_conv_w, lru_conv_b, lru_wa, lru_ba, lru_wx, lru_bx, lru_lambda, od_w_out, loss_target, m_ada_w, m_ada_b, m_ln_g, m_ln_b, m_ffn_w_in, m_ffn_w_out, m_ev_w_in, m_ssd_conv_w, m_ssd_conv_b, m_ssd_dt_bias, m_ssd_a_log, m_ssd_d, m_ssd_norm_g, m_pool_w, m_pool_scale, m_ev_w_out, m_od_w_in, m_conf_dw_w, m_conf_dw_b, m_conf_ln_g, m_conf_ln_b, m_lru_conv_w, m_lru_conv_b, m_lru_wa, m_lru_ba, m_lru_wx, m_lru_bx, m_lru_lambda, m_od_w_out, v_ada_w, v_ada_b, v_ln_g, v_ln_b, v_ffn_w_in, v_ffn_w_out, v_ev_w_in, v_ssd_conv_w, v_ssd_conv_b, v_ssd_dt_bias, v_ssd_a_log, v_ssd_d, v_ssd_norm_g, v_pool_w, v_pool_scale, v_ev_w_out, v_od_w_in, v_conf_dw_w, v_conf_dw_b, v_conf_ln_g, v_conf_ln_b, v_lru_conv_w, v_lru_conv_b, v_lru_wa, v_lru_ba, v_lru_wx, v_lru_bx, v_lru_lambda, v_od_w_out):
    given = dict(x=x, c=c, ada_w=ada_w, ada_b=ada_b, ln_g=ln_g, ln_b=ln_b, ffn_w_in=ffn_w_in, ffn_w_out=ffn_w_out, ev_w_in=ev_w_in, ssd_conv_w=ssd_conv_w, ssd_conv_b=ssd_conv_b, ssd_dt_bias=ssd_dt_bias, ssd_a_log=ssd_a_log, ssd_d=ssd_d, ssd_norm_g=ssd_norm_g, pool_w=pool_w, pool_scale=pool_scale, ev_w_out=ev_w_out, od_w_in=od_w_in, conf_dw_w=conf_dw_w, conf_dw_b=conf_dw_b, conf_ln_g=conf_ln_g, conf_ln_b=conf_ln_b, lru_conv_w=lru_conv_w, lru_conv_b=lru_conv_b, lru_wa=lru_wa, lru_ba=lru_ba, lru_wx=lru_wx, lru_bx=lru_bx, lru_lambda=lru_lambda, od_w_out=od_w_out, loss_target=loss_target, m_ada_w=m_ada_w, m_ada_b=m_ada_b, m_ln_g=m_ln_g, m_ln_b=m_ln_b, m_ffn_w_in=m_ffn_w_in, m_ffn_w_out=m_ffn_w_out, m_ev_w_in=m_ev_w_in, m_ssd_conv_w=m_ssd_conv_w, m_ssd_conv_b=m_ssd_conv_b, m_ssd_dt_bias=m_ssd_dt_bias, m_ssd_a_log=m_ssd_a_log, m_ssd_d=m_ssd_d, m_ssd_norm_g=m_ssd_norm_g, m_pool_w=m_pool_w, m_pool_scale=m_pool_scale, m_ev_w_out=m_ev_w_out, m_od_w_in=m_od_w_in, m_conf_dw_w=m_conf_dw_w, m_conf_dw_b=m_conf_dw_b, m_conf_ln_g=m_conf_ln_g, m_conf_ln_b=m_conf_ln_b, m_lru_conv_w=m_lru_conv_w, m_lru_conv_b=m_lru_conv_b, m_lru_wa=m_lru_wa, m_lru_ba=m_lru_ba, m_lru_wx=m_lru_wx, m_lru_bx=m_lru_bx, m_lru_lambda=m_lru_lambda, m_od_w_out=m_od_w_out, v_ada_w=v_ada_w, v_ada_b=v_ada_b, v_ln_g=v_ln_g, v_ln_b=v_ln_b, v_ffn_w_in=v_ffn_w_in, v_ffn_w_out=v_ffn_w_out, v_ev_w_in=v_ev_w_in, v_ssd_conv_w=v_ssd_conv_w, v_ssd_conv_b=v_ssd_conv_b, v_ssd_dt_bias=v_ssd_dt_bias, v_ssd_a_log=v_ssd_a_log, v_ssd_d=v_ssd_d, v_ssd_norm_g=v_ssd_norm_g, v_pool_w=v_pool_w, v_pool_scale=v_pool_scale, v_ev_w_out=v_ev_w_out, v_od_w_in=v_od_w_in, v_conf_dw_w=v_conf_dw_w, v_conf_dw_b=v_conf_dw_b, v_conf_ln_g=v_conf_ln_g, v_conf_ln_b=v_conf_ln_b, v_lru_conv_w=v_lru_conv_w, v_lru_conv_b=v_lru_conv_b, v_lru_wa=v_lru_wa, v_lru_ba=v_lru_ba, v_lru_wx=v_lru_wx, v_lru_bx=v_lru_bx, v_lru_lambda=v_lru_lambda, v_od_w_out=v_od_w_out)
    weights = {n: given[n] for n in TWIN_WEIGHTS}
    shared = {n: given[n] for n in SHARED_INPUTS}
    per_example = {n: given[n] for n in ['x', 'c']}
    grad_fn = _jax.value_and_grad(_loss, argnums=(0, 1))

    def one_microbatch(ex, loss_target):
        ex = dict(ex)
        diff = ex.pop(TWIN_DIFF_INPUT)
        return grad_fn(weights, diff, {**shared, **ex}, loss_target)

    if N_MICROBATCH == 1:
        loss, (grad_w, grad_x) = one_microbatch(per_example, given["loss_target"])
    else:
        def body(carry, xs):
            loss_sum, grad_sum = carry
            l_k, (gw_k, gx_k) = one_microbatch(xs[0], xs[1])
            with _jax.named_scope("update"):
                return (loss_sum + l_k, _jax.tree.map(_jnp.add, grad_sum, gw_k)), gx_k

        init = (_jnp.zeros((), _jnp.float32), _jax.tree.map(_jnp.zeros_like, weights))
        (loss, grad_w), grad_x = _jax.lax.scan(body, init, (per_example, given["loss_target"]))
    with _jax.named_scope("update"):
        delta_w, new_m, new_v = {}, {}, {}
        for n in TWIN_WEIGHTS:
            delta_w[n], new_m[n], new_v[n] = _adamw(weights[n], grad_w[n], given["m_" + n], given["v_" + n])
    return (loss, grad_x, *[grad_w[n] for n in TWIN_WEIGHTS], *[delta_w[n] for n in TWIN_WEIGHTS],
            *[new_m[n] for n in TWIN_WEIGHTS], *[new_v[n] for n in TWIN_WEIGHTS])
```

```python
import jax
import jax.numpy as jnp
from jax import lax
from jax.experimental import pallas as pl
from jax.experimental.pallas import tpu as pltpu

f32 = jnp.float32
bf16 = jnp.bfloat16

DEPTH = 4
D_MODEL = 1024
N_MOD = 9
DN_ALPHA = (2.0 * DEPTH) ** 0.25
NORM_EPS = 1e-5
SSD_CHUNK = 128
SSD_D_INNER = 1024
SSD_CONV_DIM = 1536
SSD_HEADS = 16
POOL_WINDOWS = (2, 4, 8, 16)
POOL_DIM = 512
EVEN_IN = 3088
EVEN_IN_PAD = 3200
CONF_DIM = 512
CONF_KERNEL = 31
LRU_DIM = 1024
LRU_HEADS = 8
LRU_CONV = 4
SSD_CONV = 4
LRU_C = 8.0
ADAM_LR = 0.001
ADAM_B1 = 0.9
ADAM_B2 = 0.999
ADAM_EPS = 1e-08
ADAM_WD = 0.01
ADAM_STEP = 10

LANES = 128
VMEM_LIMIT_BYTES = 56 * 2 ** 20
COL_TILE = 512
N_CHIPS = 4
N_DEV = 8
MESH = pl.DeviceIdType.MESH


def _cp():
    return pltpu.CompilerParams(vmem_limit_bytes=VMEM_LIMIT_BYTES)


def _dot(a, b):
    return jnp.dot(a, b, preferred_element_type=f32)


def _dot_nt(a, b):
    return lax.dot_general(a, b, (((1,), (1,)), ((), ())), preferred_element_type=f32)


def _dot_tn(a, b):
    return lax.dot_general(a, b, (((0,), (0,)), ((), ())), preferred_element_type=f32)


def _dot_hi(a, b):
    return jnp.dot(a, b, preferred_element_type=f32, precision=lax.Precision.HIGHEST)


def _silu(x):
    return x * jax.nn.sigmoid(x)


def _ln_stats(z):
    mu = jnp.mean(z, axis=-1, keepdims=True)
    zc = z - mu
    var = jnp.mean(zc * zc, axis=-1, keepdims=True)
    rstd = lax.rsqrt(var + NORM_EPS)
    return zc * rstd, rstd


def _ln_bwd(dxn, xhat, rstd, lg):
    dxh = dxn * lg
    return rstd * (dxh - jnp.mean(dxh, axis=-1, keepdims=True) - xhat * jnp.mean(dxh * xhat, axis=-1, keepdims=True))


def _const_spec(shape):
    nd = len(shape)
    return pl.BlockSpec(shape, lambda *_: (0,) * nd, pipeline_mode=pl.Buffered(1))


def _row_tile(t, want=256):
    return min(want, t)


def ffn_fwd(x, mod3, w_in, w_out, lg, lb):
    B, T, D = x.shape
    FS = w_in.shape[2]
    tm = _row_tile(T)

    def body(x_ref, mod_ref, win_ref, wout_ref, lg_ref, lb_ref, xn_ref, h_ref, gu_ref, a_ref, y_ref):
        xv = x_ref[...]
        sh, sc, g = mod_ref[0:1, :], mod_ref[1:2, :], mod_ref[2:3, :]
        h = (xv * (1.0 + sc) + sh).astype(bf16)
        h_ref[...] = h
        acc = jnp.zeros((tm, D), f32)
        for s in range(2):
            gate = _dot(h, win_ref[s])
            up = _dot(h, win_ref[s + 2])
            gu_ref[:, s * FS:(s + 1) * FS] = gate.astype(bf16)
            gu_ref[:, (s + 2) * FS:(s + 3) * FS] = up.astype(bf16)
            a = (_silu(gate) * up).astype(bf16)
            a_ref[:, s * FS:(s + 1) * FS] = a
            acc = acc + _dot(a, wout_ref[s * FS:(s + 1) * FS, :])
        y_ref[...] = acc
        xhat, _ = _ln_stats(DN_ALPHA * xv + 0.5 * (1.0 + g) * acc)
        xn_ref[...] = xhat * lg_ref[...] + lb_ref[...]

    row = lambda w: pl.BlockSpec((None, tm, w), lambda b, i: (b, i, 0))
    return pl.pallas_call(
        body, name="ffn_fwd", grid=(B, T // tm),
        in_specs=[row(D), pl.BlockSpec((None, 3, D), lambda b, i: (b, 0, 0)), _const_spec(w_in.shape),
                  _const_spec(w_out.shape), _const_spec((1, D)), _const_spec((1, D))],
        out_specs=[row(D), row(D), row(4 * FS), row(2 * FS), row(D)],
        out_shape=[jax.ShapeDtypeStruct((B, T, D), f32), jax.ShapeDtypeStruct((B, T, D), bf16),
                   jax.ShapeDtypeStruct((B, T, 4 * FS), bf16), jax.ShapeDtypeStruct((B, T, 2 * FS), bf16),
                   jax.ShapeDtypeStruct((B, T, D), f32)],
        compiler_params=_cp(),
    )(x, mod3, w_in, w_out, lg, lb)


def ffn_bwd(dxn, x, y, gu, mod3, w_in, w_out, lg):
    B, T, D = x.shape
    FS = w_in.shape[2]
    tm = _row_tile(T)

    def body(dxn_ref, x_ref, y_ref, gu_ref, mod_ref, win_ref, wout_ref, lg_ref,
             dx_ref, dgu_ref, dy_ref, dmod_ref, dln_ref):
        b, i = pl.program_id(0), pl.program_id(1)

        @pl.when((b == 0) & (i == 0))
        def _():
            dln_ref[...] = jnp.zeros_like(dln_ref)

        @pl.when(i == 0)
        def _():
            dmod_ref[...] = jnp.zeros_like(dmod_ref)

        xv, yv, dxn_v = x_ref[...], y_ref[...], dxn_ref[...]
        sc, g = mod_ref[1:2, :], mod_ref[2:3, :]
        xhat, rstd = _ln_stats(DN_ALPHA * xv + 0.5 * (1.0 + g) * yv)
        dln_ref[0:1, :] += jnp.sum(dxn_v * xhat, axis=0, keepdims=True)
        dln_ref[1:2, :] += jnp.sum(dxn_v, axis=0, keepdims=True)
        dz = _ln_bwd(dxn_v, xhat, rstd, lg_ref[...])
        dmod_ref[2:3, :] += jnp.sum(0.5 * dz * yv, axis=0, keepdims=True)
        dy = (0.5 * (1.0 + g) * dz).astype(bf16)
        dy_ref[...] = dy
        dh = jnp.zeros((tm, D), f32)
        for s in range(2):
            da = _dot_nt(dy, wout_ref[s * FS:(s + 1) * FS, :])
            gate = gu_ref[:, s * FS:(s + 1) * FS].astype(f32)
            up = gu_ref[:, (s + 2) * FS:(s + 3) * FS].astype(f32)
            sig = jax.nn.sigmoid(gate)
            dgate = (da * up * (sig * (1.0 + gate * (1.0 - sig)))).astype(bf16)
            dup = (da * gate * sig).astype(bf16)
            dgu_ref[:, s * FS:(s + 1) * FS] = dgate
            dgu_ref[:, (s + 2) * FS:(s + 3) * FS] = dup
            dh = dh + _dot_nt(dgate, win_ref[s]) + _dot_nt(dup, win_ref[s + 2])
        dx_ref[...] = DN_ALPHA * dz + dh * (1.0 + sc)
        dmod_ref[0:1, :] += jnp.sum(dh, axis=0, keepdims=True)
        dmod_ref[1:2, :] += jnp.sum(dh * xv, axis=0, keepdims=True)

    row = lambda w: pl.BlockSpec((None, tm, w), lambda b, i: (b, i, 0))
    return pl.pallas_call(
        body, name="ffn_bwd", grid=(B, T // tm),
        in_specs=[row(D), row(D), row(D), row(4 * FS), pl.BlockSpec((None, 3, D), lambda b, i: (b, 0, 0)),
                  _const_spec(w_in.shape), _const_spec(w_out.shape), _const_spec((1, D))],
        out_specs=[row(D), row(4 * FS), row(D), pl.BlockSpec((None, 3, D), lambda b, i: (b, 0, 0)),
                   pl.BlockSpec((2, D), lambda b, i: (0, 0))],
        out_shape=[jax.ShapeDtypeStruct((B, T, D), f32), jax.ShapeDtypeStruct((B, T, 4 * FS), bf16),
                   jax.ShapeDtypeStruct((B, T, D), bf16), jax.ShapeDtypeStruct((B, 3, D), f32),
                   jax.ShapeDtypeStruct((2, D), f32)],
        compiler_params=_cp(),
    )(dxn, x, y, gu, mod3, w_in, w_out, lg)


def wgrad(a, b, tn):
    B, T, K = a.shape
    N = b.shape[2]
    tr = _row_tile(T, 512)

    def body(a_ref, b_ref, o_ref):
        @pl.when((pl.program_id(1) == 0) & (pl.program_id(2) == 0))
        def _():
            o_ref[...] = jnp.zeros_like(o_ref)

        o_ref[...] += _dot_tn(a_ref[...], b_ref[...])

    return pl.pallas_call(
        body, name="wgrad", grid=(N // tn, B, T // tr),
        in_specs=[pl.BlockSpec((None, tr, K), lambda s, b, r: (b, r, 0)),
                  pl.BlockSpec((None, tr, tn), lambda s, b, r: (b, r, s))],
        out_specs=pl.BlockSpec((None, K, tn), lambda s, b, r: (s, 0, 0)),
        out_shape=jax.ShapeDtypeStruct((N // tn, K, tn), f32),
        compiler_params=_cp(),
    )(a, b)


def inproj_fwd(x, mod3, w):
    B, T, D = x.shape
    N = w.shape[1]
    tm = _row_tile(T)

    def body(x_ref, mod_ref, w_ref, p_ref, h_ref):
        h = (x_ref[...] * (1.0 + mod_ref[1:2, :]) + mod_ref[0:1, :]).astype(bf16)
        h_ref[...] = h
        p_ref[...] = _dot(h, w_ref[...])

    row = lambda n: pl.BlockSpec((None, tm, n), lambda b, i: (b, i, 0))
    return pl.pallas_call(
        body, name="inproj_fwd", grid=(B, T // tm),
        in_specs=[row(D), pl.BlockSpec((None, 3, D), lambda b, i: (b, 0, 0)), _const_spec(w.shape)],
        out_specs=[row(N), row(D)],
        out_shape=[jax.ShapeDtypeStruct((B, T, N), f32), jax.ShapeDtypeStruct((B, T, D), bf16)],
        compiler_params=_cp(),
    )(x, mod3, w)


def inproj_bwd(dproj, w, x, mod3, dxp):
    B, T, D = x.shape
    N = w.shape[1]
    tm = _row_tile(T)

    def body(dp_ref, w_ref, x_ref, mod_ref, dxp_ref, dx_ref, dmod_ref):
        @pl.when(pl.program_id(1) == 0)
        def _():
            dmod_ref[...] = jnp.zeros_like(dmod_ref)

        dh = _dot_nt(dp_ref[...], w_ref[...])
        dx_ref[...] = dxp_ref[...] + dh * (1.0 + mod_ref[1:2, :])
        dmod_ref[0:1, :] += jnp.sum(dh, axis=0, keepdims=True)
        dmod_ref[1:2, :] += jnp.sum(dh * x_ref[...], axis=0, keepdims=True)

    row = lambda n: pl.BlockSpec((None, tm, n), lambda b, i: (b, i, 0))
    return pl.pallas_call(
        body, name="inproj_bwd", grid=(B, T // tm),
        in_specs=[row(N), _const_spec(w.shape), row(D), pl.BlockSpec((None, 3, D), lambda b, i: (b, 0, 0)), row(D)],
        out_specs=[row(D), pl.BlockSpec((None, 2, D), lambda b, i: (b, 0, 0))],
        out_shape=[jax.ShapeDtypeStruct((B, T, D), f32), jax.ShapeDtypeStruct((B, 2, D), f32)],
        compiler_params=_cp(),
    )(dproj, w, x, mod3, dxp)


def outproj_fwd(ycat, w, x, mod3, lg, lb):
    B, T, D = x.shape
    E = w.shape[0]
    tm = _row_tile(T)

    def body(yc_ref, w_ref, x_ref, mod_ref, lg_ref, lb_ref, xn_ref, y_ref):
        yv = _dot(yc_ref[...], w_ref[...])
        y_ref[...] = yv
        xhat, _ = _ln_stats(DN_ALPHA * x_ref[...] + (1.0 + mod_ref[2:3, :]) * yv)
        xn_ref[...] = xhat * lg_ref[...] + lb_ref[...]

    row = lambda n: pl.BlockSpec((None, tm, n), lambda b, i: (b, i, 0))
    return pl.pallas_call(
        body, name="outproj_fwd", grid=(B, T // tm),
        in_specs=[row(E), _const_spec(w.shape), row(D), pl.BlockSpec((None, 3, D), lambda b, i: (b, 0, 0)),
                  _const_spec((1, D)), _const_spec((1, D))],
        out_specs=[row(D), row(D)],
        out_shape=[jax.ShapeDtypeStruct((B, T, D), f32), jax.ShapeDtypeStruct((B, T, D), f32)],
        compiler_params=_cp(),
    )(ycat, w, x, mod3, lg, lb)


def outproj_bwd(dxn, x, y, mod3, w, lg):
    B, T, D = x.shape
    E = w.shape[0]
    tm = _row_tile(T)

    def body(dxn_ref, x_ref, y_ref, mod_ref, w_ref, lg_ref, dxp_ref, dyc_ref, dy_ref, dg_ref, dln_ref):
        b, i = pl.program_id(0), pl.program_id(1)

        @pl.when((b == 0) & (i == 0))
        def _():
            dln_ref[...] = jnp.zeros_like(dln_ref)

        @pl.when(i == 0)
        def _():
            dg_ref[...] = jnp.zeros_like(dg_ref)

        xv, yv, dxn_v = x_ref[...], y_ref[...], dxn_ref[...]
        g = mod_ref[2:3, :]
        xhat, rstd = _ln_stats(DN_ALPHA * xv + (1.0 + g) * yv)
        dln_ref[0:1, :] += jnp.sum(dxn_v * xhat, axis=0, keepdims=True)
        dln_ref[1:2, :] += jnp.sum(dxn_v, axis=0, keepdims=True)
        dz = _ln_bwd(dxn_v, xhat, rstd, lg_ref[...])
        dg_ref[...] += jnp.sum(dz * yv, axis=0, keepdims=True)
        dy = ((1.0 + g) * dz).astype(bf16)
        dy_ref[...] = dy
        dxp_ref[...] = DN_ALPHA * dz
        dyc_ref[...] = _dot_nt(dy, w_ref[...])

    row = lambda n: pl.BlockSpec((None, tm, n), lambda b, i: (b, i, 0))
    return pl.pallas_call(
        body, name="outproj_bwd", grid=(B, T // tm),
        in_specs=[row(D), row(D), row(D), pl.BlockSpec((None, 3, D), lambda b, i: (b, 0, 0)), _const_spec(w.shape),
                  _const_spec((1, D))],
        out_specs=[row(D), row(E), row(D), pl.BlockSpec((None, 1, D), lambda b, i: (b, 0, 0)),
                   pl.BlockSpec((2, D), lambda b, i: (0, 0))],
        out_shape=[jax.ShapeDtypeStruct((B, T, D), f32), jax.ShapeDtypeStruct((B, T, E), f32),
                   jax.ShapeDtypeStruct((B, T, D), bf16), jax.ShapeDtypeStruct((B, 1, D), f32),
                   jax.ShapeDtypeStruct((2, D), f32)],
        compiler_params=_cp(),
    )(dxn, x, y, mod3, w, lg)


def _halo_rows(K):
    return 8 if K <= 9 else 32


def dwconv_fwd(x, col0, C, w, b, K):
    B, T, _ = x.shape
    tc, hp = COL_TILE, _halo_rows(K)
    tm = _row_tile(T)
    r = tm // hp

    def body(xh_ref, x_ref, w_ref, b_ref, o_ref):
        halo = jnp.where(pl.program_id(2) == 0, 0.0, xh_ref[...])
        xe = jnp.concatenate([halo, x_ref[...]], axis=0)
        acc = jnp.zeros((tm, tc), f32) + b_ref[...]
        for k in range(K):
            sft = K - 1 - k
            xs = xe if sft == 0 else pltpu.roll(xe, sft, 0)
            acc = acc + xs[hp:, :] * w_ref[k:k + 1, :]
        o_ref[...] = acc

    return pl.pallas_call(
        body, name=f"dwconv{K}_fwd", grid=(C // tc, B, T // tm),
        in_specs=[pl.BlockSpec((None, hp, tc), lambda j, b, i: (b, jnp.maximum(i * r - 1, 0), col0 + j)),
                  pl.BlockSpec((None, tm, tc), lambda j, b, i: (b, i, col0 + j)),
                  pl.BlockSpec((w.shape[0], tc), lambda j, b, i: (0, j)),
                  pl.BlockSpec((1, tc), lambda j, b, i: (0, j))],
        out_specs=pl.BlockSpec((None, tm, tc), lambda j, b, i: (b, i, j)),
        out_shape=jax.ShapeDtypeStruct((B, T, C), f32),
        compiler_params=_cp(),
    )(x, x, w, b)


def dwconv_bwd(dc, x, col0, C, w, K, out_dtype):
    B, T, _ = x.shape
    tc, hp = COL_TILE, _halo_rows(K)
    tm = _row_tile(T)
    r = tm // hp
    nt = T // tm
    n = tm + hp
    KP = w.shape[0]

    def body(dcn_ref, dc_ref, xh_ref, x_ref, w_ref, dx_ref, dw_ref, db_ref):
        b, i = pl.program_id(1), pl.program_id(2)

        @pl.when((b == 0) & (i == 0))
        def _():
            dw_ref[...] = jnp.zeros_like(dw_ref)
            db_ref[...] = jnp.zeros_like(db_ref)

        dcv = dc_ref[...]
        de = jnp.concatenate([dcv, jnp.where(i == nt - 1, 0.0, dcn_ref[...])], axis=0)
        acc = jnp.zeros((tm, tc), f32)
        for k in range(K):
            j = K - 1 - k
            ds = de if j == 0 else pltpu.roll(de, n - j, 0)
            acc = acc + ds[:tm, :] * w_ref[k:k + 1, :]
        dx_ref[...] = acc.astype(out_dtype)
        xe = jnp.concatenate([jnp.where(i == 0, 0.0, xh_ref[...]), x_ref[...]], axis=0)
        for k in range(K):
            sft = K - 1 - k
            xs = xe if sft == 0 else pltpu.roll(xe, sft, 0)
            dw_ref[k:k + 1, :] += jnp.sum(dcv * xs[hp:, :], axis=0, keepdims=True)
        db_ref[...] += jnp.sum(dcv, axis=0, keepdims=True)

    return pl.pallas_call(
        body, name=f"dwconv{K}_bwd", grid=(C // tc, B, nt),
        in_specs=[pl.BlockSpec((None, hp, tc), lambda j, b, i: (b, jnp.minimum((i + 1) * r, T // hp - 1), j)),
                  pl.BlockSpec((None, tm, tc), lambda j, b, i: (b, i, j)),
                  pl.BlockSpec((None, hp, tc), lambda j, b, i: (b, jnp.maximum(i * r - 1, 0), col0 + j)),
                  pl.BlockSpec((None, tm, tc), lambda j, b, i: (b, i, col0 + j)),
                  pl.BlockSpec((KP, tc), lambda j, b, i: (0, j))],
        out_specs=[pl.BlockSpec((None, tm, tc), lambda j, b, i: (b, i, j)),
                   pl.BlockSpec((KP, tc), lambda j, b, i: (0, j)),
                   pl.BlockSpec((1, tc), lambda j, b, i: (0, j))],
        out_shape=[jax.ShapeDtypeStruct((B, T, C), out_dtype), jax.ShapeDtypeStruct((KP, C), f32),
                   jax.ShapeDtypeStruct((1, C), f32)],
        compiler_params=_cp(),
    )(dc, dc, x, x, w)


POOL_HALO = 16


def _pool_windows(ue, pos, hp):
    out = []
    for g, wd in enumerate(POOL_WINDOWS):
        ug = ue[:, g * LANES:(g + 1) * LANES]
        s, span = ug, 1
        while span < wd:
            s = s + pltpu.roll(s, span, 0)
            span *= 2
        cnt = jnp.minimum(pos + 1, wd).astype(f32)
        out.append(s[hp:, :] / cnt - ug[hp:, :])
    return out


def pool_fwd(proj, colb, w, scale):
    B, T, _ = proj.shape
    hp = POOL_HALO
    tm = _row_tile(T)
    r = tm // hp

    def body(uh_ref, u_ref, w_ref, sc_ref, o_ref):
        i = pl.program_id(1)
        ue = jnp.concatenate([jnp.where(i == 0, 0.0, uh_ref[...]), u_ref[...]], axis=0)
        pos = i * tm + lax.broadcasted_iota(jnp.int32, (tm, 1), 0)
        ps = _pool_windows(ue, pos, hp)
        o = jnp.concatenate([_dot(ps[g], w_ref[g]) for g in range(4)], axis=1) * sc_ref[...]
        o_ref[...] = o.astype(bf16)

    return pl.pallas_call(
        body, name="pool_fwd", grid=(B, T // tm),
        in_specs=[pl.BlockSpec((None, hp, POOL_DIM), lambda b, i: (b, jnp.maximum(i * r - 1, 0), colb)),
                  pl.BlockSpec((None, tm, POOL_DIM), lambda b, i: (b, i, colb)),
                  _const_spec(w.shape), _const_spec((1, POOL_DIM))],
        out_specs=pl.BlockSpec((None, tm, POOL_DIM), lambda b, i: (b, i, 0)),
        out_shape=jax.ShapeDtypeStruct((B, T, POOL_DIM), bf16),
        compiler_params=_cp(),
    )(proj, proj, w, scale)


def pool_bwd(dycat, dcolb, proj, colb, w, scale):
    B, T, _ = proj.shape
    hp = POOL_HALO
    tm = _row_tile(T)
    r = tm // hp
    nt = T // tm
    n = tm + hp

    def body(dyn_ref, dy_ref, uh_ref, u_ref, w_ref, sc_ref, du_ref, dw_ref, dsc_ref):
        b, i = pl.program_id(0), pl.program_id(1)

        @pl.when((b == 0) & (i == 0))
        def _():
            dw_ref[...] = jnp.zeros_like(dw_ref)
            dsc_ref[...] = jnp.zeros_like(dsc_ref)

        dyv = dy_ref[...]
        dye = jnp.concatenate([dyv, jnp.where(i == nt - 1, 0.0, dyn_ref[...])], axis=0)
        ue = jnp.concatenate([jnp.where(i == 0, 0.0, uh_ref[...]), u_ref[...]], axis=0)
        pos = i * tm + lax.broadcasted_iota(jnp.int32, (tm, 1), 0)
        pos_e = i * tm + lax.broadcasted_iota(jnp.int32, (n, 1), 0)
        ps = _pool_windows(ue, pos, hp)
        dme = dye * sc_ref[...]
        dus, dscs = [], []
        for g, wd in enumerate(POOL_WINDOWS):
            sl = slice(g * LANES, (g + 1) * LANES)
            dscs.append(jnp.sum(dyv[:, sl] * _dot(ps[g], w_ref[g]), axis=0, keepdims=True))
            dw_ref[g] += _dot_tn(ps[g], dme[:tm, sl])
            dpe = _dot_nt(dme[:, sl], w_ref[g])
            s, span = dpe / jnp.minimum(pos_e + 1, wd).astype(f32), 1
            while span < wd:
                s = s + pltpu.roll(s, n - span, 0)
                span *= 2
            dus.append(s[:tm, :] - dpe[:tm, :])
        du_ref[...] = jnp.concatenate(dus, axis=1).astype(bf16)
        dsc_ref[...] += jnp.concatenate(dscs, axis=1)

    return pl.pallas_call(
        body, name="pool_bwd", grid=(B, nt),
        in_specs=[pl.BlockSpec((None, hp, POOL_DIM), lambda b, i: (b, jnp.minimum((i + 1) * r, T // hp - 1), dcolb)),
                  pl.BlockSpec((None, tm, POOL_DIM), lambda b, i: (b, i, dcolb)),
                  pl.BlockSpec((None, hp, POOL_DIM), lambda b, i: (b, jnp.maximum(i * r - 1, 0), colb)),
                  pl.BlockSpec((None, tm, POOL_DIM), lambda b, i: (b, i, colb)),
                  _const_spec(w.shape), _const_spec((1, POOL_DIM))],
        out_specs=[pl.BlockSpec((None, tm, POOL_DIM), lambda b, i: (b, i, 0)),
                   pl.BlockSpec(w.shape, lambda b, i: (0, 0, 0)),
                   pl.BlockSpec((1, POOL_DIM), lambda b, i: (0, 0))],
        out_shape=[jax.ShapeDtypeStruct((B, T, POOL_DIM), bf16), jax.ShapeDtypeStruct(w.shape, f32),
                   jax.ShapeDtypeStruct((1, POOL_DIM), f32)],
        compiler_params=_cp(),
    )(dycat, dycat, proj, proj, w, scale)


N_PAIRS = SSD_HEADS // 2


def _ssd_chunk(xs, bs, cs, dtp, zs, hs, dtb, alog, dsk, ngs):
    Q = SSD_CHUNK
    lane = lax.broadcasted_iota(jnp.int32, (1, LANES), 1)
    sub = lax.broadcasted_iota(jnp.int32, (LANES, 1), 0)
    causal = lax.broadcasted_iota(jnp.int32, (Q, Q), 0) >= lax.broadcasted_iota(jnp.int32, (Q, Q), 1)
    lane_lo, sub_lo = lane < 64, sub < 64

    def col(v, h):
        return jnp.sum(v * (lane == h).astype(f32), axis=1, keepdims=True)

    def row(vt, h):
        return jnp.sum(vt * (sub == h).astype(f32), axis=0, keepdims=True)

    dt = jax.nn.softplus(dtp + dtb)
    acum = _dot_hi(causal.astype(f32), dt * (-jnp.exp(alog)))
    acum_t = acum.T
    aend = jnp.sum(acum * (sub == Q - 1).astype(f32), axis=0, keepdims=True)
    outs, hn = [], []
    for grp in range(2):
        bv, cv = _silu(bs[grp]), _silu(cs[grp])
        gmat = _dot_nt(cv, bv)
        for j in range(4):
            p = grp * 4 + j
            h0, h1 = 2 * p, 2 * p + 1
            x2 = _silu(xs[p])
            c0, c1 = col(acum, h0), col(acum, h1)
            s2 = jnp.where(lane_lo, c0, c1)
            xdt = x2 * jnp.where(lane_lo, col(dt, h0), col(dt, h1))
            l0 = jnp.where(causal, jnp.exp(jnp.minimum(c0 - row(acum_t, h0), 0.0)), 0.0)
            l1 = jnp.where(causal, jnp.exp(jnp.minimum(c1 - row(acum_t, h1), 0.0)), 0.0)
            yd = _dot(gmat * l0, jnp.where(lane_lo, xdt, 0.0)) + _dot(gmat * l1, jnp.where(lane_lo, 0.0, xdt))
            e0, e1 = col(aend, h0), col(aend, h1)
            st = _dot_tn(xdt * jnp.exp(jnp.where(lane_lo, e0, e1) - s2), bv)
            yo = jnp.exp(s2) * _dot_nt(cv, hs[p])
            hn.append(jnp.exp(jnp.where(sub_lo, e0, e1)) * hs[p] + st)
            yv = yd + yo + x2 * jnp.where(lane_lo, col(dsk, h0), col(dsk, h1))
            outs.append(yv * _silu(zs[p]))
    ms = sum(jnp.sum(o * o, axis=1, keepdims=True) for o in outs) / SSD_D_INNER
    rs = lax.rsqrt(ms + NORM_EPS)
    return [outs[p] * rs * ngs[p] for p in range(N_PAIRS)], hn


def _lane_blocks(ref, n, start=0):
    return [ref[:, (start + k) * LANES:(start + k + 1) * LANES] for k in range(n)]


def _ssd_args(z_ref, cx_ref, dt_ref, dtb_ref, alog_ref, dsk_ref, ng_ref):
    xs = _lane_blocks(cx_ref, 8)
    bs = _lane_blocks(cx_ref, 2, 8)
    cs = _lane_blocks(cx_ref, 2, 10)
    zs = _lane_blocks(z_ref, 8)
    ngs = _lane_blocks(ng_ref, 8)
    return xs, bs, cs, dt_ref[...], zs, dtb_ref[...], alog_ref[...], dsk_ref[...], ngs


DT_COLB = (EVEN_IN_PAD - LANES) // LANES


def ssd_fwd(proj, cx, dtb, alog, dsk, ng):
    B, T, _ = proj.shape
    Q = SSD_CHUNK
    nc = T // Q

    def body(z_ref, cx_ref, dt_ref, dtb_ref, alog_ref, dsk_ref, ng_ref, ya_ref, hsave_ref, h_scr):
        @pl.when(pl.program_id(1) == 0)
        def _():
            h_scr[...] = jnp.zeros_like(h_scr)

        xs, bs, cs, dtp, zs, dtb_v, alog_v, dsk_v, ngs = _ssd_args(z_ref, cx_ref, dt_ref, dtb_ref, alog_ref, dsk_ref, ng_ref)
        hs = [h_scr[p] for p in range(N_PAIRS)]
        for p in range(N_PAIRS):
            hsave_ref[p] = hs[p]
        outs, hn = _ssd_chunk(xs, bs, cs, dtp, zs, hs, dtb_v, alog_v, dsk_v, ngs)
        for p in range(N_PAIRS):
            ya_ref[:, p * LANES:(p + 1) * LANES] = outs[p].astype(bf16)
            h_scr[p] = hn[p]

    return pl.pallas_call(
        body, name="ssd_fwd", grid=(B, nc),
        in_specs=[pl.BlockSpec((None, Q, SSD_D_INNER), lambda b, i: (b, i, 0)),
                  pl.BlockSpec((None, Q, SSD_CONV_DIM), lambda b, i: (b, i, 0)),
                  pl.BlockSpec((None, Q, LANES), lambda b, i: (b, i, DT_COLB)),
                  _const_spec((1, LANES)), _const_spec((1, LANES)), _const_spec((1, LANES)),
                  _const_spec((1, SSD_D_INNER))],
        out_specs=[pl.BlockSpec((None, Q, SSD_D_INNER), lambda b, i: (b, i, 0)),
                   pl.BlockSpec((None, None, N_PAIRS, LANES, LANES), lambda b, i: (b, i, 0, 0, 0))],
        out_shape=[jax.ShapeDtypeStruct((B, T, SSD_D_INNER), bf16),
                   jax.ShapeDtypeStruct((B, nc, N_PAIRS, LANES, LANES), f32)],
        scratch_shapes=[pltpu.VMEM((N_PAIRS, LANES, LANES), f32)],
        compiler_params=_cp(),
    )(proj, cx, proj, dtb, alog, dsk, ng)


def ssd_bwd(dycat, proj, cx, hsave, dtb, alog, dsk, ng):
    B, T, _ = proj.shape
    Q = SSD_CHUNK
    nc = T // Q

    def body(dya_ref, z_ref, cx_ref, dt_ref, hsave_ref, dtb_ref, alog_ref, dsk_ref, ng_ref,
             dz_ref, dcx_ref, ddt_ref, gsm_ref, gng_ref, dh_scr):
        b, i = pl.program_id(0), pl.program_id(1)

        @pl.when((b == 0) & (i == 0))
        def _():
            gsm_ref[...] = jnp.zeros_like(gsm_ref)
            gng_ref[...] = jnp.zeros_like(gng_ref)

        @pl.when(i == 0)
        def _():
            dh_scr[...] = jnp.zeros_like(dh_scr)

        xs, bs, cs, dtp, zs, dtb_v, alog_v, dsk_v, ngs = _ssd_args(z_ref, cx_ref, dt_ref, dtb_ref, alog_ref, dsk_ref, ng_ref)
        hs = [hsave_ref[p] for p in range(N_PAIRS)]
        _, vjp = jax.vjp(_ssd_chunk, xs, bs, cs, dtp, zs, hs, dtb_v, alog_v, dsk_v, ngs)
        douts = _lane_blocks(dya_ref, 8)
        dhn = [dh_scr[p] for p in range(N_PAIRS)]
        dxs, dbs, dcs, ddtp, dzs, dhs, ddtb, dalog, ddsk, dngs = vjp((douts, dhn))
        for p in range(N_PAIRS):
            dcx_ref[:, p * LANES:(p + 1) * LANES] = dxs[p]
            dz_ref[:, p * LANES:(p + 1) * LANES] = dzs[p].astype(bf16)
            dh_scr[p] = dhs[p]
            gng_ref[:, p * LANES:(p + 1) * LANES] += dngs[p]
        for k in range(2):
            dcx_ref[:, (8 + k) * LANES:(9 + k) * LANES] = dbs[k]
            dcx_ref[:, (10 + k) * LANES:(11 + k) * LANES] = dcs[k]
        ddt_ref[...] = ddtp.astype(bf16)
        gsm_ref[0:1, :] += ddtb
        gsm_ref[1:2, :] += dalog
        gsm_ref[2:3, :] += ddsk

    rev = lambda w, cb=0: pl.BlockSpec((None, Q, w), lambda b, i: (b, nc - 1 - i, cb))
    return pl.pallas_call(
        body, name="ssd_bwd", grid=(B, nc),
        in_specs=[rev(SSD_D_INNER), rev(SSD_D_INNER), rev(SSD_CONV_DIM), rev(LANES, DT_COLB),
                  pl.BlockSpec((None, None, N_PAIRS, LANES, LANES), lambda b, i: (b, nc - 1 - i, 0, 0, 0)),
                  _const_spec((1, LANES)), _const_spec((1, LANES)), _const_spec((1, LANES)),
                  _const_spec((1, SSD_D_INNER))],
        out_specs=[rev(SSD_D_INNER), rev(SSD_CONV_DIM), rev(LANES),
                   pl.BlockSpec((3, LANES), lambda b, i: (0, 0)),
                   pl.BlockSpec((1, SSD_D_INNER), lambda b, i: (0, 0))],
        out_shape=[jax.ShapeDtypeStruct((B, T, SSD_D_INNER), bf16), jax.ShapeDtypeStruct((B, T, SSD_CONV_DIM), f32),
                   jax.ShapeDtypeStruct((B, T, LANES), bf16), jax.ShapeDtypeStruct((3, LANES), f32),
                   jax.ShapeDtypeStruct((1, SSD_D_INNER), f32)],
        scratch_shapes=[pltpu.VMEM((N_PAIRS, LANES, LANES), f32)],
        compiler_params=_cp(),
    )(dycat, proj, cx, proj, hsave, dtb, alog, dsk, ng)


def glu_fwd(proj):
    B, T, _ = proj.shape
    tm = _row_tile(T)

    def body(v_ref, g_ref, o_ref):
        o_ref[...] = v_ref[...] * jax.nn.sigmoid(g_ref[...])

    blk = lambda cb: pl.BlockSpec((None, tm, CONF_DIM), lambda b, i: (b, i, cb))
    return pl.pallas_call(body, name="glu_fwd", grid=(B, T // tm), in_specs=[blk(0), blk(1)], out_specs=blk(0),
                          out_shape=jax.ShapeDtypeStruct((B, T, CONF_DIM), f32), compiler_params=_cp())(proj, proj)


def glu_bwd(dhh, proj):
    B, T, _ = proj.shape
    tm = _row_tile(T)

    def body(d_ref, v_ref, g_ref, o_ref):
        sig = jax.nn.sigmoid(g_ref[...])
        dv = d_ref[...]
        o_ref[:, :CONF_DIM] = (dv * sig).astype(bf16)
        o_ref[:, CONF_DIM:] = (dv * v_ref[...] * sig * (1.0 - sig)).astype(bf16)

    blk = lambda cb: pl.BlockSpec((None, tm, CONF_DIM), lambda b, i: (b, i, cb))
    return pl.pallas_call(body, name="glu_bwd", grid=(B, T // tm), in_specs=[blk(0), blk(0), blk(1)],
                          out_specs=pl.BlockSpec((None, tm, 2 * CONF_DIM), lambda b, i: (b, i, 0)),
                          out_shape=jax.ShapeDtypeStruct((B, T, 2 * CONF_DIM), bf16), compiler_params=_cp())(dhh, proj, proj)


def _neg_expm1(x):
    series = x * (1.0 + x * (1.0 / 2.0) * (1.0 + x * (1.0 / 3.0) * (1.0 + x * (1.0 / 4.0) * (1.0 + x * (1.0 / 5.0)))))
    return -jnp.where(x > -0.1, series, jnp.exp(x) - 1.0)


def _lru_gates(cc, xc8, gr8, clg, clb, wa8, wx8, ba8, bx8, lam8):
    xhat, _ = _ln_stats(cc)
    yc = _silu(xhat * clg + clb)
    a8, b8, ge8 = [], [], []
    for hb in range(LRU_HEADS):
        xh = xc8[hb]
        rg = jax.nn.sigmoid(_dot(xh, wa8[hb]) + ba8[hb])
        ig = jax.nn.sigmoid(_dot(xh, wx8[hb]) + bx8[hb])
        log_a = -LRU_C * rg * jax.nn.softplus(-lam8[hb])
        a8.append(jnp.exp(log_a))
        b8.append(jnp.sqrt(_neg_expm1(2.0 * log_a)) * (ig * xh))
        ge8.append(jax.nn.gelu(gr8[hb]))
    return yc, a8, b8, ge8


def _scan_fwd(a, b, h_in):
    tm = a.shape[0]
    rows = lax.broadcasted_iota(jnp.int32, (tm, 1), 0)
    s = 1
    while s < tm:
        keep = rows >= s
        b = a * jnp.where(keep, pltpu.roll(b, s, 0), 0.0) + b
        a = a * jnp.where(keep, pltpu.roll(a, s, 0), 1.0)
        s *= 2
    return a * h_in + b


def _scan_bwd(e, d, g_in):
    tm = e.shape[0]
    rows = lax.broadcasted_iota(jnp.int32, (tm, 1), 0)
    s = 1
    while s < tm:
        keep = rows < tm - s
        d = e * jnp.where(keep, pltpu.roll(d, tm - s, 0), 0.0) + d
        e = e * jnp.where(keep, pltpu.roll(e, tm - s, 0), 1.0)
        s *= 2
    return e * g_in + d


def _lru_params(wa_ref, wx_ref, vec_ref):
    wa8 = [wa_ref[h] for h in range(LRU_HEADS)]
    wx8 = [wx_ref[h] for h in range(LRU_HEADS)]
    ba8 = [vec_ref[0:1, h * LANES:(h + 1) * LANES] for h in range(LRU_HEADS)]
    bx8 = [vec_ref[1:2, h * LANES:(h + 1) * LANES] for h in range(LRU_HEADS)]
    lam8 = [vec_ref[2:3, h * LANES:(h + 1) * LANES] for h in range(LRU_HEADS)]
    return wa8, wx8, ba8, bx8, lam8


GR_COLB = 2


def lru_fwd(cc, xc, proj, cln, wa, wx, vec):
    B, T, _ = xc.shape
    tm = _row_tile(T)

    def body(cc_ref, xc_ref, gr_ref, cln_ref, wa_ref, wx_ref, vec_ref, y_ref, hs_ref, h_scr):
        @pl.when(pl.program_id(1) == 0)
        def _():
            h_scr[...] = jnp.zeros_like(h_scr)

        yc, a8, b8, ge8 = _lru_gates(cc_ref[...], _lane_blocks(xc_ref, 8), _lane_blocks(gr_ref, 8), cln_ref[0:1, :],
                                     cln_ref[1:2, :], *_lru_params(wa_ref, wx_ref, vec_ref))
        h = _scan_fwd(jnp.concatenate(a8, axis=1), jnp.concatenate(b8, axis=1), h_scr[...])
        hs_ref[...] = h
        h_scr[...] = h[tm - 1:tm, :]
        y_ref[:, :CONF_DIM] = yc.astype(bf16)
        y_ref[:, CONF_DIM:] = (h * jnp.concatenate(ge8, axis=1)).astype(bf16)

    row = lambda w, cb=0: pl.BlockSpec((None, tm, w), lambda b, i: (b, i, cb))
    return pl.pallas_call(
        body, name="lru_fwd", grid=(B, T // tm),
        in_specs=[row(CONF_DIM), row(LRU_DIM), row(LRU_DIM, GR_COLB), _const_spec((2, CONF_DIM)),
                  _const_spec(wa.shape), _const_spec(wx.shape), _const_spec((3, LRU_DIM))],
        out_specs=[row(CONF_DIM + LRU_DIM), row(LRU_DIM)],
        out_shape=[jax.ShapeDtypeStruct((B, T, CONF_DIM + LRU_DIM), bf16), jax.ShapeDtypeStruct((B, T, LRU_DIM), f32)],
        scratch_shapes=[pltpu.VMEM((1, LRU_DIM), f32)],
        compiler_params=_cp(),
    )(cc, xc, proj, cln, wa, wx, vec)


def lru_bwd(dycat, cc, xc, proj, hs, cln, wa, wx, vec):
    B, T, _ = xc.shape
    tm = _row_tile(T)
    nt = T // tm
    r = tm // 8

    def body(dy_ref, cc_ref, xc_ref, gr_ref, hs_ref, hsh_ref, cln_ref, wa_ref, wx_ref, vec_ref,
             dcc_ref, dxc_ref, dgr_ref, dcln_ref, dwa_ref, dwx_ref, dvec_ref, g_scr, a_scr):
        b, i = pl.program_id(0), pl.program_id(1)
        it = nt - 1 - i

        @pl.when((b == 0) & (i == 0))
        def _():
            dcln_ref[...] = jnp.zeros_like(dcln_ref)
            dwa_ref[...] = jnp.zeros_like(dwa_ref)
            dwx_ref[...] = jnp.zeros_like(dwx_ref)
            dvec_ref[...] = jnp.zeros_like(dvec_ref)

        @pl.when(i == 0)
        def _():
            g_scr[...] = jnp.zeros_like(g_scr)
            a_scr[...] = jnp.zeros_like(a_scr)

        (yc, a8, b8, ge8), vjp = jax.vjp(_lru_gates, cc_ref[...], _lane_blocks(xc_ref, 8), _lane_blocks(gr_ref, 8),
                                         cln_ref[0:1, :], cln_ref[1:2, :], *_lru_params(wa_ref, wx_ref, vec_ref))
        a = jnp.concatenate(a8, axis=1)
        ge = jnp.concatenate(ge8, axis=1)
        h = hs_ref[...]
        dyd = dy_ref[:, CONF_DIM:]
        rows = lax.broadcasted_iota(jnp.int32, (tm, 1), 0)
        e = jnp.where(rows < tm - 1, pltpu.roll(a, tm - 1, 0), a_scr[...])
        g = _scan_bwd(e, dyd * ge, g_scr[...])
        h_first = jnp.where(it == 0, 0.0, hsh_ref[7:8, :])
        h_prev = jnp.where(rows >= 1, pltpu.roll(h, 1, 0), h_first)
        da = g * h_prev
        g_scr[...] = g[0:1, :]
        a_scr[...] = a[0:1, :]
        split = lambda v: [v[:, k * LANES:(k + 1) * LANES] for k in range(LRU_HEADS)]
        dcc, dxc8, dgr8, dclg, dclb, dwa8, dwx8, dba8, dbx8, dlam8 = vjp((dy_ref[:, :CONF_DIM], split(da), split(g), split(dyd * h)))
        dcc_ref[...] = dcc
        dcln_ref[0:1, :] += dclg
        dcln_ref[1:2, :] += dclb
        for k in range(LRU_HEADS):
            sl = slice(k * LANES, (k + 1) * LANES)
            dxc_ref[:, sl] = dxc8[k]
            dgr_ref[:, sl] = dgr8[k].astype(bf16)
            dwa_ref[k] += dwa8[k]
            dwx_ref[k] += dwx8[k]
            dvec_ref[0:1, sl] += dba8[k]
            dvec_ref[1:2, sl] += dbx8[k]
            dvec_ref[2:3, sl] += dlam8[k]

    rev = lambda w, cb=0: pl.BlockSpec((None, tm, w), lambda b, i: (b, nt - 1 - i, cb))
    acc = lambda shape: pl.BlockSpec(shape, lambda b, i: (0,) * len(shape))
    return pl.pallas_call(
        body, name="lru_bwd", grid=(B, nt),
        in_specs=[rev(CONF_DIM + LRU_DIM), rev(CONF_DIM), rev(LRU_DIM), rev(LRU_DIM, GR_COLB), rev(LRU_DIM),
                  pl.BlockSpec((None, 8, LRU_DIM), lambda b, i: (b, jnp.maximum((nt - 1 - i) * r - 1, 0), 0)),
                  _const_spec((2, CONF_DIM)), _const_spec(wa.shape), _const_spec(wx.shape), _const_spec((3, LRU_DIM))],
        out_specs=[rev(CONF_DIM), rev(LRU_DIM), rev(LRU_DIM), acc((2, CONF_DIM)), acc(wa.shape), acc(wx.shape),
                   acc((3, LRU_DIM))],
        out_shape=[jax.ShapeDtypeStruct((B, T, CONF_DIM), f32), jax.ShapeDtypeStruct((B, T, LRU_DIM), f32),
                   jax.ShapeDtypeStruct((B, T, LRU_DIM), bf16), jax.ShapeDtypeStruct((2, CONF_DIM), f32),
                   jax.ShapeDtypeStruct(wa.shape, f32), jax.ShapeDtypeStruct(wx.shape, f32),
                   jax.ShapeDtypeStruct((3, LRU_DIM), f32)],
        scratch_shapes=[pltpu.VMEM((1, LRU_DIM), f32), pltpu.VMEM((1, LRU_DIM), f32)],
        compiler_params=_cp(),
    )(dycat, cc, xc, proj, hs, hs, cln, wa, wx, vec)


def loss_fwd(y, target):
    B, T, D = y.shape
    tm = _row_tile(T)

    def body(y_ref, t_ref, l_ref, dy_ref):
        @pl.when((pl.program_id(0) == 0) & (pl.program_id(1) == 0))
        def _():
            l_ref[...] = jnp.zeros_like(l_ref)

        d = y_ref[...] - t_ref[...]
        dy_ref[...] = d * (1.0 / D)
        l_ref[...] += jnp.sum(jnp.sum(d * d, axis=1, keepdims=True), axis=0, keepdims=True)

    row = pl.BlockSpec((None, tm, D), lambda b, i: (b, i, 0))
    return pl.pallas_call(
        body, name="loss_fwd", grid=(B, T // tm), in_specs=[row, row],
        out_specs=[pl.BlockSpec((1, 1), lambda b, i: (0, 0)), row],
        out_shape=[jax.ShapeDtypeStruct((1, 1), f32), jax.ShapeDtypeStruct((B, T, D), f32)],
        compiler_params=_cp(),
    )(y, target)


ADA_COL_TILE = 768


def ada_fwd(c_all, w, b):
    L, D, N = w.shape
    nb = c_all.shape[0]
    tn = ADA_COL_TILE

    def body(c_ref, w_ref, b_ref, o_ref):
        o_ref[...] = _dot_hi(_silu(c_ref[...]), w_ref[...]) + b_ref[...]

    return pl.pallas_call(
        body, name="ada_fwd", grid=(L, N // tn),
        in_specs=[pl.BlockSpec((nb, D), lambda l, j: (0, 0)), pl.BlockSpec((None, D, tn), lambda l, j: (l, 0, j)),
                  pl.BlockSpec((None, 1, tn), lambda l, j: (l, 0, j))],
        out_specs=pl.BlockSpec((None, nb, tn), lambda l, j: (l, 0, j)),
        out_shape=jax.ShapeDtypeStruct((L, nb, N), f32),
        compiler_params=_cp(),
    )(c_all, w, b)


def ada_bwd(c_all, dmod_my, dmod_all):
    L, nb, N = dmod_my.shape
    D = c_all.shape[1]
    NA = dmod_all.shape[2]
    tn = ADA_COL_TILE
    nj = N // tn
    ta = NA // nj

    def body(c_ref, dm_ref, da_ref, gw_ref, gb_ref):
        gw_ref[...] = lax.dot_general(_silu(c_ref[...]), dm_ref[...], (((0,), (0,)), ((), ())),
                                      preferred_element_type=f32, precision=lax.Precision.HIGHEST)
        gb_ref[...] = jnp.sum(da_ref[...], axis=0, keepdims=True)

    return pl.pallas_call(
        body, name="ada_bwd", grid=(L, nj),
        in_specs=[pl.BlockSpec((nb, D), lambda l, j: (0, 0)), pl.BlockSpec((None, nb, tn), lambda l, j: (l, 0, j)),
                  pl.BlockSpec((None, nb, ta), lambda l, j: (l, 0, j))],
        out_specs=[pl.BlockSpec((None, D, tn), lambda l, j: (l, 0, j)), pl.BlockSpec((None, 1, ta), lambda l, j: (l, 0, j))],
        out_shape=[jax.ShapeDtypeStruct((L, D, N), f32), jax.ShapeDtypeStruct((L, 1, NA), f32)],
        compiler_params=_cp(),
    )(c_all, dmod_my, dmod_all)


def adamw(w, m, v, gs, offs, tr):
    R, C = w.shape
    ng = len(gs)
    c1 = 1.0 - ADAM_B1 ** ADAM_STEP
    c2 = 1.0 - ADAM_B2 ** ADAM_STEP

    def body(*refs):
        w_ref, m_ref, v_ref = refs[:3]
        g_refs = refs[3:3 + ng]
        g_out, d_out, m_out, v_out = refs[3 + ng:]
        g = g_refs[0][...]
        for r in g_refs[1:]:
            g = g + r[...]
        mn = ADAM_B1 * m_ref[...] + (1.0 - ADAM_B1) * g
        vn = ADAM_B2 * v_ref[...] + (1.0 - ADAM_B2) * (g * g)
        g_out[...] = g
        m_out[...] = mn
        v_out[...] = vn
        d_out[...] = -ADAM_LR * ((mn / c1) / (jnp.sqrt(vn / c2) + ADAM_EPS) + ADAM_WD * w_ref[...])

    blk = pl.BlockSpec((tr, C), lambda i: (i, 0))
    gspec = lambda off: pl.BlockSpec((tr, C), lambda i: (i + off // tr, 0))
    sds = jax.ShapeDtypeStruct((R, C), f32)
    return pl.pallas_call(
        body, name="adamw", grid=(pl.cdiv(R, tr),), in_specs=[blk, blk, blk] + [gspec(o) for o in offs],
        out_specs=[blk] * 4, out_shape=[sds] * 4, compiler_params=_cp(),
    )(w, m, v, *gs)


def sum_leading(a, tr):
    k, R, C = a.shape

    def body(a_ref, o_ref):
        s = a_ref[0]
        for j in range(1, k):
            s = s + a_ref[j]
        o_ref[...] = s

    return pl.pallas_call(
        body, name="sum_leading", grid=(R // tr,), in_specs=[pl.BlockSpec((k, tr, C), lambda i: (0, i, 0))],
        out_specs=pl.BlockSpec((tr, C), lambda i: (i, 0)), out_shape=jax.ShapeDtypeStruct((R, C), a.dtype),
        compiler_params=_cp(),
    )(a)


ANY = pl.BlockSpec(memory_space=pl.ANY)
CHIP_FLIPS = ((1, 0), (0, 1), (1, 1))
DEV_FLIPS = tuple((fx, fy, fc) for fx in (0, 1) for fy in (0, 1) for fc in (0, 1))[1:]


def _flip(v, f):
    return 1 - v if f else v


def allgather8(v):
    R, C = v.shape
    n = len(DEV_FLIPS)

    def body(v_ref, o_ref, send_sems, recv_sems, local_sem):
        x, y, c = lax.axis_index("x"), lax.axis_index("y"), lax.axis_index("c")
        me = 4 * x + 2 * y + c
        mine = pltpu.make_async_copy(v_ref, o_ref.at[me], local_sem)
        mine.start()
        peers = [(_flip(x, fx), _flip(y, fy), _flip(c, fc)) for fx, fy, fc in DEV_FLIPS]

        def copy(k, slot):
            return pltpu.make_async_remote_copy(src_ref=v_ref, dst_ref=o_ref.at[slot], send_sem=send_sems.at[k],
                                                recv_sem=recv_sems.at[k], device_id=peers[k], device_id_type=MESH)

        for k in range(n):
            copy(k, me).start()
        for k, (px, py, pc) in enumerate(peers):
            copy(k, 4 * px + 2 * py + pc).wait_recv()
        for k in range(n):
            copy(k, me).wait_send()
        mine.wait()

    return pl.pallas_call(
        body, name="allgather8", in_specs=[ANY], out_specs=ANY, out_shape=jax.ShapeDtypeStruct((N_DEV, R, C), v.dtype),
        scratch_shapes=[pltpu.SemaphoreType.DMA((n,)), pltpu.SemaphoreType.DMA((n,)), pltpu.SemaphoreType.DMA],
    )(v)


def allgather4(v):
    R, C = v.shape
    n = len(CHIP_FLIPS)

    def body(v_ref, o_ref, send_sems, recv_sems, local_sem):
        x, y, c = lax.axis_index("x"), lax.axis_index("y"), lax.axis_index("c")
        me = 2 * x + y
        mine = pltpu.make_async_copy(v_ref, o_ref.at[me], local_sem)
        mine.start()
        peers = [(_flip(x, fx), _flip(y, fy), c) for fx, fy in CHIP_FLIPS]

        def copy(k, slot):
            return pltpu.make_async_remote_copy(src_ref=v_ref, dst_ref=o_ref.at[slot], send_sem=send_sems.at[k],
                                                recv_sem=recv_sems.at[k], device_id=peers[k], device_id_type=MESH)

        for k in range(n):
            copy(k, me).start()
        for k, (px, py, _) in enumerate(peers):
            copy(k, 2 * px + py).wait_recv()
        for k in range(n):
            copy(k, me).wait_send()
        mine.wait()

    return pl.pallas_call(
        body, name="allgather4", in_specs=[ANY], out_specs=ANY, out_shape=jax.ShapeDtypeStruct((N_CHIPS, R, C), v.dtype),
        scratch_shapes=[pltpu.SemaphoreType.DMA((n,)), pltpu.SemaphoreType.DMA((n,)), pltpu.SemaphoreType.DMA],
    )(v)


def exchange4(v):
    _, R, C = v.shape
    n = len(CHIP_FLIPS)

    def body(v_ref, o_ref, send_sems, recv_sems, local_sem):
        x, y, c = lax.axis_index("x"), lax.axis_index("y"), lax.axis_index("c")
        me = 2 * x + y
        mine = pltpu.make_async_copy(v_ref.at[me], o_ref.at[me], local_sem)
        mine.start()
        peers = [(_flip(x, fx), _flip(y, fy), c) for fx, fy in CHIP_FLIPS]

        def copy(k, src_slot, dst_slot):
            return pltpu.make_async_remote_copy(src_ref=v_ref.at[src_slot], dst_ref=o_ref.at[dst_slot],
                                                send_sem=send_sems.at[k], recv_sem=recv_sems.at[k],
                                                device_id=peers[k], device_id_type=MESH)

        for k, (px, py, _) in enumerate(peers):
            copy(k, 2 * px + py, me).start()
        for k, (px, py, _) in enumerate(peers):
            copy(k, me, 2 * px + py).wait_recv()
        for k, (px, py, _) in enumerate(peers):
            copy(k, 2 * px + py, me).wait_send()
        mine.wait()

    return pl.pallas_call(
        body, name="exchange4", in_specs=[ANY], out_specs=ANY, out_shape=jax.ShapeDtypeStruct(v.shape, v.dtype),
        scratch_shapes=[pltpu.SemaphoreType.DMA((n,)), pltpu.SemaphoreType.DMA((n,)), pltpu.SemaphoreType.DMA],
    )(v)


def swap_sibling(v):
    def body(v_ref, o_ref, send_sem, recv_sem):
        x, y, c = lax.axis_index("x"), lax.axis_index("y"), lax.axis_index("c")
        cp = pltpu.make_async_remote_copy(src_ref=v_ref, dst_ref=o_ref, send_sem=send_sem, recv_sem=recv_sem,
                                          device_id=(x, y, 1 - c), device_id_type=MESH)
        cp.start()
        cp.wait()

    return pl.pallas_call(
        body, name="swap_sibling", in_specs=[ANY], out_specs=ANY, out_shape=jax.ShapeDtypeStruct(v.shape, v.dtype),
        scratch_shapes=[pltpu.SemaphoreType.DMA, pltpu.SemaphoreType.DMA],
    )(v)


WEIGHTS = ('ada_w', 'ada_b', 'ln_g', 'ln_b', 'ffn_w_in', 'ffn_w_out', 'ev_w_in', 'ssd_conv_w', 'ssd_conv_b',
           'ssd_dt_bias', 'ssd_a_log', 'ssd_d', 'ssd_norm_g', 'pool_w', 'pool_scale', 'ev_w_out', 'od_w_in',
           'conf_dw_w', 'conf_dw_b', 'conf_ln_g', 'conf_ln_b', 'lru_conv_w', 'lru_conv_b', 'lru_wa', 'lru_ba',
           'lru_wx', 'lru_bx', 'lru_lambda', 'od_w_out')
BIG = ('ffn_w_in', 'ffn_w_out', 'ev_w_out', 'od_w_in', 'od_w_out', 'ev_w_in')
SMALL_SHARDED = ('ln_g', 'ln_b', 'ssd_conv_w', 'conf_dw_w', 'conf_dw_b', 'conf_ln_g', 'conf_ln_b', 'lru_conv_w',
                 'lru_conv_b', 'lru_ba', 'lru_bx', 'lru_lambda')
SMALL_REPLICATED = ('ssd_conv_b', 'ssd_dt_bias', 'ssd_a_log', 'ssd_d', 'ssd_norm_g', 'pool_w', 'pool_scale',
                    'lru_wa', 'lru_wx')
PACK_COLS = 1024
BIG_ROW_TILE = 256


def _pack(arrs, row_mult):
    flat = jnp.concatenate([a.reshape(-1) for a in arrs])
    rows = -(-flat.shape[0] // (PACK_COLS * row_mult)) * row_mult
    return jnp.pad(flat, (0, rows * PACK_COLS - flat.shape[0])).reshape(rows, PACK_COLS)


def _unpack(flat, shapes):
    out, off = [], 0
    for s in shapes:
        n = 1
        for d in s:
            n *= d
        out.append(flat[off:off + n].reshape(s))
        off += n
    return out


def _rows(shape):
    n = 1
    for d in shape:
        n *= d
    assert n % PACK_COLS == 0
    return n // PACK_COLS


def _unshard_last(g4):
    m = jnp.moveaxis(g4, 0, -2)
    return m.reshape(m.shape[:-2] + (m.shape[-2] * m.shape[-1],))


def _pad_rows(a, rows):
    return jnp.pad(a, ((0, rows - a.shape[0]),) + ((0, 0),) * (a.ndim - 1))


def _pad_lanes(a):
    return jnp.pad(a, ((0, 0), (0, LANES - a.shape[1])))


def kernel(x, c, ada_w, ada_b, ln_g, ln_b, ffn_w_in, ffn_w_out, ev_w_in, ssd_conv_w, ssd_conv_b, ssd_dt_bias, ssd_a_log, ssd_d, ssd_norm_g, pool_w, pool_scale, ev_w_out, od_w_in, conf_dw_w, conf_dw_b, conf_ln_g, conf_ln_b, lru_conv_w, lru_conv_b, lru_wa, lru_ba, lru_wx, lru_bx, lru_lambda, od_w_out, loss_target, m_ada_w, m_ada_b, m_ln_g, m_ln_b, m_ffn_w_in, m_ffn_w_out, m_ev_w_in, m_ssd_conv_w, m_ssd_conv_b, m_ssd_dt_bias, m_ssd_a_log, m_ssd_d, m_ssd_norm_g, m_pool_w, m_pool_scale, m_ev_w_out, m_od_w_in, m_conf_dw_w, m_conf_dw_b, m_conf_ln_g, m_conf_ln_b, m_lru_conv_w, m_lru_conv_b, m_lru_wa, m_lru_ba, m_lru_wx, m_lru_bx, m_lru_lambda, m_od_w_out, v_ada_w, v_ada_b, v_ln_g, v_ln_b, v_ffn_w_in, v_ffn_w_out, v_ev_w_in, v_ssd_conv_w, v_ssd_conv_b, v_ssd_dt_bias, v_ssd_a_log, v_ssd_d, v_ssd_norm_g, v_pool_w, v_pool_scale, v_ev_w_out, v_od_w_in, v_conf_dw_w, v_conf_dw_b, v_conf_ln_g, v_conf_ln_b, v_lru_conv_w, v_lru_conv_b, v_lru_wa, v_lru_ba, v_lru_wx, v_lru_bx, v_lru_lambda, v_od_w_out):
    given = dict(locals())
    W = {n: given[n] for n in WEIGHTS}
    M = {n: given["m_" + n] for n in WEIGHTS}
    V = {n: given["v_" + n] for n in WEIGHTS}
    B, T, D = x.shape
    L = DEPTH
    chip = 2 * lax.axis_index("x") + lax.axis_index("y")
    dev = 2 * chip + lax.axis_index("c")

    g1 = allgather8(_pack([c] + [W[n] for n in SMALL_SHARDED], 8)).reshape(N_DEV, -1)
    c_all = g1[:, :B * D].reshape(N_DEV * B, D)
    per_chip = g1[0::2, B * D:]
    full = dict(zip(SMALL_SHARDED, [_unshard_last(jnp.stack(p)) for p in zip(*[
        _unpack(per_chip[k], [W[n].shape for n in SMALL_SHARDED]) for k in range(N_CHIPS)])]))
    for n in SMALL_REPLICATED:
        full[n] = W[n]

    n_ada = ada_w.shape[2]
    ada_b_cols = lax.dynamic_slice_in_dim(ada_b, chip * n_ada, n_ada, axis=1)[:, None, :]
    mod_cols = ada_fwd(c_all, ada_w, ada_b_cols)
    g2 = allgather8(mod_cols.reshape(-1, PACK_COLS))[0::2].reshape(N_CHIPS, L, N_DEV * B, n_ada)
    mod_all = jnp.moveaxis(g2, 0, 2).reshape(L, N_DEV * B, N_CHIPS * n_ada)
    mod = lax.dynamic_slice_in_dim(mod_all, dev * B, B, axis=1).reshape(L, B, N_MOD, D)

    big_rows = {n: _rows(W[n].shape) for n in BIG}
    big_rows_padded = dict(big_rows)
    big_rows_padded['ev_w_in'] = -(-big_rows['ev_w_in'] // BIG_ROW_TILE) * BIG_ROW_TILE
    big_off, off = {}, 0
    for n in BIG:
        big_off[n] = off
        off += big_rows_padded[n]
    total_rows = off
    wpack = jnp.concatenate([_pad_rows(W[n].astype(bf16).reshape(-1, PACK_COLS), big_rows_padded[n]) for n in BIG])
    wg = allgather4(wpack)
    gathered = {n: wg[:, big_off[n]:big_off[n] + big_rows[n]].reshape((N_CHIPS,) + W[n].shape) for n in BIG}
    FS = ffn_w_in.shape[3]

    def even_w_in(e):
        w = _unshard_last(gathered['ev_w_in'][:, e])
        return jnp.concatenate([w[:, :2560], w[:, 2576:], w[:, 2560:2576], jnp.zeros((D, EVEN_IN_PAD - EVEN_IN), bf16)], axis=1)

    saved = []
    xs = x
    for l in range(L):
        lg, lb = full['ln_g'][l], full['ln_b'][l]
        rec = {}
        w_in_a, w_out_a = gathered['ffn_w_in'][:, l, 0], gathered['ffn_w_out'][:, l, 0].reshape(-1, D)
        w_in_b, w_out_b = gathered['ffn_w_in'][:, l, 1], gathered['ffn_w_out'][:, l, 1].reshape(-1, D)
        m1, m2, m3 = mod[l][:, 0:3], mod[l][:, 3:6], mod[l][:, 6:9]
        xn, h, gu, a, y = ffn_fwd(xs, m1, w_in_a, w_out_a, lg[0:1], lb[0:1])
        rec['ffa'] = (xs, h, gu, a, y, m1, w_in_a, w_out_a, lg[0:1])
        xs = xn
        if l % 2 == 0:
            e = l // 2
            w_in_m, w_out_m = even_w_in(e), gathered['ev_w_out'][:, e].reshape(-1, D)
            cw = _pad_rows(full['ssd_conv_w'][e], 8)
            cb = full['ssd_conv_b'][e][None]
            dtb, alog, dsk = (_pad_lanes(full[n][e][None]) for n in ('ssd_dt_bias', 'ssd_a_log', 'ssd_d'))
            ng, pw, ps = full['ssd_norm_g'][e][None], full['pool_w'][e], full['pool_scale'][e][None]
            proj, hm = inproj_fwd(xs, m2, w_in_m)
            cx = dwconv_fwd(proj, 2, SSD_CONV_DIM, cw, cb, SSD_CONV)
            ya, hsave = ssd_fwd(proj, cx, dtb, alog, dsk, ng)
            yb = pool_fwd(proj, 5, pw, ps)
            ycat = jnp.concatenate([ya, yb], axis=-1)
            rec['mix'] = (proj, cx, hsave, cw, dtb, alog, dsk, ng, pw, ps)
        else:
            o = l // 2
            w_in_m, w_out_m = jnp.moveaxis(gathered['od_w_in'][:, o], 0, 1).reshape(D, -1), gathered['od_w_out'][:, o].reshape(-1, D)
            dww = _pad_rows(full['conf_dw_w'][o], 32)
            dwb = full['conf_dw_b'][o][None]
            cw = _pad_rows(full['lru_conv_w'][o], 8)
            cb = full['lru_conv_b'][o][None]
            cln = jnp.stack([full['conf_ln_g'][o], full['conf_ln_b'][o]])
            vec = jnp.stack([full['lru_ba'][o], full['lru_bx'][o], full['lru_lambda'][o]])
            wa, wx = full['lru_wa'][o], full['lru_wx'][o]
            proj, hm = inproj_fwd(xs, m2, w_in_m)
            hh = glu_fwd(proj)
            cc = dwconv_fwd(hh, 0, CONF_DIM, dww, dwb, CONF_KERNEL)
            xc = dwconv_fwd(proj, 2, LRU_DIM, cw, cb, LRU_CONV)
            ycat, hst = lru_fwd(cc, xc, proj, cln, wa, wx, vec)
            rec['mix'] = (proj, hh, cc, xc, hst, dww, cw, cln, wa, wx, vec)
        xn, ym = outproj_fwd(ycat, w_out_m, xs, m2, lg[1:2], lb[1:2])
        rec['mixio'] = (xs, hm, ycat, ym, m2, w_in_m, w_out_m, lg[1:2])
        xs = xn
        xn, h, gu, a, y = ffn_fwd(xs, m3, w_in_b, w_out_b, lg[2:3], lb[2:3])
        rec['ffb'] = (xs, h, gu, a, y, m3, w_in_b, w_out_b, lg[2:3])
        xs = xn
        saved.append(rec)

    sq, dxs = loss_fwd(xs, loss_target)
    loss = lax.psum(sq[0, 0], ("x", "y", "c")) * (0.5 / D)

    gpart = {n: [None] * W[n].shape[0] for n in WEIGHTS}
    gpart['ffn_w_in'] = [[None, None] for _ in range(L)]
    gpart['ffn_w_out'] = [[None, None] for _ in range(L)]
    gpart['ln_g'] = [[None] * 3 for _ in range(L)]
    gpart['ln_b'] = [[None] * 3 for _ in range(L)]
    dmod = [None] * L

    def ffn_back(dxn, rec, l, f):
        xin, h, gu, a, y, m3_, w_in_, w_out_, lg_ = rec
        dx, dgu, dy, dm3, dln = ffn_bwd(dxn, xin, y, gu, m3_, w_in_, w_out_, lg_)
        gpart['ffn_w_in'][l][f] = wgrad(h, dgu, FS)
        gpart['ffn_w_out'][l][f] = wgrad(a, dy, D)[0].reshape(N_CHIPS, -1, D)
        gpart['ln_g'][l][2 * f] = dln[0]
        gpart['ln_b'][l][2 * f] = dln[1]
        return dx, dm3

    for l in reversed(range(L)):
        rec = saved[l]
        dxs, dm3 = ffn_back(dxs, rec['ffb'], l, 1)
        xin, hm, ycat, ym, m2, w_in_m, w_out_m, lg_ = rec['mixio']
        dxp, dycat, dy, dg2, dln = outproj_bwd(dxs, xin, ym, m2, w_out_m, lg_)
        gpart['ln_g'][l][1] = dln[0]
        gpart['ln_b'][l][1] = dln[1]
        gw_out = wgrad(ycat, dy, D)[0].reshape(N_CHIPS, -1, D)
        if l % 2 == 0:
            e = l // 2
            proj, cx, hsave, cw, dtb, alog, dsk, ng, pw, ps = rec['mix']
            dz, dcx, ddt, gsm, gng = ssd_bwd(dycat, proj, cx, hsave, dtb, alog, dsk, ng)
            dxbc, dcw, dcb = dwconv_bwd(dcx, proj, 2, SSD_CONV_DIM, cw, SSD_CONV, bf16)
            du, dpw, dps = pool_bwd(dycat, 2, proj, 5, pw, ps)
            dproj = jnp.concatenate([dz, dxbc, du, ddt], axis=-1)
            gwp = wgrad(hm, dproj, EVEN_IN_PAD)[0]
            gw = jnp.concatenate([gwp[:, :2560], gwp[:, 3072:3072 + 16], gwp[:, 2560:3072]], axis=1)
            gpart['ev_w_in'][e] = jnp.moveaxis(gw.reshape(D, N_CHIPS, -1), 1, 0)
            gpart['ev_w_out'][e] = gw_out
            gpart['ssd_conv_w'][e], gpart['ssd_conv_b'][e] = dcw[:SSD_CONV], dcb[0]
            gpart['ssd_dt_bias'][e], gpart['ssd_a_log'][e], gpart['ssd_d'][e] = (gsm[k, :SSD_HEADS] for k in range(3))
            gpart['ssd_norm_g'][e], gpart['pool_w'][e], gpart['pool_scale'][e] = gng[0], dpw, dps[0]
        else:
            o = l // 2
            proj, hh, cc, xc, hst, dww, cw, cln, wa, wx, vec = rec['mix']
            dcc, dxc, dgr, dcln, dwa, dwx, dvec = lru_bwd(dycat, cc, xc, proj, hst, cln, wa, wx, vec)
            dhh, ddw, ddb = dwconv_bwd(dcc, hh, 0, CONF_DIM, dww, CONF_KERNEL, f32)
            dvg = glu_bwd(dhh, proj)
            dxr, dcw, dcb = dwconv_bwd(dxc, proj, 2, LRU_DIM, cw, LRU_CONV, bf16)
            dproj = jnp.concatenate([dvg, dxr, dgr], axis=-1)
            gpart['od_w_in'][o] = wgrad(hm, dproj, dproj.shape[-1] // N_CHIPS)
            gpart['od_w_out'][o] = gw_out
            gpart['conf_dw_w'][o], gpart['conf_dw_b'][o] = ddw[:CONF_KERNEL], ddb[0]
            gpart['conf_ln_g'][o], gpart['conf_ln_b'][o] = dcln[0], dcln[1]
            gpart['lru_conv_w'][o], gpart['lru_conv_b'][o] = dcw[:LRU_CONV], dcb[0]
            gpart['lru_wa'][o], gpart['lru_wx'][o] = dwa, dwx
            gpart['lru_ba'][o], gpart['lru_bx'][o], gpart['lru_lambda'][o] = dvec[0], dvec[1], dvec[2]
        dxs, dm2 = inproj_bwd(dproj, w_in_m, xin, m2, dxp)
        dxs, dm1 = ffn_back(dxs, rec['ffa'], l, 0)
        dmod[l] = jnp.concatenate([dm1, dm2, dg2, dm3], axis=1)
    grad_x = dxs

    def stack(v):
        return jnp.stack([stack(u) if isinstance(u, list) else u for u in v])

    def chip_major(n):
        g = stack(gpart[n])
        lead = W[n].ndim - 2
        return _pad_rows(jnp.moveaxis(g, lead, 0).reshape(N_CHIPS, -1, PACK_COLS).swapaxes(0, 1), big_rows_padded[n]).swapaxes(0, 1)

    gsend = jnp.concatenate([chip_major(n) for n in BIG], axis=1)
    s_mine = sum_leading(exchange4(gsend), BIG_ROW_TILE)
    s_other = swap_sibling(s_mine)

    out_g, out_d, out_m, out_v = {}, {}, {}, {}
    for n in BIG:
        shp = W[n].shape
        if big_rows[n] % BIG_ROW_TILE == 0:
            cols, gs, offs = PACK_COLS, [s_mine, s_other], [big_off[n], big_off[n]]
        else:
            cols, offs = shp[-1], [0, 0]
            gs = [s[big_off[n]:big_off[n] + big_rows[n]].reshape(-1, cols) for s in (s_mine, s_other)]
        res = adamw(W[n].reshape(-1, cols), M[n].reshape(-1, cols), V[n].reshape(-1, cols), gs, offs, BIG_ROW_TILE)
        out_g[n], out_d[n], out_m[n], out_v[n] = (r.reshape(shp) for r in res)

    small = SMALL_SHARDED + SMALL_REPLICATED
    dmod_flat = stack(dmod).reshape(L, B, N_MOD * D)
    g3 = allgather8(_pack([dmod_flat] + [stack(gpart[n]) for n in small], 64))
    n_dmod = L * B * N_MOD * D
    dmod_all = jnp.moveaxis(g3.reshape(N_DEV, -1)[:, :n_dmod].reshape(N_DEV, L, B, N_MOD * D), 0, 1).reshape(L, N_DEV * B, N_MOD * D)
    ssum = sum_leading(g3, 64).reshape(-1)[n_dmod:]
    gsmall = dict(zip(small, _unpack(ssum, [full[n].shape for n in small])))
    for n in SMALL_SHARDED:
        wdt = W[n].shape[-1]
        gsmall[n] = lax.dynamic_slice_in_dim(gsmall[n], chip * wdt, wdt, axis=gsmall[n].ndim - 1)
    dmod_my = lax.dynamic_slice_in_dim(dmod_all, chip * n_ada, n_ada, axis=2)
    g_ada_w, g_ada_b = ada_bwd(c_all, dmod_my, dmod_all)
    gsmall['ada_b'] = g_ada_b[:, 0, :]

    res = adamw(ada_w.reshape(-1, n_ada), M['ada_w'].reshape(-1, n_ada), V['ada_w'].reshape(-1, n_ada),
                [g_ada_w.reshape(-1, n_ada)], [0], BIG_ROW_TILE)
    out_g['ada_w'], out_d['ada_w'], out_m['ada_w'], out_v['ada_w'] = (r.reshape(ada_w.shape) for r in res)

    names = ('ada_b',) + small
    shapes = [W[n].shape for n in names]
    res = adamw(_pack([W[n] for n in names], 64), _pack([M[n] for n in names], 64), _pack([V[n] for n in names], 64),
                [_pack([gsmall[n] for n in names], 64)], [0], 64)
    for dst, r in zip((out_g, out_d, out_m, out_v), res):
        dst.update(zip(names, _unpack(r.reshape(-1), shapes)))

    return (loss, grad_x, *[out_g[n] for n in WEIGHTS], *[out_d[n] for n in WEIGHTS], *[out_m[n] for n in WEIGHTS],
            *[out_v[n] for n in WEIGHTS])
```

```python
import jax
import jax.numpy as jnp
from jax import lax
from jax.experimental import pallas as pl
from jax.experimental.pallas import tpu as pltpu

f32 = jnp.float32
bf16 = jnp.bfloat16

DEPTH = 4
D_MODEL = 1024
N_MOD = 9
DN_ALPHA = (2.0 * DEPTH) ** 0.25
NORM_EPS = 1e-5
SSD_CHUNK = 128
SSD_D_INNER = 1024
SSD_CONV_DIM = 1536
SSD_HEADS = 16
POOL_WINDOWS = (2, 4, 8, 16)
POOL_DIM = 512
EVEN_IN = 3088
EVEN_IN_PAD = 3200
CONF_DIM = 512
CONF_KERNEL = 31
LRU_DIM = 1024
LRU_HEADS = 8
LRU_CONV = 4
SSD_CONV = 4
LRU_C = 8.0
ADAM_LR = 0.001
ADAM_B1 = 0.9
ADAM_B2 = 0.999
ADAM_EPS = 1e-08
ADAM_WD = 0.01
ADAM_STEP = 10

LANES = 128
VMEM_LIMIT_BYTES = 56 * 2 ** 20
COL_TILE = 512
N_CHIPS = 4
N_DEV = 8
MESH = pl.DeviceIdType.MESH


def _cp():
    return pltpu.CompilerParams(vmem_limit_bytes=VMEM_LIMIT_BYTES)


def _dot(a, b):
    return jnp.dot(a, b, preferred_element_type=f32)


def _dot_nt(a, b):
    return lax.dot_general(a, b, (((1,), (1,)), ((), ())), preferred_element_type=f32)


def _dot_tn(a, b):
    return lax.dot_general(a, b, (((0,), (0,)), ((), ())), preferred_element_type=f32)


def _dot_hi(a, b):
    return jnp.dot(a, b, preferred_element_type=f32, precision=lax.Precision.HIGHEST)


def _silu(x):
    return x * jax.nn.sigmoid(x)


def _ln_stats(z):
    mu = jnp.mean(z, axis=-1, keepdims=True)
    zc = z - mu
    var = jnp.mean(zc * zc, axis=-1, keepdims=True)
    rstd = lax.rsqrt(var + NORM_EPS)
    return zc * rstd, rstd


def _ln_bwd(dxn, xhat, rstd, lg):
    dxh = dxn * lg
    return rstd * (dxh - jnp.mean(dxh, axis=-1, keepdims=True) - xhat * jnp.mean(dxh * xhat, axis=-1, keepdims=True))


def _const_spec(shape):
    nd = len(shape)
    return pl.BlockSpec(shape, lambda *_: (0,) * nd, pipeline_mode=pl.Buffered(1))


def _row_tile(t, want=256):
    return min(want, t)


def ffn_fwd(x, mod3, w_in, w_out, lg, lb):
    B, T, D = x.shape
    FS = w_in.shape[2]
    tm = _row_tile(T)

    def body(x_ref, mod_ref, win_ref, wout_ref, lg_ref, lb_ref, xn_ref, h_ref, gu_ref, a_ref, y_ref):
        xv = x_ref[...]
        sh, sc, g = mod_ref[0:1, :], mod_ref[1:2, :], mod_ref[2:3, :]
        h = (xv * (1.0 + sc) + sh).astype(bf16)
        h_ref[...] = h
        acc = jnp.zeros((tm, D), f32)
        for s in range(2):
            gate = _dot(h, win_ref[s])
            up = _dot(h, win_ref[s + 2])
            gu_ref[:, s * FS:(s + 1) * FS] = gate.astype(bf16)
            gu_ref[:, (s + 2) * FS:(s + 3) * FS] = up.astype(bf16)
            a = (_silu(gate) * up).astype(bf16)
            a_ref[:, s * FS:(s + 1) * FS] = a
            acc = acc + _dot(a, wout_ref[s * FS:(s + 1) * FS, :])
        y_ref[...] = acc
        xhat, _ = _ln_stats(DN_ALPHA * xv + 0.5 * (1.0 + g) * acc)
        xn_ref[...] = xhat * lg_ref[...] + lb_ref[...]

    row = lambda w: pl.BlockSpec((None, tm, w), lambda b, i: (b, i, 0))
    return pl.pallas_call(
        body, name="ffn_fwd", grid=(B, T // tm),
        in_specs=[row(D), pl.BlockSpec((None, 3, D), lambda b, i: (b, 0, 0)), _const_spec(w_in.shape),
                  _const_spec(w_out.shape), _const_spec((1, D)), _const_spec((1, D))],
        out_specs=[row(D), row(D), row(4 * FS), row(2 * FS), row(D)],
        out_shape=[jax.ShapeDtypeStruct((B, T, D), f32), jax.ShapeDtypeStruct((B, T, D), bf16),
                   jax.ShapeDtypeStruct((B, T, 4 * FS), bf16), jax.ShapeDtypeStruct((B, T, 2 * FS), bf16),
                   jax.ShapeDtypeStruct((B, T, D), f32)],
        compiler_params=_cp(),
    )(x, mod3, w_in, w_out, lg, lb)


def ffn_bwd(dxn, x, y, gu, mod3, w_in, w_out, lg):
    B, T, D = x.shape
    FS = w_in.shape[2]
    tm = _row_tile(T)

    def body(dxn_ref, x_ref, y_ref, gu_ref, mod_ref, win_ref, wout_ref, lg_ref,
             dx_ref, dgu_ref, dy_ref, dmod_ref, dln_ref):
        b, i = pl.program_id(0), pl.program_id(1)

        @pl.when((b == 0) & (i == 0))
        def _():
            dln_ref[...] = jnp.zeros_like(dln_ref)

        @pl.when(i == 0)
        def _():
            dmod_ref[...] = jnp.zeros_like(dmod_ref)

        xv, yv, dxn_v = x_ref[...], y_ref[...], dxn_ref[...]
        sc, g = mod_ref[1:2, :], mod_ref[2:3, :]
        xhat, rstd = _ln_stats(DN_ALPHA * xv + 0.5 * (1.0 + g) * yv)
        dln_ref[0:1, :] += jnp.sum(dxn_v * xhat, axis=0, keepdims=True)
        dln_ref[1:2, :] += jnp.sum(dxn_v, axis=0, keepdims=True)
        dz = _ln_bwd(dxn_v, xhat, rstd, lg_ref[...])
        dmod_ref[2:3, :] += jnp.sum(0.5 * dz * yv, axis=0, keepdims=True)
        dy = (0.5 * (1.0 + g) * dz).astype(bf16)
        dy_ref[...] = dy
        dh = jnp.zeros((tm, D), f32)
        for s in range(2):
            da = _dot_nt(dy, wout_ref[s * FS:(s + 1) * FS, :])
            gate = gu_ref[:, s * FS:(s + 1) * FS].astype(f32)
            up = gu_ref[:, (s + 2) * FS:(s + 3) * FS].astype(f32)
            sig = jax.nn.sigmoid(gate)
            dgate = (da * up * (sig * (1.0 + gate * (1.0 - sig)))).astype(bf16)
            dup = (da * gate * sig).astype(bf16)
            dgu_ref[:, s * FS:(s + 1) * FS] = dgate
            dgu_ref[:, (s + 2) * FS:(s + 3) * FS] = dup
            dh = dh + _dot_nt(dgate, win_ref[s]) + _dot_nt(dup, win_ref[s + 2])
        dx_ref[...] = DN_ALPHA * dz + dh * (1.0 + sc)
        dmod_ref[0:1, :] += jnp.sum(dh, axis=0, keepdims=True)
        dmod_ref[1:2, :] += jnp.sum(dh * xv, axis=0, keepdims=True)

    row = lambda w: pl.BlockSpec((None, tm, w), lambda b, i: (b, i, 0))
    return pl.pallas_call(
        body, name="ffn_bwd", grid=(B, T // tm),
        in_specs=[row(D), row(D), row(D), row(4 * FS), pl.BlockSpec((None, 3, D), lambda b, i: (b, 0, 0)),
                  _const_spec(w_in.shape), _const_spec(w_out.shape), _const_spec((1, D))],
        out_specs=[row(D), row(4 * FS), row(D), pl.BlockSpec((None, 3, D), lambda b, i: (b, 0, 0)),
                   pl.BlockSpec((2, D), lambda b, i: (0, 0))],
        out_shape=[jax.ShapeDtypeStruct((B, T, D), f32), jax.ShapeDtypeStruct((B, T, 4 * FS), bf16),
                   jax.ShapeDtypeStruct((B, T, D), bf16), jax.ShapeDtypeStruct((B, 3, D), f32),
                   jax.ShapeDtypeStruct((2, D), f32)],
        compiler_params=_cp(),
    )(dxn, x, y, gu, mod3, w_in, w_out, lg)


def wgrad(a, b, tn):
    B, T, K = a.shape
    N = b.shape[2]
    tr = _row_tile(T, 512)

    def body(a_ref, b_ref, o_ref):
        @pl.when((pl.program_id(1) == 0) & (pl.program_id(2) == 0))
        def _():
            o_ref[...] = jnp.zeros_like(o_ref)

        o_ref[...] += _dot_tn(a_ref[...], b_ref[...])

    return pl.pallas_call(
        body, name="wgrad", grid=(N // tn, B, T // tr),
        in_specs=[pl.BlockSpec((None, tr, K), lambda s, b, r: (b, r, 0)),
                  pl.BlockSpec((None, tr, tn), lambda s, b, r: (b, r, s))],
        out_specs=pl.BlockSpec((None, K, tn), lambda s, b, r: (s, 0, 0)),
        out_shape=jax.ShapeDtypeStruct((N // tn, K, tn), f32),
        compiler_params=_cp(),
    )(a, b)


def inproj_fwd(x, mod3, w):
    B, T, D = x.shape
    N = w.shape[1]
    tm = _row_tile(T)

    def body(x_ref, mod_ref, w_ref, p_ref, h_ref):
        h = (x_ref[...] * (1.0 + mod_ref[1:2, :]) + mod_ref[0:1, :]).astype(bf16)
        h_ref[...] = h
        p_ref[...] = _dot(h, w_ref[...])

    row = lambda n: pl.BlockSpec((None, tm, n), lambda b, i: (b, i, 0))
    return pl.pallas_call(
        body, name="inproj_fwd", grid=(B, T // tm),
        in_specs=[row(D), pl.BlockSpec((None, 3, D), lambda b, i: (b, 0, 0)), _const_spec(w.shape)],
        out_specs=[row(N), row(D)],
        out_shape=[jax.ShapeDtypeStruct((B, T, N), f32), jax.ShapeDtypeStruct((B, T, D), bf16)],
        compiler_params=_cp(),
    )(x, mod3, w)


def inproj_bwd(dproj, w, x, mod3, dxp):
    B, T, D = x.shape
    N = w.shape[1]
    tm = _row_tile(T)

    def body(dp_ref, w_ref, x_ref, mod_ref, dxp_ref, dx_ref, dmod_ref):
        @pl.when(pl.program_id(1) == 0)
        def _():
            dmod_ref[...] = jnp.zeros_like(dmod_ref)

        dh = _dot_nt(dp_ref[...], w_ref[...])
        dx_ref[...] = dxp_ref[...] + dh * (1.0 + mod_ref[1:2, :])
        dmod_ref[0:1, :] += jnp.sum(dh, axis=0, keepdims=True)
        dmod_ref[1:2, :] += jnp.sum(dh * x_ref[...], axis=0, keepdims=True)

    row = lambda n: pl.BlockSpec((None, tm, n), lambda b, i: (b, i, 0))
    return pl.pallas_call(
        body, name="inproj_bwd", grid=(B, T // tm),
        in_specs=[row(N), _const_spec(w.shape), row(D), pl.BlockSpec((None, 3, D), lambda b, i: (b, 0, 0)), row(D)],
        out_specs=[row(D), pl.BlockSpec((None, 2, D), lambda b, i: (b, 0, 0))],
        out_shape=[jax.ShapeDtypeStruct((B, T, D), f32), jax.ShapeDtypeStruct((B, 2, D), f32)],
        compiler_params=_cp(),
    )(dproj, w, x, mod3, dxp)


def outproj_fwd(ycat, w, x, mod3, lg, lb):
    B, T, D = x.shape
    E = w.shape[0]
    tm = _row_tile(T)

    def body(yc_ref, w_ref, x_ref, mod_ref, lg_ref, lb_ref, xn_ref, y_ref):
        yv = _dot(yc_ref[...], w_ref[...])
        y_ref[...] = yv
        xhat, _ = _ln_stats(DN_ALPHA * x_ref[...] + (1.0 + mod_ref[2:3, :]) * yv)
        xn_ref[...] = xhat * lg_ref[...] + lb_ref[...]

    row = lambda n: pl.BlockSpec((None, tm, n), lambda b, i: (b, i, 0))
    return pl.pallas_call(
        body, name="outproj_fwd", grid=(B, T // tm),
        in_specs=[row(E), _const_spec(w.shape), row(D), pl.BlockSpec((None, 3, D), lambda b, i: (b, 0, 0)),
                  _const_spec((1, D)), _const_spec((1, D))],
        out_specs=[row(D), row(D)],
        out_shape=[jax.ShapeDtypeStruct((B, T, D), f32), jax.ShapeDtypeStruct((B, T, D), f32)],
        compiler_params=_cp(),
    )(ycat, w, x, mod3, lg, lb)


def outproj_bwd(dxn, x, y, mod3, w, lg):
    B, T, D = x.shape
    E = w.shape[0]
    tm = _row_tile(T)

    def body(dxn_ref, x_ref, y_ref, mod_ref, w_ref, lg_ref, dxp_ref, dyc_ref, dy_ref, dg_ref, dln_ref):
        b, i = pl.program_id(0), pl.program_id(1)

        @pl.when((b == 0) & (i == 0))
        def _():
            dln_ref[...] = jnp.zeros_like(dln_ref)

        @pl.when(i == 0)
        def _():
            dg_ref[...] = jnp.zeros_like(dg_ref)

        xv, yv, dxn_v = x_ref[...], y_ref[...], dxn_ref[...]
        g = mod_ref[2:3, :]
        xhat, rstd = _ln_stats(DN_ALPHA * xv + (1.0 + g) * yv)
        dln_ref[0:1, :] += jnp.sum(dxn_v * xhat, axis=0, keepdims=True)
        dln_ref[1:2, :] += jnp.sum(dxn_v, axis=0, keepdims=True)
        dz = _ln_bwd(dxn_v, xhat, rstd, lg_ref[...])
        dg_ref[...] += jnp.sum(dz * yv, axis=0, keepdims=True)
        dy = ((1.0 + g) * dz).astype(bf16)
        dy_ref[...] = dy
        dxp_ref[...] = DN_ALPHA * dz
        dyc_ref[...] = _dot_nt(dy, w_ref[...])

    row = lambda n: pl.BlockSpec((None, tm, n), lambda b, i: (b, i, 0))
    return pl.pallas_call(
        body, name="outproj_bwd", grid=(B, T // tm),
        in_specs=[row(D), row(D), row(D), pl.BlockSpec((None, 3, D), lambda b, i: (b, 0, 0)), _const_spec(w.shape),
                  _const_spec((1, D))],
        out_specs=[row(D), row(E), row(D), pl.BlockSpec((None, 1, D), lambda b, i: (b, 0, 0)),
                   pl.BlockSpec((2, D), lambda b, i: (0, 0))],
        out_shape=[jax.ShapeDtypeStruct((B, T, D), f32), jax.ShapeDtypeStruct((B, T, E), f32),
                   jax.ShapeDtypeStruct((B, T, D), bf16), jax.ShapeDtypeStruct((B, 1, D), f32),
                   jax.ShapeDtypeStruct((2, D), f32)],
        compiler_params=_cp(),
    )(dxn, x, y, mod3, w, lg)


def _halo_rows(K):
    return 8 if K <= 9 else 32


def dwconv_fwd(x, col0, C, w, b, K):
    B, T, _ = x.shape
    tc, hp = COL_TILE, _halo_rows(K)
    tm = _row_tile(T)
    r = tm // hp

    def body(xh_ref, x_ref, w_ref, b_ref, o_ref):
        halo = jnp.where(pl.program_id(2) == 0, 0.0, xh_ref[...])
        xe = jnp.concatenate([halo, x_ref[...]], axis=0)
        acc = jnp.zeros((tm, tc), f32) + b_ref[...]
        for k in range(K):
            sft = K - 1 - k
            xs = xe if sft == 0 else pltpu.roll(xe, sft, 0)
            acc = acc + xs[hp:, :] * w_ref[k:k + 1, :]
        o_ref[...] = acc

    return pl.pallas_call(
        body, name=f"dwconv{K}_fwd", grid=(C // tc, B, T // tm),
        in_specs=[pl.BlockSpec((None, hp, tc), lambda j, b, i: (b, jnp.maximum(i * r - 1, 0), col0 + j)),
                  pl.BlockSpec((None, tm, tc), lambda j, b, i: (b, i, col0 + j)),
                  pl.BlockSpec((w.shape[0], tc), lambda j, b, i: (0, j)),
                  pl.BlockSpec((1, tc), lambda j, b, i: (0, j))],
        out_specs=pl.BlockSpec((None, tm, tc), lambda j, b, i: (b, i, j)),
        out_shape=jax.ShapeDtypeStruct((B, T, C), f32),
        compiler_params=_cp(),
    )(x, x, w, b)


def dwconv_bwd(dc, x, col0, C, w, K, out_dtype):
    B, T, _ = x.shape
    tc, hp = COL_TILE, _halo_rows(K)
    tm = _row_tile(T)
    r = tm // hp
    nt = T // tm
    n = tm + hp
    KP = w.shape[0]

    def body(dcn_ref, dc_ref, xh_ref, x_ref, w_ref, dx_ref, dw_ref, db_ref):
        b, i = pl.program_id(1), pl.program_id(2)

        @pl.when((b == 0) & (i == 0))
        def _():
            dw_ref[...] = jnp.zeros_like(dw_ref)
            db_ref[...] = jnp.zeros_like(db_ref)

        dcv = dc_ref[...]
        de = jnp.concatenate([dcv, jnp.where(i == nt - 1, 0.0, dcn_ref[...])], axis=0)
        acc = jnp.zeros((tm, tc), f32)
        for k in range(K):
            j = K - 1 - k
            ds = de if j == 0 else pltpu.roll(de, n - j, 0)
            acc = acc + ds[:tm, :] * w_ref[k:k + 1, :]
        dx_ref[...] = acc.astype(out_dtype)
        xe = jnp.concatenate([jnp.where(i == 0, 0.0, xh_ref[...]), x_ref[...]], axis=0)
        for k in range(K):
            sft = K - 1 - k
            xs = xe if sft == 0 else pltpu.roll(xe, sft, 0)
            dw_ref[k:k + 1, :] += jnp.sum(dcv * xs[hp:, :], axis=0, keepdims=True)
        db_ref[...] += jnp.sum(dcv, axis=0, keepdims=True)

    return pl.pallas_call(
        body, name=f"dwconv{K}_bwd", grid=(C // tc, B, nt),
        in_specs=[pl.BlockSpec((None, hp, tc), lambda j, b, i: (b, jnp.minimum((i + 1) * r, T // hp - 1), j)),
                  pl.BlockSpec((None, tm, tc), lambda j, b, i: (b, i, j)),
                  pl.BlockSpec((None, hp, tc), lambda j, b, i: (b, jnp.maximum(i * r - 1, 0), col0 + j)),
                  pl.BlockSpec((None, tm, tc), lambda j, b, i: (b, i, col0 + j)),
                  pl.BlockSpec((KP, tc), lambda j, b, i: (0, j))],
        out_specs=[pl.BlockSpec((None, tm, tc), lambda j, b, i: (b, i, j)),
                   pl.BlockSpec((KP, tc), lambda j, b, i: (0, j)),
                   pl.BlockSpec((1, tc), lambda j, b, i: (0, j))],
        out_shape=[jax.ShapeDtypeStruct((B, T, C), out_dtype), jax.ShapeDtypeStruct((KP, C), f32),
                   jax.ShapeDtypeStruct((1, C), f32)],
        compiler_params=_cp(),
    )(dc, dc, x, x, w)


POOL_HALO = 16


def _pool_windows(ue, pos, hp):
    out = []
    for g, wd in enumerate(POOL_WINDOWS):
        ug = ue[:, g * LANES:(g + 1) * LANES]
        s, span = ug, 1
        while span < wd:
            s = s + pltpu.roll(s, span, 0)
            span *= 2
        cnt = jnp.minimum(pos + 1, wd).astype(f32)
        out.append(s[hp:, :] / cnt - ug[hp:, :])
    return out


def pool_fwd(proj, colb, w, scale):
    B, T, _ = proj.shape
    hp = POOL_HALO
    tm = _row_tile(T)
    r = tm // hp

    def body(uh_ref, u_ref, w_ref, sc_ref, o_ref):
        i = pl.program_id(1)
        ue = jnp.concatenate([jnp.where(i == 0, 0.0, uh_ref[...]), u_ref[...]], axis=0)
        pos = i * tm + lax.broadcasted_iota(jnp.int32, (tm, 1), 0)
        ps = _pool_windows(ue, pos, hp)
        o = jnp.concatenate([_dot(ps[g], w_ref[g]) for g in range(4)], axis=1) * sc_ref[...]
        o_ref[...] = o.astype(bf16)

    return pl.pallas_call(
        body, name="pool_fwd", grid=(B, T // tm),
        in_specs=[pl.BlockSpec((None, hp, POOL_DIM), lambda b, i: (b, jnp.maximum(i * r - 1, 0), colb)),
                  pl.BlockSpec((None, tm, POOL_DIM), lambda b, i: (b, i, colb)),
                  _const_spec(w.shape), _const_spec((1, POOL_DIM))],
        out_specs=pl.BlockSpec((None, tm, POOL_DIM), lambda b, i: (b, i, 0)),
        out_shape=jax.ShapeDtypeStruct((B, T, POOL_DIM), bf16),
        compiler_params=_cp(),
    )(proj, proj, w, scale)


def pool_bwd(dycat, dcolb, proj, colb, w, scale):
    B, T, _ = proj.shape
    hp = POOL_HALO
    tm = _row_tile(T)
    r = tm // hp
    nt = T // tm
    n = tm + hp

    def body(dyn_ref, dy_ref, uh_ref, u_ref, w_ref, sc_ref, du_ref, dw_ref, dsc_ref):
        b, i = pl.program_id(0), pl.program_id(1)

        @pl.when((b == 0) & (i == 0))
        def _():
            dw_ref[...] = jnp.zeros_like(dw_ref)
            dsc_ref[...] = jnp.zeros_like(dsc_ref)

        dyv = dy_ref[...]
        dye = jnp.concatenate([dyv, jnp.where(i == nt - 1, 0.0, dyn_ref[...])], axis=0)
        ue = jnp.concatenate([jnp.where(i == 0, 0.0, uh_ref[...]), u_ref[...]], axis=0)
        pos = i * tm + lax.broadcasted_iota(jnp.int32, (tm, 1), 0)
        pos_e = i * tm + lax.broadcasted_iota(jnp.int32, (n, 1), 0)
        ps = _pool_windows(ue, pos, hp)
        dme = dye * sc_ref[...]
        dus, dscs = [], []
        for g, wd in enumerate(POOL_WINDOWS):
            sl = slice(g * LANES, (g + 1) * LANES)
            dscs.append(jnp.sum(dyv[:, sl] * _dot(ps[g], w_ref[g]), axis=0, keepdims=True))
            dw_ref[g] += _dot_tn(ps[g], dme[:tm, sl])
            dpe = _dot_nt(dme[:, sl], w_ref[g])
            s, span = dpe / jnp.minimum(pos_e + 1, wd).astype(f32), 1
            while span < wd:
                s = s + pltpu.roll(s, n - span, 0)
                span *= 2
            dus.append(s[:tm, :] - dpe[:tm, :])
        du_ref[...] = jnp.concatenate(dus, axis=1).astype(bf16)
        dsc_ref[...] += jnp.concatenate(dscs, axis=1)

    return pl.pallas_call(
        body, name="pool_bwd", grid=(B, nt),
        in_specs=[pl.BlockSpec((None, hp, POOL_DIM), lambda b, i: (b, jnp.minimum((i + 1) * r, T // hp - 1), dcolb)),
                  pl.BlockSpec((None, tm, POOL_DIM), lambda b, i: (b, i, dcolb)),
                  pl.BlockSpec((None, hp, POOL_DIM), lambda b, i: (b, jnp.maximum(i * r - 1, 0), colb)),
                  pl.BlockSpec((None, tm, POOL_DIM), lambda b, i: (b, i, colb)),
                  _const_spec(w.shape), _const_spec((1, POOL_DIM))],
        out_specs=[pl.BlockSpec((None, tm, POOL_DIM), lambda b, i: (b, i, 0)),
                   pl.BlockSpec(w.shape, lambda b, i: (0, 0, 0)),
                   pl.BlockSpec((1, POOL_DIM), lambda b, i: (0, 0))],
        out_shape=[jax.ShapeDtypeStruct((B, T, POOL_DIM), bf16), jax.ShapeDtypeStruct(w.shape, f32),
                   jax.ShapeDtypeStruct((1, POOL_DIM), f32)],
        compiler_params=_cp(),
    )(dycat, dycat, proj, proj, w, scale)


N_PAIRS = SSD_HEADS // 2


def _ssd_chunk(xs, bs, cs, dtp, zs, hs, dtb, alog, dsk, ngs):
    Q = SSD_CHUNK
    lane = lax.broadcasted_iota(jnp.int32, (1, LANES), 1)
    sub = lax.broadcasted_iota(jnp.int32, (LANES, 1), 0)
    causal = lax.broadcasted_iota(jnp.int32, (Q, Q), 0) >= lax.broadcasted_iota(jnp.int32, (Q, Q), 1)
    lane_lo, sub_lo = lane < 64, sub < 64

    def col(v, h):
        return jnp.sum(v * (lane == h).astype(f32), axis=1, keepdims=True)

    def row(vt, h):
        return jnp.sum(vt * (sub == h).astype(f32), axis=0, keepdims=True)

    dt = jax.nn.softplus(dtp + dtb)
    acum = _dot_hi(causal.astype(f32), dt * (-jnp.exp(alog)))
    acum_t = acum.T
    aend = jnp.sum(acum * (sub == Q - 1).astype(f32), axis=0, keepdims=True)
    outs, hn = [], []
    for grp in range(2):
        bv, cv = _silu(bs[grp]), _silu(cs[grp])
        gmat = _dot_nt(cv, bv)
        for j in range(4):
            p = grp * 4 + j
            h0, h1 = 2 * p, 2 * p + 1
            x2 = _silu(xs[p])
            c0, c1 = col(acum, h0), col(acum, h1)
            s2 = jnp.where(lane_lo, c0, c1)
            xdt = x2 * jnp.where(lane_lo, col(dt, h0), col(dt, h1))
            l0 = jnp.where(causal, jnp.exp(jnp.minimum(c0 - row(acum_t, h0), 0.0)), 0.0)
            l1 = jnp.where(causal, jnp.exp(jnp.minimum(c1 - row(acum_t, h1), 0.0)), 0.0)
            yd = _dot(gmat * l0, jnp.where(lane_lo, xdt, 0.0)) + _dot(gmat * l1, jnp.where(lane_lo, 0.0, xdt))
            e0, e1 = col(aend, h0), col(aend, h1)
            st = _dot_tn(xdt * jnp.exp(jnp.where(lane_lo, e0, e1) - s2), bv)
            yo = jnp.exp(s2) * _dot_nt(cv, hs[p])
            hn.append(jnp.exp(jnp.where(sub_lo, e0, e1)) * hs[p] + st)
            yv = yd + yo + x2 * jnp.where(lane_lo, col(dsk, h0), col(dsk, h1))
            outs.append(yv * _silu(zs[p]))
    ms = sum(jnp.sum(o * o, axis=1, keepdims=True) for o in outs) / SSD_D_INNER
    rs = lax.rsqrt(ms + NORM_EPS)
    return [outs[p] * rs * ngs[p] for p in range(N_PAIRS)], hn


def _lane_blocks(ref, n, start=0):
    return [ref[:, (start + k) * LANES:(start + k + 1) * LANES] for k in range(n)]


def _ssd_args(z_ref, cx_ref, dt_ref, dtb_ref, alog_ref, dsk_ref, ng_ref):
    xs = _lane_blocks(cx_ref, 8)
    bs = _lane_blocks(cx_ref, 2, 8)
    cs = _lane_blocks(cx_ref, 2, 10)
    zs = _lane_blocks(z_ref, 8)
    ngs = _lane_blocks(ng_ref, 8)
    return xs, bs, cs, dt_ref[...], zs, dtb_ref[...], alog_ref[...], dsk_ref[...], ngs


DT_COLB = (EVEN_IN_PAD - LANES) // LANES


def ssd_fwd(proj, cx, dtb, alog, dsk, ng):
    B, T, _ = proj.shape
    Q = SSD_CHUNK
    nc = T // Q

    def body(z_ref, cx_ref, dt_ref, dtb_ref, alog_ref, dsk_ref, ng_ref, ya_ref, hsave_ref, h_scr):
        @pl.when(pl.program_id(1) == 0)
        def _():
            h_scr[...] = jnp.zeros_like(h_scr)

        xs, bs, cs, dtp, zs, dtb_v, alog_v, dsk_v, ngs = _ssd_args(z_ref, cx_ref, dt_ref, dtb_ref, alog_ref, dsk_ref, ng_ref)
        hs = [h_scr[p] for p in range(N_PAIRS)]
        for p in range(N_PAIRS):
            hsave_ref[p] = hs[p]
        outs, hn = _ssd_chunk(xs, bs, cs, dtp, zs, hs, dtb_v, alog_v, dsk_v, ngs)
        for p in range(N_PAIRS):
            ya_ref[:, p * LANES:(p + 1) * LANES] = outs[p].astype(bf16)
            h_scr[p] = hn[p]

    return pl.pallas_call(
        body, name="ssd_fwd", grid=(B, nc),
        in_specs=[pl.BlockSpec((None, Q, SSD_D_INNER), lambda b, i: (b, i, 0)),
                  pl.BlockSpec((None, Q, SSD_CONV_DIM), lambda b, i: (b, i, 0)),
                  pl.BlockSpec((None, Q, LANES), lambda b, i: (b, i, DT_COLB)),
                  _const_spec((1, LANES)), _const_spec((1, LANES)), _const_spec((1, LANES)),
                  _const_spec((1, SSD_D_INNER))],
        out_specs=[pl.BlockSpec((None, Q, SSD_D_INNER), lambda b, i: (b, i, 0)),
                   pl.BlockSpec((None, None, N_PAIRS, LANES, LANES), lambda b, i: (b, i, 0, 0, 0))],
        out_shape=[jax.ShapeDtypeStruct((B, T, SSD_D_INNER), bf16),
                   jax.ShapeDtypeStruct((B, nc, N_PAIRS, LANES, LANES), f32)],
        scratch_shapes=[pltpu.VMEM((N_PAIRS, LANES, LANES), f32)],
        compiler_params=_cp(),
    )(proj, cx, proj, dtb, alog, dsk, ng)


def ssd_bwd(dycat, proj, cx, hsave, dtb, alog, dsk, ng):
    B, T, _ = proj.shape
    Q = SSD_CHUNK
    nc = T // Q

    def body(dya_ref, z_ref, cx_ref, dt_ref, hsave_ref, dtb_ref, alog_ref, dsk_ref, ng_ref,
             dz_ref, dcx_ref, ddt_ref, gsm_ref, gng_ref, dh_scr):
        b, i = pl.program_id(0), pl.program_id(1)

        @pl.when((b == 0) & (i == 0))
        def _():
            gsm_ref[...] = jnp.zeros_like(gsm_ref)
            gng_ref[...] = jnp.zeros_like(gng_ref)

        @pl.when(i == 0)
        def _():
            dh_scr[...] = jnp.zeros_like(dh_scr)

        xs, bs, cs, dtp, zs, dtb_v, alog_v, dsk_v, ngs = _ssd_args(z_ref, cx_ref, dt_ref, dtb_ref, alog_ref, dsk_ref, ng_ref)
        hs = [hsave_ref[p] for p in range(N_PAIRS)]
        _, vjp = jax.vjp(_ssd_chunk, xs, bs, cs, dtp, zs, hs, dtb_v, alog_v, dsk_v, ngs)
        douts = _lane_blocks(dya_ref, 8)
        dhn = [dh_scr[p] for p in range(N_PAIRS)]
        dxs, dbs, dcs, ddtp, dzs, dhs, ddtb, dalog, ddsk, dngs = vjp((douts, dhn))
        for p in range(N_PAIRS):
            dcx_ref[:, p * LANES:(p + 1) * LANES] = dxs[p]
            dz_ref[:, p * LANES:(p + 1) * LANES] = dzs[p].astype(bf16)
            dh_scr[p] = dhs[p]
            gng_ref[:, p * LANES:(p + 1) * LANES] += dngs[p]
        for k in range(2):
            dcx_ref[:, (8 + k) * LANES:(9 + k) * LANES] = dbs[k]
            dcx_ref[:, (10 + k) * LANES:(11 + k) * LANES] = dcs[k]
        ddt_ref[...] = ddtp.astype(bf16)
        gsm_ref[0:1, :] += ddtb
        gsm_ref[1:2, :] += dalog
        gsm_ref[2:3, :] += ddsk

    rev = lambda w, cb=0: pl.BlockSpec((None, Q, w), lambda b, i: (b, nc - 1 - i, cb))
    return pl.pallas_call(
        body, name="ssd_bwd", grid=(B, nc),
        in_specs=[rev(SSD_D_INNER), rev(SSD_D_INNER), rev(SSD_CONV_DIM), rev(LANES, DT_COLB),
                  pl.BlockSpec((None, None, N_PAIRS, LANES, LANES), lambda b, i: (b, nc - 1 - i, 0, 0, 0)),
                  _const_spec((1, LANES)), _const_spec((1, LANES)), _const_spec((1, LANES)),
                  _const_spec((1, SSD_D_INNER))],
        out_specs=[rev(SSD_D_INNER), rev(SSD_CONV_DIM), rev(LANES),
                   pl.BlockSpec((3, LANES), lambda b, i: (0, 0)),
                   pl.BlockSpec((1, SSD_D_INNER), lambda b, i: (0, 0))],
        out_shape=[jax.ShapeDtypeStruct((B, T, SSD_D_INNER), bf16), jax.ShapeDtypeStruct((B, T, SSD_CONV_DIM), f32),
                   jax.ShapeDtypeStruct((B, T, LANES), bf16), jax.ShapeDtypeStruct((3, LANES), f32),
                   jax.ShapeDtypeStruct((1, SSD_D_INNER), f32)],
        scratch_shapes=[pltpu.VMEM((N_PAIRS, LANES, LANES), f32)],
        compiler_params=_cp(),
    )(dycat, proj, cx, proj, hsave, dtb, alog, dsk, ng)


def glu_fwd(proj):
    B, T, _ = proj.shape
    tm = _row_tile(T)

    def body(v_ref, g_ref, o_ref):
        o_ref[...] = v_ref[...] * jax.nn.sigmoid(g_ref[...])

    blk = lambda cb: pl.BlockSpec((None, tm, CONF_DIM), lambda b, i: (b, i, cb))
    return pl.pallas_call(body, name="glu_fwd", grid=(B, T // tm), in_specs=[blk(0), blk(1)], out_specs=blk(0),
                          out_shape=jax.ShapeDtypeStruct((B, T, CONF_DIM), f32), compiler_params=_cp())(proj, proj)


def glu_bwd(dhh, proj):
    B, T, _ = proj.shape
    tm = _row_tile(T)

    def body(d_ref, v_ref, g_ref, o_ref):
        sig = jax.nn.sigmoid(g_ref[...])
        dv = d_ref[...]
        o_ref[:, :CONF_DIM] = (dv * sig).astype(bf16)
        o_ref[:, CONF_DIM:] = (dv * v_ref[...] * sig * (1.0 - sig)).astype(bf16)

    blk = lambda cb: pl.BlockSpec((None, tm, CONF_DIM), lambda b, i: (b, i, cb))
    return pl.pallas_call(body, name="glu_bwd", grid=(B, T // tm), in_specs=[blk(0), blk(0), blk(1)],
                          out_specs=pl.BlockSpec((None, tm, 2 * CONF_DIM), lambda b, i: (b, i, 0)),
                          out_shape=jax.ShapeDtypeStruct((B, T, 2 * CONF_DIM), bf16), compiler_params=_cp())(dhh, proj, proj)


def _neg_expm1(x):
    series = x * (1.0 + x * (1.0 / 2.0) * (1.0 + x * (1.0 / 3.0) * (1.0 + x * (1.0 / 4.0) * (1.0 + x * (1.0 / 5.0)))))
    return -jnp.where(x > -0.1, series, jnp.exp(x) - 1.0)


def _lru_gates(cc, xc8, gr8, clg, clb, wa8, wx8, ba8, bx8, lam8):
    xhat, _ = _ln_stats(cc)
    yc = _silu(xhat * clg + clb)
    a8, b8, ge8 = [], [], []
    for hb in range(LRU_HEADS):
        xh = xc8[hb]
        rg = jax.nn.sigmoid(_dot(xh, wa8[hb]) + ba8[hb])
        ig = jax.nn.sigmoid(_dot(xh, wx8[hb]) + bx8[hb])
        log_a = -LRU_C * rg * jax.nn.softplus(-lam8[hb])
        a8.append(jnp.exp(log_a))
        b8.append(jnp.sqrt(_neg_expm1(2.0 * log_a)) * (ig * xh))
        ge8.append(jax.nn.gelu(gr8[hb]))
    return yc, a8, b8, ge8


def _scan_fwd(a, b, h_in):
    tm = a.shape[0]
    rows = lax.broadcasted_iota(jnp.int32, (tm, 1), 0)
    s = 1
    while s < tm:
        keep = rows >= s
        b = a * jnp.where(keep, pltpu.roll(b, s, 0), 0.0) + b
        a = a * jnp.where(keep, pltpu.roll(a, s, 0), 1.0)
        s *= 2
    return a * h_in + b


def _scan_bwd(e, d, g_in):
    tm = e.shape[0]
    rows = lax.broadcasted_iota(jnp.int32, (tm, 1), 0)
    s = 1
    while s < tm:
        keep = rows < tm - s
        d = e * jnp.where(keep, pltpu.roll(d, tm - s, 0), 0.0) + d
        e = e * jnp.where(keep, pltpu.roll(e, tm - s, 0), 1.0)
        s *= 2
    return e * g_in + d


def _lru_params(wa_ref, wx_ref, vec_ref):
    wa8 = [wa_ref[h] for h in range(LRU_HEADS)]
    wx8 = [wx_ref[h] for h in range(LRU_HEADS)]
    ba8 = [vec_ref[0:1, h * LANES:(h + 1) * LANES] for h in range(LRU_HEADS)]
    bx8 = [vec_ref[1:2, h * LANES:(h + 1) * LANES] for h in range(LRU_HEADS)]
    lam8 = [vec_ref[2:3, h * LANES:(h + 1) * LANES] for h in range(LRU_HEADS)]
    return wa8, wx8, ba8, bx8, lam8


GR_COLB = 2


def lru_fwd(cc, xc, proj, cln, wa, wx, vec):
    B, T, _ = xc.shape
    tm = _row_tile(T)

    def body(cc_ref, xc_ref, gr_ref, cln_ref, wa_ref, wx_ref, vec_ref, y_ref, hs_ref, h_scr):
        @pl.when(pl.program_id(1) == 0)
        def _():
            h_scr[...] = jnp.zeros_like(h_scr)

        yc, a8, b8, ge8 = _lru_gates(cc_ref[...], _lane_blocks(xc_ref, 8), _lane_blocks(gr_ref, 8), cln_ref[0:1, :],
                                     cln_ref[1:2, :], *_lru_params(wa_ref, wx_ref, vec_ref))
        h = _scan_fwd(jnp.concatenate(a8, axis=1), jnp.concatenate(b8, axis=1), h_scr[...])
        hs_ref[...] = h
        h_scr[...] = h[tm - 1:tm, :]
        y_ref[:, :CONF_DIM] = yc.astype(bf16)
        y_ref[:, CONF_DIM:] = (h * jnp.concatenate(ge8, axis=1)).astype(bf16)

    row = lambda w, cb=0: pl.BlockSpec((None, tm, w), lambda b, i: (b, i, cb))
    return pl.pallas_call(
        body, name="lru_fwd", grid=(B, T // tm),
        in_specs=[row(CONF_DIM), row(LRU_DIM), row(LRU_DIM, GR_COLB), _const_spec((2, CONF_DIM)),
                  _const_spec(wa.shape), _const_spec(wx.shape), _const_spec((3, LRU_DIM))],
        out_specs=[row(CONF_DIM + LRU_DIM), row(LRU_DIM)],
        out_shape=[jax.ShapeDtypeStruct((B, T, CONF_DIM + LRU_DIM), bf16), jax.ShapeDtypeStruct((B, T, LRU_DIM), f32)],
        scratch_shapes=[pltpu.VMEM((1, LRU_DIM), f32)],
        compiler_params=_cp(),
    )(cc, xc, proj, cln, wa, wx, vec)


def lru_bwd(dycat, cc, xc, proj, hs, cln, wa, wx, vec):
    B, T, _ = xc.shape
    tm = _row_tile(T)
    nt = T // tm
    r = tm // 8

    def body(dy_ref, cc_ref, xc_ref, gr_ref, hs_ref, hsh_ref, cln_ref, wa_ref, wx_ref, vec_ref,
             dcc_ref, dxc_ref, dgr_ref, dcln_ref, dwa_ref, dwx_ref, dvec_ref, g_scr, a_scr):
        b, i = pl.program_id(0), pl.program_id(1)
        it = nt - 1 - i

        @pl.when((b == 0) & (i == 0))
        def _():
            dcln_ref[...] = jnp.zeros_like(dcln_ref)
            dwa_ref[...] = jnp.zeros_like(dwa_ref)
            dwx_ref[...] = jnp.zeros_like(dwx_ref)
            dvec_ref[...] = jnp.zeros_like(dvec_ref)

        @pl.when(i == 0)
        def _():
            g_scr[...] = jnp.zeros_like(g_scr)
            a_scr[...] = jnp.zeros_like(a_scr)

        (yc, a8, b8, ge8), vjp = jax.vjp(_lru_gates, cc_ref[...], _lane_blocks(xc_ref, 8), _lane_blocks(gr_ref, 8),
                                         cln_ref[0:1, :], cln_ref[1:2, :], *_lru_params(wa_ref, wx_ref, vec_ref))
        a = jnp.concatenate(a8, axis=1)
        ge = jnp.concatenate(ge8, axis=1)
        h = hs_ref[...]
        dyd = dy_ref[:, CONF_DIM:]
        rows = lax.broadcasted_iota(jnp.int32, (tm, 1), 0)
        e = jnp.where(rows < tm - 1, pltpu.roll(a, tm - 1, 0), a_scr[...])
        g = _scan_bwd(e, dyd * ge, g_scr[...])
        h_first = jnp.where(it == 0, 0.0, hsh_ref[7:8, :])
        h_prev = jnp.where(rows >= 1, pltpu.roll(h, 1, 0), h_first)
        da = g * h_prev
        g_scr[...] = g[0:1, :]
        a_scr[...] = a[0:1, :]
        split = lambda v: [v[:, k * LANES:(k + 1) * LANES] for k in range(LRU_HEADS)]
        dcc, dxc8, dgr8, dclg, dclb, dwa8, dwx8, dba8, dbx8, dlam8 = vjp((dy_ref[:, :CONF_DIM], split(da), split(g), split(dyd * h)))
        dcc_ref[...] = dcc
        dcln_ref[0:1, :] += dclg
        dcln_ref[1:2, :] += dclb
        for k in range(LRU_HEADS):
            sl = slice(k * LANES, (k + 1) * LANES)
            dxc_ref[:, sl] = dxc8[k]
            dgr_ref[:, sl] = dgr8[k].astype(bf16)
            dwa_ref[k] += dwa8[k]
            dwx_ref[k] += dwx8[k]
            dvec_ref[0:1, sl] += dba8[k]
            dvec_ref[1:2, sl] += dbx8[k]
            dvec_ref[2:3, sl] += dlam8[k]

    rev = lambda w, cb=0: pl.BlockSpec((None, tm, w), lambda b, i: (b, nt - 1 - i, cb))
    acc = lambda shape: pl.BlockSpec(shape, lambda b, i: (0,) * len(shape))
    return pl.pallas_call(
        body, name="lru_bwd", grid=(B, nt),
        in_specs=[rev(CONF_DIM + LRU_DIM), rev(CONF_DIM), rev(LRU_DIM), rev(LRU_DIM, GR_COLB), rev(LRU_DIM),
                  pl.BlockSpec((None, 8, LRU_DIM), lambda b, i: (b, jnp.maximum((nt - 1 - i) * r - 1, 0), 0)),
                  _const_spec((2, CONF_DIM)), _const_spec(wa.shape), _const_spec(wx.shape), _const_spec((3, LRU_DIM))],
        out_specs=[rev(CONF_DIM), rev(LRU_DIM), rev(LRU_DIM), acc((2, CONF_DIM)), acc(wa.shape), acc(wx.shape),
                   acc((3, LRU_DIM))],
        out_shape=[jax.ShapeDtypeStruct((B, T, CONF_DIM), f32), jax.ShapeDtypeStruct((B, T, LRU_DIM), f32),
                   jax.ShapeDtypeStruct((B, T, LRU_DIM), bf16), jax.ShapeDtypeStruct((2, CONF_DIM), f32),
                   jax.ShapeDtypeStruct(wa.shape, f32), jax.ShapeDtypeStruct(wx.shape, f32),
                   jax.ShapeDtypeStruct((3, LRU_DIM), f32)],
        scratch_shapes=[pltpu.VMEM((1, LRU_DIM), f32), pltpu.VMEM((1, LRU_DIM), f32)],
        compiler_params=_cp(),
    )(dycat, cc, xc, proj, hs, hs, cln, wa, wx, vec)


def loss_fwd(y, target):
    B, T, D = y.shape
    tm = _row_tile(T)

    def body(y_ref, t_ref, l_ref, dy_ref):
        @pl.when((pl.program_id(0) == 0) & (pl.program_id(1) == 0))
        def _():
            l_ref[...] = jnp.zeros_like(l_ref)

        d = y_ref[...] - t_ref[...]
        dy_ref[...] = d * (1.0 / D)
        l_ref[...] += jnp.sum(jnp.sum(d * d, axis=1, keepdims=True), axis=0, keepdims=True)

    row = pl.BlockSpec((None, tm, D), lambda b, i: (b, i, 0))
    return pl.pallas_call(
        body, name="loss_fwd", grid=(B, T // tm), in_specs=[row, row],
        out_specs=[pl.BlockSpec((1, 1), lambda b, i: (0, 0)), row],
        out_shape=[jax.ShapeDtypeStruct((1, 1), f32), jax.ShapeDtypeStruct((B, T, D), f32)],
        compiler_params=_cp(),
    )(y, target)


ADA_COL_TILE = 768


def ada_fwd(c_all, w, b):
    L, D, N = w.shape
    nb = c_all.shape[0]
    tn = ADA_COL_TILE

    def body(c_ref, w_ref, b_ref, o_ref):
        o_ref[...] = _dot_hi(_silu(c_ref[...]), w_ref[...]) + b_ref[...]

    return pl.pallas_call(
        body, name="ada_fwd", grid=(L, N // tn),
        in_specs=[pl.BlockSpec((nb, D), lambda l, j: (0, 0)), pl.BlockSpec((None, D, tn), lambda l, j: (l, 0, j)),
                  pl.BlockSpec((None, 1, tn), lambda l, j: (l, 0, j))],
        out_specs=pl.BlockSpec((None, nb, tn), lambda l, j: (l, 0, j)),
        out_shape=jax.ShapeDtypeStruct((L, nb, N), f32),
        compiler_params=_cp(),
    )(c_all, w, b)


def ada_bwd(c_all, dmod_my, dmod_all):
    L, nb, N = dmod_my.shape
    D = c_all.shape[1]
    NA = dmod_all.shape[2]
    tn = ADA_COL_TILE
    nj = N // tn
    ta = NA // nj

    def body(c_ref, dm_ref, da_ref, gw_ref, gb_ref):
        gw_ref[...] = lax.dot_general(_silu(c_ref[...]), dm_ref[...], (((0,), (0,)), ((), ())),
                                      preferred_element_type=f32, precision=lax.Precision.HIGHEST)
        gb_ref[...] = jnp.sum(da_ref[...], axis=0, keepdims=True)

    return pl.pallas_call(
        body, name="ada_bwd", grid=(L, nj),
        in_specs=[pl.BlockSpec((nb, D), lambda l, j: (0, 0)), pl.BlockSpec((None, nb, tn), lambda l, j: (l, 0, j)),
                  pl.BlockSpec((None, nb, ta), lambda l, j: (l, 0, j))],
        out_specs=[pl.BlockSpec((None, D, tn), lambda l, j: (l, 0, j)), pl.BlockSpec((None, 1, ta), lambda l, j: (l, 0, j))],
        out_shape=[jax.ShapeDtypeStruct((L, D, N), f32), jax.ShapeDtypeStruct((L, 1, NA), f32)],
        compiler_params=_cp(),
    )(c_all, dmod_my, dmod_all)


def adamw(w, m, v, gs, offs, tr):
    R, C = w.shape
    ng = len(gs)
    c1 = 1.0 - ADAM_B1 ** ADAM_STEP
    c2 = 1.0 - ADAM_B2 ** ADAM_STEP

    def body(*refs):
        w_ref, m_ref, v_ref = refs[:3]
        g_refs = refs[3:3 + ng]
        d_out, m_out, v_out = refs[3 + ng:]
        g = g_refs[0][...]
        for r in g_refs[1:]:
            g = g + r[...]
        mn = ADAM_B1 * m_ref[...] + (1.0 - ADAM_B1) * g
        vn = ADAM_B2 * v_ref[...] + (1.0 - ADAM_B2) * (g * g)
        m_out[...] = mn
        v_out[...] = vn
        d_out[...] = -ADAM_LR * ((mn / c1) / (jnp.sqrt(vn / c2) + ADAM_EPS) + ADAM_WD * w_ref[...])

    blk = pl.BlockSpec((tr, C), lambda i: (i, 0))
    gspec = lambda off: pl.BlockSpec((tr, C), lambda i: (i + off // tr, 0))
    sds = jax.ShapeDtypeStruct((R, C), f32)
    return pl.pallas_call(
        body, name="adamw", grid=(pl.cdiv(R, tr),), in_specs=[blk, blk, blk] + [gspec(o) for o in offs],
        out_specs=[blk] * 3, out_shape=[sds] * 3, compiler_params=_cp(),
    )(w, m, v, *gs)


def sum_leading(a, tr):
    k, R, C = a.shape

    def body(a_ref, o_ref):
        s = a_ref[0]
        for j in range(1, k):
            s = s + a_ref[j]
        o_ref[...] = s

    return pl.pallas_call(
        body, name="sum_leading", grid=(R // tr,), in_specs=[pl.BlockSpec((k, tr, C), lambda i: (0, i, 0))],
        out_specs=pl.BlockSpec((tr, C), lambda i: (i, 0)), out_shape=jax.ShapeDtypeStruct((R, C), a.dtype),
        compiler_params=_cp(),
    )(a)


ANY = pl.BlockSpec(memory_space=pl.ANY)
CHIP_FLIPS = ((1, 0), (0, 1), (1, 1))
DEV_FLIPS = tuple((fx, fy, fc) for fx in (0, 1) for fy in (0, 1) for fc in (0, 1))[1:]


def _flip(v, f):
    return 1 - v if f else v


def allgather8(v):
    R, C = v.shape
    n = len(DEV_FLIPS)

    def body(v_ref, o_ref, send_sems, recv_sems, local_sem):
        x, y, c = lax.axis_index("x"), lax.axis_index("y"), lax.axis_index("c")
        me = 4 * x + 2 * y + c
        mine = pltpu.make_async_copy(v_ref, o_ref.at[me], local_sem)
        mine.start()
        peers = [(_flip(x, fx), _flip(y, fy), _flip(c, fc)) for fx, fy, fc in DEV_FLIPS]

        def copy(k, slot):
            return pltpu.make_async_remote_copy(src_ref=v_ref, dst_ref=o_ref.at[slot], send_sem=send_sems.at[k],
                                                recv_sem=recv_sems.at[k], device_id=peers[k], device_id_type=MESH)

        for k in range(n):
            copy(k, me).start()
        for k, (px, py, pc) in enumerate(peers):
            copy(k, 4 * px + 2 * py + pc).wait_recv()
        for k in range(n):
            copy(k, me).wait_send()
        mine.wait()

    return pl.pallas_call(
        body, name="allgather8", in_specs=[ANY], out_specs=ANY, out_shape=jax.ShapeDtypeStruct((N_DEV, R, C), v.dtype),
        scratch_shapes=[pltpu.SemaphoreType.DMA((n,)), pltpu.SemaphoreType.DMA((n,)), pltpu.SemaphoreType.DMA],
    )(v)


def _half(ref_or_shape0, c):
    hsz = ref_or_shape0 // 2
    return pl.ds(c * hsz, hsz)


def gather_weights(ws):
    nw, nc = len(ws), len(CHIP_FLIPS)

    def body(*refs):
        w_refs, o_refs = refs[:nw], refs[nw:2 * nw]
        send_sems, recv_sems, local_sems = refs[2 * nw:]
        x, y, c = lax.axis_index("x"), lax.axis_index("y"), lax.axis_index("c")
        me, sibling = 2 * x + y, (x, y, 1 - c)
        peers = [(_flip(x, fx), _flip(y, fy), c) for fx, fy in CHIP_FLIPS]
        slots = [2 * px + py for px, py, _ in peers]

        def copy(a, j, slot, half, to):
            hs = _half(ws[a].shape[0], half)
            return pltpu.make_async_remote_copy(src_ref=o_refs[a].at[slot, hs], dst_ref=o_refs[a].at[slot, hs],
                                                send_sem=send_sems.at[j], recv_sem=recv_sems.at[j],
                                                device_id=to, device_id_type=MESH)

        mine = [pltpu.make_async_copy(w_refs[a], o_refs[a].at[me], local_sems.at[a]) for a in range(nw)]
        for cp in mine:
            cp.start()
        for cp in mine:
            cp.wait()
        first = [[copy(a, a * nc + k, me, c, peers[k]) for k in range(nc)] for a in range(nw)]
        for a in range(nw):
            for cp in first[a]:
                cp.start()
        passed = []
        for a in range(nw):
            for k in range(nc):
                copy(a, a * nc + k, slots[k], c, peers[k]).wait_recv()
                passed.append(copy(a, nw * nc + a * nc + k, slots[k], c, sibling))
                passed[-1].start()
        for a in range(nw):
            for k in range(nc):
                copy(a, nw * nc + a * nc + k, slots[k], 1 - c, sibling).wait_recv()
        for a in range(nw):
            for cp in first[a]:
                cp.wait_send()
        for cp in passed:
            cp.wait_send()

    return pl.pallas_call(
        body, name="gather_weights", in_specs=[ANY] * nw, out_specs=[ANY] * nw,
        out_shape=[jax.ShapeDtypeStruct((N_CHIPS,) + w.shape, w.dtype) for w in ws],
        scratch_shapes=[pltpu.SemaphoreType.DMA((2 * nw * nc,)), pltpu.SemaphoreType.DMA((2 * nw * nc,)),
                        pltpu.SemaphoreType.DMA((nw,))],
    )(*ws)


def swap_halves(gs):
    nw = len(gs)

    def body(*refs):
        g_refs, t_refs = refs[:nw], refs[nw:2 * nw]
        send_sems, recv_sems = refs[2 * nw:]
        x, y, c = lax.axis_index("x"), lax.axis_index("y"), lax.axis_index("c")
        cps = [pltpu.make_async_remote_copy(src_ref=g_refs[a].at[_half(gs[a].shape[0], 1 - c)], dst_ref=t_refs[a],
                                            send_sem=send_sems.at[a], recv_sem=recv_sems.at[a],
                                            device_id=(x, y, 1 - c), device_id_type=MESH) for a in range(nw)]
        for cp in cps:
            cp.start()
        for cp in cps:
            cp.wait()

    return pl.pallas_call(
        body, name="swap_halves", in_specs=[ANY] * nw, out_specs=[ANY] * nw,
        out_shape=[jax.ShapeDtypeStruct((g.shape[0] // 2,) + g.shape[1:], g.dtype) for g in gs],
        scratch_shapes=[pltpu.SemaphoreType.DMA((nw,)), pltpu.SemaphoreType.DMA((nw,))],
    )(*gs)


def exchange_blocks(ps):
    nw, nc = len(ps), len(CHIP_FLIPS)

    def body(*refs):
        p_refs, r_refs = refs[:nw], refs[nw:2 * nw]
        send_sems, recv_sems, local_sems = refs[2 * nw:]
        x, y, c = lax.axis_index("x"), lax.axis_index("y"), lax.axis_index("c")
        me = 2 * x + y
        peers = [(_flip(x, fx), _flip(y, fy), c) for fx, fy in CHIP_FLIPS]
        slots = [2 * px + py for px, py, _ in peers]

        def copy(a, k, src_slot, dst_slot):
            return pltpu.make_async_remote_copy(src_ref=p_refs[a].at[src_slot], dst_ref=r_refs[a].at[dst_slot],
                                                send_sem=send_sems.at[a * nc + k], recv_sem=recv_sems.at[a * nc + k],
                                                device_id=peers[k], device_id_type=MESH)

        mine = [pltpu.make_async_copy(p_refs[a].at[me], r_refs[a].at[me], local_sems.at[a]) for a in range(nw)]
        for cp in mine:
            cp.start()
        for a in range(nw):
            for k in range(nc):
                copy(a, k, slots[k], me).start()
        for a in range(nw):
            for k in range(nc):
                copy(a, k, me, slots[k]).wait_recv()
        for a in range(nw):
            for k in range(nc):
                copy(a, k, slots[k], me).wait_send()
        for cp in mine:
            cp.wait()

    return pl.pallas_call(
        body, name="exchange_blocks", in_specs=[ANY] * nw, out_specs=[ANY] * nw,
        out_shape=[jax.ShapeDtypeStruct(p.shape, p.dtype) for p in ps],
        scratch_shapes=[pltpu.SemaphoreType.DMA((nw * nc,)), pltpu.SemaphoreType.DMA((nw * nc,)),
                        pltpu.SemaphoreType.DMA((nw,))],
    )(*ps)


def join_halves(ss):
    nw = len(ss)

    def body(*refs):
        s_refs, o_refs = refs[:nw], refs[nw:2 * nw]
        send_sems, recv_sems, local_sems = refs[2 * nw:]
        x, y, c = lax.axis_index("x"), lax.axis_index("y"), lax.axis_index("c")
        mine = [pltpu.make_async_copy(s_refs[a], o_refs[a].at[_half(2 * ss[a].shape[0], c)], local_sems.at[a])
                for a in range(nw)]
        for cp in mine:
            cp.start()

        def copy(a, half):
            hs = _half(2 * ss[a].shape[0], half)
            return pltpu.make_async_remote_copy(src_ref=s_refs[a], dst_ref=o_refs[a].at[hs], send_sem=send_sems.at[a],
                                                recv_sem=recv_sems.at[a], device_id=(x, y, 1 - c), device_id_type=MESH)

        for a in range(nw):
            copy(a, c).start()
        for a in range(nw):
            copy(a, 1 - c).wait_recv()
        for a in range(nw):
            copy(a, c).wait_send()
        for cp in mine:
            cp.wait()

    return pl.pallas_call(
        body, name="join_halves", in_specs=[ANY] * nw, out_specs=[ANY] * nw,
        out_shape=[jax.ShapeDtypeStruct((2 * s.shape[0],) + s.shape[1:], s.dtype) for s in ss],
        scratch_shapes=[pltpu.SemaphoreType.DMA((nw,)), pltpu.SemaphoreType.DMA((nw,)), pltpu.SemaphoreType.DMA((nw,))],
    )(*ss)


def _tile_rows(a, b, itemsize=4, budget=4 * 2 ** 20):
    best = 8
    for t in range(8, a + 1, 8):
        if a % t == 0 and t * b * itemsize <= budget:
            best = t
    return best


def add_half(g, t, cidx):
    n0, _, A, B = g.shape
    hsz = n0 // 2
    ta = _tile_rows(A, B)

    def body(c_ref, g_ref, t_ref, o_ref):
        o_ref[...] = (g_ref[...] + t_ref[...]).astype(bf16)

    return pl.pallas_call(
        body, name="add_half",
        grid_spec=pltpu.PrefetchScalarGridSpec(
            num_scalar_prefetch=1, grid=(hsz, N_CHIPS, A // ta),
            in_specs=[pl.BlockSpec((None, None, ta, B), lambda h, k, i, c_ref: (c_ref[0] * hsz + h, k, i, 0)),
                      pl.BlockSpec((None, None, ta, B), lambda h, k, i, c_ref: (h, k, i, 0))],
            out_specs=pl.BlockSpec((None, None, ta, B), lambda h, k, i, c_ref: (k, h, i, 0))),
        out_shape=jax.ShapeDtypeStruct((N_CHIPS, hsz, A, B), bf16),
        compiler_params=_cp(),
    )(cidx, g, t)


def sum_chips(r):
    _, h, A, B = r.shape
    ta = _tile_rows(A, B)

    def body(r_ref, o_ref):
        s = r_ref[0].astype(f32)
        for j in range(1, N_CHIPS):
            s = s + r_ref[j].astype(f32)
        o_ref[...] = s

    return pl.pallas_call(
        body, name="sum_chips", grid=(h, A // ta),
        in_specs=[pl.BlockSpec((N_CHIPS, None, ta, B), lambda hh, i: (0, hh, i, 0))],
        out_specs=pl.BlockSpec((None, ta, B), lambda hh, i: (hh, i, 0)),
        out_shape=jax.ShapeDtypeStruct((h, A, B), f32),
        compiler_params=_cp(),
    )(r)


WEIGHTS = ('ada_w', 'ada_b', 'ln_g', 'ln_b', 'ffn_w_in', 'ffn_w_out', 'ev_w_in', 'ssd_conv_w', 'ssd_conv_b',
           'ssd_dt_bias', 'ssd_a_log', 'ssd_d', 'ssd_norm_g', 'pool_w', 'pool_scale', 'ev_w_out', 'od_w_in',
           'conf_dw_w', 'conf_dw_b', 'conf_ln_g', 'conf_ln_b', 'lru_conv_w', 'lru_conv_b', 'lru_wa', 'lru_ba',
           'lru_wx', 'lru_bx', 'lru_lambda', 'od_w_out')
BIG =('ffn_w_in', 'ffn_w_out', 'ev_w_out', 'od_w_in', 'od_w_out', 'ev_w_in')
SMALL_SHARDED = ('ln_g', 'ln_b', 'ssd_conv_w', 'conf_dw_w', 'conf_dw_b', 'conf_ln_g', 'conf_ln_b', 'lru_conv_w',
                 'lru_conv_b', 'lru_ba', 'lru_bx', 'lru_lambda')
SMALL_REPLICATED = ('ssd_conv_b', 'ssd_dt_bias', 'ssd_a_log', 'ssd_d', 'ssd_norm_g', 'pool_w', 'pool_scale',
                    'lru_wa', 'lru_wx')
PACK_COLS = 1024
BIG_ROW_TILE = 256


def _pack(arrs, row_mult):
    flat = jnp.concatenate([a.reshape(-1) for a in arrs])
    rows = -(-flat.shape[0] // (PACK_COLS * row_mult)) * row_mult
    return jnp.pad(flat, (0, rows * PACK_COLS - flat.shape[0])).reshape(rows, PACK_COLS)


def _unpack(flat, shapes):
    out, off = [], 0
    for s in shapes:
        n = 1
        for d in s:
            n *= d
        out.append(flat[off:off + n].reshape(s))
        off += n
    return out


def _unshard_last(g4):
    m = jnp.moveaxis(g4, 0, -2)
    return m.reshape(m.shape[:-2] + (m.shape[-2] * m.shape[-1],))


def _pad_rows(a, rows):
    return jnp.pad(a, ((0, rows - a.shape[0]),) + ((0, 0),) * (a.ndim - 1))


def _pad_lanes(a):
    return jnp.pad(a, ((0, 0), (0, LANES - a.shape[1])))


def kernel(x, c, ada_w, ada_b, ln_g, ln_b, ffn_w_in, ffn_w_out, ev_w_in, ssd_conv_w, ssd_conv_b, ssd_dt_bias, ssd_a_log, ssd_d, ssd_norm_g, pool_w, pool_scale, ev_w_out, od_w_in, conf_dw_w, conf_dw_b, conf_ln_g, conf_ln_b, lru_conv_w, lru_conv_b, lru_wa, lru_ba, lru_wx, lru_bx, lru_lambda, od_w_out, loss_target, m_ada_w, m_ada_b, m_ln_g, m_ln_b, m_ffn_w_in, m_ffn_w_out, m_ev_w_in, m_ssd_conv_w, m_ssd_conv_b, m_ssd_dt_bias, m_ssd_a_log, m_ssd_d, m_ssd_norm_g, m_pool_w, m_pool_scale, m_ev_w_out, m_od_w_in, m_conf_dw_w, m_conf_dw_b, m_conf_ln_g, m_conf_ln_b, m_lru_conv_w, m_lru_conv_b, m_lru_wa, m_lru_ba, m_lru_wx, m_lru_bx, m_lru_lambda, m_od_w_out, v_ada_w, v_ada_b, v_ln_g, v_ln_b, v_ffn_w_in, v_ffn_w_out, v_ev_w_in, v_ssd_conv_w, v_ssd_conv_b, v_ssd_dt_bias, v_ssd_a_log, v_ssd_d, v_ssd_norm_g, v_pool_w, v_pool_scale, v_ev_w_out, v_od_w_in, v_conf_dw_w, v_conf_dw_b, v_conf_ln_g, v_conf_ln_b, v_lru_conv_w, v_lru_conv_b, v_lru_wa, v_lru_ba, v_lru_wx, v_lru_bx, v_lru_lambda, v_od_w_out):
    given = dict(locals())
    W = {n: given[n] for n in WEIGHTS}
    M = {n: given["m_" + n] for n in WEIGHTS}
    V = {n: given["v_" + n] for n in WEIGHTS}
    B, T, D = x.shape
    L = DEPTH
    chip = 2 * lax.axis_index("x") + lax.axis_index("y")
    dev = 2 * chip + lax.axis_index("c")

    g1 = allgather8(_pack([c] + [W[n] for n in SMALL_SHARDED], 8)).reshape(N_DEV, -1)
    c_all = g1[:, :B * D].reshape(N_DEV * B, D)
    per_chip = g1[0::2, B * D:]
    full = dict(zip(SMALL_SHARDED, [_unshard_last(jnp.stack(p)) for p in zip(*[
        _unpack(per_chip[k], [W[n].shape for n in SMALL_SHARDED]) for k in range(N_CHIPS)])]))
    for n in SMALL_REPLICATED:
        full[n] = W[n]

    n_ada = ada_w.shape[2]
    ada_b_cols = lax.dynamic_slice_in_dim(ada_b, chip * n_ada, n_ada, axis=1)[:, None, :]
    mod_cols = ada_fwd(c_all, ada_w, ada_b_cols)
    g2 = allgather8(mod_cols.reshape(-1, PACK_COLS))[0::2].reshape(N_CHIPS, L, N_DEV * B, n_ada)
    mod_all = jnp.moveaxis(g2, 0, 2).reshape(L, N_DEV * B, N_CHIPS * n_ada)
    mod = lax.dynamic_slice_in_dim(mod_all, dev * B, B, axis=1).reshape(L, B, N_MOD, D)

    gathered = dict(zip(BIG, gather_weights([W[n].astype(bf16) for n in BIG])))
    FS = ffn_w_in.shape[3]

    def even_w_in(e):
        w = _unshard_last(gathered['ev_w_in'][:, e])
        return jnp.concatenate([w[:, :2560], w[:, 2576:], w[:, 2560:2576], jnp.zeros((D, EVEN_IN_PAD - EVEN_IN), bf16)], axis=1)

    saved = []
    xs = x
    for l in range(L):
        lg, lb = full['ln_g'][l], full['ln_b'][l]
        rec = {}
        w_in_a, w_out_a = gathered['ffn_w_in'][:, l, 0], gathered['ffn_w_out'][:, l, 0].reshape(-1, D)
        w_in_b, w_out_b = gathered['ffn_w_in'][:, l, 1], gathered['ffn_w_out'][:, l, 1].reshape(-1, D)
        m1, m2, m3 = mod[l][:, 0:3], mod[l][:, 3:6], mod[l][:, 6:9]
        xn, h, gu, a, y = ffn_fwd(xs, m1, w_in_a, w_out_a, lg[0:1], lb[0:1])
        rec['ffa'] = (xs, h, gu, a, y, m1, w_in_a, w_out_a, lg[0:1])
        xs = xn
        if l % 2 == 0:
            e = l // 2
            w_in_m, w_out_m = even_w_in(e), gathered['ev_w_out'][:, e].reshape(-1, D)
            cw = _pad_rows(full['ssd_conv_w'][e], 8)
            cb = full['ssd_conv_b'][e][None]
            dtb, alog, dsk = (_pad_lanes(full[n][e][None]) for n in ('ssd_dt_bias', 'ssd_a_log', 'ssd_d'))
            ng, pw, ps = full['ssd_norm_g'][e][None], full['pool_w'][e], full['pool_scale'][e][None]
            proj, hm = inproj_fwd(xs, m2, w_in_m)
            cx = dwconv_fwd(proj, 2, SSD_CONV_DIM, cw, cb, SSD_CONV)
            ya, hsave = ssd_fwd(proj, cx, dtb, alog, dsk, ng)
            yb = pool_fwd(proj, 5, pw, ps)
            ycat = jnp.concatenate([ya, yb], axis=-1)
            rec['mix'] = (proj, cx, hsave, cw, dtb, alog, dsk, ng, pw, ps)
        else:
            o = l // 2
            w_in_m, w_out_m = jnp.moveaxis(gathered['od_w_in'][:, o], 0, 1).reshape(D, -1), gathered['od_w_out'][:, o].reshape(-1, D)
            dww = _pad_rows(full['conf_dw_w'][o], 32)
            dwb = full['conf_dw_b'][o][None]
            cw = _pad_rows(full['lru_conv_w'][o], 8)
            cb = full['lru_conv_b'][o][None]
            cln = jnp.stack([full['conf_ln_g'][o], full['conf_ln_b'][o]])
            vec = jnp.stack([full['lru_ba'][o], full['lru_bx'][o], full['lru_lambda'][o]])
            wa, wx = full['lru_wa'][o], full['lru_wx'][o]
            proj, hm = inproj_fwd(xs, m2, w_in_m)
            hh = glu_fwd(proj)
            cc = dwconv_fwd(hh, 0, CONF_DIM, dww, dwb, CONF_KERNEL)
            xc = dwconv_fwd(proj, 2, LRU_DIM, cw, cb, LRU_CONV)
            ycat, hst = lru_fwd(cc, xc, proj, cln, wa, wx, vec)
            rec['mix'] = (proj, hh, cc, xc, hst, dww, cw, cln, wa, wx, vec)
        xn, ym = outproj_fwd(ycat, w_out_m, xs, m2, lg[1:2], lb[1:2])
        rec['mixio'] = (xs, hm, ycat, ym, m2, w_in_m, w_out_m, lg[1:2])
        xs = xn
        xn, h, gu, a, y = ffn_fwd(xs, m3, w_in_b, w_out_b, lg[2:3], lb[2:3])
        rec['ffb'] = (xs, h, gu, a, y, m3, w_in_b, w_out_b, lg[2:3])
        xs = xn
        saved.append(rec)

    sq, dxs = loss_fwd(xs, loss_target)
    loss = lax.psum(sq[0, 0], ("x", "y", "c")) * (0.5 / D)

    gpart = {n: [None] * W[n].shape[0] for n in WEIGHTS}
    gpart['ffn_w_in'] = [[None, None] for _ in range(L)]
    gpart['ffn_w_out'] = [[None, None] for _ in range(L)]
    gpart['ln_g'] = [[None] * 3 for _ in range(L)]
    gpart['ln_b'] = [[None] * 3 for _ in range(L)]
    dmod = [None] * L

    def ffn_back(dxn, rec, l, f):
        xin, h, gu, a, y, m3_, w_in_, w_out_, lg_ = rec
        dx, dgu, dy, dm3, dln = ffn_bwd(dxn, xin, y, gu, m3_, w_in_, w_out_, lg_)
        gpart['ffn_w_in'][l][f] = wgrad(h, dgu, FS)
        gpart['ffn_w_out'][l][f] = wgrad(a, dy, D)[0].reshape(N_CHIPS, -1, D)
        gpart['ln_g'][l][2 * f] = dln[0]
        gpart['ln_b'][l][2 * f] = dln[1]
        return dx, dm3

    for l in reversed(range(L)):
        rec = saved[l]
        dxs, dm3 = ffn_back(dxs, rec['ffb'], l, 1)
        xin, hm, ycat, ym, m2, w_in_m, w_out_m, lg_ = rec['mixio']
        dxp, dycat, dy, dg2, dln = outproj_bwd(dxs, xin, ym, m2, w_out_m, lg_)
        gpart['ln_g'][l][1] = dln[0]
        gpart['ln_b'][l][1] = dln[1]
        gw_out = wgrad(ycat, dy, D)[0].reshape(N_CHIPS, -1, D)
        if l % 2 == 0:
            e = l // 2
            proj, cx, hsave, cw, dtb, alog, dsk, ng, pw, ps = rec['mix']
            dz, dcx, ddt, gsm, gng = ssd_bwd(dycat, proj, cx, hsave, dtb, alog, dsk, ng)
            dxbc, dcw, dcb = dwconv_bwd(dcx, proj, 2, SSD_CONV_DIM, cw, SSD_CONV, bf16)
            du, dpw, dps = pool_bwd(dycat, 2, proj, 5, pw, ps)
            dproj = jnp.concatenate([dz, dxbc, du, ddt], axis=-1)
            gwp = wgrad(hm, dproj, EVEN_IN_PAD)[0]
            gw = jnp.concatenate([gwp[:, :2560], gwp[:, 3072:3072 + 16], gwp[:, 2560:3072]], axis=1)
            gpart['ev_w_in'][e] = jnp.moveaxis(gw.reshape(D, N_CHIPS, -1), 1, 0)
            gpart['ev_w_out'][e] = gw_out
            gpart['ssd_conv_w'][e], gpart['ssd_conv_b'][e] = dcw[:SSD_CONV], dcb[0]
            gpart['ssd_dt_bias'][e], gpart['ssd_a_log'][e], gpart['ssd_d'][e] = (gsm[k, :SSD_HEADS] for k in range(3))
            gpart['ssd_norm_g'][e], gpart['pool_w'][e], gpart['pool_scale'][e] = gng[0], dpw, dps[0]
        else:
            o = l // 2
            proj, hh, cc, xc, hst, dww, cw, cln, wa, wx, vec = rec['mix']
            dcc, dxc, dgr, dcln, dwa, dwx, dvec = lru_bwd(dycat, cc, xc, proj, hst, cln, wa, wx, vec)
            dhh, ddw, ddb = dwconv_bwd(dcc, hh, 0, CONF_DIM, dww, CONF_KERNEL, f32)
            dvg = glu_bwd(dhh, proj)
            dxr, dcw, dcb = dwconv_bwd(dxc, proj, 2, LRU_DIM, cw, LRU_CONV, bf16)
            dproj = jnp.concatenate([dvg, dxr, dgr], axis=-1)
            gpart['od_w_in'][o] = wgrad(hm, dproj, dproj.shape[-1] // N_CHIPS)
            gpart['od_w_out'][o] = gw_out
            gpart['conf_dw_w'][o], gpart['conf_dw_b'][o] = ddw[:CONF_KERNEL], ddb[0]
            gpart['conf_ln_g'][o], gpart['conf_ln_b'][o] = dcln[0], dcln[1]
            gpart['lru_conv_w'][o], gpart['lru_conv_b'][o] = dcw[:LRU_CONV], dcb[0]
            gpart['lru_wa'][o], gpart['lru_wx'][o] = dwa, dwx
            gpart['lru_ba'][o], gpart['lru_bx'][o], gpart['lru_lambda'][o] = dvec[0], dvec[1], dvec[2]
        dxs, dm2 = inproj_bwd(dproj, w_in_m, xin, m2, dxp)
        dxs, dm1 = ffn_back(dxs, rec['ffa'], l, 0)
        dmod[l] = jnp.concatenate([dm1, dm2, dg2, dm3], axis=1)
    grad_x = dxs

    def stack(v):
        return jnp.stack([stack(u) if isinstance(u, list) else u for u in v])

    def per_chip(n):
        if W[n].ndim == 4:
            g = jnp.stack([jnp.stack(lf, axis=1) for lf in gpart[n]])
            return g.reshape(g.shape[:2] + (g.shape[2] * g.shape[3], g.shape[4]))
        return jnp.stack(gpart[n])

    gfull = [per_chip(n) for n in BIG]
    cidx = lax.axis_index("c").astype(jnp.int32).reshape(1)
    summed = [add_half(g, t, cidx) for g, t in zip(gfull, swap_halves(gfull))]
    reduced = join_halves([sum_chips(r) for r in exchange_blocks(summed)])

    out_g, out_d, out_m, out_v = {}, {}, {}, {}
    for n, g in zip(BIG, reduced):
        shp = W[n].shape
        as2d = lambda a: a.reshape(-1, shp[-1])
        res = adamw(as2d(W[n]), as2d(M[n]), as2d(V[n]), [as2d(g)], [0], BIG_ROW_TILE)
        out_g[n] = g.reshape(shp)
        out_d[n], out_m[n], out_v[n] = (r.reshape(shp) for r in res)

    small = SMALL_SHARDED + SMALL_REPLICATED
    dmod_flat = stack(dmod).reshape(L, B, N_MOD * D)
    g3 = allgather8(_pack([dmod_flat] + [stack(gpart[n]) for n in small], 64))
    n_dmod = L * B * N_MOD * D
    dmod_all = jnp.moveaxis(g3.reshape(N_DEV, -1)[:, :n_dmod].reshape(N_DEV, L, B, N_MOD * D), 0, 1).reshape(L, N_DEV * B, N_MOD * D)
    ssum = sum_leading(g3, 64).reshape(-1)[n_dmod:]
    gsmall = dict(zip(small, _unpack(ssum, [full[n].shape for n in small])))
    for n in SMALL_SHARDED:
        wdt = W[n].shape[-1]
        gsmall[n] = lax.dynamic_slice_in_dim(gsmall[n], chip * wdt, wdt, axis=gsmall[n].ndim - 1)
    dmod_my = lax.dynamic_slice_in_dim(dmod_all, chip * n_ada, n_ada, axis=2)
    g_ada_w, g_ada_b = ada_bwd(c_all, dmod_my, dmod_all)
    gsmall['ada_b'] = g_ada_b[:, 0, :]

    res = adamw(ada_w.reshape(-1, n_ada), M['ada_w'].reshape(-1, n_ada), V['ada_w'].reshape(-1, n_ada),
                [g_ada_w.reshape(-1, n_ada)], [0], BIG_ROW_TILE)
    out_g['ada_w'] = g_ada_w
    out_d['ada_w'], out_m['ada_w'], out_v['ada_w'] = (r.reshape(ada_w.shape) for r in res)

    names = ('ada_b',) + small
    shapes = [W[n].shape for n in names]
    res = adamw(_pack([W[n] for n in names], 64), _pack([M[n] for n in names], 64), _pack([V[n] for n in names], 64),
                [_pack([gsmall[n] for n in names], 64)], [0], 64)
    out_g.update({n: gsmall[n] for n in names})
    for dst, r in zip((out_d, out_m, out_v), res):
        dst.update(zip(names, _unpack(r.reshape(-1), shapes)))

    return (loss, grad_x, *[out_g[n] for n in WEIGHTS], *[out_d[n] for n in WEIGHTS], *[out_m[n] for n in WEIGHTS],
            *[out_v[n] for n in WEIGHTS])
```

```python
import jax
import jax.numpy as jnp
from jax import lax
from jax.experimental import pallas as pl
from jax.experimental.pallas import tpu as pltpu

f32 = jnp.float32
bf16 = jnp.bfloat16

DEPTH = 4
D_MODEL = 1024
N_MOD = 9
DN_ALPHA = (2.0 * DEPTH) ** 0.25
NORM_EPS = 1e-5
SSD_CHUNK = 128
SSD_D_INNER = 1024
SSD_CONV_DIM = 1536
SSD_HEADS = 16
POOL_WINDOWS = (2, 4, 8, 16)
POOL_DIM = 512
EVEN_IN = 3088
EVEN_IN_PAD = 3200
CONF_DIM = 512
CONF_KERNEL = 31
LRU_DIM = 1024
LRU_HEADS = 8
LRU_CONV = 4
SSD_CONV = 4
LRU_C = 8.0
ADAM_LR = 0.001
ADAM_B1 = 0.9
ADAM_B2 = 0.999
ADAM_EPS = 1e-08
ADAM_WD = 0.01
ADAM_STEP = 10

LANES = 128
VMEM_LIMIT_BYTES = 56 * 2 ** 20
COL_TILE = 512
N_CHIPS = 4
N_DEV = 8
MESH = pl.DeviceIdType.MESH


def _cp():
    return pltpu.CompilerParams(vmem_limit_bytes=VMEM_LIMIT_BYTES)


def _dot(a, b):
    return jnp.dot(a, b, preferred_element_type=f32)


def _dot_nt(a, b):
    return lax.dot_general(a, b, (((1,), (1,)), ((), ())), preferred_element_type=f32)


def _dot_tn(a, b):
    return lax.dot_general(a, b, (((0,), (0,)), ((), ())), preferred_element_type=f32)


def _dot_hi(a, b):
    return jnp.dot(a, b, preferred_element_type=f32, precision=lax.Precision.HIGHEST)


def _silu(x):
    return x * jax.nn.sigmoid(x)


def _ln_stats(z):
    mu = jnp.mean(z, axis=-1, keepdims=True)
    zc = z - mu
    var = jnp.mean(zc * zc, axis=-1, keepdims=True)
    rstd = lax.rsqrt(var + NORM_EPS)
    return zc * rstd, rstd


def _ln_bwd(dxn, xhat, rstd, lg):
    dxh = dxn * lg
    return rstd * (dxh - jnp.mean(dxh, axis=-1, keepdims=True) - xhat * jnp.mean(dxh * xhat, axis=-1, keepdims=True))


def _const_spec(shape):
    nd = len(shape)
    return pl.BlockSpec(shape, lambda *_: (0,) * nd, pipeline_mode=pl.Buffered(1))


def _row_tile(t, want=256):
    return min(want, t)


def ffn_fwd(x, mod3, w_in, w_out, lg, lb):
    B, T, D = x.shape
    FS = w_in.shape[2]
    tm = _row_tile(T)

    def body(x_ref, mod_ref, win_ref, wout_ref, lg_ref, lb_ref, xn_ref, h_ref, gu_ref, a_ref, y_ref):
        xv = x_ref[...]
        sh, sc, g = mod_ref[0:1, :], mod_ref[1:2, :], mod_ref[2:3, :]
        h = (xv * (1.0 + sc) + sh).astype(bf16)
        h_ref[...] = h
        acc = jnp.zeros((tm, D), f32)
        for s in range(2):
            gate = _dot(h, win_ref[s])
            up = _dot(h, win_ref[s + 2])
            gu_ref[:, s * FS:(s + 1) * FS] = gate.astype(bf16)
            gu_ref[:, (s + 2) * FS:(s + 3) * FS] = up.astype(bf16)
            a = (_silu(gate) * up).astype(bf16)
            a_ref[:, s * FS:(s + 1) * FS] = a
            acc = acc + _dot(a, wout_ref[s * FS:(s + 1) * FS, :])
        y_ref[...] = acc
        xhat, _ = _ln_stats(DN_ALPHA * xv + 0.5 * (1.0 + g) * acc)
        xn_ref[...] = xhat * lg_ref[...] + lb_ref[...]

    row = lambda w: pl.BlockSpec((None, tm, w), lambda b, i: (b, i, 0))
    return pl.pallas_call(
        body, name="ffn_fwd", grid=(B, T // tm),
        in_specs=[row(D), pl.BlockSpec((None, 3, D), lambda b, i: (b, 0, 0)), _const_spec(w_in.shape),
                  _const_spec(w_out.shape), _const_spec((1, D)), _const_spec((1, D))],
        out_specs=[row(D), row(D), row(4 * FS), row(2 * FS), row(D)],
        out_shape=[jax.ShapeDtypeStruct((B, T, D), f32), jax.ShapeDtypeStruct((B, T, D), bf16),
                   jax.ShapeDtypeStruct((B, T, 4 * FS), bf16), jax.ShapeDtypeStruct((B, T, 2 * FS), bf16),
                   jax.ShapeDtypeStruct((B, T, D), f32)],
        compiler_params=_cp(),
    )(x, mod3, w_in, w_out, lg, lb)


def ffn_bwd(dxn, x, y, gu, mod3, w_in, w_out, lg):
    B, T, D = x.shape
    FS = w_in.shape[2]
    tm = _row_tile(T)

    def body(dxn_ref, x_ref, y_ref, gu_ref, mod_ref, win_ref, wout_ref, lg_ref,
             dx_ref, dgu_ref, dy_ref, dmod_ref, dln_ref):
        b, i = pl.program_id(0), pl.program_id(1)

        @pl.when((b == 0) & (i == 0))
        def _():
            dln_ref[...] = jnp.zeros_like(dln_ref)

        @pl.when(i == 0)
        def _():
            dmod_ref[...] = jnp.zeros_like(dmod_ref)

        xv, yv, dxn_v = x_ref[...], y_ref[...], dxn_ref[...]
        sc, g = mod_ref[1:2, :], mod_ref[2:3, :]
        xhat, rstd = _ln_stats(DN_ALPHA * xv + 0.5 * (1.0 + g) * yv)
        dln_ref[0:1, :] += jnp.sum(dxn_v * xhat, axis=0, keepdims=True)
        dln_ref[1:2, :] += jnp.sum(dxn_v, axis=0, keepdims=True)
        dz = _ln_bwd(dxn_v, xhat, rstd, lg_ref[...])
        dmod_ref[2:3, :] += jnp.sum(0.5 * dz * yv, axis=0, keepdims=True)
        dy = (0.5 * (1.0 + g) * dz).astype(bf16)
        dy_ref[...] = dy
        dh = jnp.zeros((tm, D), f32)
        for s in range(2):
            da = _dot_nt(dy, wout_ref[s * FS:(s + 1) * FS, :])
            gate = gu_ref[:, s * FS:(s + 1) * FS].astype(f32)
            up = gu_ref[:, (s + 2) * FS:(s + 3) * FS].astype(f32)
            sig = jax.nn.sigmoid(gate)
            dgate = (da * up * (sig * (1.0 + gate * (1.0 - sig)))).astype(bf16)
            dup = (da * gate * sig).astype(bf16)
            dgu_ref[:, s * FS:(s + 1) * FS] = dgate
            dgu_ref[:, (s + 2) * FS:(s + 3) * FS] = dup
            dh = dh + _dot_nt(dgate, win_ref[s]) + _dot_nt(dup, win_ref[s + 2])
        dx_ref[...] = DN_ALPHA * dz + dh * (1.0 + sc)
        dmod_ref[0:1, :] += jnp.sum(dh, axis=0, keepdims=True)
        dmod_ref[1:2, :] += jnp.sum(dh * xv, axis=0, keepdims=True)

    row = lambda w: pl.BlockSpec((None, tm, w), lambda b, i: (b, i, 0))
    return pl.pallas_call(
        body, name="ffn_bwd", grid=(B, T // tm),
        in_specs=[row(D), row(D), row(D), row(4 * FS), pl.BlockSpec((None, 3, D), lambda b, i: (b, 0, 0)),
                  _const_spec(w_in.shape), _const_spec(w_out.shape), _const_spec((1, D))],
        out_specs=[row(D), row(4 * FS), row(D), pl.BlockSpec((None, 3, D), lambda b, i: (b, 0, 0)),
                   pl.BlockSpec((2, D), lambda b, i: (0, 0))],
        out_shape=[jax.ShapeDtypeStruct((B, T, D), f32), jax.ShapeDtypeStruct((B, T, 4 * FS), bf16),
                   jax.ShapeDtypeStruct((B, T, D), bf16), jax.ShapeDtypeStruct((B, 3, D), f32),
                   jax.ShapeDtypeStruct((2, D), f32)],
        compiler_params=_cp(),
    )(dxn, x, y, gu, mod3, w_in, w_out, lg)


def wgrad(a, b, tn):
    B, T, K = a.shape
    N = b.shape[2]
    tr = _row_tile(T, 1024 if K * tn <= 1024 * 1536 else 512)

    def body(a_ref, b_ref, o_ref):
        @pl.when((pl.program_id(1) == 0) & (pl.program_id(2) == 0))
        def _():
            o_ref[...] = jnp.zeros_like(o_ref)

        o_ref[...] += _dot_tn(a_ref[...], b_ref[...])

    return pl.pallas_call(
        body, name="wgrad", grid=(N // tn, B, T // tr),
        in_specs=[pl.BlockSpec((None, tr, K), lambda s, b, r: (b, r, 0)),
                  pl.BlockSpec((None, tr, tn), lambda s, b, r: (b, r, s))],
        out_specs=pl.BlockSpec((None, K, tn), lambda s, b, r: (s, 0, 0)),
        out_shape=jax.ShapeDtypeStruct((N // tn, K, tn), f32),
        compiler_params=_cp(),
    )(a, b)


def inproj_fwd(x, mod3, w):
    B, T, D = x.shape
    N = w.shape[1]
    tm = _row_tile(T, 512)

    def body(x_ref, mod_ref, w_ref, p_ref, h_ref):
        h = (x_ref[...] * (1.0 + mod_ref[1:2, :]) + mod_ref[0:1, :]).astype(bf16)
        h_ref[...] = h
        p_ref[...] = _dot(h, w_ref[...])

    row = lambda n: pl.BlockSpec((None, tm, n), lambda b, i: (b, i, 0))
    return pl.pallas_call(
        body, name="inproj_fwd", grid=(B, T // tm),
        in_specs=[row(D), pl.BlockSpec((None, 3, D), lambda b, i: (b, 0, 0)), _const_spec(w.shape)],
        out_specs=[row(N), row(D)],
        out_shape=[jax.ShapeDtypeStruct((B, T, N), f32), jax.ShapeDtypeStruct((B, T, D), bf16)],
        compiler_params=_cp(),
    )(x, mod3, w)


def inproj_bwd(dproj, w, x, mod3, dxp):
    B, T, D = x.shape
    N = w.shape[1]
    tm = _row_tile(T, 512)

    def body(dp_ref, w_ref, x_ref, mod_ref, dxp_ref, dx_ref, dmod_ref):
        @pl.when(pl.program_id(1) == 0)
        def _():
            dmod_ref[...] = jnp.zeros_like(dmod_ref)

        dh = _dot_nt(dp_ref[...], w_ref[...])
        dx_ref[...] = dxp_ref[...] + dh * (1.0 + mod_ref[1:2, :])
        dmod_ref[0:1, :] += jnp.sum(dh, axis=0, keepdims=True)
        dmod_ref[1:2, :] += jnp.sum(dh * x_ref[...], axis=0, keepdims=True)

    row = lambda n: pl.BlockSpec((None, tm, n), lambda b, i: (b, i, 0))
    return pl.pallas_call(
        body, name="inproj_bwd", grid=(B, T // tm),
        in_specs=[row(N), _const_spec(w.shape), row(D), pl.BlockSpec((None, 3, D), lambda b, i: (b, 0, 0)), row(D)],
        out_specs=[row(D), pl.BlockSpec((None, 2, D), lambda b, i: (b, 0, 0))],
        out_shape=[jax.ShapeDtypeStruct((B, T, D), f32), jax.ShapeDtypeStruct((B, 2, D), f32)],
        compiler_params=_cp(),
    )(dproj, w, x, mod3, dxp)


def outproj_fwd(ycat, w, x, mod3, lg, lb):
    B, T, D = x.shape
    E = w.shape[0]
    tm = _row_tile(T, 512)

    def body(yc_ref, w_ref, x_ref, mod_ref, lg_ref, lb_ref, xn_ref, y_ref):
        yv = _dot(yc_ref[...], w_ref[...])
        y_ref[...] = yv
        xhat, _ = _ln_stats(DN_ALPHA * x_ref[...] + (1.0 + mod_ref[2:3, :]) * yv)
        xn_ref[...] = xhat * lg_ref[...] + lb_ref[...]

    row = lambda n: pl.BlockSpec((None, tm, n), lambda b, i: (b, i, 0))
    return pl.pallas_call(
        body, name="outproj_fwd", grid=(B, T // tm),
        in_specs=[row(E), _const_spec(w.shape), row(D), pl.BlockSpec((None, 3, D), lambda b, i: (b, 0, 0)),
                  _const_spec((1, D)), _const_spec((1, D))],
        out_specs=[row(D), row(D)],
        out_shape=[jax.ShapeDtypeStruct((B, T, D), f32), jax.ShapeDtypeStruct((B, T, D), f32)],
        compiler_params=_cp(),
    )(ycat, w, x, mod3, lg, lb)


def outproj_bwd(dxn, x, y, mod3, w, lg):
    B, T, D = x.shape
    E = w.shape[0]
    tm = _row_tile(T, 512)

    def body(dxn_ref, x_ref, y_ref, mod_ref, w_ref, lg_ref, dxp_ref, dyc_ref, dy_ref, dg_ref, dln_ref):
        b, i = pl.program_id(0), pl.program_id(1)

        @pl.when((b == 0) & (i == 0))
        def _():
            dln_ref[...] = jnp.zeros_like(dln_ref)

        @pl.when(i == 0)
        def _():
            dg_ref[...] = jnp.zeros_like(dg_ref)

        xv, yv, dxn_v = x_ref[...], y_ref[...], dxn_ref[...]
        g = mod_ref[2:3, :]
        xhat, rstd = _ln_stats(DN_ALPHA * xv + (1.0 + g) * yv)
        dln_ref[0:1, :] += jnp.sum(dxn_v * xhat, axis=0, keepdims=True)
        dln_ref[1:2, :] += jnp.sum(dxn_v, axis=0, keepdims=True)
        dz = _ln_bwd(dxn_v, xhat, rstd, lg_ref[...])
        dg_ref[...] += jnp.sum(dz * yv, axis=0, keepdims=True)
        dy = ((1.0 + g) * dz).astype(bf16)
        dy_ref[...] = dy
        dxp_ref[...] = DN_ALPHA * dz
        dyc_ref[...] = _dot_nt(dy, w_ref[...])

    row = lambda n: pl.BlockSpec((None, tm, n), lambda b, i: (b, i, 0))
    return pl.pallas_call(
        body, name="outproj_bwd", grid=(B, T // tm),
        in_specs=[row(D), row(D), row(D), pl.BlockSpec((None, 3, D), lambda b, i: (b, 0, 0)), _const_spec(w.shape),
                  _const_spec((1, D))],
        out_specs=[row(D), row(E), row(D), pl.BlockSpec((None, 1, D), lambda b, i: (b, 0, 0)),
                   pl.BlockSpec((2, D), lambda b, i: (0, 0))],
        out_shape=[jax.ShapeDtypeStruct((B, T, D), f32), jax.ShapeDtypeStruct((B, T, E), f32),
                   jax.ShapeDtypeStruct((B, T, D), bf16), jax.ShapeDtypeStruct((B, 1, D), f32),
                   jax.ShapeDtypeStruct((2, D), f32)],
        compiler_params=_cp(),
    )(dxn, x, y, mod3, w, lg)


def _halo_rows(K):
    return 8 if K <= 9 else 32


def dwconv_fwd(x, col0, C, w, b, K):
    B, T, _ = x.shape
    tc, hp = COL_TILE, _halo_rows(K)
    tm = _row_tile(T)
    r = tm // hp

    def body(xh_ref, x_ref, w_ref, b_ref, o_ref):
        halo = jnp.where(pl.program_id(2) == 0, 0.0, xh_ref[...])
        xe = jnp.concatenate([halo, x_ref[...]], axis=0)
        acc = jnp.zeros((tm, tc), f32) + b_ref[...]
        for k in range(K):
            sft = K - 1 - k
            xs = xe if sft == 0 else pltpu.roll(xe, sft, 0)
            acc = acc + xs[hp:, :] * w_ref[k:k + 1, :]
        o_ref[...] = acc

    return pl.pallas_call(
        body, name=f"dwconv{K}_fwd", grid=(C // tc, B, T // tm),
        in_specs=[pl.BlockSpec((None, hp, tc), lambda j, b, i: (b, jnp.maximum(i * r - 1, 0), col0 + j)),
                  pl.BlockSpec((None, tm, tc), lambda j, b, i: (b, i, col0 + j)),
                  pl.BlockSpec((w.shape[0], tc), lambda j, b, i: (0, j)),
                  pl.BlockSpec((1, tc), lambda j, b, i: (0, j))],
        out_specs=pl.BlockSpec((None, tm, tc), lambda j, b, i: (b, i, j)),
        out_shape=jax.ShapeDtypeStruct((B, T, C), f32),
        compiler_params=_cp(),
    )(x, x, w, b)


def dwconv_bwd(dc, x, col0, C, w, K, out_dtype):
    B, T, _ = x.shape
    tc, hp = COL_TILE, _halo_rows(K)
    tm = _row_tile(T)
    r = tm // hp
    nt = T // tm
    n = tm + hp
    KP = w.shape[0]

    def body(dcn_ref, dc_ref, xh_ref, x_ref, w_ref, dx_ref, dw_ref, db_ref):
        b, i = pl.program_id(1), pl.program_id(2)

        @pl.when((b == 0) & (i == 0))
        def _():
            dw_ref[...] = jnp.zeros_like(dw_ref)
            db_ref[...] = jnp.zeros_like(db_ref)

        dcv = dc_ref[...]
        de = jnp.concatenate([dcv, jnp.where(i == nt - 1, 0.0, dcn_ref[...])], axis=0)
        acc = jnp.zeros((tm, tc), f32)
        for k in range(K):
            j = K - 1 - k
            ds = de if j == 0 else pltpu.roll(de, n - j, 0)
            acc = acc + ds[:tm, :] * w_ref[k:k + 1, :]
        dx_ref[...] = acc.astype(out_dtype)
        xe = jnp.concatenate([jnp.where(i == 0, 0.0, xh_ref[...]), x_ref[...]], axis=0)
        for k in range(K):
            sft = K - 1 - k
            xs = xe if sft == 0 else pltpu.roll(xe, sft, 0)
            dw_ref[k:k + 1, :] += jnp.sum(dcv * xs[hp:, :], axis=0, keepdims=True)
        db_ref[...] += jnp.sum(dcv, axis=0, keepdims=True)

    return pl.pallas_call(
        body, name=f"dwconv{K}_bwd", grid=(C // tc, B, nt),
        in_specs=[pl.BlockSpec((None, hp, tc), lambda j, b, i: (b, jnp.minimum((i + 1) * r, T // hp - 1), j)),
                  pl.BlockSpec((None, tm, tc), lambda j, b, i: (b, i, j)),
                  pl.BlockSpec((None, hp, tc), lambda j, b, i: (b, jnp.maximum(i * r - 1, 0), col0 + j)),
                  pl.BlockSpec((None, tm, tc), lambda j, b, i: (b, i, col0 + j)),
                  pl.BlockSpec((KP, tc), lambda j, b, i: (0, j))],
        out_specs=[pl.BlockSpec((None, tm, tc), lambda j, b, i: (b, i, j)),
                   pl.BlockSpec((KP, tc), lambda j, b, i: (0, j)),
                   pl.BlockSpec((1, tc), lambda j, b, i: (0, j))],
        out_shape=[jax.ShapeDtypeStruct((B, T, C), out_dtype), jax.ShapeDtypeStruct((KP, C), f32),
                   jax.ShapeDtypeStruct((1, C), f32)],
        compiler_params=_cp(),
    )(dc, dc, x, x, w)


POOL_HALO = 16


def _pool_windows(ue, pos, hp):
    out = []
    for g, wd in enumerate(POOL_WINDOWS):
        ug = ue[:, g * LANES:(g + 1) * LANES]
        s, span = ug, 1
        while span < wd:
            s = s + pltpu.roll(s, span, 0)
            span *= 2
        cnt = jnp.minimum(pos + 1, wd).astype(f32)
        out.append(s[hp:, :] / cnt - ug[hp:, :])
    return out


def pool_fwd(proj, colb, w, scale):
    B, T, _ = proj.shape
    hp = POOL_HALO
    tm = _row_tile(T)
    r = tm // hp

    def body(uh_ref, u_ref, w_ref, sc_ref, o_ref):
        i = pl.program_id(1)
        ue = jnp.concatenate([jnp.where(i == 0, 0.0, uh_ref[...]), u_ref[...]], axis=0)
        pos = i * tm + lax.broadcasted_iota(jnp.int32, (tm, 1), 0)
        ps = _pool_windows(ue, pos, hp)
        o = jnp.concatenate([_dot(ps[g], w_ref[g]) for g in range(4)], axis=1) * sc_ref[...]
        o_ref[...] = o.astype(bf16)

    return pl.pallas_call(
        body, name="pool_fwd", grid=(B, T // tm),
        in_specs=[pl.BlockSpec((None, hp, POOL_DIM), lambda b, i: (b, jnp.maximum(i * r - 1, 0), colb)),
                  pl.BlockSpec((None, tm, POOL_DIM), lambda b, i: (b, i, colb)),
                  _const_spec(w.shape), _const_spec((1, POOL_DIM))],
        out_specs=pl.BlockSpec((None, tm, POOL_DIM), lambda b, i: (b, i, 0)),
        out_shape=jax.ShapeDtypeStruct((B, T, POOL_DIM), bf16),
        compiler_params=_cp(),
    )(proj, proj, w, scale)


def pool_bwd(dycat, dcolb, proj, colb, w, scale):
    B, T, _ = proj.shape
    hp = POOL_HALO
    tm = _row_tile(T)
    r = tm // hp
    nt = T // tm
    n = tm + hp

    def body(dyn_ref, dy_ref, uh_ref, u_ref, w_ref, sc_ref, du_ref, dw_ref, dsc_ref):
        b, i = pl.program_id(0), pl.program_id(1)

        @pl.when((b == 0) & (i == 0))
        def _():
            dw_ref[...] = jnp.zeros_like(dw_ref)
            dsc_ref[...] = jnp.zeros_like(dsc_ref)

        dyv = dy_ref[...]
        dye = jnp.concatenate([dyv, jnp.where(i == nt - 1, 0.0, dyn_ref[...])], axis=0)
        ue = jnp.concatenate([jnp.where(i == 0, 0.0, uh_ref[...]), u_ref[...]], axis=0)
        pos = i * tm + lax.broadcasted_iota(jnp.int32, (tm, 1), 0)
        pos_e = i * tm + lax.broadcasted_iota(jnp.int32, (n, 1), 0)
        ps = _pool_windows(ue, pos, hp)
        dme = dye * sc_ref[...]
        dus, dscs = [], []
        for g, wd in enumerate(POOL_WINDOWS):
            sl = slice(g * LANES, (g + 1) * LANES)
            dscs.append(jnp.sum(dyv[:, sl] * _dot(ps[g], w_ref[g]), axis=0, keepdims=True))
            dw_ref[g] += _dot_tn(ps[g], dme[:tm, sl])
            dpe = _dot_nt(dme[:, sl], w_ref[g])
            s, span = dpe / jnp.minimum(pos_e + 1, wd).astype(f32), 1
            while span < wd:
                s = s + pltpu.roll(s, n - span, 0)
                span *= 2
            dus.append(s[:tm, :] - dpe[:tm, :])
        du_ref[...] = jnp.concatenate(dus, axis=1).astype(bf16)
        dsc_ref[...] += jnp.concatenate(dscs, axis=1)

    return pl.pallas_call(
        body, name="pool_bwd", grid=(B, nt),
        in_specs=[pl.BlockSpec((None, hp, POOL_DIM), lambda b, i: (b, jnp.minimum((i + 1) * r, T // hp - 1), dcolb)),
                  pl.BlockSpec((None, tm, POOL_DIM), lambda b, i: (b, i, dcolb)),
                  pl.BlockSpec((None, hp, POOL_DIM), lambda b, i: (b, jnp.maximum(i * r - 1, 0), colb)),
                  pl.BlockSpec((None, tm, POOL_DIM), lambda b, i: (b, i, colb)),
                  _const_spec(w.shape), _const_spec((1, POOL_DIM))],
        out_specs=[pl.BlockSpec((None, tm, POOL_DIM), lambda b, i: (b, i, 0)),
                   pl.BlockSpec(w.shape, lambda b, i: (0, 0, 0)),
                   pl.BlockSpec((1, POOL_DIM), lambda b, i: (0, 0))],
        out_shape=[jax.ShapeDtypeStruct((B, T, POOL_DIM), bf16), jax.ShapeDtypeStruct(w.shape, f32),
                   jax.ShapeDtypeStruct((1, POOL_DIM), f32)],
        compiler_params=_cp(),
    )(dycat, dycat, proj, proj, w, scale)


N_PAIRS = SSD_HEADS // 2


def _ssd_chunk(xs, bs, cs, dtp, zs, hs, dtb, alog, dsk, ngs):
    Q = SSD_CHUNK
    lane = lax.broadcasted_iota(jnp.int32, (1, LANES), 1)
    sub = lax.broadcasted_iota(jnp.int32, (LANES, 1), 0)
    causal = lax.broadcasted_iota(jnp.int32, (Q, Q), 0) >= lax.broadcasted_iota(jnp.int32, (Q, Q), 1)
    lane_lo, sub_lo = lane < 64, sub < 64

    def col(v, h):
        return jnp.sum(v * (lane == h).astype(f32), axis=1, keepdims=True)

    def row(vt, h):
        return jnp.sum(vt * (sub == h).astype(f32), axis=0, keepdims=True)

    dt = jax.nn.softplus(dtp + dtb)
    acum = _dot_hi(causal.astype(f32), dt * (-jnp.exp(alog)))
    acum_t = acum.T
    aend = jnp.sum(acum * (sub == Q - 1).astype(f32), axis=0, keepdims=True)
    outs, hn = [], []
    for grp in range(2):
        bv, cv = _silu(bs[grp]), _silu(cs[grp])
        gmat = _dot_nt(cv, bv)
        for j in range(4):
            p = grp * 4 + j
            h0, h1 = 2 * p, 2 * p + 1
            x2 = _silu(xs[p])
            c0, c1 = col(acum, h0), col(acum, h1)
            s2 = jnp.where(lane_lo, c0, c1)
            xdt = x2 * jnp.where(lane_lo, col(dt, h0), col(dt, h1))
            l0 = jnp.where(causal, jnp.exp(jnp.minimum(c0 - row(acum_t, h0), 0.0)), 0.0)
            l1 = jnp.where(causal, jnp.exp(jnp.minimum(c1 - row(acum_t, h1), 0.0)), 0.0)
            yd = _dot(gmat * l0, jnp.where(lane_lo, xdt, 0.0)) + _dot(gmat * l1, jnp.where(lane_lo, 0.0, xdt))
            e0, e1 = col(aend, h0), col(aend, h1)
            st = _dot_tn(xdt * jnp.exp(jnp.where(lane_lo, e0, e1) - s2), bv)
            yo = jnp.exp(s2) * _dot_nt(cv, hs[p])
            hn.append(jnp.exp(jnp.where(sub_lo, e0, e1)) * hs[p] + st)
            yv = yd + yo + x2 * jnp.where(lane_lo, col(dsk, h0), col(dsk, h1))
            outs.append(yv * _silu(zs[p]))
    ms = sum(jnp.sum(o * o, axis=1, keepdims=True) for o in outs) / SSD_D_INNER
    rs = lax.rsqrt(ms + NORM_EPS)
    return [outs[p] * rs * ngs[p] for p in range(N_PAIRS)], hn


def _lane_blocks(ref, n, start=0):
    return [ref[:, (start + k) * LANES:(start + k + 1) * LANES] for k in range(n)]


def _ssd_args(z_ref, cx_ref, dt_ref, dtb_ref, alog_ref, dsk_ref, ng_ref):
    xs = _lane_blocks(cx_ref, 8)
    bs = _lane_blocks(cx_ref, 2, 8)
    cs = _lane_blocks(cx_ref, 2, 10)
    zs = _lane_blocks(z_ref, 8)
    ngs = _lane_blocks(ng_ref, 8)
    return xs, bs, cs, dt_ref[...], zs, dtb_ref[...], alog_ref[...], dsk_ref[...], ngs


DT_COLB = (EVEN_IN_PAD - LANES) // LANES


def ssd_fwd(proj, cx, dtb, alog, dsk, ng):
    B, T, _ = proj.shape
    Q = SSD_CHUNK
    nc = T // Q

    def body(z_ref, cx_ref, dt_ref, dtb_ref, alog_ref, dsk_ref, ng_ref, ya_ref, hsave_ref, h_scr):
        @pl.when(pl.program_id(1) == 0)
        def _():
            h_scr[...] = jnp.zeros_like(h_scr)

        xs, bs, cs, dtp, zs, dtb_v, alog_v, dsk_v, ngs = _ssd_args(z_ref, cx_ref, dt_ref, dtb_ref, alog_ref, dsk_ref, ng_ref)
        hs = [h_scr[p] for p in range(N_PAIRS)]
        for p in range(N_PAIRS):
            hsave_ref[p] = hs[p]
        outs, hn = _ssd_chunk(xs, bs, cs, dtp, zs, hs, dtb_v, alog_v, dsk_v, ngs)
        for p in range(N_PAIRS):
            ya_ref[:, p * LANES:(p + 1) * LANES] = outs[p].astype(bf16)
            h_scr[p] = hn[p]

    return pl.pallas_call(
        body, name="ssd_fwd", grid=(B, nc),
        in_specs=[pl.BlockSpec((None, Q, SSD_D_INNER), lambda b, i: (b, i, 0)),
                  pl.BlockSpec((None, Q, SSD_CONV_DIM), lambda b, i: (b, i, 0)),
                  pl.BlockSpec((None, Q, LANES), lambda b, i: (b, i, DT_COLB)),
                  _const_spec((1, LANES)), _const_spec((1, LANES)), _const_spec((1, LANES)),
                  _const_spec((1, SSD_D_INNER))],
        out_specs=[pl.BlockSpec((None, Q, SSD_D_INNER), lambda b, i: (b, i, 0)),
                   pl.BlockSpec((None, None, N_PAIRS, LANES, LANES), lambda b, i: (b, i, 0, 0, 0))],
        out_shape=[jax.ShapeDtypeStruct((B, T, SSD_D_INNER), bf16),
                   jax.ShapeDtypeStruct((B, nc, N_PAIRS, LANES, LANES), f32)],
        scratch_shapes=[pltpu.VMEM((N_PAIRS, LANES, LANES), f32)],
        compiler_params=_cp(),
    )(proj, cx, proj, dtb, alog, dsk, ng)


def ssd_bwd(dycat, proj, cx, hsave, dtb, alog, dsk, ng):
    B, T, _ = proj.shape
    Q = SSD_CHUNK
    nc = T // Q

    def body(dya_ref, z_ref, cx_ref, dt_ref, hsave_ref, dtb_ref, alog_ref, dsk_ref, ng_ref,
             dz_ref, dcx_ref, ddt_ref, gsm_ref, gng_ref, dh_scr):
        b, i = pl.program_id(0), pl.program_id(1)

        @pl.when((b == 0) & (i == 0))
        def _():
            gsm_ref[...] = jnp.zeros_like(gsm_ref)
            gng_ref[...] = jnp.zeros_like(gng_ref)

        @pl.when(i == 0)
        def _():
            dh_scr[...] = jnp.zeros_like(dh_scr)

        xs, bs, cs, dtp, zs, dtb_v, alog_v, dsk_v, ngs = _ssd_args(z_ref, cx_ref, dt_ref, dtb_ref, alog_ref, dsk_ref, ng_ref)
        hs = [hsave_ref[p] for p in range(N_PAIRS)]
        _, vjp = jax.vjp(_ssd_chunk, xs, bs, cs, dtp, zs, hs, dtb_v, alog_v, dsk_v, ngs)
        douts = _lane_blocks(dya_ref, 8)
        dhn = [dh_scr[p] for p in range(N_PAIRS)]
        dxs, dbs, dcs, ddtp, dzs, dhs, ddtb, dalog, ddsk, dngs = vjp((douts, dhn))
        for p in range(N_PAIRS):
            dcx_ref[:, p * LANES:(p + 1) * LANES] = dxs[p]
            dz_ref[:, p * LANES:(p + 1) * LANES] = dzs[p].astype(bf16)
            dh_scr[p] = dhs[p]
            gng_ref[:, p * LANES:(p + 1) * LANES] += dngs[p]
        for k in range(2):
            dcx_ref[:, (8 + k) * LANES:(9 + k) * LANES] = dbs[k]
            dcx_ref[:, (10 + k) * LANES:(11 + k) * LANES] = dcs[k]
        ddt_ref[...] = ddtp.astype(bf16)
        gsm_ref[0:1, :] += ddtb
        gsm_ref[1:2, :] += dalog
        gsm_ref[2:3, :] += ddsk

    rev = lambda w, cb=0: pl.BlockSpec((None, Q, w), lambda b, i: (b, nc - 1 - i, cb))
    return pl.pallas_call(
        body, name="ssd_bwd", grid=(B, nc),
        in_specs=[rev(SSD_D_INNER), rev(SSD_D_INNER), rev(SSD_CONV_DIM), rev(LANES, DT_COLB),
                  pl.BlockSpec((None, None, N_PAIRS, LANES, LANES), lambda b, i: (b, nc - 1 - i, 0, 0, 0)),
                  _const_spec((1, LANES)), _const_spec((1, LANES)), _const_spec((1, LANES)),
                  _const_spec((1, SSD_D_INNER))],
        out_specs=[rev(SSD_D_INNER), rev(SSD_CONV_DIM), rev(LANES),
                   pl.BlockSpec((3, LANES), lambda b, i: (0, 0)),
                   pl.BlockSpec((1, SSD_D_INNER), lambda b, i: (0, 0))],
        out_shape=[jax.ShapeDtypeStruct((B, T, SSD_D_INNER), bf16), jax.ShapeDtypeStruct((B, T, SSD_CONV_DIM), f32),
                   jax.ShapeDtypeStruct((B, T, LANES), bf16), jax.ShapeDtypeStruct((3, LANES), f32),
                   jax.ShapeDtypeStruct((1, SSD_D_INNER), f32)],
        scratch_shapes=[pltpu.VMEM((N_PAIRS, LANES, LANES), f32)],
        compiler_params=_cp(),
    )(dycat, proj, cx, proj, hsave, dtb, alog, dsk, ng)


def glu_fwd(proj):
    B, T, _ = proj.shape
    tm = _row_tile(T)

    def body(v_ref, g_ref, o_ref):
        o_ref[...] = v_ref[...] * jax.nn.sigmoid(g_ref[...])

    blk = lambda cb: pl.BlockSpec((None, tm, CONF_DIM), lambda b, i: (b, i, cb))
    return pl.pallas_call(body, name="glu_fwd", grid=(B, T // tm), in_specs=[blk(0), blk(1)], out_specs=blk(0),
                          out_shape=jax.ShapeDtypeStruct((B, T, CONF_DIM), f32), compiler_params=_cp())(proj, proj)


def glu_bwd(dhh, proj):
    B, T, _ = proj.shape
    tm = _row_tile(T)

    def body(d_ref, v_ref, g_ref, o_ref):
        sig = jax.nn.sigmoid(g_ref[...])
        dv = d_ref[...]
        o_ref[:, :CONF_DIM] = (dv * sig).astype(bf16)
        o_ref[:, CONF_DIM:] = (dv * v_ref[...] * sig * (1.0 - sig)).astype(bf16)

    blk = lambda cb: pl.BlockSpec((None, tm, CONF_DIM), lambda b, i: (b, i, cb))
    return pl.pallas_call(body, name="glu_bwd", grid=(B, T // tm), in_specs=[blk(0), blk(0), blk(1)],
                          out_specs=pl.BlockSpec((None, tm, 2 * CONF_DIM), lambda b, i: (b, i, 0)),
                          out_shape=jax.ShapeDtypeStruct((B, T, 2 * CONF_DIM), bf16), compiler_params=_cp())(dhh, proj, proj)


def _neg_expm1(x):
    series = x * (1.0 + x * (1.0 / 2.0) * (1.0 + x * (1.0 / 3.0) * (1.0 + x * (1.0 / 4.0) * (1.0 + x * (1.0 / 5.0)))))
    return -jnp.where(x > -0.1, series, jnp.exp(x) - 1.0)


def _lru_gates(cc, xc8, gr8, clg, clb, wa8, wx8, ba8, bx8, lam8):
    xhat, _ = _ln_stats(cc)
    yc = _silu(xhat * clg + clb)
    a8, b8, ge8 = [], [], []
    for hb in range(LRU_HEADS):
        xh = xc8[hb]
        rg = jax.nn.sigmoid(_dot(xh, wa8[hb]) + ba8[hb])
        ig = jax.nn.sigmoid(_dot(xh, wx8[hb]) + bx8[hb])
        log_a = -LRU_C * rg * jax.nn.softplus(-lam8[hb])
        a8.append(jnp.exp(log_a))
        b8.append(jnp.sqrt(_neg_expm1(2.0 * log_a)) * (ig * xh))
        ge8.append(jax.nn.gelu(gr8[hb]))
    return yc, a8, b8, ge8


def _scan_fwd(a, b, h_in):
    tm = a.shape[0]
    rows = lax.broadcasted_iota(jnp.int32, (tm, 1), 0)
    s = 1
    while s < tm:
        keep = rows >= s
        b = a * jnp.where(keep, pltpu.roll(b, s, 0), 0.0) + b
        a = a * jnp.where(keep, pltpu.roll(a, s, 0), 1.0)
        s *= 2
    return a * h_in + b


def _scan_bwd(e, d, g_in):
    tm = e.shape[0]
    rows = lax.broadcasted_iota(jnp.int32, (tm, 1), 0)
    s = 1
    while s < tm:
        keep = rows < tm - s
        d = e * jnp.where(keep, pltpu.roll(d, tm - s, 0), 0.0) + d
        e = e * jnp.where(keep, pltpu.roll(e, tm - s, 0), 1.0)
        s *= 2
    return e * g_in + d


def _lru_params(wa_ref, wx_ref, vec_ref):
    wa8 = [wa_ref[h] for h in range(LRU_HEADS)]
    wx8 = [wx_ref[h] for h in range(LRU_HEADS)]
    ba8 = [vec_ref[0:1, h * LANES:(h + 1) * LANES] for h in range(LRU_HEADS)]
    bx8 = [vec_ref[1:2, h * LANES:(h + 1) * LANES] for h in range(LRU_HEADS)]
    lam8 = [vec_ref[2:3, h * LANES:(h + 1) * LANES] for h in range(LRU_HEADS)]
    return wa8, wx8, ba8, bx8, lam8


GR_COLB = 2


def lru_fwd(cc, xc, proj, cln, wa, wx, vec):
    B, T, _ = xc.shape
    tm = _row_tile(T)

    def body(cc_ref, xc_ref, gr_ref, cln_ref, wa_ref, wx_ref, vec_ref, y_ref, hs_ref, h_scr):
        @pl.when(pl.program_id(1) == 0)
        def _():
            h_scr[...] = jnp.zeros_like(h_scr)

        yc, a8, b8, ge8 = _lru_gates(cc_ref[...], _lane_blocks(xc_ref, 8), _lane_blocks(gr_ref, 8), cln_ref[0:1, :],
                                     cln_ref[1:2, :], *_lru_params(wa_ref, wx_ref, vec_ref))
        h = _scan_fwd(jnp.concatenate(a8, axis=1), jnp.concatenate(b8, axis=1), h_scr[...])
        hs_ref[...] = h
        h_scr[...] = h[tm - 1:tm, :]
        y_ref[:, :CONF_DIM] = yc.astype(bf16)
        y_ref[:, CONF_DIM:] = (h * jnp.concatenate(ge8, axis=1)).astype(bf16)

    row = lambda w, cb=0: pl.BlockSpec((None, tm, w), lambda b, i: (b, i, cb))
    return pl.pallas_call(
        body, name="lru_fwd", grid=(B, T // tm),
        in_specs=[row(CONF_DIM), row(LRU_DIM), row(LRU_DIM, GR_COLB), _const_spec((2, CONF_DIM)),
                  _const_spec(wa.shape), _const_spec(wx.shape), _const_spec((3, LRU_DIM))],
        out_specs=[row(CONF_DIM + LRU_DIM), row(LRU_DIM)],
        out_shape=[jax.ShapeDtypeStruct((B, T, CONF_DIM + LRU_DIM), bf16), jax.ShapeDtypeStruct((B, T, LRU_DIM), f32)],
        scratch_shapes=[pltpu.VMEM((1, LRU_DIM), f32)],
        compiler_params=_cp(),
    )(cc, xc, proj, cln, wa, wx, vec)


def lru_bwd(dycat, cc, xc, proj, hs, cln, wa, wx, vec):
    B, T, _ = xc.shape
    tm = _row_tile(T)
    nt = T // tm
    r = tm // 8

    def body(dy_ref, cc_ref, xc_ref, gr_ref, hs_ref, hsh_ref, cln_ref, wa_ref, wx_ref, vec_ref,
             dcc_ref, dxc_ref, dgr_ref, dcln_ref, dwa_ref, dwx_ref, dvec_ref, g_scr, a_scr):
        b, i = pl.program_id(0), pl.program_id(1)
        it = nt - 1 - i

        @pl.when((b == 0) & (i == 0))
        def _():
            dcln_ref[...] = jnp.zeros_like(dcln_ref)
            dwa_ref[...] = jnp.zeros_like(dwa_ref)
            dwx_ref[...] = jnp.zeros_like(dwx_ref)
            dvec_ref[...] = jnp.zeros_like(dvec_ref)

        @pl.when(i == 0)
        def _():
            g_scr[...] = jnp.zeros_like(g_scr)
            a_scr[...] = jnp.zeros_like(a_scr)

        (yc, a8, b8, ge8), vjp = jax.vjp(_lru_gates, cc_ref[...], _lane_blocks(xc_ref, 8), _lane_blocks(gr_ref, 8),
                                         cln_ref[0:1, :], cln_ref[1:2, :], *_lru_params(wa_ref, wx_ref, vec_ref))
        a = jnp.concatenate(a8, axis=1)
        ge = jnp.concatenate(ge8, axis=1)
        h = hs_ref[...]
        dyd = dy_ref[:, CONF_DIM:]
        rows = lax.broadcasted_iota(jnp.int32, (tm, 1), 0)
        e = jnp.where(rows < tm - 1, pltpu.roll(a, tm - 1, 0), a_scr[...])
        g = _scan_bwd(e, dyd * ge, g_scr[...])
        h_first = jnp.where(it == 0, 0.0, hsh_ref[7:8, :])
        h_prev = jnp.where(rows >= 1, pltpu.roll(h, 1, 0), h_first)
        da = g * h_prev
        g_scr[...] = g[0:1, :]
        a_scr[...] = a[0:1, :]
        split = lambda v: [v[:, k * LANES:(k + 1) * LANES] for k in range(LRU_HEADS)]
        dcc, dxc8, dgr8, dclg, dclb, dwa8, dwx8, dba8, dbx8, dlam8 = vjp((dy_ref[:, :CONF_DIM], split(da), split(g), split(dyd * h)))
        dcc_ref[...] = dcc
        dcln_ref[0:1, :] += dclg
        dcln_ref[1:2, :] += dclb
        for k in range(LRU_HEADS):
            sl = slice(k * LANES, (k + 1) * LANES)
            dxc_ref[:, sl] = dxc8[k]
            dgr_ref[:, sl] = dgr8[k].astype(bf16)
            dwa_ref[k] += dwa8[k]
            dwx_ref[k] += dwx8[k]
            dvec_ref[0:1, sl] += dba8[k]
            dvec_ref[1:2, sl] += dbx8[k]
            dvec_ref[2:3, sl] += dlam8[k]

    rev = lambda w, cb=0: pl.BlockSpec((None, tm, w), lambda b, i: (b, nt - 1 - i, cb))
    acc = lambda shape: pl.BlockSpec(shape, lambda b, i: (0,) * len(shape))
    return pl.pallas_call(
        body, name="lru_bwd", grid=(B, nt),
        in_specs=[rev(CONF_DIM + LRU_DIM), rev(CONF_DIM), rev(LRU_DIM), rev(LRU_DIM, GR_COLB), rev(LRU_DIM),
                  pl.BlockSpec((None, 8, LRU_DIM), lambda b, i: (b, jnp.maximum((nt - 1 - i) * r - 1, 0), 0)),
                  _const_spec((2, CONF_DIM)), _const_spec(wa.shape), _const_spec(wx.shape), _const_spec((3, LRU_DIM))],
        out_specs=[rev(CONF_DIM), rev(LRU_DIM), rev(LRU_DIM), acc((2, CONF_DIM)), acc(wa.shape), acc(wx.shape),
                   acc((3, LRU_DIM))],
        out_shape=[jax.ShapeDtypeStruct((B, T, CONF_DIM), f32), jax.ShapeDtypeStruct((B, T, LRU_DIM), f32),
                   jax.ShapeDtypeStruct((B, T, LRU_DIM), bf16), jax.ShapeDtypeStruct((2, CONF_DIM), f32),
                   jax.ShapeDtypeStruct(wa.shape, f32), jax.ShapeDtypeStruct(wx.shape, f32),
                   jax.ShapeDtypeStruct((3, LRU_DIM), f32)],
        scratch_shapes=[pltpu.VMEM((1, LRU_DIM), f32), pltpu.VMEM((1, LRU_DIM), f32)],
        compiler_params=_cp(),
    )(dycat, cc, xc, proj, hs, hs, cln, wa, wx, vec)


def loss_fwd(y, target):
    B, T, D = y.shape
    tm = _row_tile(T)

    def body(y_ref, t_ref, l_ref, dy_ref):
        @pl.when((pl.program_id(0) == 0) & (pl.program_id(1) == 0))
        def _():
            l_ref[...] = jnp.zeros_like(l_ref)

        d = y_ref[...] - t_ref[...]
        dy_ref[...] = d * (1.0 / D)
        l_ref[...] += jnp.sum(jnp.sum(d * d, axis=1, keepdims=True), axis=0, keepdims=True)

    row = pl.BlockSpec((None, tm, D), lambda b, i: (b, i, 0))
    return pl.pallas_call(
        body, name="loss_fwd", grid=(B, T // tm), in_specs=[row, row],
        out_specs=[pl.BlockSpec((1, 1), lambda b, i: (0, 0)), row],
        out_shape=[jax.ShapeDtypeStruct((1, 1), f32), jax.ShapeDtypeStruct((B, T, D), f32)],
        compiler_params=_cp(),
    )(y, target)


ADA_COL_TILE = 768


def ada_fwd(c_all, w, b):
    L, D, N = w.shape
    nb = c_all.shape[0]
    tn = ADA_COL_TILE

    def body(c_ref, w_ref, b_ref, o_ref):
        o_ref[...] = _dot_hi(_silu(c_ref[...]), w_ref[...]) + b_ref[...]

    return pl.pallas_call(
        body, name="ada_fwd", grid=(L, N // tn),
        in_specs=[pl.BlockSpec((nb, D), lambda l, j: (0, 0)), pl.BlockSpec((None, D, tn), lambda l, j: (l, 0, j)),
                  pl.BlockSpec((None, 1, tn), lambda l, j: (l, 0, j))],
        out_specs=pl.BlockSpec((None, nb, tn), lambda l, j: (l, 0, j)),
        out_shape=jax.ShapeDtypeStruct((L, nb, N), f32),
        compiler_params=_cp(),
    )(c_all, w, b)


def ada_bwd(c_all, dmod_my, dmod_all):
    L, nb, N = dmod_my.shape
    D = c_all.shape[1]
    NA = dmod_all.shape[2]
    tn = ADA_COL_TILE
    nj = N // tn
    ta = NA // nj

    def body(c_ref, dm_ref, da_ref, gw_ref, gb_ref):
        gw_ref[...] = lax.dot_general(_silu(c_ref[...]), dm_ref[...], (((0,), (0,)), ((), ())),
                                      preferred_element_type=f32, precision=lax.Precision.HIGHEST)
        gb_ref[...] = jnp.sum(da_ref[...], axis=0, keepdims=True)

    return pl.pallas_call(
        body, name="ada_bwd", grid=(L, nj),
        in_specs=[pl.BlockSpec((nb, D), lambda l, j: (0, 0)), pl.BlockSpec((None, nb, tn), lambda l, j: (l, 0, j)),
                  pl.BlockSpec((None, nb, ta), lambda l, j: (l, 0, j))],
        out_specs=[pl.BlockSpec((None, D, tn), lambda l, j: (l, 0, j)), pl.BlockSpec((None, 1, ta), lambda l, j: (l, 0, j))],
        out_shape=[jax.ShapeDtypeStruct((L, D, N), f32), jax.ShapeDtypeStruct((L, 1, NA), f32)],
        compiler_params=_cp(),
    )(c_all, dmod_my, dmod_all)


def adamw(w, m, v, gs, offs, tr):
    R, C = w.shape
    ng = len(gs)
    c1 = 1.0 - ADAM_B1 ** ADAM_STEP
    c2 = 1.0 - ADAM_B2 ** ADAM_STEP

    def body(*refs):
        w_ref, m_ref, v_ref = refs[:3]
        g_refs = refs[3:3 + ng]
        d_out, m_out, v_out = refs[3 + ng:]
        g = g_refs[0][...]
        for r in g_refs[1:]:
            g = g + r[...]
        mn = ADAM_B1 * m_ref[...] + (1.0 - ADAM_B1) * g
        vn = ADAM_B2 * v_ref[...] + (1.0 - ADAM_B2) * (g * g)
        m_out[...] = mn
        v_out[...] = vn
        d_out[...] = -ADAM_LR * ((mn / c1) / (jnp.sqrt(vn / c2) + ADAM_EPS) + ADAM_WD * w_ref[...])

    blk = pl.BlockSpec((tr, C), lambda i: (i, 0))
    gspec = lambda off: pl.BlockSpec((tr, C), lambda i: (i + off // tr, 0))
    sds = jax.ShapeDtypeStruct((R, C), f32)
    return pl.pallas_call(
        body, name="adamw", grid=(pl.cdiv(R, tr),), in_specs=[blk, blk, blk] + [gspec(o) for o in offs],
        out_specs=[blk] * 3, out_shape=[sds] * 3, compiler_params=_cp(),
    )(w, m, v, *gs)


def sum_leading(a, tr):
    k, R, C = a.shape

    def body(a_ref, o_ref):
        s = a_ref[0]
        for j in range(1, k):
            s = s + a_ref[j]
        o_ref[...] = s

    return pl.pallas_call(
        body, name="sum_leading", grid=(R // tr,), in_specs=[pl.BlockSpec((k, tr, C), lambda i: (0, i, 0))],
        out_specs=pl.BlockSpec((tr, C), lambda i: (i, 0)), out_shape=jax.ShapeDtypeStruct((R, C), a.dtype),
        compiler_params=_cp(),
    )(a)


ANY = pl.BlockSpec(memory_space=pl.ANY)
CHIP_FLIPS = ((1, 0), (0, 1), (1, 1))
DEV_FLIPS = tuple((fx, fy, fc) for fx in (0, 1) for fy in (0, 1) for fc in (0, 1))[1:]


def _flip(v, f):
    return 1 - v if f else v


def _put(out, v, idx, axis=0):
    return lax.dynamic_update_slice_in_dim(out, jnp.expand_dims(v, axis) if v.ndim < out.ndim else v, idx, axis)


def allgather8(v):
    R, C = v.shape
    n = len(DEV_FLIPS)

    def body(v_ref, o_ref, send_sems, recv_sems):
        x, y, c = lax.axis_index("x"), lax.axis_index("y"), lax.axis_index("c")
        me = 4 * x + 2 * y + c
        peers = [(_flip(x, fx), _flip(y, fy), _flip(c, fc)) for fx, fy, fc in DEV_FLIPS]

        def copy(k, slot):
            return pltpu.make_async_remote_copy(src_ref=v_ref, dst_ref=o_ref.at[slot], send_sem=send_sems.at[k],
                                                recv_sem=recv_sems.at[k], device_id=peers[k], device_id_type=MESH)

        for k in range(n):
            copy(k, me).start()
        for k, (px, py, pc) in enumerate(peers):
            copy(k, 4 * px + 2 * py + pc).wait_recv()
        for k in range(n):
            copy(k, me).wait_send()

    out = pl.pallas_call(
        body, name="allgather8", in_specs=[ANY], out_specs=ANY, out_shape=jax.ShapeDtypeStruct((N_DEV, R, C), v.dtype),
        scratch_shapes=[pltpu.SemaphoreType.DMA((n,)), pltpu.SemaphoreType.DMA((n,))],
    )(v)
    return _put(out, v, 4 * lax.axis_index("x") + 2 * lax.axis_index("y") + lax.axis_index("c"))


def _half(ref_or_shape0, c):
    hsz = ref_or_shape0 // 2
    return pl.ds(c * hsz, hsz)


def gather_weights(ws):
    nw, nc = len(ws), len(CHIP_FLIPS)

    def body(*refs):
        w_refs, o_refs = refs[:nw], refs[nw:2 * nw]
        send_sems, recv_sems = refs[2 * nw:]
        x, y, c = lax.axis_index("x"), lax.axis_index("y"), lax.axis_index("c")
        me, sibling = 2 * x + y, (x, y, 1 - c)
        peers = [(_flip(x, fx), _flip(y, fy), c) for fx, fy in CHIP_FLIPS]
        slots = [2 * px + py for px, py, _ in peers]

        def copy(a, j, slot, half, to, own=False):
            hs = _half(ws[a].shape[0], half)
            return pltpu.make_async_remote_copy(src_ref=w_refs[a].at[hs] if own else o_refs[a].at[slot, hs],
                                                dst_ref=o_refs[a].at[slot, hs], send_sem=send_sems.at[j],
                                                recv_sem=recv_sems.at[j], device_id=to, device_id_type=MESH)

        first = [[copy(a, a * nc + k, me, c, peers[k], own=True) for k in range(nc)] for a in range(nw)]
        for a in range(nw):
            for cp in first[a]:
                cp.start()
        passed = []
        for a in range(nw):
            for k in range(nc):
                copy(a, a * nc + k, slots[k], c, peers[k]).wait_recv()
                passed.append(copy(a, nw * nc + a * nc + k, slots[k], c, sibling))
                passed[-1].start()
        for a in range(nw):
            for k in range(nc):
                copy(a, nw * nc + a * nc + k, slots[k], 1 - c, sibling).wait_recv()
        for a in range(nw):
            for cp in first[a]:
                cp.wait_send()
        for cp in passed:
            cp.wait_send()

    outs = pl.pallas_call(
        body, name="gather_weights", in_specs=[ANY] * nw, out_specs=[ANY] * nw,
        out_shape=[jax.ShapeDtypeStruct((N_CHIPS,) + w.shape, w.dtype) for w in ws],
        scratch_shapes=[pltpu.SemaphoreType.DMA((2 * nw * nc,)), pltpu.SemaphoreType.DMA((2 * nw * nc,))],
    )(*ws)
    chip = 2 * lax.axis_index("x") + lax.axis_index("y")
    return [_put(o, w, chip) for o, w in zip(outs, ws)]


def swap_halves(gs):
    nw = len(gs)

    def body(*refs):
        g_refs, t_refs = refs[:nw], refs[nw:2 * nw]
        send_sems, recv_sems = refs[2 * nw:]
        x, y, c = lax.axis_index("x"), lax.axis_index("y"), lax.axis_index("c")
        cps = [pltpu.make_async_remote_copy(src_ref=g_refs[a].at[_half(gs[a].shape[0], 1 - c)], dst_ref=t_refs[a],
                                            send_sem=send_sems.at[a], recv_sem=recv_sems.at[a],
                                            device_id=(x, y, 1 - c), device_id_type=MESH) for a in range(nw)]
        for cp in cps:
            cp.start()
        for cp in cps:
            cp.wait()

    return pl.pallas_call(
        body, name="swap_halves", in_specs=[ANY] * nw, out_specs=[ANY] * nw,
        out_shape=[jax.ShapeDtypeStruct((g.shape[0] // 2,) + g.shape[1:], g.dtype) for g in gs],
        scratch_shapes=[pltpu.SemaphoreType.DMA((nw,)), pltpu.SemaphoreType.DMA((nw,))],
    )(*gs)


def exchange_blocks(ps):
    nw, nc = len(ps), len(CHIP_FLIPS)

    def body(*refs):
        p_refs, r_refs = refs[:nw], refs[nw:2 * nw]
        send_sems, recv_sems = refs[2 * nw:]
        x, y, c = lax.axis_index("x"), lax.axis_index("y"), lax.axis_index("c")
        me = 2 * x + y
        peers = [(_flip(x, fx), _flip(y, fy), c) for fx, fy in CHIP_FLIPS]
        slots = [2 * px + py for px, py, _ in peers]

        def copy(a, k, src_slot, dst_slot):
            return pltpu.make_async_remote_copy(src_ref=p_refs[a].at[src_slot], dst_ref=r_refs[a].at[dst_slot],
                                                send_sem=send_sems.at[a * nc + k], recv_sem=recv_sems.at[a * nc + k],
                                                device_id=peers[k], device_id_type=MESH)

        for a in range(nw):
            for k in range(nc):
                copy(a, k, slots[k], me).start()
        for a in range(nw):
            for k in range(nc):
                copy(a, k, me, slots[k]).wait_recv()
        for a in range(nw):
            for k in range(nc):
                copy(a, k, slots[k], me).wait_send()

    outs = pl.pallas_call(
        body, name="exchange_blocks", in_specs=[ANY] * nw, out_specs=[ANY] * nw,
        out_shape=[jax.ShapeDtypeStruct(p.shape, p.dtype) for p in ps],
        scratch_shapes=[pltpu.SemaphoreType.DMA((nw * nc,)), pltpu.SemaphoreType.DMA((nw * nc,))],
    )(*ps)
    chip = 2 * lax.axis_index("x") + lax.axis_index("y")
    return [_put(o, lax.dynamic_slice_in_dim(p, chip, 1, axis=0), chip) for o, p in zip(outs, ps)]


def join_halves(ss):
    nw = len(ss)

    def body(*refs):
        s_refs, o_refs = refs[:nw], refs[nw:2 * nw]
        send_sems, recv_sems = refs[2 * nw:]
        x, y, c = lax.axis_index("x"), lax.axis_index("y"), lax.axis_index("c")

        def copy(a, half):
            hs = _half(2 * ss[a].shape[0], half)
            return pltpu.make_async_remote_copy(src_ref=s_refs[a], dst_ref=o_refs[a].at[hs], send_sem=send_sems.at[a],
                                                recv_sem=recv_sems.at[a], device_id=(x, y, 1 - c), device_id_type=MESH)

        for a in range(nw):
            copy(a, c).start()
        for a in range(nw):
            copy(a, 1 - c).wait_recv()
        for a in range(nw):
            copy(a, c).wait_send()

    outs = pl.pallas_call(
        body, name="join_halves", in_specs=[ANY] * nw, out_specs=[ANY] * nw,
        out_shape=[jax.ShapeDtypeStruct((2 * s.shape[0],) + s.shape[1:], s.dtype) for s in ss],
        scratch_shapes=[pltpu.SemaphoreType.DMA((nw,)), pltpu.SemaphoreType.DMA((nw,))],
    )(*ss)
    c = lax.axis_index("c")
    return [_put(o, s, c * s.shape[0]) for o, s in zip(outs, ss)]


def _tile_rows(a, b, itemsize=4, budget=4 * 2 ** 20):
    best = 8
    for t in range(8, a + 1, 8):
        if a % t == 0 and t * b * itemsize <= budget:
            best = t
    return best


def add_half(g, t, cidx):
    n0, _, A, B = g.shape
    hsz = n0 // 2
    ta = _tile_rows(A, B)

    def body(c_ref, g_ref, t_ref, o_ref):
        o_ref[...] = (g_ref[...] + t_ref[...]).astype(bf16)

    return pl.pallas_call(
        body, name="add_half",
        grid_spec=pltpu.PrefetchScalarGridSpec(
            num_scalar_prefetch=1, grid=(hsz, N_CHIPS, A // ta),
            in_specs=[pl.BlockSpec((None, None, ta, B), lambda h, k, i, c_ref: (c_ref[0] * hsz + h, k, i, 0)),
                      pl.BlockSpec((None, None, ta, B), lambda h, k, i, c_ref: (h, k, i, 0))],
            out_specs=pl.BlockSpec((None, None, ta, B), lambda h, k, i, c_ref: (k, h, i, 0))),
        out_shape=jax.ShapeDtypeStruct((N_CHIPS, hsz, A, B), bf16),
        compiler_params=_cp(),
    )(cidx, g, t)


def sum_chips(r):
    _, h, A, B = r.shape
    ta = _tile_rows(A, B)

    def body(r_ref, o_ref):
        s = r_ref[0].astype(f32)
        for j in range(1, N_CHIPS):
            s = s + r_ref[j].astype(f32)
        o_ref[...] = s

    return pl.pallas_call(
        body, name="sum_chips", grid=(h, A // ta),
        in_specs=[pl.BlockSpec((N_CHIPS, None, ta, B), lambda hh, i: (0, hh, i, 0))],
        out_specs=pl.BlockSpec((None, ta, B), lambda hh, i: (hh, i, 0)),
        out_shape=jax.ShapeDtypeStruct((h, A, B), f32),
        compiler_params=_cp(),
    )(r)


WEIGHTS = ('ada_w', 'ada_b', 'ln_g', 'ln_b', 'ffn_w_in', 'ffn_w_out', 'ev_w_in', 'ssd_conv_w', 'ssd_conv_b',
           'ssd_dt_bias', 'ssd_a_log', 'ssd_d', 'ssd_norm_g', 'pool_w', 'pool_scale', 'ev_w_out', 'od_w_in',
           'conf_dw_w', 'conf_dw_b', 'conf_ln_g', 'conf_ln_b', 'lru_conv_w', 'lru_conv_b', 'lru_wa', 'lru_ba',
           'lru_wx', 'lru_bx', 'lru_lambda', 'od_w_out')
BIG =('ffn_w_in', 'ffn_w_out', 'ev_w_out', 'od_w_in', 'od_w_out', 'ev_w_in')
SMALL_SHARDED = ('ln_g', 'ln_b', 'ssd_conv_w', 'conf_dw_w', 'conf_dw_b', 'conf_ln_g', 'conf_ln_b', 'lru_conv_w',
                 'lru_conv_b', 'lru_ba', 'lru_bx', 'lru_lambda')
SMALL_REPLICATED = ('ssd_conv_b', 'ssd_dt_bias', 'ssd_a_log', 'ssd_d', 'ssd_norm_g', 'pool_w', 'pool_scale',
                    'lru_wa', 'lru_wx')
PACK_COLS = 1024
BIG_ROW_TILE = 256


def _pack(arrs, row_mult):
    flat = jnp.concatenate([a.reshape(-1) for a in arrs])
    rows = -(-flat.shape[0] // (PACK_COLS * row_mult)) * row_mult
    return jnp.pad(flat, (0, rows * PACK_COLS - flat.shape[0])).reshape(rows, PACK_COLS)


def _unpack(flat, shapes):
    out, off = [], 0
    for s in shapes:
        n = 1
        for d in s:
            n *= d
        out.append(flat[off:off + n].reshape(s))
        off += n
    return out


def _unshard_last(g4):
    m = jnp.moveaxis(g4, 0, -2)
    return m.reshape(m.shape[:-2] + (m.shape[-2] * m.shape[-1],))


def _pad_rows(a, rows):
    return jnp.pad(a, ((0, rows - a.shape[0]),) + ((0, 0),) * (a.ndim - 1))


def _pad_lanes(a):
    return jnp.pad(a, ((0, 0), (0, LANES - a.shape[1])))


def kernel(x, c, ada_w, ada_b, ln_g, ln_b, ffn_w_in, ffn_w_out, ev_w_in, ssd_conv_w, ssd_conv_b, ssd_dt_bias, ssd_a_log, ssd_d, ssd_norm_g, pool_w, pool_scale, ev_w_out, od_w_in, conf_dw_w, conf_dw_b, conf_ln_g, conf_ln_b, lru_conv_w, lru_conv_b, lru_wa, lru_ba, lru_wx, lru_bx, lru_lambda, od_w_out, loss_target, m_ada_w, m_ada_b, m_ln_g, m_ln_b, m_ffn_w_in, m_ffn_w_out, m_ev_w_in, m_ssd_conv_w, m_ssd_conv_b, m_ssd_dt_bias, m_ssd_a_log, m_ssd_d, m_ssd_norm_g, m_pool_w, m_pool_scale, m_ev_w_out, m_od_w_in, m_conf_dw_w, m_conf_dw_b, m_conf_ln_g, m_conf_ln_b, m_lru_conv_w, m_lru_conv_b, m_lru_wa, m_lru_ba, m_lru_wx, m_lru_bx, m_lru_lambda, m_od_w_out, v_ada_w, v_ada_b, v_ln_g, v_ln_b, v_ffn_w_in, v_ffn_w_out, v_ev_w_in, v_ssd_conv_w, v_ssd_conv_b, v_ssd_dt_bias, v_ssd_a_log, v_ssd_d, v_ssd_norm_g, v_pool_w, v_pool_scale, v_ev_w_out, v_od_w_in, v_conf_dw_w, v_conf_dw_b, v_conf_ln_g, v_conf_ln_b, v_lru_conv_w, v_lru_conv_b, v_lru_wa, v_lru_ba, v_lru_wx, v_lru_bx, v_lru_lambda, v_od_w_out):
    given = dict(locals())
    W = {n: given[n] for n in WEIGHTS}
    M = {n: given["m_" + n] for n in WEIGHTS}
    V = {n: given["v_" + n] for n in WEIGHTS}
    B, T, D = x.shape
    L = DEPTH
    chip = 2 * lax.axis_index("x") + lax.axis_index("y")
    dev = 2 * chip + lax.axis_index("c")

    g1 = allgather8(_pack([c] + [W[n] for n in SMALL_SHARDED], 8)).reshape(N_DEV, -1)
    c_all = g1[:, :B * D].reshape(N_DEV * B, D)
    per_chip = g1[0::2, B * D:]
    full = dict(zip(SMALL_SHARDED, [_unshard_last(jnp.stack(p)) for p in zip(*[
        _unpack(per_chip[k], [W[n].shape for n in SMALL_SHARDED]) for k in range(N_CHIPS)])]))
    for n in SMALL_REPLICATED:
        full[n] = W[n]

    n_ada = ada_w.shape[2]
    ada_b_cols = lax.dynamic_slice_in_dim(ada_b, chip * n_ada, n_ada, axis=1)[:, None, :]
    mod_cols = ada_fwd(c_all, ada_w, ada_b_cols)
    g2 = allgather8(mod_cols.reshape(-1, PACK_COLS))[0::2].reshape(N_CHIPS, L, N_DEV * B, n_ada)
    mod_all = jnp.moveaxis(g2, 0, 2).reshape(L, N_DEV * B, N_CHIPS * n_ada)
    mod = lax.dynamic_slice_in_dim(mod_all, dev * B, B, axis=1).reshape(L, B, N_MOD, D)

    gathered = dict(zip(BIG, gather_weights([W[n].astype(bf16) for n in BIG])))
    FS = ffn_w_in.shape[3]

    def even_w_in(e):
        w = _unshard_last(gathered['ev_w_in'][:, e])
        return jnp.concatenate([w[:, :2560], w[:, 2576:], w[:, 2560:2576], jnp.zeros((D, EVEN_IN_PAD - EVEN_IN), bf16)], axis=1)

    saved = []
    xs = x
    for l in range(L):
        lg, lb = full['ln_g'][l], full['ln_b'][l]
        rec = {}
        w_in_a, w_out_a = gathered['ffn_w_in'][:, l, 0], gathered['ffn_w_out'][:, l, 0].reshape(-1, D)
        w_in_b, w_out_b = gathered['ffn_w_in'][:, l, 1], gathered['ffn_w_out'][:, l, 1].reshape(-1, D)
        m1, m2, m3 = mod[l][:, 0:3], mod[l][:, 3:6], mod[l][:, 6:9]
        xn, h, gu, a, y = ffn_fwd(xs, m1, w_in_a, w_out_a, lg[0:1], lb[0:1])
        rec['ffa'] = (xs, h, gu, a, y, m1, w_in_a, w_out_a, lg[0:1])
        xs = xn
        if l % 2 == 0:
            e = l // 2
            w_in_m, w_out_m = even_w_in(e), gathered['ev_w_out'][:, e].reshape(-1, D)
            cw = _pad_rows(full['ssd_conv_w'][e], 8)
            cb = full['ssd_conv_b'][e][None]
            dtb, alog, dsk = (_pad_lanes(full[n][e][None]) for n in ('ssd_dt_bias', 'ssd_a_log', 'ssd_d'))
            ng, pw, ps = full['ssd_norm_g'][e][None], full['pool_w'][e], full['pool_scale'][e][None]
            proj, hm = inproj_fwd(xs, m2, w_in_m)
            cx = dwconv_fwd(proj, 2, SSD_CONV_DIM, cw, cb, SSD_CONV)
            ya, hsave = ssd_fwd(proj, cx, dtb, alog, dsk, ng)
            yb = pool_fwd(proj, 5, pw, ps)
            ycat = jnp.concatenate([ya, yb], axis=-1)
            rec['mix'] = (proj, cx, hsave, cw, dtb, alog, dsk, ng, pw, ps)
        else:
            o = l // 2
            w_in_m, w_out_m = jnp.moveaxis(gathered['od_w_in'][:, o], 0, 1).reshape(D, -1), gathered['od_w_out'][:, o].reshape(-1, D)
            dww = _pad_rows(full['conf_dw_w'][o], 32)
            dwb = full['conf_dw_b'][o][None]
            cw = _pad_rows(full['lru_conv_w'][o], 8)
            cb = full['lru_conv_b'][o][None]
            cln = jnp.stack([full['conf_ln_g'][o], full['conf_ln_b'][o]])
            vec = jnp.stack([full['lru_ba'][o], full['lru_bx'][o], full['lru_lambda'][o]])
            wa, wx = full['lru_wa'][o], full['lru_wx'][o]
            proj, hm = inproj_fwd(xs, m2, w_in_m)
            hh = glu_fwd(proj)
            cc = dwconv_fwd(hh, 0, CONF_DIM, dww, dwb, CONF_KERNEL)
            xc = dwconv_fwd(proj, 2, LRU_DIM, cw, cb, LRU_CONV)
            ycat, hst = lru_fwd(cc, xc, proj, cln, wa, wx, vec)
            rec['mix'] = (proj, hh, cc, xc, hst, dww, cw, cln, wa, wx, vec)
        xn, ym = outproj_fwd(ycat, w_out_m, xs, m2, lg[1:2], lb[1:2])
        rec['mixio'] = (xs, hm, ycat, ym, m2, w_in_m, w_out_m, lg[1:2])
        xs = xn
        xn, h, gu, a, y = ffn_fwd(xs, m3, w_in_b, w_out_b, lg[2:3], lb[2:3])
        rec['ffb'] = (xs, h, gu, a, y, m3, w_in_b, w_out_b, lg[2:3])
        xs = xn
        saved.append(rec)

    sq, dxs = loss_fwd(xs, loss_target)
    loss = lax.psum(sq[0, 0], ("x", "y", "c")) * (0.5 / D)

    gpart = {n: [None] * W[n].shape[0] for n in WEIGHTS}
    gpart['ffn_w_in'] = [[None, None] for _ in range(L)]
    gpart['ffn_w_out'] = [[None, None] for _ in range(L)]
    gpart['ln_g'] = [[None] * 3 for _ in range(L)]
    gpart['ln_b'] = [[None] * 3 for _ in range(L)]
    dmod = [None] * L

    def ffn_back(dxn, rec, l, f):
        xin, h, gu, a, y, m3_, w_in_, w_out_, lg_ = rec
        dx, dgu, dy, dm3, dln = ffn_bwd(dxn, xin, y, gu, m3_, w_in_, w_out_, lg_)
        gpart['ffn_w_in'][l][f] = wgrad(h, dgu, FS)
        gpart['ffn_w_out'][l][f] = wgrad(a, dy, D)[0].reshape(N_CHIPS, -1, D)
        gpart['ln_g'][l][2 * f] = dln[0]
        gpart['ln_b'][l][2 * f] = dln[1]
        return dx, dm3

    for l in reversed(range(L)):
        rec = saved[l]
        dxs, dm3 = ffn_back(dxs, rec['ffb'], l, 1)
        xin, hm, ycat, ym, m2, w_in_m, w_out_m, lg_ = rec['mixio']
        dxp, dycat, dy, dg2, dln = outproj_bwd(dxs, xin, ym, m2, w_out_m, lg_)
        gpart['ln_g'][l][1] = dln[0]
        gpart['ln_b'][l][1] = dln[1]
        gw_out = wgrad(ycat, dy, D)[0].reshape(N_CHIPS, -1, D)
        if l % 2 == 0:
            e = l // 2
            proj, cx, hsave, cw, dtb, alog, dsk, ng, pw, ps = rec['mix']
            dz, dcx, ddt, gsm, gng = ssd_bwd(dycat, proj, cx, hsave, dtb, alog, dsk, ng)
            dxbc, dcw, dcb = dwconv_bwd(dcx, proj, 2, SSD_CONV_DIM, cw, SSD_CONV, bf16)
            du, dpw, dps = pool_bwd(dycat, 2, proj, 5, pw, ps)
            dproj = jnp.concatenate([dz, dxbc, du, ddt], axis=-1)
            gwp = wgrad(hm, dproj, EVEN_IN_PAD)[0]
            gw = jnp.concatenate([gwp[:, :2560], gwp[:, 3072:3072 + 16], gwp[:, 2560:3072]], axis=1)
            gpart['ev_w_in'][e] = jnp.moveaxis(gw.reshape(D, N_CHIPS, -1), 1, 0)
            gpart['ev_w_out'][e] = gw_out
            gpart['ssd_conv_w'][e], gpart['ssd_conv_b'][e] = dcw[:SSD_CONV], dcb[0]
            gpart['ssd_dt_bias'][e], gpart['ssd_a_log'][e], gpart['ssd_d'][e] = (gsm[k, :SSD_HEADS] for k in range(3))
            gpart['ssd_norm_g'][e], gpart['pool_w'][e], gpart['pool_scale'][e] = gng[0], dpw, dps[0]
        else:
            o = l // 2
            proj, hh, cc, xc, hst, dww, cw, cln, wa, wx, vec = rec['mix']
            dcc, dxc, dgr, dcln, dwa, dwx, dvec = lru_bwd(dycat, cc, xc, proj, hst, cln, wa, wx, vec)
            dhh, ddw, ddb = dwconv_bwd(dcc, hh, 0, CONF_DIM, dww, CONF_KERNEL, f32)
            dvg = glu_bwd(dhh, proj)
            dxr, dcw, dcb = dwconv_bwd(dxc, proj, 2, LRU_DIM, cw, LRU_CONV, bf16)
            dproj = jnp.concatenate([dvg, dxr, dgr], axis=-1)
            gpart['od_w_in'][o] = wgrad(hm, dproj, dproj.shape[-1] // N_CHIPS)
            gpart['od_w_out'][o] = gw_out
            gpart['conf_dw_w'][o], gpart['conf_dw_b'][o] = ddw[:CONF_KERNEL], ddb[0]
            gpart['conf_ln_g'][o], gpart['conf_ln_b'][o] = dcln[0], dcln[1]
            gpart['lru_conv_w'][o], gpart['lru_conv_b'][o] = dcw[:LRU_CONV], dcb[0]
            gpart['lru_wa'][o], gpart['lru_wx'][o] = dwa, dwx
            gpart['lru_ba'][o], gpart['lru_bx'][o], gpart['lru_lambda'][o] = dvec[0], dvec[1], dvec[2]
        dxs, dm2 = inproj_bwd(dproj, w_in_m, xin, m2, dxp)
        dxs, dm1 = ffn_back(dxs, rec['ffa'], l, 0)
        dmod[l] = jnp.concatenate([dm1, dm2, dg2, dm3], axis=1)
    grad_x = dxs

    def stack(v):
        return jnp.stack([stack(u) if isinstance(u, list) else u for u in v])

    def per_chip(n):
        if W[n].ndim == 4:
            g = jnp.stack([jnp.stack(lf, axis=1) for lf in gpart[n]])
            return g.reshape(g.shape[:2] + (g.shape[2] * g.shape[3], g.shape[4]))
        return jnp.stack(gpart[n])

    gfull = [per_chip(n) for n in BIG]
    cidx = lax.axis_index("c").astype(jnp.int32).reshape(1)
    summed = [add_half(g, t, cidx) for g, t in zip(gfull, swap_halves(gfull))]
    reduced = join_halves([sum_chips(r) for r in exchange_blocks(summed)])

    out_g, out_d, out_m, out_v = {}, {}, {}, {}
    for n, g in zip(BIG, reduced):
        shp = W[n].shape
        as2d = lambda a: a.reshape(-1, shp[-1])
        res = adamw(as2d(W[n]), as2d(M[n]), as2d(V[n]), [as2d(g)], [0], BIG_ROW_TILE)
        out_g[n] = g.reshape(shp)
        out_d[n], out_m[n], out_v[n] = (r.reshape(shp) for r in res)

    small = SMALL_SHARDED + SMALL_REPLICATED
    dmod_flat = stack(dmod).reshape(L, B, N_MOD * D)
    g3 = allgather8(_pack([dmod_flat] + [stack(gpart[n]) for n in small], 64))
    n_dmod = L * B * N_MOD * D
    dmod_all = jnp.moveaxis(g3.reshape(N_DEV, -1)[:, :n_dmod].reshape(N_DEV, L, B, N_MOD * D), 0, 1).reshape(L, N_DEV * B, N_MOD * D)
    ssum = sum_leading(g3, 64).reshape(-1)[n_dmod:]
    gsmall = dict(zip(small, _unpack(ssum, [full[n].shape for n in small])))
    for n in SMALL_SHARDED:
        wdt = W[n].shape[-1]
        gsmall[n] = lax.dynamic_slice_in_dim(gsmall[n], chip * wdt, wdt, axis=gsmall[n].ndim - 1)
    dmod_my = lax.dynamic_slice_in_dim(dmod_all, chip * n_ada, n_ada, axis=2)
    g_ada_w, g_ada_b = ada_bwd(c_all, dmod_my, dmod_all)
    gsmall['ada_b'] = g_ada_b[:, 0, :]

    res = adamw(ada_w.reshape(-1, n_ada), M['ada_w'].reshape(-1, n_ada), V['ada_w'].reshape(-1, n_ada),
                [g_ada_w.reshape(-1, n_ada)], [0], BIG_ROW_TILE)
    out_g['ada_w'] = g_ada_w
    out_d['ada_w'], out_m['ada_w'], out_v['ada_w'] = (r.reshape(ada_w.shape) for r in res)

    names = ('ada_b',) + small
    shapes = [W[n].shape for n in names]
    res = adamw(_pack([W[n] for n in names], 64), _pack([M[n] for n in names], 64), _pack([V[n] for n in names], 64),
                [_pack([gsmall[n] for n in names], 64)], [0], 64)
    out_g.update({n: gsmall[n] for n in names})
    for dst, r in zip((out_d, out_m, out_v), res):
        dst.update(zip(names, _unpack(r.reshape(-1), shapes)))

    return (loss, grad_x, *[out_g[n] for n in WEIGHTS], *[out_d[n] for n in WEIGHTS], *[out_m[n] for n in WEIGHTS],
            *[out_v[n] for n in WEIGHTS])
```

```python
import jax
import jax.numpy as jnp
from jax import lax
from jax.experimental import pallas as pl
from jax.experimental.pallas import tpu as pltpu

f32 = jnp.float32
bf16 = jnp.bfloat16

DEPTH = 4
D_MODEL = 1024
N_MOD = 9
DN_ALPHA = (2.0 * DEPTH) ** 0.25
NORM_EPS = 1e-5
SSD_CHUNK = 128
SSD_D_INNER = 1024
SSD_CONV_DIM = 1536
SSD_HEADS = 16
POOL_WINDOWS = (2, 4, 8, 16)
POOL_DIM = 512
EVEN_IN = 3088
EVEN_IN_PAD = 3200
CONF_DIM = 512
CONF_KERNEL = 31
LRU_DIM = 1024
LRU_HEADS = 8
LRU_CONV = 4
SSD_CONV = 4
LRU_C = 8.0
ADAM_LR = 0.001
ADAM_B1 = 0.9
ADAM_B2 = 0.999
ADAM_EPS = 1e-08
ADAM_WD = 0.01
ADAM_STEP = 10

LANES = 128
VMEM_LIMIT_BYTES = 56 * 2 ** 20
COL_TILE = 512
N_CHIPS = 4
N_DEV = 8
MESH = pl.DeviceIdType.MESH


def _cp():
    return pltpu.CompilerParams(vmem_limit_bytes=VMEM_LIMIT_BYTES)


def _dot(a, b):
    return jnp.dot(a, b, preferred_element_type=f32)


def _dot_nt(a, b):
    return lax.dot_general(a, b, (((1,), (1,)), ((), ())), preferred_element_type=f32)


def _dot_tn(a, b):
    return lax.dot_general(a, b, (((0,), (0,)), ((), ())), preferred_element_type=f32)


def _dot_hi(a, b):
    return jnp.dot(a, b, preferred_element_type=f32, precision=lax.Precision.HIGHEST)


def _silu(x):
    return x * jax.nn.sigmoid(x)


def _ln_stats(z):
    mu = jnp.mean(z, axis=-1, keepdims=True)
    zc = z - mu
    var = jnp.mean(zc * zc, axis=-1, keepdims=True)
    rstd = lax.rsqrt(var + NORM_EPS)
    return zc * rstd, rstd


def _ln_bwd(dxn, xhat, rstd, lg):
    dxh = dxn * lg
    return rstd * (dxh - jnp.mean(dxh, axis=-1, keepdims=True) - xhat * jnp.mean(dxh * xhat, axis=-1, keepdims=True))


def _const_spec(shape):
    nd = len(shape)
    return pl.BlockSpec(shape, lambda *_: (0,) * nd, pipeline_mode=pl.Buffered(1))


def _row_tile(t, want=256):
    return min(want, t)


def ffn_fwd(x, mod3, w_in, w_out, lg, lb, carry=()):
    B, T, D = x.shape
    FS = w_in.shape[2]
    tm = _row_tile(T)
    nt = T // tm
    nw = len(carry)

    def body(x_ref, mod_ref, win_ref, wout_ref, lg_ref, lb_ref, *rest):
        xn_ref, h_ref, gu_ref, a_ref, y_ref = rest[nw:nw + 5]
        if nw:
            start, finish = _gather_steps(rest[:nw], rest[nw + 5:2 * nw + 5], *rest[2 * nw + 5:])
            b, i = pl.program_id(0), pl.program_id(1)
            pl.when((b == 0) & (i == 0))(start)
        xv = x_ref[...]
        sh, sc, g = mod_ref[0:1, :], mod_ref[1:2, :], mod_ref[2:3, :]
        h = (xv * (1.0 + sc) + sh).astype(bf16)
        h_ref[...] = h
        acc = jnp.zeros((tm, D), f32)
        for s in range(2):
            gate = _dot(h, win_ref[s])
            up = _dot(h, win_ref[s + 2])
            gu_ref[:, s * FS:(s + 1) * FS] = gate.astype(bf16)
            gu_ref[:, (s + 2) * FS:(s + 3) * FS] = up.astype(bf16)
            a = (_silu(gate) * up).astype(bf16)
            a_ref[:, s * FS:(s + 1) * FS] = a
            acc = acc + _dot(a, wout_ref[s * FS:(s + 1) * FS, :])
        y_ref[...] = acc
        xhat, _ = _ln_stats(DN_ALPHA * xv + 0.5 * (1.0 + g) * acc)
        xn_ref[...] = xhat * lg_ref[...] + lb_ref[...]
        if nw:
            pl.when((b == B - 1) & (i == nt - 1))(finish)

    row = lambda w: pl.BlockSpec((None, tm, w), lambda b, i: (b, i, 0))
    res = pl.pallas_call(
        body, name="ffn_fwd_gather" if nw else "ffn_fwd", grid=(B, nt),
        in_specs=[row(D), pl.BlockSpec((None, 3, D), lambda b, i: (b, 0, 0)), _const_spec(w_in.shape),
                  _const_spec(w_out.shape), _const_spec((1, D)), _const_spec((1, D))] + [ANY] * nw,
        out_specs=[row(D), row(D), row(4 * FS), row(2 * FS), row(D)] + [ANY] * nw,
        out_shape=[jax.ShapeDtypeStruct((B, T, D), f32), jax.ShapeDtypeStruct((B, T, D), bf16),
                   jax.ShapeDtypeStruct((B, T, 4 * FS), bf16), jax.ShapeDtypeStruct((B, T, 2 * FS), bf16),
                   jax.ShapeDtypeStruct((B, T, D), f32)] + _gather_out_shapes(carry),
        scratch_shapes=_gather_sems(nw) if nw else [],
        compiler_params=_cp(),
    )(x, mod3, w_in, w_out, lg, lb, *carry)
    return tuple(res[:5]) + (_fill_own(res[5:], carry),)


def ffn_bwd(dxn, x, y, gu, mod3, w_in, w_out, lg):
    B, T, D = x.shape
    FS = w_in.shape[2]
    tm = _row_tile(T)

    def body(dxn_ref, x_ref, y_ref, gu_ref, mod_ref, win_ref, wout_ref, lg_ref,
             dx_ref, dgu_ref, dy_ref, dmod_ref, dln_ref):
        b, i = pl.program_id(0), pl.program_id(1)

        @pl.when((b == 0) & (i == 0))
        def _():
            dln_ref[...] = jnp.zeros_like(dln_ref)

        @pl.when(i == 0)
        def _():
            dmod_ref[...] = jnp.zeros_like(dmod_ref)

        xv, yv, dxn_v = x_ref[...], y_ref[...], dxn_ref[...]
        sc, g = mod_ref[1:2, :], mod_ref[2:3, :]
        xhat, rstd = _ln_stats(DN_ALPHA * xv + 0.5 * (1.0 + g) * yv)
        dln_ref[0:1, :] += jnp.sum(dxn_v * xhat, axis=0, keepdims=True)
        dln_ref[1:2, :] += jnp.sum(dxn_v, axis=0, keepdims=True)
        dz = _ln_bwd(dxn_v, xhat, rstd, lg_ref[...])
        dmod_ref[2:3, :] += jnp.sum(0.5 * dz * yv, axis=0, keepdims=True)
        dy = (0.5 * (1.0 + g) * dz).astype(bf16)
        dy_ref[...] = dy
        dh = jnp.zeros((tm, D), f32)
        for s in range(2):
            da = _dot_nt(dy, wout_ref[s * FS:(s + 1) * FS, :])
            gate = gu_ref[:, s * FS:(s + 1) * FS].astype(f32)
            up = gu_ref[:, (s + 2) * FS:(s + 3) * FS].astype(f32)
            sig = jax.nn.sigmoid(gate)
            dgate = (da * up * (sig * (1.0 + gate * (1.0 - sig)))).astype(bf16)
            dup = (da * gate * sig).astype(bf16)
            dgu_ref[:, s * FS:(s + 1) * FS] = dgate
            dgu_ref[:, (s + 2) * FS:(s + 3) * FS] = dup
            dh = dh + _dot_nt(dgate, win_ref[s]) + _dot_nt(dup, win_ref[s + 2])
        dx_ref[...] = DN_ALPHA * dz + dh * (1.0 + sc)
        dmod_ref[0:1, :] += jnp.sum(dh, axis=0, keepdims=True)
        dmod_ref[1:2, :] += jnp.sum(dh * xv, axis=0, keepdims=True)

    row = lambda w: pl.BlockSpec((None, tm, w), lambda b, i: (b, i, 0))
    return pl.pallas_call(
        body, name="ffn_bwd", grid=(B, T // tm),
        in_specs=[row(D), row(D), row(D), row(4 * FS), pl.BlockSpec((None, 3, D), lambda b, i: (b, 0, 0)),
                  _const_spec(w_in.shape), _const_spec(w_out.shape), _const_spec((1, D))],
        out_specs=[row(D), row(4 * FS), row(D), pl.BlockSpec((None, 3, D), lambda b, i: (b, 0, 0)),
                   pl.BlockSpec((2, D), lambda b, i: (0, 0))],
        out_shape=[jax.ShapeDtypeStruct((B, T, D), f32), jax.ShapeDtypeStruct((B, T, 4 * FS), bf16),
                   jax.ShapeDtypeStruct((B, T, D), bf16), jax.ShapeDtypeStruct((B, 3, D), f32),
                   jax.ShapeDtypeStruct((2, D), f32)],
        compiler_params=_cp(),
    )(dxn, x, y, gu, mod3, w_in, w_out, lg)


def wgrad(a, b, tn):
    B, T, K = a.shape
    N = b.shape[2]
    tr = _row_tile(T, 1024 if K * tn <= 1024 * 1536 else 512)

    def body(a_ref, b_ref, o_ref):
        @pl.when((pl.program_id(1) == 0) & (pl.program_id(2) == 0))
        def _():
            o_ref[...] = jnp.zeros_like(o_ref)

        o_ref[...] += _dot_tn(a_ref[...], b_ref[...])

    return pl.pallas_call(
        body, name="wgrad", grid=(N // tn, B, T // tr),
        in_specs=[pl.BlockSpec((None, tr, K), lambda s, b, r: (b, r, 0)),
                  pl.BlockSpec((None, tr, tn), lambda s, b, r: (b, r, s))],
        out_specs=pl.BlockSpec((None, K, tn), lambda s, b, r: (s, 0, 0)),
        out_shape=jax.ShapeDtypeStruct((N // tn, K, tn), f32),
        compiler_params=_cp(),
    )(a, b)


def wgrad_into(a, b, tn, lead, pos, buf=None):
    B, T, K = a.shape
    N = b.shape[2]
    tr = _row_tile(T, 1024 if K * tn <= 1024 * 1536 else 512)
    nl = len(lead)

    def body(pos_ref, a_ref, b_ref, *rest):
        o_ref = rest[-1]

        @pl.when((pl.program_id(1) == 0) & (pl.program_id(2) == 0))
        def _():
            o_ref[...] = jnp.zeros_like(o_ref)

        o_ref[...] += _dot_tn(a_ref[...], b_ref[...])

    return pl.pallas_call(
        body, name="wgrad_into",
        grid_spec=pltpu.PrefetchScalarGridSpec(
            num_scalar_prefetch=1, grid=(N // tn, B, T // tr),
            in_specs=[pl.BlockSpec((None, tr, K), lambda s, b, r, p: (b, r, 0)),
                      pl.BlockSpec((None, tr, tn), lambda s, b, r, p: (b, r, s))] + ([] if buf is None else [ANY]),
            out_specs=pl.BlockSpec((None,) * (nl + 1) + (K, tn),
                                   lambda s, b, r, p: tuple(p[j] for j in range(nl)) + (s, 0, 0))),
        out_shape=jax.ShapeDtypeStruct(tuple(lead) + (N // tn, K, tn), f32),
        input_output_aliases={} if buf is None else {3: 0},
        compiler_params=_cp(),
    )(jnp.asarray(pos, jnp.int32), a, b, *([] if buf is None else [buf]))


def inproj_fwd(x, mod3, w):
    B, T, D = x.shape
    N = w.shape[1]
    tm = _row_tile(T, 512)

    def body(x_ref, mod_ref, w_ref, p_ref, h_ref):
        h = (x_ref[...] * (1.0 + mod_ref[1:2, :]) + mod_ref[0:1, :]).astype(bf16)
        h_ref[...] = h
        p_ref[...] = _dot(h, w_ref[...])

    row = lambda n: pl.BlockSpec((None, tm, n), lambda b, i: (b, i, 0))
    return pl.pallas_call(
        body, name="inproj_fwd", grid=(B, T // tm),
        in_specs=[row(D), pl.BlockSpec((None, 3, D), lambda b, i: (b, 0, 0)), _const_spec(w.shape)],
        out_specs=[row(N), row(D)],
        out_shape=[jax.ShapeDtypeStruct((B, T, N), f32), jax.ShapeDtypeStruct((B, T, D), bf16)],
        compiler_params=_cp(),
    )(x, mod3, w)


def inproj_bwd(dproj, w, x, mod3, dxp):
    B, T, D = x.shape
    N = w.shape[1]
    tm = _row_tile(T, 512)

    def body(dp_ref, w_ref, x_ref, mod_ref, dxp_ref, dx_ref, dmod_ref):
        @pl.when(pl.program_id(1) == 0)
        def _():
            dmod_ref[...] = jnp.zeros_like(dmod_ref)

        dh = _dot_nt(dp_ref[...], w_ref[...])
        dx_ref[...] = dxp_ref[...] + dh * (1.0 + mod_ref[1:2, :])
        dmod_ref[0:1, :] += jnp.sum(dh, axis=0, keepdims=True)
        dmod_ref[1:2, :] += jnp.sum(dh * x_ref[...], axis=0, keepdims=True)

    row = lambda n: pl.BlockSpec((None, tm, n), lambda b, i: (b, i, 0))
    return pl.pallas_call(
        body, name="inproj_bwd", grid=(B, T // tm),
        in_specs=[row(N), _const_spec(w.shape), row(D), pl.BlockSpec((None, 3, D), lambda b, i: (b, 0, 0)), row(D)],
        out_specs=[row(D), pl.BlockSpec((None, 2, D), lambda b, i: (b, 0, 0))],
        out_shape=[jax.ShapeDtypeStruct((B, T, D), f32), jax.ShapeDtypeStruct((B, 2, D), f32)],
        compiler_params=_cp(),
    )(dproj, w, x, mod3, dxp)


def outproj_fwd(ycat, w, x, mod3, lg, lb):
    B, T, D = x.shape
    E = w.shape[0]
    tm = _row_tile(T, 512)

    def body(yc_ref, w_ref, x_ref, mod_ref, lg_ref, lb_ref, xn_ref, y_ref):
        yv = _dot(yc_ref[...], w_ref[...])
        y_ref[...] = yv
        xhat, _ = _ln_stats(DN_ALPHA * x_ref[...] + (1.0 + mod_ref[2:3, :]) * yv)
        xn_ref[...] = xhat * lg_ref[...] + lb_ref[...]

    row = lambda n: pl.BlockSpec((None, tm, n), lambda b, i: (b, i, 0))
    return pl.pallas_call(
        body, name="outproj_fwd", grid=(B, T // tm),
        in_specs=[row(E), _const_spec(w.shape), row(D), pl.BlockSpec((None, 3, D), lambda b, i: (b, 0, 0)),
                  _const_spec((1, D)), _const_spec((1, D))],
        out_specs=[row(D), row(D)],
        out_shape=[jax.ShapeDtypeStruct((B, T, D), f32), jax.ShapeDtypeStruct((B, T, D), f32)],
        compiler_params=_cp(),
    )(ycat, w, x, mod3, lg, lb)


def outproj_bwd(dxn, x, y, mod3, w, lg):
    B, T, D = x.shape
    E = w.shape[0]
    tm = _row_tile(T, 512)

    def body(dxn_ref, x_ref, y_ref, mod_ref, w_ref, lg_ref, dxp_ref, dyc_ref, dy_ref, dg_ref, dln_ref):
        b, i = pl.program_id(0), pl.program_id(1)

        @pl.when((b == 0) & (i == 0))
        def _():
            dln_ref[...] = jnp.zeros_like(dln_ref)

        @pl.when(i == 0)
        def _():
            dg_ref[...] = jnp.zeros_like(dg_ref)

        xv, yv, dxn_v = x_ref[...], y_ref[...], dxn_ref[...]
        g = mod_ref[2:3, :]
        xhat, rstd = _ln_stats(DN_ALPHA * xv + (1.0 + g) * yv)
        dln_ref[0:1, :] += jnp.sum(dxn_v * xhat, axis=0, keepdims=True)
        dln_ref[1:2, :] += jnp.sum(dxn_v, axis=0, keepdims=True)
        dz = _ln_bwd(dxn_v, xhat, rstd, lg_ref[...])
        dg_ref[...] += jnp.sum(dz * yv, axis=0, keepdims=True)
        dy = ((1.0 + g) * dz).astype(bf16)
        dy_ref[...] = dy
        dxp_ref[...] = DN_ALPHA * dz
        dyc_ref[...] = _dot_nt(dy, w_ref[...])

    row = lambda n: pl.BlockSpec((None, tm, n), lambda b, i: (b, i, 0))
    return pl.pallas_call(
        body, name="outproj_bwd", grid=(B, T // tm),
        in_specs=[row(D), row(D), row(D), pl.BlockSpec((None, 3, D), lambda b, i: (b, 0, 0)), _const_spec(w.shape),
                  _const_spec((1, D))],
        out_specs=[row(D), row(E), row(D), pl.BlockSpec((None, 1, D), lambda b, i: (b, 0, 0)),
                   pl.BlockSpec((2, D), lambda b, i: (0, 0))],
        out_shape=[jax.ShapeDtypeStruct((B, T, D), f32), jax.ShapeDtypeStruct((B, T, E), f32),
                   jax.ShapeDtypeStruct((B, T, D), bf16), jax.ShapeDtypeStruct((B, 1, D), f32),
                   jax.ShapeDtypeStruct((2, D), f32)],
        compiler_params=_cp(),
    )(dxn, x, y, mod3, w, lg)


def _halo_rows(K):
    return 8 if K <= 9 else 32


def dwconv_fwd(x, col0, C, w, b, K):
    B, T, _ = x.shape
    tc, hp = COL_TILE, _halo_rows(K)
    tm = _row_tile(T)
    r = tm // hp

    def body(xh_ref, x_ref, w_ref, b_ref, o_ref):
        halo = jnp.where(pl.program_id(2) == 0, 0.0, xh_ref[...])
        xe = jnp.concatenate([halo, x_ref[...]], axis=0)
        acc = jnp.zeros((tm, tc), f32) + b_ref[...]
        for k in range(K):
            sft = K - 1 - k
            xs = xe if sft == 0 else pltpu.roll(xe, sft, 0)
            acc = acc + xs[hp:, :] * w_ref[k:k + 1, :]
        o_ref[...] = acc

    return pl.pallas_call(
        body, name=f"dwconv{K}_fwd", grid=(C // tc, B, T // tm),
        in_specs=[pl.BlockSpec((None, hp, tc), lambda j, b, i: (b, jnp.maximum(i * r - 1, 0), col0 + j)),
                  pl.BlockSpec((None, tm, tc), lambda j, b, i: (b, i, col0 + j)),
                  pl.BlockSpec((w.shape[0], tc), lambda j, b, i: (0, j)),
                  pl.BlockSpec((1, tc), lambda j, b, i: (0, j))],
        out_specs=pl.BlockSpec((None, tm, tc), lambda j, b, i: (b, i, j)),
        out_shape=jax.ShapeDtypeStruct((B, T, C), f32),
        compiler_params=_cp(),
    )(x, x, w, b)


def dwconv_bwd(dc, x, col0, C, w, K, out_dtype):
    B, T, _ = x.shape
    tc, hp = COL_TILE, _halo_rows(K)
    tm = _row_tile(T)
    r = tm // hp
    nt = T // tm
    n = tm + hp
    KP = w.shape[0]

    def body(dcn_ref, dc_ref, xh_ref, x_ref, w_ref, dx_ref, dw_ref, db_ref):
        b, i = pl.program_id(1), pl.program_id(2)

        @pl.when((b == 0) & (i == 0))
        def _():
            dw_ref[...] = jnp.zeros_like(dw_ref)
            db_ref[...] = jnp.zeros_like(db_ref)

        dcv = dc_ref[...]
        de = jnp.concatenate([dcv, jnp.where(i == nt - 1, 0.0, dcn_ref[...])], axis=0)
        acc = jnp.zeros((tm, tc), f32)
        for k in range(K):
            j = K - 1 - k
            ds = de if j == 0 else pltpu.roll(de, n - j, 0)
            acc = acc + ds[:tm, :] * w_ref[k:k + 1, :]
        dx_ref[...] = acc.astype(out_dtype)
        xe = jnp.concatenate([jnp.where(i == 0, 0.0, xh_ref[...]), x_ref[...]], axis=0)
        for k in range(K):
            sft = K - 1 - k
            xs = xe if sft == 0 else pltpu.roll(xe, sft, 0)
            dw_ref[k:k + 1, :] += jnp.sum(dcv * xs[hp:, :], axis=0, keepdims=True)
        db_ref[...] += jnp.sum(dcv, axis=0, keepdims=True)

    return pl.pallas_call(
        body, name=f"dwconv{K}_bwd", grid=(C // tc, B, nt),
        in_specs=[pl.BlockSpec((None, hp, tc), lambda j, b, i: (b, jnp.minimum((i + 1) * r, T // hp - 1), j)),
                  pl.BlockSpec((None, tm, tc), lambda j, b, i: (b, i, j)),
                  pl.BlockSpec((None, hp, tc), lambda j, b, i: (b, jnp.maximum(i * r - 1, 0), col0 + j)),
                  pl.BlockSpec((None, tm, tc), lambda j, b, i: (b, i, col0 + j)),
                  pl.BlockSpec((KP, tc), lambda j, b, i: (0, j))],
        out_specs=[pl.BlockSpec((None, tm, tc), lambda j, b, i: (b, i, j)),
                   pl.BlockSpec((KP, tc), lambda j, b, i: (0, j)),
                   pl.BlockSpec((1, tc), lambda j, b, i: (0, j))],
        out_shape=[jax.ShapeDtypeStruct((B, T, C), out_dtype), jax.ShapeDtypeStruct((KP, C), f32),
                   jax.ShapeDtypeStruct((1, C), f32)],
        compiler_params=_cp(),
    )(dc, dc, x, x, w)


POOL_HALO = 16


def _pool_windows(ue, pos, hp):
    out = []
    for g, wd in enumerate(POOL_WINDOWS):
        ug = ue[:, g * LANES:(g + 1) * LANES]
        s, span = ug, 1
        while span < wd:
            s = s + pltpu.roll(s, span, 0)
            span *= 2
        cnt = jnp.minimum(pos + 1, wd).astype(f32)
        out.append(s[hp:, :] / cnt - ug[hp:, :])
    return out


def pool_fwd(proj, colb, w, scale):
    B, T, _ = proj.shape
    hp = POOL_HALO
    tm = _row_tile(T)
    r = tm // hp

    def body(uh_ref, u_ref, w_ref, sc_ref, o_ref):
        i = pl.program_id(1)
        ue = jnp.concatenate([jnp.where(i == 0, 0.0, uh_ref[...]), u_ref[...]], axis=0)
        pos = i * tm + lax.broadcasted_iota(jnp.int32, (tm, 1), 0)
        ps = _pool_windows(ue, pos, hp)
        o = jnp.concatenate([_dot(ps[g], w_ref[g]) for g in range(4)], axis=1) * sc_ref[...]
        o_ref[...] = o.astype(bf16)

    return pl.pallas_call(
        body, name="pool_fwd", grid=(B, T // tm),
        in_specs=[pl.BlockSpec((None, hp, POOL_DIM), lambda b, i: (b, jnp.maximum(i * r - 1, 0), colb)),
                  pl.BlockSpec((None, tm, POOL_DIM), lambda b, i: (b, i, colb)),
                  _const_spec(w.shape), _const_spec((1, POOL_DIM))],
        out_specs=pl.BlockSpec((None, tm, POOL_DIM), lambda b, i: (b, i, 0)),
        out_shape=jax.ShapeDtypeStruct((B, T, POOL_DIM), bf16),
        compiler_params=_cp(),
    )(proj, proj, w, scale)


def pool_bwd(dycat, dcolb, proj, colb, w, scale):
    B, T, _ = proj.shape
    hp = POOL_HALO
    tm = _row_tile(T)
    r = tm // hp
    nt = T // tm
    n = tm + hp

    def body(dyn_ref, dy_ref, uh_ref, u_ref, w_ref, sc_ref, du_ref, dw_ref, dsc_ref):
        b, i = pl.program_id(0), pl.program_id(1)

        @pl.when((b == 0) & (i == 0))
        def _():
            dw_ref[...] = jnp.zeros_like(dw_ref)
            dsc_ref[...] = jnp.zeros_like(dsc_ref)

        dyv = dy_ref[...]
        dye = jnp.concatenate([dyv, jnp.where(i == nt - 1, 0.0, dyn_ref[...])], axis=0)
        ue = jnp.concatenate([jnp.where(i == 0, 0.0, uh_ref[...]), u_ref[...]], axis=0)
        pos = i * tm + lax.broadcasted_iota(jnp.int32, (tm, 1), 0)
        pos_e = i * tm + lax.broadcasted_iota(jnp.int32, (n, 1), 0)
        ps = _pool_windows(ue, pos, hp)
        dme = dye * sc_ref[...]
        dus, dscs = [], []
        for g, wd in enumerate(POOL_WINDOWS):
            sl = slice(g * LANES, (g + 1) * LANES)
            dscs.append(jnp.sum(dyv[:, sl] * _dot(ps[g], w_ref[g]), axis=0, keepdims=True))
            dw_ref[g] += _dot_tn(ps[g], dme[:tm, sl])
            dpe = _dot_nt(dme[:, sl], w_ref[g])
            s, span = dpe / jnp.minimum(pos_e + 1, wd).astype(f32), 1
            while span < wd:
                s = s + pltpu.roll(s, n - span, 0)
                span *= 2
            dus.append(s[:tm, :] - dpe[:tm, :])
        du_ref[...] = jnp.concatenate(dus, axis=1).astype(bf16)
        dsc_ref[...] += jnp.concatenate(dscs, axis=1)

    return pl.pallas_call(
        body, name="pool_bwd", grid=(B, nt),
        in_specs=[pl.BlockSpec((None, hp, POOL_DIM), lambda b, i: (b, jnp.minimum((i + 1) * r, T // hp - 1), dcolb)),
                  pl.BlockSpec((None, tm, POOL_DIM), lambda b, i: (b, i, dcolb)),
                  pl.BlockSpec((None, hp, POOL_DIM), lambda b, i: (b, jnp.maximum(i * r - 1, 0), colb)),
                  pl.BlockSpec((None, tm, POOL_DIM), lambda b, i: (b, i, colb)),
                  _const_spec(w.shape), _const_spec((1, POOL_DIM))],
        out_specs=[pl.BlockSpec((None, tm, POOL_DIM), lambda b, i: (b, i, 0)),
                   pl.BlockSpec(w.shape, lambda b, i: (0, 0, 0)),
                   pl.BlockSpec((1, POOL_DIM), lambda b, i: (0, 0))],
        out_shape=[jax.ShapeDtypeStruct((B, T, POOL_DIM), bf16), jax.ShapeDtypeStruct(w.shape, f32),
                   jax.ShapeDtypeStruct((1, POOL_DIM), f32)],
        compiler_params=_cp(),
    )(dycat, dycat, proj, proj, w, scale)


N_PAIRS = SSD_HEADS // 2


def _ssd_chunk(xs, bs, cs, dtp, zs, hs, dtb, alog, dsk, ngs):
    Q = SSD_CHUNK
    lane = lax.broadcasted_iota(jnp.int32, (1, LANES), 1)
    sub = lax.broadcasted_iota(jnp.int32, (LANES, 1), 0)
    causal = lax.broadcasted_iota(jnp.int32, (Q, Q), 0) >= lax.broadcasted_iota(jnp.int32, (Q, Q), 1)
    lane_lo, sub_lo = lane < 64, sub < 64

    def col(v, h):
        return jnp.sum(v * (lane == h).astype(f32), axis=1, keepdims=True)

    def row(vt, h):
        return jnp.sum(vt * (sub == h).astype(f32), axis=0, keepdims=True)

    dt = jax.nn.softplus(dtp + dtb)
    acum = _dot_hi(causal.astype(f32), dt * (-jnp.exp(alog)))
    acum_t = acum.T
    aend = jnp.sum(acum * (sub == Q - 1).astype(f32), axis=0, keepdims=True)
    outs, hn = [], []
    for grp in range(2):
        bv, cv = _silu(bs[grp]), _silu(cs[grp])
        gmat = _dot_nt(cv, bv)
        for j in range(4):
            p = grp * 4 + j
            h0, h1 = 2 * p, 2 * p + 1
            x2 = _silu(xs[p])
            c0, c1 = col(acum, h0), col(acum, h1)
            s2 = jnp.where(lane_lo, c0, c1)
            xdt = x2 * jnp.where(lane_lo, col(dt, h0), col(dt, h1))
            l0 = jnp.where(causal, jnp.exp(jnp.minimum(c0 - row(acum_t, h0), 0.0)), 0.0)
            l1 = jnp.where(causal, jnp.exp(jnp.minimum(c1 - row(acum_t, h1), 0.0)), 0.0)
            yd = _dot(gmat * l0, jnp.where(lane_lo, xdt, 0.0)) + _dot(gmat * l1, jnp.where(lane_lo, 0.0, xdt))
            e0, e1 = col(aend, h0), col(aend, h1)
            st = _dot_tn(xdt * jnp.exp(jnp.where(lane_lo, e0, e1) - s2), bv)
            yo = jnp.exp(s2) * _dot_nt(cv, hs[p])
            hn.append(jnp.exp(jnp.where(sub_lo, e0, e1)) * hs[p] + st)
            yv = yd + yo + x2 * jnp.where(lane_lo, col(dsk, h0), col(dsk, h1))
            outs.append(yv * _silu(zs[p]))
    ms = sum(jnp.sum(o * o, axis=1, keepdims=True) for o in outs) / SSD_D_INNER
    rs = lax.rsqrt(ms + NORM_EPS)
    return [outs[p] * rs * ngs[p] for p in range(N_PAIRS)], hn


def _lane_blocks(ref, n, start=0):
    return [ref[:, (start + k) * LANES:(start + k + 1) * LANES] for k in range(n)]


def _ssd_args(z_ref, cx_ref, dt_ref, dtb_ref, alog_ref, dsk_ref, ng_ref):
    xs = _lane_blocks(cx_ref, 8)
    bs = _lane_blocks(cx_ref, 2, 8)
    cs = _lane_blocks(cx_ref, 2, 10)
    zs = _lane_blocks(z_ref, 8)
    ngs = _lane_blocks(ng_ref, 8)
    return xs, bs, cs, dt_ref[...], zs, dtb_ref[...], alog_ref[...], dsk_ref[...], ngs


DT_COLB = (EVEN_IN_PAD - LANES) // LANES


def ssd_fwd(proj, cx, dtb, alog, dsk, ng):
    B, T, _ = proj.shape
    Q = SSD_CHUNK
    nc = T // Q

    def body(z_ref, cx_ref, dt_ref, dtb_ref, alog_ref, dsk_ref, ng_ref, ya_ref, hsave_ref, h_scr):
        @pl.when(pl.program_id(1) == 0)
        def _():
            h_scr[...] = jnp.zeros_like(h_scr)

        xs, bs, cs, dtp, zs, dtb_v, alog_v, dsk_v, ngs = _ssd_args(z_ref, cx_ref, dt_ref, dtb_ref, alog_ref, dsk_ref, ng_ref)
        hs = [h_scr[p] for p in range(N_PAIRS)]
        for p in range(N_PAIRS):
            hsave_ref[p] = hs[p]
        outs, hn = _ssd_chunk(xs, bs, cs, dtp, zs, hs, dtb_v, alog_v, dsk_v, ngs)
        for p in range(N_PAIRS):
            ya_ref[:, p * LANES:(p + 1) * LANES] = outs[p].astype(bf16)
            h_scr[p] = hn[p]

    return pl.pallas_call(
        body, name="ssd_fwd", grid=(B, nc),
        in_specs=[pl.BlockSpec((None, Q, SSD_D_INNER), lambda b, i: (b, i, 0)),
                  pl.BlockSpec((None, Q, SSD_CONV_DIM), lambda b, i: (b, i, 0)),
                  pl.BlockSpec((None, Q, LANES), lambda b, i: (b, i, DT_COLB)),
                  _const_spec((1, LANES)), _const_spec((1, LANES)), _const_spec((1, LANES)),
                  _const_spec((1, SSD_D_INNER))],
        out_specs=[pl.BlockSpec((None, Q, SSD_D_INNER), lambda b, i: (b, i, 0)),
                   pl.BlockSpec((None, None, N_PAIRS, LANES, LANES), lambda b, i: (b, i, 0, 0, 0))],
        out_shape=[jax.ShapeDtypeStruct((B, T, SSD_D_INNER), bf16),
                   jax.ShapeDtypeStruct((B, nc, N_PAIRS, LANES, LANES), f32)],
        scratch_shapes=[pltpu.VMEM((N_PAIRS, LANES, LANES), f32)],
        compiler_params=_cp(),
    )(proj, cx, proj, dtb, alog, dsk, ng)


def ssd_bwd(dycat, proj, cx, hsave, dtb, alog, dsk, ng):
    B, T, _ = proj.shape
    Q = SSD_CHUNK
    nc = T // Q

    def body(dya_ref, z_ref, cx_ref, dt_ref, hsave_ref, dtb_ref, alog_ref, dsk_ref, ng_ref,
             dz_ref, dcx_ref, ddt_ref, gsm_ref, gng_ref, dh_scr):
        b, i = pl.program_id(0), pl.program_id(1)

        @pl.when((b == 0) & (i == 0))
        def _():
            gsm_ref[...] = jnp.zeros_like(gsm_ref)
            gng_ref[...] = jnp.zeros_like(gng_ref)

        @pl.when(i == 0)
        def _():
            dh_scr[...] = jnp.zeros_like(dh_scr)

        xs, bs, cs, dtp, zs, dtb_v, alog_v, dsk_v, ngs = _ssd_args(z_ref, cx_ref, dt_ref, dtb_ref, alog_ref, dsk_ref, ng_ref)
        hs = [hsave_ref[p] for p in range(N_PAIRS)]
        _, vjp = jax.vjp(_ssd_chunk, xs, bs, cs, dtp, zs, hs, dtb_v, alog_v, dsk_v, ngs)
        douts = _lane_blocks(dya_ref, 8)
        dhn = [dh_scr[p] for p in range(N_PAIRS)]
        dxs, dbs, dcs, ddtp, dzs, dhs, ddtb, dalog, ddsk, dngs = vjp((douts, dhn))
        for p in range(N_PAIRS):
            dcx_ref[:, p * LANES:(p + 1) * LANES] = dxs[p]
            dz_ref[:, p * LANES:(p + 1) * LANES] = dzs[p].astype(bf16)
            dh_scr[p] = dhs[p]
            gng_ref[:, p * LANES:(p + 1) * LANES] += dngs[p]
        for k in range(2):
            dcx_ref[:, (8 + k) * LANES:(9 + k) * LANES] = dbs[k]
            dcx_ref[:, (10 + k) * LANES:(11 + k) * LANES] = dcs[k]
        ddt_ref[...] = ddtp.astype(bf16)
        gsm_ref[0:1, :] += ddtb
        gsm_ref[1:2, :] += dalog
        gsm_ref[2:3, :] += ddsk

    rev = lambda w, cb=0: pl.BlockSpec((None, Q, w), lambda b, i: (b, nc - 1 - i, cb))
    return pl.pallas_call(
        body, name="ssd_bwd", grid=(B, nc),
        in_specs=[rev(SSD_D_INNER), rev(SSD_D_INNER), rev(SSD_CONV_DIM), rev(LANES, DT_COLB),
                  pl.BlockSpec((None, None, N_PAIRS, LANES, LANES), lambda b, i: (b, nc - 1 - i, 0, 0, 0)),
                  _const_spec((1, LANES)), _const_spec((1, LANES)), _const_spec((1, LANES)),
                  _const_spec((1, SSD_D_INNER))],
        out_specs=[rev(SSD_D_INNER), rev(SSD_CONV_DIM), rev(LANES),
                   pl.BlockSpec((3, LANES), lambda b, i: (0, 0)),
                   pl.BlockSpec((1, SSD_D_INNER), lambda b, i: (0, 0))],
        out_shape=[jax.ShapeDtypeStruct((B, T, SSD_D_INNER), bf16), jax.ShapeDtypeStruct((B, T, SSD_CONV_DIM), f32),
                   jax.ShapeDtypeStruct((B, T, LANES), bf16), jax.ShapeDtypeStruct((3, LANES), f32),
                   jax.ShapeDtypeStruct((1, SSD_D_INNER), f32)],
        scratch_shapes=[pltpu.VMEM((N_PAIRS, LANES, LANES), f32)],
        compiler_params=_cp(),
    )(dycat, proj, cx, proj, hsave, dtb, alog, dsk, ng)


def glu_fwd(proj):
    B, T, _ = proj.shape
    tm = _row_tile(T)

    def body(v_ref, g_ref, o_ref):
        o_ref[...] = v_ref[...] * jax.nn.sigmoid(g_ref[...])

    blk = lambda cb: pl.BlockSpec((None, tm, CONF_DIM), lambda b, i: (b, i, cb))
    return pl.pallas_call(body, name="glu_fwd", grid=(B, T // tm), in_specs=[blk(0), blk(1)], out_specs=blk(0),
                          out_shape=jax.ShapeDtypeStruct((B, T, CONF_DIM), f32), compiler_params=_cp())(proj, proj)


def glu_bwd(dhh, proj):
    B, T, _ = proj.shape
    tm = _row_tile(T)

    def body(d_ref, v_ref, g_ref, o_ref):
        sig = jax.nn.sigmoid(g_ref[...])
        dv = d_ref[...]
        o_ref[:, :CONF_DIM] = (dv * sig).astype(bf16)
        o_ref[:, CONF_DIM:] = (dv * v_ref[...] * sig * (1.0 - sig)).astype(bf16)

    blk = lambda cb: pl.BlockSpec((None, tm, CONF_DIM), lambda b, i: (b, i, cb))
    return pl.pallas_call(body, name="glu_bwd", grid=(B, T // tm), in_specs=[blk(0), blk(0), blk(1)],
                          out_specs=pl.BlockSpec((None, tm, 2 * CONF_DIM), lambda b, i: (b, i, 0)),
                          out_shape=jax.ShapeDtypeStruct((B, T, 2 * CONF_DIM), bf16), compiler_params=_cp())(dhh, proj, proj)


def _neg_expm1(x):
    series = x * (1.0 + x * (1.0 / 2.0) * (1.0 + x * (1.0 / 3.0) * (1.0 + x * (1.0 / 4.0) * (1.0 + x * (1.0 / 5.0)))))
    return -jnp.where(x > -0.1, series, jnp.exp(x) - 1.0)


def _lru_gates(cc, xc8, gr8, clg, clb, wa8, wx8, ba8, bx8, lam8):
    xhat, _ = _ln_stats(cc)
    yc = _silu(xhat * clg + clb)
    a8, b8, ge8 = [], [], []
    for hb in range(LRU_HEADS):
        xh = xc8[hb]
        rg = jax.nn.sigmoid(_dot(xh, wa8[hb]) + ba8[hb])
        ig = jax.nn.sigmoid(_dot(xh, wx8[hb]) + bx8[hb])
        log_a = -LRU_C * rg * jax.nn.softplus(-lam8[hb])
        a8.append(jnp.exp(log_a))
        b8.append(jnp.sqrt(_neg_expm1(2.0 * log_a)) * (ig * xh))
        ge8.append(jax.nn.gelu(gr8[hb]))
    return yc, a8, b8, ge8


def _scan_fwd(a, b, h_in):
    tm = a.shape[0]
    rows = lax.broadcasted_iota(jnp.int32, (tm, 1), 0)
    s = 1
    while s < tm:
        keep = rows >= s
        b = a * jnp.where(keep, pltpu.roll(b, s, 0), 0.0) + b
        a = a * jnp.where(keep, pltpu.roll(a, s, 0), 1.0)
        s *= 2
    return a * h_in + b


def _scan_bwd(e, d, g_in):
    tm = e.shape[0]
    rows = lax.broadcasted_iota(jnp.int32, (tm, 1), 0)
    s = 1
    while s < tm:
        keep = rows < tm - s
        d = e * jnp.where(keep, pltpu.roll(d, tm - s, 0), 0.0) + d
        e = e * jnp.where(keep, pltpu.roll(e, tm - s, 0), 1.0)
        s *= 2
    return e * g_in + d


def _lru_params(wa_ref, wx_ref, vec_ref):
    wa8 = [wa_ref[h] for h in range(LRU_HEADS)]
    wx8 = [wx_ref[h] for h in range(LRU_HEADS)]
    ba8 = [vec_ref[0:1, h * LANES:(h + 1) * LANES] for h in range(LRU_HEADS)]
    bx8 = [vec_ref[1:2, h * LANES:(h + 1) * LANES] for h in range(LRU_HEADS)]
    lam8 = [vec_ref[2:3, h * LANES:(h + 1) * LANES] for h in range(LRU_HEADS)]
    return wa8, wx8, ba8, bx8, lam8


GR_COLB = 2


def lru_fwd(cc, xc, proj, cln, wa, wx, vec):
    B, T, _ = xc.shape
    tm = _row_tile(T)

    def body(cc_ref, xc_ref, gr_ref, cln_ref, wa_ref, wx_ref, vec_ref, y_ref, hs_ref, h_scr):
        @pl.when(pl.program_id(1) == 0)
        def _():
            h_scr[...] = jnp.zeros_like(h_scr)

        yc, a8, b8, ge8 = _lru_gates(cc_ref[...], _lane_blocks(xc_ref, 8), _lane_blocks(gr_ref, 8), cln_ref[0:1, :],
                                     cln_ref[1:2, :], *_lru_params(wa_ref, wx_ref, vec_ref))
        h = _scan_fwd(jnp.concatenate(a8, axis=1), jnp.concatenate(b8, axis=1), h_scr[...])
        hs_ref[...] = h
        h_scr[...] = h[tm - 1:tm, :]
        y_ref[:, :CONF_DIM] = yc.astype(bf16)
        y_ref[:, CONF_DIM:] = (h * jnp.concatenate(ge8, axis=1)).astype(bf16)

    row = lambda w, cb=0: pl.BlockSpec((None, tm, w), lambda b, i: (b, i, cb))
    return pl.pallas_call(
        body, name="lru_fwd", grid=(B, T // tm),
        in_specs=[row(CONF_DIM), row(LRU_DIM), row(LRU_DIM, GR_COLB), _const_spec((2, CONF_DIM)),
                  _const_spec(wa.shape), _const_spec(wx.shape), _const_spec((3, LRU_DIM))],
        out_specs=[row(CONF_DIM + LRU_DIM), row(LRU_DIM)],
        out_shape=[jax.ShapeDtypeStruct((B, T, CONF_DIM + LRU_DIM), bf16), jax.ShapeDtypeStruct((B, T, LRU_DIM), f32)],
        scratch_shapes=[pltpu.VMEM((1, LRU_DIM), f32)],
        compiler_params=_cp(),
    )(cc, xc, proj, cln, wa, wx, vec)


def lru_bwd(dycat, cc, xc, proj, hs, cln, wa, wx, vec):
    B, T, _ = xc.shape
    tm = _row_tile(T)
    nt = T // tm
    r = tm // 8

    def body(dy_ref, cc_ref, xc_ref, gr_ref, hs_ref, hsh_ref, cln_ref, wa_ref, wx_ref, vec_ref,
             dcc_ref, dxc_ref, dgr_ref, dcln_ref, dwa_ref, dwx_ref, dvec_ref, g_scr, a_scr):
        b, i = pl.program_id(0), pl.program_id(1)
        it = nt - 1 - i

        @pl.when((b == 0) & (i == 0))
        def _():
            dcln_ref[...] = jnp.zeros_like(dcln_ref)
            dwa_ref[...] = jnp.zeros_like(dwa_ref)
            dwx_ref[...] = jnp.zeros_like(dwx_ref)
            dvec_ref[...] = jnp.zeros_like(dvec_ref)

        @pl.when(i == 0)
        def _():
            g_scr[...] = jnp.zeros_like(g_scr)
            a_scr[...] = jnp.zeros_like(a_scr)

        (yc, a8, b8, ge8), vjp = jax.vjp(_lru_gates, cc_ref[...], _lane_blocks(xc_ref, 8), _lane_blocks(gr_ref, 8),
                                         cln_ref[0:1, :], cln_ref[1:2, :], *_lru_params(wa_ref, wx_ref, vec_ref))
        a = jnp.concatenate(a8, axis=1)
        ge = jnp.concatenate(ge8, axis=1)
        h = hs_ref[...]
        dyd = dy_ref[:, CONF_DIM:]
        rows = lax.broadcasted_iota(jnp.int32, (tm, 1), 0)
        e = jnp.where(rows < tm - 1, pltpu.roll(a, tm - 1, 0), a_scr[...])
        g = _scan_bwd(e, dyd * ge, g_scr[...])
        h_first = jnp.where(it == 0, 0.0, hsh_ref[7:8, :])
        h_prev = jnp.where(rows >= 1, pltpu.roll(h, 1, 0), h_first)
        da = g * h_prev
        g_scr[...] = g[0:1, :]
        a_scr[...] = a[0:1, :]
        split = lambda v: [v[:, k * LANES:(k + 1) * LANES] for k in range(LRU_HEADS)]
        dcc, dxc8, dgr8, dclg, dclb, dwa8, dwx8, dba8, dbx8, dlam8 = vjp((dy_ref[:, :CONF_DIM], split(da), split(g), split(dyd * h)))
        dcc_ref[...] = dcc
        dcln_ref[0:1, :] += dclg
        dcln_ref[1:2, :] += dclb
        for k in range(LRU_HEADS):
            sl = slice(k * LANES, (k + 1) * LANES)
            dxc_ref[:, sl] = dxc8[k]
            dgr_ref[:, sl] = dgr8[k].astype(bf16)
            dwa_ref[k] += dwa8[k]
            dwx_ref[k] += dwx8[k]
            dvec_ref[0:1, sl] += dba8[k]
            dvec_ref[1:2, sl] += dbx8[k]
            dvec_ref[2:3, sl] += dlam8[k]

    rev = lambda w, cb=0: pl.BlockSpec((None, tm, w), lambda b, i: (b, nt - 1 - i, cb))
    acc = lambda shape: pl.BlockSpec(shape, lambda b, i: (0,) * len(shape))
    return pl.pallas_call(
        body, name="lru_bwd", grid=(B, nt),
        in_specs=[rev(CONF_DIM + LRU_DIM), rev(CONF_DIM), rev(LRU_DIM), rev(LRU_DIM, GR_COLB), rev(LRU_DIM),
                  pl.BlockSpec((None, 8, LRU_DIM), lambda b, i: (b, jnp.maximum((nt - 1 - i) * r - 1, 0), 0)),
                  _const_spec((2, CONF_DIM)), _const_spec(wa.shape), _const_spec(wx.shape), _const_spec((3, LRU_DIM))],
        out_specs=[rev(CONF_DIM), rev(LRU_DIM), rev(LRU_DIM), acc((2, CONF_DIM)), acc(wa.shape), acc(wx.shape),
                   acc((3, LRU_DIM))],
        out_shape=[jax.ShapeDtypeStruct((B, T, CONF_DIM), f32), jax.ShapeDtypeStruct((B, T, LRU_DIM), f32),
                   jax.ShapeDtypeStruct((B, T, LRU_DIM), bf16), jax.ShapeDtypeStruct((2, CONF_DIM), f32),
                   jax.ShapeDtypeStruct(wa.shape, f32), jax.ShapeDtypeStruct(wx.shape, f32),
                   jax.ShapeDtypeStruct((3, LRU_DIM), f32)],
        scratch_shapes=[pltpu.VMEM((1, LRU_DIM), f32), pltpu.VMEM((1, LRU_DIM), f32)],
        compiler_params=_cp(),
    )(dycat, cc, xc, proj, hs, hs, cln, wa, wx, vec)


def loss_fwd(y, target):
    B, T, D = y.shape
    tm = _row_tile(T)

    def body(y_ref, t_ref, l_ref, dy_ref):
        @pl.when((pl.program_id(0) == 0) & (pl.program_id(1) == 0))
        def _():
            l_ref[...] = jnp.zeros_like(l_ref)

        d = y_ref[...] - t_ref[...]
        dy_ref[...] = d * (1.0 / D)
        l_ref[...] += jnp.sum(jnp.sum(d * d, axis=1, keepdims=True), axis=0, keepdims=True)

    row = pl.BlockSpec((None, tm, D), lambda b, i: (b, i, 0))
    return pl.pallas_call(
        body, name="loss_fwd", grid=(B, T // tm), in_specs=[row, row],
        out_specs=[pl.BlockSpec((1, 1), lambda b, i: (0, 0)), row],
        out_shape=[jax.ShapeDtypeStruct((1, 1), f32), jax.ShapeDtypeStruct((B, T, D), f32)],
        compiler_params=_cp(),
    )(y, target)


ADA_COL_TILE = 768


def ada_fwd(c_all, w, b):
    L, D, N = w.shape
    nb = c_all.shape[0]
    tn = ADA_COL_TILE

    def body(c_ref, w_ref, b_ref, o_ref):
        o_ref[...] = _dot_hi(_silu(c_ref[...]), w_ref[...]) + b_ref[...]

    return pl.pallas_call(
        body, name="ada_fwd", grid=(L, N // tn),
        in_specs=[pl.BlockSpec((nb, D), lambda l, j: (0, 0)), pl.BlockSpec((None, D, tn), lambda l, j: (l, 0, j)),
                  pl.BlockSpec((None, 1, tn), lambda l, j: (l, 0, j))],
        out_specs=pl.BlockSpec((None, nb, tn), lambda l, j: (l, 0, j)),
        out_shape=jax.ShapeDtypeStruct((L, nb, N), f32),
        compiler_params=_cp(),
    )(c_all, w, b)


def ada_bwd(c_all, dmod_my, dmod_all):
    L, nb, N = dmod_my.shape
    D = c_all.shape[1]
    NA = dmod_all.shape[2]
    tn = ADA_COL_TILE
    nj = N // tn
    ta = NA // nj

    def body(c_ref, dm_ref, da_ref, gw_ref, gb_ref):
        gw_ref[...] = lax.dot_general(_silu(c_ref[...]), dm_ref[...], (((0,), (0,)), ((), ())),
                                      preferred_element_type=f32, precision=lax.Precision.HIGHEST)
        gb_ref[...] = jnp.sum(da_ref[...], axis=0, keepdims=True)

    return pl.pallas_call(
        body, name="ada_bwd", grid=(L, nj),
        in_specs=[pl.BlockSpec((nb, D), lambda l, j: (0, 0)), pl.BlockSpec((None, nb, tn), lambda l, j: (l, 0, j)),
                  pl.BlockSpec((None, nb, ta), lambda l, j: (l, 0, j))],
        out_specs=[pl.BlockSpec((None, D, tn), lambda l, j: (l, 0, j)), pl.BlockSpec((None, 1, ta), lambda l, j: (l, 0, j))],
        out_shape=[jax.ShapeDtypeStruct((L, D, N), f32), jax.ShapeDtypeStruct((L, 1, NA), f32)],
        compiler_params=_cp(),
    )(c_all, dmod_my, dmod_all)


def adamw(w, m, v, gs, offs, tr):
    R, C = w.shape
    ng = len(gs)
    c1 = 1.0 - ADAM_B1 ** ADAM_STEP
    c2 = 1.0 - ADAM_B2 ** ADAM_STEP

    def body(*refs):
        w_ref, m_ref, v_ref = refs[:3]
        g_refs = refs[3:3 + ng]
        d_out, m_out, v_out = refs[3 + ng:]
        g = g_refs[0][...]
        for r in g_refs[1:]:
            g = g + r[...]
        mn = ADAM_B1 * m_ref[...] + (1.0 - ADAM_B1) * g
        vn = ADAM_B2 * v_ref[...] + (1.0 - ADAM_B2) * (g * g)
        m_out[...] = mn
        v_out[...] = vn
        d_out[...] = -ADAM_LR * ((mn / c1) / (jnp.sqrt(vn / c2) + ADAM_EPS) + ADAM_WD * w_ref[...])

    blk = pl.BlockSpec((tr, C), lambda i: (i, 0))
    gspec = lambda off: pl.BlockSpec((tr, C), lambda i: (i + off // tr, 0))
    sds = jax.ShapeDtypeStruct((R, C), f32)
    return pl.pallas_call(
        body, name="adamw", grid=(pl.cdiv(R, tr),), in_specs=[blk, blk, blk] + [gspec(o) for o in offs],
        out_specs=[blk] * 3, out_shape=[sds] * 3, compiler_params=_cp(),
    )(w, m, v, *gs)


def sum_leading(a, tr):
    k, R, C = a.shape

    def body(a_ref, o_ref):
        s = a_ref[0]
        for j in range(1, k):
            s = s + a_ref[j]
        o_ref[...] = s

    return pl.pallas_call(
        body, name="sum_leading", grid=(R // tr,), in_specs=[pl.BlockSpec((k, tr, C), lambda i: (0, i, 0))],
        out_specs=pl.BlockSpec((tr, C), lambda i: (i, 0)), out_shape=jax.ShapeDtypeStruct((R, C), a.dtype),
        compiler_params=_cp(),
    )(a)


ANY = pl.BlockSpec(memory_space=pl.ANY)
CHIP_FLIPS = ((1, 0), (0, 1), (1, 1))
DEV_FLIPS = tuple((fx, fy, fc) for fx in (0, 1) for fy in (0, 1) for fc in (0, 1))[1:]


def _flip(v, f):
    return 1 - v if f else v


def _put(out, v, idx, axis=0):
    return lax.dynamic_update_slice_in_dim(out, jnp.expand_dims(v, axis) if v.ndim < out.ndim else v, idx, axis)


def allgather8(v):
    R, C = v.shape
    n = len(DEV_FLIPS)

    def body(v_ref, o_ref, send_sems, recv_sems):
        x, y, c = lax.axis_index("x"), lax.axis_index("y"), lax.axis_index("c")
        me = 4 * x + 2 * y + c
        peers = [(_flip(x, fx), _flip(y, fy), _flip(c, fc)) for fx, fy, fc in DEV_FLIPS]

        def copy(k, slot):
            return pltpu.make_async_remote_copy(src_ref=v_ref, dst_ref=o_ref.at[slot], send_sem=send_sems.at[k],
                                                recv_sem=recv_sems.at[k], device_id=peers[k], device_id_type=MESH)

        for k in range(n):
            copy(k, me).start()
        for k, (px, py, pc) in enumerate(peers):
            copy(k, 4 * px + 2 * py + pc).wait_recv()
        for k in range(n):
            copy(k, me).wait_send()

    out = pl.pallas_call(
        body, name="allgather8", in_specs=[ANY], out_specs=ANY, out_shape=jax.ShapeDtypeStruct((N_DEV, R, C), v.dtype),
        scratch_shapes=[pltpu.SemaphoreType.DMA((n,)), pltpu.SemaphoreType.DMA((n,))],
    )(v)
    return _put(out, v, 4 * lax.axis_index("x") + 2 * lax.axis_index("y") + lax.axis_index("c"))


def _half(ref_or_shape0, c):
    hsz = ref_or_shape0 // 2
    return pl.ds(c * hsz, hsz)


def _gather_steps(w_refs, o_refs, send_sems, recv_sems):
    nw, nc = len(w_refs), len(CHIP_FLIPS)
    x, y, c = lax.axis_index("x"), lax.axis_index("y"), lax.axis_index("c")
    me, sibling = 2 * x + y, (x, y, 1 - c)
    peers = [(_flip(x, fx), _flip(y, fy), c) for fx, fy in CHIP_FLIPS]
    slots = [2 * px + py for px, py, _ in peers]

    def copy(a, j, slot, half, to, own=False):
        hs = _half(w_refs[a].shape[0], half)
        return pltpu.make_async_remote_copy(src_ref=w_refs[a].at[hs] if own else o_refs[a].at[slot, hs],
                                            dst_ref=o_refs[a].at[slot, hs], send_sem=send_sems.at[j],
                                            recv_sem=recv_sems.at[j], device_id=to, device_id_type=MESH)

    first = [copy(a, a * nc + k, me, c, peers[k], own=True) for a in range(nw) for k in range(nc)]

    def start():
        for cp in first:
            cp.start()

    def finish():
        passed = []
        for a in range(nw):
            for k in range(nc):
                copy(a, a * nc + k, slots[k], c, peers[k]).wait_recv()
                passed.append(copy(a, nw * nc + a * nc + k, slots[k], c, sibling))
                passed[-1].start()
        for a in range(nw):
            for k in range(nc):
                copy(a, nw * nc + a * nc + k, slots[k], 1 - c, sibling).wait_recv()
        for cp in first + passed:
            cp.wait_send()

    return start, finish


def _gather_sems(nw):
    n = 2 * nw * len(CHIP_FLIPS)
    return [pltpu.SemaphoreType.DMA((n,)), pltpu.SemaphoreType.DMA((n,))]


def _gather_out_shapes(ws):
    return [jax.ShapeDtypeStruct((N_CHIPS,) + w.shape, w.dtype) for w in ws]


def _fill_own(outs, ws):
    chip = 2 * lax.axis_index("x") + lax.axis_index("y")
    return [_put(o, w, chip) for o, w in zip(outs, ws)]


def gather_weights(ws):
    nw = len(ws)

    def body(*refs):
        start, finish = _gather_steps(refs[:nw], refs[nw:2 * nw], *refs[2 * nw:])
        start()
        finish()

    outs = pl.pallas_call(
        body, name="gather_weights", in_specs=[ANY] * nw, out_specs=[ANY] * nw,
        out_shape=_gather_out_shapes(ws), scratch_shapes=_gather_sems(nw),
    )(*ws)
    return _fill_own(outs, ws)


def swap_halves(gs):
    nw = len(gs)

    def body(*refs):
        g_refs, t_refs = refs[:nw], refs[nw:2 * nw]
        send_sems, recv_sems = refs[2 * nw:]
        x, y, c = lax.axis_index("x"), lax.axis_index("y"), lax.axis_index("c")
        cps = [pltpu.make_async_remote_copy(src_ref=g_refs[a].at[_half(gs[a].shape[0], 1 - c)], dst_ref=t_refs[a],
                                            send_sem=send_sems.at[a], recv_sem=recv_sems.at[a],
                                            device_id=(x, y, 1 - c), device_id_type=MESH) for a in range(nw)]
        for cp in cps:
            cp.start()
        for cp in cps:
            cp.wait()

    return pl.pallas_call(
        body, name="swap_halves", in_specs=[ANY] * nw, out_specs=[ANY] * nw,
        out_shape=[jax.ShapeDtypeStruct((g.shape[0] // 2,) + g.shape[1:], g.dtype) for g in gs],
        scratch_shapes=[pltpu.SemaphoreType.DMA((nw,)), pltpu.SemaphoreType.DMA((nw,))],
    )(*gs)


def exchange_blocks(ps):
    nw, nc = len(ps), len(CHIP_FLIPS)

    def body(*refs):
        p_refs, r_refs = refs[:nw], refs[nw:2 * nw]
        send_sems, recv_sems = refs[2 * nw:]
        x, y, c = lax.axis_index("x"), lax.axis_index("y"), lax.axis_index("c")
        me = 2 * x + y
        peers = [(_flip(x, fx), _flip(y, fy), c) for fx, fy in CHIP_FLIPS]
        slots = [2 * px + py for px, py, _ in peers]

        def copy(a, k, src_slot, dst_slot):
            return pltpu.make_async_remote_copy(src_ref=p_refs[a].at[src_slot], dst_ref=r_refs[a].at[dst_slot],
                                                send_sem=send_sems.at[a * nc + k], recv_sem=recv_sems.at[a * nc + k],
                                                device_id=peers[k], device_id_type=MESH)

        for a in range(nw):
            for k in range(nc):
                copy(a, k, slots[k], me).start()
        for a in range(nw):
            for k in range(nc):
                copy(a, k, me, slots[k]).wait_recv()
        for a in range(nw):
            for k in range(nc):
                copy(a, k, slots[k], me).wait_send()

    outs = pl.pallas_call(
        body, name="exchange_blocks", in_specs=[ANY] * nw, out_specs=[ANY] * nw,
        out_shape=[jax.ShapeDtypeStruct(p.shape, p.dtype) for p in ps],
        scratch_shapes=[pltpu.SemaphoreType.DMA((nw * nc,)), pltpu.SemaphoreType.DMA((nw * nc,))],
    )(*ps)
    chip = 2 * lax.axis_index("x") + lax.axis_index("y")
    return [_put(o, lax.dynamic_slice_in_dim(p, chip, 1, axis=0), chip) for o, p in zip(outs, ps)]


def join_halves(ss):
    nw = len(ss)

    def body(*refs):
        s_refs, o_refs = refs[:nw], refs[nw:2 * nw]
        send_sems, recv_sems = refs[2 * nw:]
        x, y, c = lax.axis_index("x"), lax.axis_index("y"), lax.axis_index("c")

        def copy(a, half):
            hs = _half(2 * ss[a].shape[0], half)
            return pltpu.make_async_remote_copy(src_ref=s_refs[a], dst_ref=o_refs[a].at[hs], send_sem=send_sems.at[a],
                                                recv_sem=recv_sems.at[a], device_id=(x, y, 1 - c), device_id_type=MESH)

        for a in range(nw):
            copy(a, c).start()
        for a in range(nw):
            copy(a, 1 - c).wait_recv()
        for a in range(nw):
            copy(a, c).wait_send()

    outs = pl.pallas_call(
        body, name="join_halves", in_specs=[ANY] * nw, out_specs=[ANY] * nw,
        out_shape=[jax.ShapeDtypeStruct((2 * s.shape[0],) + s.shape[1:], s.dtype) for s in ss],
        scratch_shapes=[pltpu.SemaphoreType.DMA((nw,)), pltpu.SemaphoreType.DMA((nw,))],
    )(*ss)
    c = lax.axis_index("c")
    return [_put(o, s, c * s.shape[0]) for o, s in zip(outs, ss)]


def _tile_rows(a, b, itemsize=4, budget=4 * 2 ** 20):
    best = 8
    for t in range(8, a + 1, 8):
        if a % t == 0 and t * b * itemsize <= budget:
            best = t
    return best


def add_half(g, t, cidx):
    def body(c_ref, g_ref, t_ref, o_ref):
        o_ref[...] = (g_ref[...] + t_ref[...]).astype(bf16)

    if g.ndim == 5:
        n0, F, _, A, B = g.shape
        hsz, ta = n0 // 2, _tile_rows(A, B)
        nta = A // ta
        grid = (hsz, N_CHIPS, F, nta)
        in_specs = [pl.BlockSpec((None, None, None, ta, B), lambda h, k, f, i, c_ref: (c_ref[0] * hsz + h, f, k, i, 0)),
                    pl.BlockSpec((None, None, None, ta, B), lambda h, k, f, i, c_ref: (h, f, k, i, 0))]
        out_spec = pl.BlockSpec((None, None, ta, B), lambda h, k, f, i, c_ref: (k, h, f * nta + i, 0))
    else:
        n0, _, A, B = g.shape
        F, hsz, ta = 1, n0 // 2, _tile_rows(A, B)
        grid = (hsz, N_CHIPS, A // ta)
        in_specs = [pl.BlockSpec((None, None, ta, B), lambda h, k, i, c_ref: (c_ref[0] * hsz + h, k, i, 0)),
                    pl.BlockSpec((None, None, ta, B), lambda h, k, i, c_ref: (h, k, i, 0))]
        out_spec = pl.BlockSpec((None, None, ta, B), lambda h, k, i, c_ref: (k, h, i, 0))
    return pl.pallas_call(
        body, name="add_half",
        grid_spec=pltpu.PrefetchScalarGridSpec(num_scalar_prefetch=1, grid=grid, in_specs=in_specs, out_specs=out_spec),
        out_shape=jax.ShapeDtypeStruct((N_CHIPS, hsz, F * A, B), bf16),
        compiler_params=_cp(),
    )(cidx, g, t)


def sum_chips(r):
    _, h, A, B = r.shape
    ta = _tile_rows(A, B)

    def body(r_ref, o_ref):
        s = r_ref[0].astype(f32)
        for j in range(1, N_CHIPS):
            s = s + r_ref[j].astype(f32)
        o_ref[...] = s

    return pl.pallas_call(
        body, name="sum_chips", grid=(h, A // ta),
        in_specs=[pl.BlockSpec((N_CHIPS, None, ta, B), lambda hh, i: (0, hh, i, 0))],
        out_specs=pl.BlockSpec((None, ta, B), lambda hh, i: (hh, i, 0)),
        out_shape=jax.ShapeDtypeStruct((h, A, B), f32),
        compiler_params=_cp(),
    )(r)


WEIGHTS = ('ada_w', 'ada_b', 'ln_g', 'ln_b', 'ffn_w_in', 'ffn_w_out', 'ev_w_in', 'ssd_conv_w', 'ssd_conv_b',
           'ssd_dt_bias', 'ssd_a_log', 'ssd_d', 'ssd_norm_g', 'pool_w', 'pool_scale', 'ev_w_out', 'od_w_in',
           'conf_dw_w', 'conf_dw_b', 'conf_ln_g', 'conf_ln_b', 'lru_conv_w', 'lru_conv_b', 'lru_wa', 'lru_ba',
           'lru_wx', 'lru_bx', 'lru_lambda', 'od_w_out')
BIG =('ffn_w_in', 'ffn_w_out', 'ev_w_out', 'od_w_in', 'od_w_out', 'ev_w_in')
SMALL_SHARDED = ('ln_g', 'ln_b', 'ssd_conv_w', 'conf_dw_w', 'conf_dw_b', 'conf_ln_g', 'conf_ln_b', 'lru_conv_w',
                 'lru_conv_b', 'lru_ba', 'lru_bx', 'lru_lambda')
SMALL_REPLICATED = ('ssd_conv_b', 'ssd_dt_bias', 'ssd_a_log', 'ssd_d', 'ssd_norm_g', 'pool_w', 'pool_scale',
                    'lru_wa', 'lru_wx')
PACK_COLS = 1024
BIG_ROW_TILE = 256


def _pack(arrs, row_mult):
    flat = jnp.concatenate([a.reshape(-1) for a in arrs])
    rows = -(-flat.shape[0] // (PACK_COLS * row_mult)) * row_mult
    return jnp.pad(flat, (0, rows * PACK_COLS - flat.shape[0])).reshape(rows, PACK_COLS)


def _unpack(flat, shapes):
    out, off = [], 0
    for s in shapes:
        n = 1
        for d in s:
            n *= d
        out.append(flat[off:off + n].reshape(s))
        off += n
    return out


def _unshard_last(g4):
    m = jnp.moveaxis(g4, 0, -2)
    return m.reshape(m.shape[:-2] + (m.shape[-2] * m.shape[-1],))


def _pad_rows(a, rows):
    return jnp.pad(a, ((0, rows - a.shape[0]),) + ((0, 0),) * (a.ndim - 1))


def _pad_lanes(a):
    return jnp.pad(a, ((0, 0), (0, LANES - a.shape[1])))


def kernel(x, c, ada_w, ada_b, ln_g, ln_b, ffn_w_in, ffn_w_out, ev_w_in, ssd_conv_w, ssd_conv_b, ssd_dt_bias, ssd_a_log, ssd_d, ssd_norm_g, pool_w, pool_scale, ev_w_out, od_w_in, conf_dw_w, conf_dw_b, conf_ln_g, conf_ln_b, lru_conv_w, lru_conv_b, lru_wa, lru_ba, lru_wx, lru_bx, lru_lambda, od_w_out, loss_target, m_ada_w, m_ada_b, m_ln_g, m_ln_b, m_ffn_w_in, m_ffn_w_out, m_ev_w_in, m_ssd_conv_w, m_ssd_conv_b, m_ssd_dt_bias, m_ssd_a_log, m_ssd_d, m_ssd_norm_g, m_pool_w, m_pool_scale, m_ev_w_out, m_od_w_in, m_conf_dw_w, m_conf_dw_b, m_conf_ln_g, m_conf_ln_b, m_lru_conv_w, m_lru_conv_b, m_lru_wa, m_lru_ba, m_lru_wx, m_lru_bx, m_lru_lambda, m_od_w_out, v_ada_w, v_ada_b, v_ln_g, v_ln_b, v_ffn_w_in, v_ffn_w_out, v_ev_w_in, v_ssd_conv_w, v_ssd_conv_b, v_ssd_dt_bias, v_ssd_a_log, v_ssd_d, v_ssd_norm_g, v_pool_w, v_pool_scale, v_ev_w_out, v_od_w_in, v_conf_dw_w, v_conf_dw_b, v_conf_ln_g, v_conf_ln_b, v_lru_conv_w, v_lru_conv_b, v_lru_wa, v_lru_ba, v_lru_wx, v_lru_bx, v_lru_lambda, v_od_w_out):
    given = dict(locals())
    W = {n: given[n] for n in WEIGHTS}
    M = {n: given["m_" + n] for n in WEIGHTS}
    V = {n: given["v_" + n] for n in WEIGHTS}
    B, T, D = x.shape
    L = DEPTH
    chip = 2 * lax.axis_index("x") + lax.axis_index("y")
    dev = 2 * chip + lax.axis_index("c")

    g1 = allgather8(_pack([c] + [W[n] for n in SMALL_SHARDED], 8)).reshape(N_DEV, -1)
    c_all = g1[:, :B * D].reshape(N_DEV * B, D)
    per_chip = g1[0::2, B * D:]
    full = dict(zip(SMALL_SHARDED, [_unshard_last(jnp.stack(p)) for p in zip(*[
        _unpack(per_chip[k], [W[n].shape for n in SMALL_SHARDED]) for k in range(N_CHIPS)])]))
    for n in SMALL_REPLICATED:
        full[n] = W[n]

    n_ada = ada_w.shape[2]
    ada_b_cols = lax.dynamic_slice_in_dim(ada_b, chip * n_ada, n_ada, axis=1)[:, None, :]
    mod_cols = ada_fwd(c_all, ada_w, ada_b_cols)
    g2 = allgather8(mod_cols.reshape(-1, PACK_COLS))[0::2].reshape(N_CHIPS, L, N_DEV * B, n_ada)
    mod_all = jnp.moveaxis(g2, 0, 2).reshape(L, N_DEV * B, N_CHIPS * n_ada)
    mod = lax.dynamic_slice_in_dim(mod_all, dev * B, B, axis=1).reshape(L, B, N_MOD, D)

    FS = ffn_w_in.shape[3]

    def ffn_shards(l, f):
        return [ffn_w_in[l, f].astype(bf16), ffn_w_out[l, f].astype(bf16)]

    def mix_shards(l):
        w_i, w_o = (ev_w_in, ev_w_out) if l % 2 == 0 else (od_w_in, od_w_out)
        return [w_i[l // 2].astype(bf16), w_o[l // 2].astype(bf16)]

    def ffn_weights(g):
        return g[0], g[1].reshape(-1, D)

    def mix_weights(l, g):
        if l % 2 == 0:
            w = _unshard_last(g[0])
            w_i = jnp.concatenate([w[:, :2560], w[:, 2576:], w[:, 2560:2576], jnp.zeros((D, EVEN_IN_PAD - EVEN_IN), bf16)], axis=1)
        else:
            w_i = jnp.moveaxis(g[0], 0, 1).reshape(D, -1)
        return w_i, g[1].reshape(-1, D)

    saved = []
    xs = x
    next_ffn = gather_weights(ffn_shards(0, 0))
    next_mix = None
    for l in range(L):
        lg, lb = full['ln_g'][l], full['ln_b'][l]
        rec = {}
        m1, m2, m3 = mod[l][:, 0:3], mod[l][:, 3:6], mod[l][:, 6:9]
        w_in_a, w_out_a = ffn_weights(next_ffn)
        carry = ffn_shards(l, 1) + (mix_shards(l) if l == 0 else [])
        xn, h, gu, a, y, got = ffn_fwd(xs, m1, w_in_a, w_out_a, lg[0:1], lb[0:1], carry)
        w_in_b, w_out_b = ffn_weights(got[:2])
        w_in_m, w_out_m = mix_weights(l, got[2:] if l == 0 else next_mix)
        rec['ffa'] = (xs, h, gu, a, y, m1, w_in_a, w_out_a, lg[0:1])
        xs = xn
        if l % 2 == 0:
            e = l // 2
            cw = _pad_rows(full['ssd_conv_w'][e], 8)
            cb = full['ssd_conv_b'][e][None]
            dtb, alog, dsk = (_pad_lanes(full[n][e][None]) for n in ('ssd_dt_bias', 'ssd_a_log', 'ssd_d'))
            ng, pw, ps = full['ssd_norm_g'][e][None], full['pool_w'][e], full['pool_scale'][e][None]
            proj, hm = inproj_fwd(xs, m2, w_in_m)
            cx = dwconv_fwd(proj, 2, SSD_CONV_DIM, cw, cb, SSD_CONV)
            ya, hsave = ssd_fwd(proj, cx, dtb, alog, dsk, ng)
            yb = pool_fwd(proj, 5, pw, ps)
            ycat = jnp.concatenate([ya, yb], axis=-1)
            rec['mix'] = (proj, cx, hsave, cw, dtb, alog, dsk, ng, pw, ps)
        else:
            o = l // 2
            dww =_pad_rows(full['conf_dw_w'][o], 32)
            dwb = full['conf_dw_b'][o][None]
            cw = _pad_rows(full['lru_conv_w'][o], 8)
            cb = full['lru_conv_b'][o][None]
            cln = jnp.stack([full['conf_ln_g'][o], full['conf_ln_b'][o]])
            vec = jnp.stack([full['lru_ba'][o], full['lru_bx'][o], full['lru_lambda'][o]])
            wa, wx = full['lru_wa'][o], full['lru_wx'][o]
            proj, hm = inproj_fwd(xs, m2, w_in_m)
            hh = glu_fwd(proj)
            cc = dwconv_fwd(hh, 0, CONF_DIM, dww, dwb, CONF_KERNEL)
            xc = dwconv_fwd(proj, 2, LRU_DIM, cw, cb, LRU_CONV)
            ycat, hst = lru_fwd(cc, xc, proj, cln, wa, wx, vec)
            rec['mix'] = (proj, hh, cc, xc, hst, dww, cw, cln, wa, wx, vec)
        xn, ym = outproj_fwd(ycat, w_out_m, xs, m2, lg[1:2], lb[1:2])
        rec['mixio'] = (xs, hm, ycat, ym, m2, w_in_m, w_out_m, lg[1:2])
        xs = xn
        carry = ffn_shards(l + 1, 0) + mix_shards(l + 1) if l + 1 < L else []
        xn, h, gu, a, y, got = ffn_fwd(xs, m3, w_in_b, w_out_b, lg[2:3], lb[2:3], carry)
        next_ffn, next_mix = got[:2], got[2:]
        rec['ffb'] = (xs, h, gu, a, y, m3, w_in_b, w_out_b, lg[2:3])
        xs = xn
        saved.append(rec)

    sq, dxs = loss_fwd(xs, loss_target)
    loss = lax.psum(sq[0, 0], ("x", "y", "c")) * (0.5 / D)

    gpart = {n: [None] * W[n].shape[0] for n in WEIGHTS}
    gpart['ffn_w_in'] = gpart['ffn_w_out'] = None
    gpart['ln_g'] = [[None] * 3 for _ in range(L)]
    gpart['ln_b'] = [[None] * 3 for _ in range(L)]
    dmod = [None] * L

    def ffn_back(dxn, rec, l, f):
        xin, h, gu, a, y, m3_, w_in_, w_out_, lg_ = rec
        dx, dgu, dy, dm3, dln = ffn_bwd(dxn, xin, y, gu, m3_, w_in_, w_out_, lg_)
        gpart['ffn_w_in'] = wgrad_into(h, dgu, FS, (L, 2), (l, f), gpart['ffn_w_in'])
        gpart['ffn_w_out'] = wgrad_into(a, dy, D, (L, 2), (l, f), gpart['ffn_w_out'])
        gpart['ln_g'][l][2 * f] = dln[0]
        gpart['ln_b'][l][2 * f] = dln[1]
        return dx, dm3

    for l in reversed(range(L)):
        rec = saved[l]
        dxs, dm3 = ffn_back(dxs, rec['ffb'], l, 1)
        xin, hm, ycat, ym, m2, w_in_m, w_out_m, lg_ = rec['mixio']
        dxp, dycat, dy, dg2, dln = outproj_bwd(dxs, xin, ym, m2, w_out_m, lg_)
        gpart['ln_g'][l][1] = dln[0]
        gpart['ln_b'][l][1] = dln[1]
        gw_out = wgrad(ycat, dy, D)[0].reshape(N_CHIPS, -1, D)
        if l % 2 == 0:
            e = l // 2
            proj, cx, hsave, cw, dtb, alog, dsk, ng, pw, ps = rec['mix']
            dz, dcx, ddt, gsm, gng = ssd_bwd(dycat, proj, cx, hsave, dtb, alog, dsk, ng)
            dxbc, dcw, dcb = dwconv_bwd(dcx, proj, 2, SSD_CONV_DIM, cw, SSD_CONV, bf16)
            du, dpw, dps = pool_bwd(dycat, 2, proj, 5, pw, ps)
            dproj = jnp.concatenate([dz, dxbc, du, ddt], axis=-1)
            gwp = wgrad(hm, dproj, EVEN_IN_PAD)[0]
            gw = jnp.concatenate([gwp[:, :2560], gwp[:, 3072:3072 + 16], gwp[:, 2560:3072]], axis=1)
            gpart['ev_w_in'][e] = jnp.moveaxis(gw.reshape(D, N_CHIPS, -1), 1, 0)
            gpart['ev_w_out'][e] = gw_out
            gpart['ssd_conv_w'][e], gpart['ssd_conv_b'][e] = dcw[:SSD_CONV], dcb[0]
            gpart['ssd_dt_bias'][e], gpart['ssd_a_log'][e], gpart['ssd_d'][e] = (gsm[k, :SSD_HEADS] for k in range(3))
            gpart['ssd_norm_g'][e], gpart['pool_w'][e], gpart['pool_scale'][e] = gng[0], dpw, dps[0]
        else:
            o = l // 2
            proj, hh, cc, xc, hst, dww, cw, cln, wa, wx, vec = rec['mix']
            dcc, dxc, dgr, dcln, dwa, dwx, dvec = lru_bwd(dycat, cc, xc, proj, hst, cln, wa, wx, vec)
            dhh, ddw, ddb = dwconv_bwd(dcc, hh, 0, CONF_DIM, dww, CONF_KERNEL, f32)
            dvg = glu_bwd(dhh, proj)
            dxr, dcw, dcb = dwconv_bwd(dxc, proj, 2, LRU_DIM, cw, LRU_CONV, bf16)
            dproj = jnp.concatenate([dvg, dxr, dgr], axis=-1)
            gpart['od_w_in'][o] = wgrad(hm, dproj, dproj.shape[-1] // N_CHIPS)
            gpart['od_w_out'][o] = gw_out
            gpart['conf_dw_w'][o], gpart['conf_dw_b'][o] = ddw[:CONF_KERNEL], ddb[0]
            gpart['conf_ln_g'][o], gpart['conf_ln_b'][o] = dcln[0], dcln[1]
            gpart['lru_conv_w'][o], gpart['lru_conv_b'][o] = dcw[:LRU_CONV], dcb[0]
            gpart['lru_wa'][o], gpart['lru_wx'][o] = dwa, dwx
            gpart['lru_ba'][o], gpart['lru_bx'][o], gpart['lru_lambda'][o] = dvec[0], dvec[1], dvec[2]
        dxs, dm2 = inproj_bwd(dproj, w_in_m, xin, m2, dxp)
        dxs, dm1 = ffn_back(dxs, rec['ffa'], l, 0)
        dmod[l] = jnp.concatenate([dm1, dm2, dg2, dm3], axis=1)
    grad_x = dxs

    def stack(v):
        return jnp.stack([stack(u) if isinstance(u, list) else u for u in v])

    def per_chip(n):
        if W[n].ndim == 4:
            return gpart[n].reshape(W[n].shape[:2] + (N_CHIPS,) + W[n].shape[2:])
        return jnp.stack(gpart[n])

    gfull = [per_chip(n) for n in BIG]
    cidx = lax.axis_index("c").astype(jnp.int32).reshape(1)
    summed = [add_half(g, t, cidx) for g, t in zip(gfull, swap_halves(gfull))]
    reduced = join_halves([sum_chips(r) for r in exchange_blocks(summed)])

    out_g, out_d, out_m, out_v = {}, {}, {}, {}
    for n, g in zip(BIG, reduced):
        shp = W[n].shape
        as2d = lambda a: a.reshape(-1, shp[-1])
        res = adamw(as2d(W[n]), as2d(M[n]), as2d(V[n]), [as2d(g)], [0], BIG_ROW_TILE)
        out_g[n] = g.reshape(shp)
        out_d[n], out_m[n], out_v[n] = (r.reshape(shp) for r in res)

    small = SMALL_SHARDED + SMALL_REPLICATED
    dmod_flat = stack(dmod).reshape(L, B, N_MOD * D)
    g3 = allgather8(_pack([dmod_flat] + [stack(gpart[n]) for n in small], 64))
    n_dmod = L * B * N_MOD * D
    dmod_all = jnp.moveaxis(g3.reshape(N_DEV, -1)[:, :n_dmod].reshape(N_DEV, L, B, N_MOD * D), 0, 1).reshape(L, N_DEV * B, N_MOD * D)
    ssum = sum_leading(g3, 64).reshape(-1)[n_dmod:]
    gsmall = dict(zip(small, _unpack(ssum, [full[n].shape for n in small])))
    for n in SMALL_SHARDED:
        wdt = W[n].shape[-1]
        gsmall[n] = lax.dynamic_slice_in_dim(gsmall[n], chip * wdt, wdt, axis=gsmall[n].ndim - 1)
    dmod_my = lax.dynamic_slice_in_dim(dmod_all, chip * n_ada, n_ada, axis=2)
    g_ada_w, g_ada_b = ada_bwd(c_all, dmod_my, dmod_all)
    gsmall['ada_b'] = g_ada_b[:, 0, :]

    res = adamw(ada_w.reshape(-1, n_ada), M['ada_w'].reshape(-1, n_ada), V['ada_w'].reshape(-1, n_ada),
                [g_ada_w.reshape(-1, n_ada)], [0], BIG_ROW_TILE)
    out_g['ada_w'] = g_ada_w
    out_d['ada_w'], out_m['ada_w'], out_v['ada_w'] = (r.reshape(ada_w.shape) for r in res)

    names = ('ada_b',) + small
    shapes = [W[n].shape for n in names]
    res = adamw(_pack([W[n] for n in names], 64), _pack([M[n] for n in names], 64), _pack([V[n] for n in names], 64),
                [_pack([gsmall[n] for n in names], 64)], [0], 64)
    out_g.update({n: gsmall[n] for n in names})
    for dst, r in zip((out_d, out_m, out_v), res):
        dst.update(zip(names, _unpack(r.reshape(-1), shapes)))

    return (loss, grad_x, *[out_g[n] for n in WEIGHTS], *[out_d[n] for n in WEIGHTS], *[out_m[n] for n in WEIGHTS],
            *[out_v[n] for n in WEIGHTS])
```

```python
import jax
import jax.numpy as jnp
from jax import lax
from jax.experimental import pallas as pl
from jax.experimental.pallas import tpu as pltpu

f32 = jnp.float32
bf16 = jnp.bfloat16

DEPTH = 4
D_MODEL = 1024
N_MOD = 9
DN_ALPHA = (2.0 * DEPTH) ** 0.25
NORM_EPS = 1e-5
SSD_CHUNK = 128
SSD_D_INNER = 1024
SSD_CONV_DIM = 1536
SSD_HEADS = 16
POOL_WINDOWS = (2, 4, 8, 16)
POOL_DIM = 512
EVEN_IN = 3088
EVEN_IN_PAD = 3200
CONF_DIM = 512
CONF_KERNEL = 31
LRU_DIM = 1024
LRU_HEADS = 8
LRU_CONV = 4
SSD_CONV = 4
LRU_C = 8.0
ADAM_LR = 0.001
ADAM_B1 = 0.9
ADAM_B2 = 0.999
ADAM_EPS = 1e-08
ADAM_WD = 0.01
ADAM_STEP = 10

LANES = 128
VMEM_LIMIT_BYTES = 56 * 2 ** 20
COL_TILE = 512
N_CHIPS = 4
N_DEV = 8
MESH = pl.DeviceIdType.MESH


def _cp():
    return pltpu.CompilerParams(vmem_limit_bytes=VMEM_LIMIT_BYTES)


def _dot(a, b):
    return jnp.dot(a, b, preferred_element_type=f32)


def _dot_nt(a, b):
    return lax.dot_general(a, b, (((1,), (1,)), ((), ())), preferred_element_type=f32)


def _dot_tn(a, b):
    return lax.dot_general(a, b, (((0,), (0,)), ((), ())), preferred_element_type=f32)


def _dot_hi(a, b):
    return jnp.dot(a, b, preferred_element_type=f32, precision=lax.Precision.HIGHEST)


def _silu(x):
    return x * jax.nn.sigmoid(x)


def _ln_stats(z):
    mu = jnp.mean(z, axis=-1, keepdims=True)
    zc = z - mu
    var = jnp.mean(zc * zc, axis=-1, keepdims=True)
    rstd = lax.rsqrt(var + NORM_EPS)
    return zc * rstd, rstd


def _ln_bwd(dxn, xhat, rstd, lg):
    dxh = dxn * lg
    return rstd * (dxh - jnp.mean(dxh, axis=-1, keepdims=True) - xhat * jnp.mean(dxh * xhat, axis=-1, keepdims=True))


def _const_spec(shape):
    nd = len(shape)
    return pl.BlockSpec(shape, lambda *_: (0,) * nd, pipeline_mode=pl.Buffered(1))


def _row_tile(t, want=256):
    return min(want, t)


def ffn_fwd(x, mod3, w_in, w_out, lg, lb, carry=()):
    B, T, D = x.shape
    FS = w_in.shape[2]
    tm = _row_tile(T, 512)
    nt = T // tm
    nw = len(carry)

    def body(x_ref, mod_ref, win_ref, wout_ref, lg_ref, lb_ref, *rest):
        xn_ref, h_ref, gu_ref, a_ref, y_ref = rest[nw:nw + 5]
        if nw:
            start, finish = _gather_steps(rest[:nw], rest[nw + 5:2 * nw + 5], *rest[2 * nw + 5:])
            b, i = pl.program_id(0), pl.program_id(1)
            pl.when((b == 0) & (i == 0))(start)
        xv = x_ref[...]
        sh, sc, g = mod_ref[0:1, :], mod_ref[1:2, :], mod_ref[2:3, :]
        h = (xv * (1.0 + sc) + sh).astype(bf16)
        h_ref[...] = h
        acc = jnp.zeros((tm, D), f32)
        for s in range(2):
            gate = _dot(h, win_ref[s])
            up = _dot(h, win_ref[s + 2])
            gu_ref[:, s * FS:(s + 1) * FS] = gate.astype(bf16)
            gu_ref[:, (s + 2) * FS:(s + 3) * FS] = up.astype(bf16)
            a = (_silu(gate) * up).astype(bf16)
            a_ref[:, s * FS:(s + 1) * FS] = a
            acc = acc + _dot(a, wout_ref[s * FS:(s + 1) * FS, :])
        y_ref[...] = acc
        xhat, _ = _ln_stats(DN_ALPHA * xv + 0.5 * (1.0 + g) * acc)
        xn_ref[...] = xhat * lg_ref[...] + lb_ref[...]
        if nw:
            pl.when((b == B - 1) & (i == nt - 1))(finish)

    row = lambda w: pl.BlockSpec((None, tm, w), lambda b, i: (b, i, 0))
    res = pl.pallas_call(
        body, name="ffn_fwd_gather" if nw else "ffn_fwd", grid=(B, nt),
        in_specs=[row(D), pl.BlockSpec((None, 3, D), lambda b, i: (b, 0, 0)), _const_spec(w_in.shape),
                  _const_spec(w_out.shape), _const_spec((1, D)), _const_spec((1, D))] + [ANY] * nw,
        out_specs=[row(D), row(D), row(4 * FS), row(2 * FS), row(D)] + [ANY] * nw,
        out_shape=[jax.ShapeDtypeStruct((B, T, D), f32), jax.ShapeDtypeStruct((B, T, D), bf16),
                   jax.ShapeDtypeStruct((B, T, 4 * FS), bf16), jax.ShapeDtypeStruct((B, T, 2 * FS), bf16),
                   jax.ShapeDtypeStruct((B, T, D), f32)] + _gather_out_shapes(carry),
        scratch_shapes=_gather_sems(nw) if nw else [],
        compiler_params=_cp(),
    )(x, mod3, w_in, w_out, lg, lb, *carry)
    return tuple(res[:5]) + (_fill_own(res[5:], carry),)


def ffn_bwd(dxn, x, y, gu, mod3, w_in, w_out, lg):
    B, T, D = x.shape
    FS = w_in.shape[2]
    tm = _row_tile(T)

    def body(dxn_ref, x_ref, y_ref, gu_ref, mod_ref, win_ref, wout_ref, lg_ref,
             dx_ref, dgu_ref, dy_ref, dmod_ref, dln_ref):
        b, i = pl.program_id(0), pl.program_id(1)

        @pl.when((b == 0) & (i == 0))
        def _():
            dln_ref[...] = jnp.zeros_like(dln_ref)

        @pl.when(i == 0)
        def _():
            dmod_ref[...] = jnp.zeros_like(dmod_ref)

        xv, yv, dxn_v = x_ref[...], y_ref[...], dxn_ref[...]
        sc, g = mod_ref[1:2, :], mod_ref[2:3, :]
        xhat, rstd = _ln_stats(DN_ALPHA * xv + 0.5 * (1.0 + g) * yv)
        dln_ref[0:1, :] += jnp.sum(dxn_v * xhat, axis=0, keepdims=True)
        dln_ref[1:2, :] += jnp.sum(dxn_v, axis=0, keepdims=True)
        dz = _ln_bwd(dxn_v, xhat, rstd, lg_ref[...])
        dmod_ref[2:3, :] += jnp.sum(0.5 * dz * yv, axis=0, keepdims=True)
        dy = (0.5 * (1.0 + g) * dz).astype(bf16)
        dy_ref[...] = dy
        dh = jnp.zeros((tm, D), f32)
        for s in range(2):
            da = _dot_nt(dy, wout_ref[s * FS:(s + 1) * FS, :])
            gate = gu_ref[:, s * FS:(s + 1) * FS].astype(f32)
            up = gu_ref[:, (s + 2) * FS:(s + 3) * FS].astype(f32)
            sig = jax.nn.sigmoid(gate)
            dgate = (da * up * (sig * (1.0 + gate * (1.0 - sig)))).astype(bf16)
            dup = (da * gate * sig).astype(bf16)
            dgu_ref[:, s * FS:(s + 1) * FS] = dgate
            dgu_ref[:, (s + 2) * FS:(s + 3) * FS] = dup
            dh = dh + _dot_nt(dgate, win_ref[s]) + _dot_nt(dup, win_ref[s + 2])
        dx_ref[...] = DN_ALPHA * dz + dh * (1.0 + sc)
        dmod_ref[0:1, :] += jnp.sum(dh, axis=0, keepdims=True)
        dmod_ref[1:2, :] += jnp.sum(dh * xv, axis=0, keepdims=True)

    row = lambda w: pl.BlockSpec((None, tm, w), lambda b, i: (b, i, 0))
    return pl.pallas_call(
        body, name="ffn_bwd", grid=(B, T // tm),
        in_specs=[row(D), row(D), row(D), row(4 * FS), pl.BlockSpec((None, 3, D), lambda b, i: (b, 0, 0)),
                  _const_spec(w_in.shape), _const_spec(w_out.shape), _const_spec((1, D))],
        out_specs=[row(D), row(4 * FS), row(D), pl.BlockSpec((None, 3, D), lambda b, i: (b, 0, 0)),
                   pl.BlockSpec((2, D), lambda b, i: (0, 0))],
        out_shape=[jax.ShapeDtypeStruct((B, T, D), f32), jax.ShapeDtypeStruct((B, T, 4 * FS), bf16),
                   jax.ShapeDtypeStruct((B, T, D), bf16), jax.ShapeDtypeStruct((B, 3, D), f32),
                   jax.ShapeDtypeStruct((2, D), f32)],
        compiler_params=_cp(),
    )(dxn, x, y, gu, mod3, w_in, w_out, lg)


def wgrad(a, b, tn):
    B, T, K = a.shape
    N = b.shape[2]
    tr = _row_tile(T, 1024 if K * tn <= 1024 * 1536 else 512)

    def body(a_ref, b_ref, o_ref):
        @pl.when((pl.program_id(1) == 0) & (pl.program_id(2) == 0))
        def _():
            o_ref[...] = jnp.zeros_like(o_ref)

        o_ref[...] += _dot_tn(a_ref[...], b_ref[...])

    return pl.pallas_call(
        body, name="wgrad", grid=(N // tn, B, T // tr),
        in_specs=[pl.BlockSpec((None, tr, K), lambda s, b, r: (b, r, 0)),
                  pl.BlockSpec((None, tr, tn), lambda s, b, r: (b, r, s))],
        out_specs=pl.BlockSpec((None, K, tn), lambda s, b, r: (s, 0, 0)),
        out_shape=jax.ShapeDtypeStruct((N // tn, K, tn), f32),
        compiler_params=_cp(),
    )(a, b)


def wgrad_into(a, b, tn, lead, pos, buf=None):
    B, T, K = a.shape
    N = b.shape[2]
    tr = _row_tile(T, 1024 if K * tn <= 1024 * 1536 else 512)
    nl = len(lead)

    def body(pos_ref, a_ref, b_ref, *rest):
        o_ref = rest[-1]

        @pl.when((pl.program_id(1) == 0) & (pl.program_id(2) == 0))
        def _():
            o_ref[...] = jnp.zeros_like(o_ref)

        o_ref[...] += _dot_tn(a_ref[...], b_ref[...])

    return pl.pallas_call(
        body, name="wgrad_into",
        grid_spec=pltpu.PrefetchScalarGridSpec(
            num_scalar_prefetch=1, grid=(N // tn, B, T // tr),
            in_specs=[pl.BlockSpec((None, tr, K), lambda s, b, r, p: (b, r, 0)),
                      pl.BlockSpec((None, tr, tn), lambda s, b, r, p: (b, r, s))] + ([] if buf is None else [ANY]),
            out_specs=pl.BlockSpec((None,) * (nl + 1) + (K, tn),
                                   lambda s, b, r, p: tuple(p[j] for j in range(nl)) + (s, 0, 0))),
        out_shape=jax.ShapeDtypeStruct(tuple(lead) + (N // tn, K, tn), f32),
        input_output_aliases={} if buf is None else {3: 0},
        compiler_params=_cp(),
    )(jnp.asarray(pos, jnp.int32), a, b, *([] if buf is None else [buf]))


def inproj_fwd(x, mod3, w):
    B, T, D = x.shape
    N = w.shape[1]
    tm = _row_tile(T, 512)

    def body(x_ref, mod_ref, w_ref, p_ref, h_ref):
        h = (x_ref[...] * (1.0 + mod_ref[1:2, :]) + mod_ref[0:1, :]).astype(bf16)
        h_ref[...] = h
        p_ref[...] = _dot(h, w_ref[...])

    row = lambda n: pl.BlockSpec((None, tm, n), lambda b, i: (b, i, 0))
    return pl.pallas_call(
        body, name="inproj_fwd", grid=(B, T // tm),
        in_specs=[row(D), pl.BlockSpec((None, 3, D), lambda b, i: (b, 0, 0)), _const_spec(w.shape)],
        out_specs=[row(N), row(D)],
        out_shape=[jax.ShapeDtypeStruct((B, T, N), f32), jax.ShapeDtypeStruct((B, T, D), bf16)],
        compiler_params=_cp(),
    )(x, mod3, w)


def inproj_bwd(dproj, w, x, mod3, dxp):
    B, T, D = x.shape
    N = w.shape[1]
    tm = _row_tile(T, 512)

    def body(dp_ref, w_ref, x_ref, mod_ref, dxp_ref, dx_ref, dmod_ref):
        @pl.when(pl.program_id(1) == 0)
        def _():
            dmod_ref[...] = jnp.zeros_like(dmod_ref)

        dh = _dot_nt(dp_ref[...], w_ref[...])
        dx_ref[...] = dxp_ref[...] + dh * (1.0 + mod_ref[1:2, :])
        dmod_ref[0:1, :] += jnp.sum(dh, axis=0, keepdims=True)
        dmod_ref[1:2, :] += jnp.sum(dh * x_ref[...], axis=0, keepdims=True)

    row = lambda n: pl.BlockSpec((None, tm, n), lambda b, i: (b, i, 0))
    return pl.pallas_call(
        body, name="inproj_bwd", grid=(B, T // tm),
        in_specs=[row(N), _const_spec(w.shape), row(D), pl.BlockSpec((None, 3, D), lambda b, i: (b, 0, 0)), row(D)],
        out_specs=[row(D), pl.BlockSpec((None, 2, D), lambda b, i: (b, 0, 0))],
        out_shape=[jax.ShapeDtypeStruct((B, T, D), f32), jax.ShapeDtypeStruct((B, 2, D), f32)],
        compiler_params=_cp(),
    )(dproj, w, x, mod3, dxp)


def outproj_fwd(ycat, w, x, mod3, lg, lb):
    B, T, D = x.shape
    E = w.shape[0]
    tm = _row_tile(T, 512)

    def body(yc_ref, w_ref, x_ref, mod_ref, lg_ref, lb_ref, xn_ref, y_ref):
        yv = _dot(yc_ref[...], w_ref[...])
        y_ref[...] = yv
        xhat, _ = _ln_stats(DN_ALPHA * x_ref[...] + (1.0 + mod_ref[2:3, :]) * yv)
        xn_ref[...] = xhat * lg_ref[...] + lb_ref[...]

    row = lambda n: pl.BlockSpec((None, tm, n), lambda b, i: (b, i, 0))
    return pl.pallas_call(
        body, name="outproj_fwd", grid=(B, T // tm),
        in_specs=[row(E), _const_spec(w.shape), row(D), pl.BlockSpec((None, 3, D), lambda b, i: (b, 0, 0)),
                  _const_spec((1, D)), _const_spec((1, D))],
        out_specs=[row(D), row(D)],
        out_shape=[jax.ShapeDtypeStruct((B, T, D), f32), jax.ShapeDtypeStruct((B, T, D), f32)],
        compiler_params=_cp(),
    )(ycat, w, x, mod3, lg, lb)


def outproj_bwd(dxn, x, y, mod3, w, lg):
    B, T, D = x.shape
    E = w.shape[0]
    tm = _row_tile(T, 512)

    def body(dxn_ref, x_ref, y_ref, mod_ref, w_ref, lg_ref, dxp_ref, dyc_ref, dy_ref, dg_ref, dln_ref):
        b, i = pl.program_id(0), pl.program_id(1)

        @pl.when((b == 0) & (i == 0))
        def _():
            dln_ref[...] = jnp.zeros_like(dln_ref)

        @pl.when(i == 0)
        def _():
            dg_ref[...] = jnp.zeros_like(dg_ref)

        xv, yv, dxn_v = x_ref[...], y_ref[...], dxn_ref[...]
        g = mod_ref[2:3, :]
        xhat, rstd = _ln_stats(DN_ALPHA * xv + (1.0 + g) * yv)
        dln_ref[0:1, :] += jnp.sum(dxn_v * xhat, axis=0, keepdims=True)
        dln_ref[1:2, :] += jnp.sum(dxn_v, axis=0, keepdims=True)
        dz = _ln_bwd(dxn_v, xhat, rstd, lg_ref[...])
        dg_ref[...] += jnp.sum(dz * yv, axis=0, keepdims=True)
        dy = ((1.0 + g) * dz).astype(bf16)
        dy_ref[...] = dy
        dxp_ref[...] = DN_ALPHA * dz
        dyc_ref[...] = _dot_nt(dy, w_ref[...])

    row = lambda n: pl.BlockSpec((None, tm, n), lambda b, i: (b, i, 0))
    return pl.pallas_call(
        body, name="outproj_bwd", grid=(B, T // tm),
        in_specs=[row(D), row(D), row(D), pl.BlockSpec((None, 3, D), lambda b, i: (b, 0, 0)), _const_spec(w.shape),
                  _const_spec((1, D))],
        out_specs=[row(D), row(E), row(D), pl.BlockSpec((None, 1, D), lambda b, i: (b, 0, 0)),
                   pl.BlockSpec((2, D), lambda b, i: (0, 0))],
        out_shape=[jax.ShapeDtypeStruct((B, T, D), f32), jax.ShapeDtypeStruct((B, T, E), f32),
                   jax.ShapeDtypeStruct((B, T, D), bf16), jax.ShapeDtypeStruct((B, 1, D), f32),
                   jax.ShapeDtypeStruct((2, D), f32)],
        compiler_params=_cp(),
    )(dxn, x, y, mod3, w, lg)


def _halo_rows(K):
    return 8 if K <= 9 else 32


def dwconv_fwd(x, col0, C, w, b, K):
    B, T, _ = x.shape
    tc, hp = COL_TILE, _halo_rows(K)
    tm = _row_tile(T, 1024 if K <= 9 else 512)
    r = tm // hp

    def body(xh_ref, x_ref, w_ref, b_ref, o_ref):
        halo = jnp.where(pl.program_id(2) == 0, 0.0, xh_ref[...])
        xe = jnp.concatenate([halo, x_ref[...]], axis=0)
        acc = jnp.zeros((tm, tc), f32) + b_ref[...]
        for k in range(K):
            sft = K - 1 - k
            xs = xe if sft == 0 else pltpu.roll(xe, sft, 0)
            acc = acc + xs[hp:, :] * w_ref[k:k + 1, :]
        o_ref[...] = acc

    return pl.pallas_call(
        body, name=f"dwconv{K}_fwd", grid=(C // tc, B, T // tm),
        in_specs=[pl.BlockSpec((None, hp, tc), lambda j, b, i: (b, jnp.maximum(i * r - 1, 0), col0 + j)),
                  pl.BlockSpec((None, tm, tc), lambda j, b, i: (b, i, col0 + j)),
                  pl.BlockSpec((w.shape[0], tc), lambda j, b, i: (0, j)),
                  pl.BlockSpec((1, tc), lambda j, b, i: (0, j))],
        out_specs=pl.BlockSpec((None, tm, tc), lambda j, b, i: (b, i, j)),
        out_shape=jax.ShapeDtypeStruct((B, T, C), f32),
        compiler_params=_cp(),
    )(x, x, w, b)


def dwconv_bwd(dc, x, col0, C, w, K, out_dtype):
    B, T, _ = x.shape
    tc, hp = COL_TILE, _halo_rows(K)
    tm = _row_tile(T, 1024 if K <= 9 else 512)
    r = tm // hp
    nt = T // tm
    n = tm + hp
    KP = w.shape[0]

    def body(dcn_ref, dc_ref, xh_ref, x_ref, w_ref, dx_ref, dw_ref, db_ref):
        b, i = pl.program_id(1), pl.program_id(2)

        @pl.when((b == 0) & (i == 0))
        def _():
            dw_ref[...] = jnp.zeros_like(dw_ref)
            db_ref[...] = jnp.zeros_like(db_ref)

        dcv = dc_ref[...]
        de = jnp.concatenate([dcv, jnp.where(i == nt - 1, 0.0, dcn_ref[...])], axis=0)
        acc = jnp.zeros((tm, tc), f32)
        for k in range(K):
            j = K - 1 - k
            ds = de if j == 0 else pltpu.roll(de, n - j, 0)
            acc = acc + ds[:tm, :] * w_ref[k:k + 1, :]
        dx_ref[...] = acc.astype(out_dtype)
        xe = jnp.concatenate([jnp.where(i == 0, 0.0, xh_ref[...]), x_ref[...]], axis=0)
        for k in range(K):
            sft = K - 1 - k
            xs = xe if sft == 0 else pltpu.roll(xe, sft, 0)
            dw_ref[k:k + 1, :] += jnp.sum(dcv * xs[hp:, :], axis=0, keepdims=True)
        db_ref[...] += jnp.sum(dcv, axis=0, keepdims=True)

    return pl.pallas_call(
        body, name=f"dwconv{K}_bwd", grid=(C // tc, B, nt),
        in_specs=[pl.BlockSpec((None, hp, tc), lambda j, b, i: (b, jnp.minimum((i + 1) * r, T // hp - 1), j)),
                  pl.BlockSpec((None, tm, tc), lambda j, b, i: (b, i, j)),
                  pl.BlockSpec((None, hp, tc), lambda j, b, i: (b, jnp.maximum(i * r - 1, 0), col0 + j)),
                  pl.BlockSpec((None, tm, tc), lambda j, b, i: (b, i, col0 + j)),
                  pl.BlockSpec((KP, tc), lambda j, b, i: (0, j))],
        out_specs=[pl.BlockSpec((None, tm, tc), lambda j, b, i: (b, i, j)),
                   pl.BlockSpec((KP, tc), lambda j, b, i: (0, j)),
                   pl.BlockSpec((1, tc), lambda j, b, i: (0, j))],
        out_shape=[jax.ShapeDtypeStruct((B, T, C), out_dtype), jax.ShapeDtypeStruct((KP, C), f32),
                   jax.ShapeDtypeStruct((1, C), f32)],
        compiler_params=_cp(),
    )(dc, dc, x, x, w)


POOL_HALO = 16


def _pool_windows(ue, pos, hp):
    out = []
    for g, wd in enumerate(POOL_WINDOWS):
        ug = ue[:, g * LANES:(g + 1) * LANES]
        s, span = ug, 1
        while span < wd:
            s = s + pltpu.roll(s, span, 0)
            span *= 2
        cnt = jnp.minimum(pos + 1, wd).astype(f32)
        out.append(s[hp:, :] / cnt - ug[hp:, :])
    return out


def pool_fwd(proj, colb, w, scale):
    B, T, _ = proj.shape
    hp = POOL_HALO
    tm = _row_tile(T, 512)
    r = tm // hp

    def body(uh_ref, u_ref, w_ref, sc_ref, o_ref):
        i = pl.program_id(1)
        ue = jnp.concatenate([jnp.where(i == 0, 0.0, uh_ref[...]), u_ref[...]], axis=0)
        pos = i * tm + lax.broadcasted_iota(jnp.int32, (tm, 1), 0)
        ps = _pool_windows(ue, pos, hp)
        o = jnp.concatenate([_dot(ps[g], w_ref[g]) for g in range(4)], axis=1) * sc_ref[...]
        o_ref[...] = o.astype(bf16)

    return pl.pallas_call(
        body, name="pool_fwd", grid=(B, T // tm),
        in_specs=[pl.BlockSpec((None, hp, POOL_DIM), lambda b, i: (b, jnp.maximum(i * r - 1, 0), colb)),
                  pl.BlockSpec((None, tm, POOL_DIM), lambda b, i: (b, i, colb)),
                  _const_spec(w.shape), _const_spec((1, POOL_DIM))],
        out_specs=pl.BlockSpec((None, tm, POOL_DIM), lambda b, i: (b, i, 0)),
        out_shape=jax.ShapeDtypeStruct((B, T, POOL_DIM), bf16),
        compiler_params=_cp(),
    )(proj, proj, w, scale)


def pool_bwd(dycat, dcolb, proj, colb, w, scale):
    B, T, _ = proj.shape
    hp = POOL_HALO
    tm = _row_tile(T, 512)
    r = tm // hp
    nt = T // tm
    n = tm + hp

    def body(dyn_ref, dy_ref, uh_ref, u_ref, w_ref, sc_ref, du_ref, dw_ref, dsc_ref):
        b, i = pl.program_id(0), pl.program_id(1)

        @pl.when((b == 0) & (i == 0))
        def _():
            dw_ref[...] = jnp.zeros_like(dw_ref)
            dsc_ref[...] = jnp.zeros_like(dsc_ref)

        dyv = dy_ref[...]
        dye = jnp.concatenate([dyv, jnp.where(i == nt - 1, 0.0, dyn_ref[...])], axis=0)
        ue = jnp.concatenate([jnp.where(i == 0, 0.0, uh_ref[...]), u_ref[...]], axis=0)
        pos = i * tm + lax.broadcasted_iota(jnp.int32, (tm, 1), 0)
        pos_e = i * tm + lax.broadcasted_iota(jnp.int32, (n, 1), 0)
        ps = _pool_windows(ue, pos, hp)
        dme = dye * sc_ref[...]
        dus, dscs = [], []
        for g, wd in enumerate(POOL_WINDOWS):
            sl = slice(g * LANES, (g + 1) * LANES)
            dscs.append(jnp.sum(dyv[:, sl] * _dot(ps[g], w_ref[g]), axis=0, keepdims=True))
            dw_ref[g] += _dot_tn(ps[g], dme[:tm, sl])
            dpe = _dot_nt(dme[:, sl], w_ref[g])
            s, span = dpe / jnp.minimum(pos_e + 1, wd).astype(f32), 1
            while span < wd:
                s = s + pltpu.roll(s, n - span, 0)
                span *= 2
            dus.append(s[:tm, :] - dpe[:tm, :])
        du_ref[...] = jnp.concatenate(dus, axis=1).astype(bf16)
        dsc_ref[...] += jnp.concatenate(dscs, axis=1)

    return pl.pallas_call(
        body, name="pool_bwd", grid=(B, nt),
        in_specs=[pl.BlockSpec((None, hp, POOL_DIM), lambda b, i: (b, jnp.minimum((i + 1) * r, T // hp - 1), dcolb)),
                  pl.BlockSpec((None, tm, POOL_DIM), lambda b, i: (b, i, dcolb)),
                  pl.BlockSpec((None, hp, POOL_DIM), lambda b, i: (b, jnp.maximum(i * r - 1, 0), colb)),
                  pl.BlockSpec((None, tm, POOL_DIM), lambda b, i: (b, i, colb)),
                  _const_spec(w.shape), _const_spec((1, POOL_DIM))],
        out_specs=[pl.BlockSpec((None, tm, POOL_DIM), lambda b, i: (b, i, 0)),
                   pl.BlockSpec(w.shape, lambda b, i: (0, 0, 0)),
                   pl.BlockSpec((1, POOL_DIM), lambda b, i: (0, 0))],
        out_shape=[jax.ShapeDtypeStruct((B, T, POOL_DIM), bf16), jax.ShapeDtypeStruct(w.shape, f32),
                   jax.ShapeDtypeStruct((1, POOL_DIM), f32)],
        compiler_params=_cp(),
    )(dycat, dycat, proj, proj, w, scale)


N_PAIRS = SSD_HEADS // 2


def _ssd_chunk(xs, bs, cs, dtp, zs, hs, dtb, alog, dsk, ngs):
    Q = SSD_CHUNK
    lane = lax.broadcasted_iota(jnp.int32, (1, LANES), 1)
    sub = lax.broadcasted_iota(jnp.int32, (LANES, 1), 0)
    causal = lax.broadcasted_iota(jnp.int32, (Q, Q), 0) >= lax.broadcasted_iota(jnp.int32, (Q, Q), 1)
    lane_lo, sub_lo = lane < 64, sub < 64

    def col(v, h):
        return jnp.sum(v * (lane == h).astype(f32), axis=1, keepdims=True)

    def row(vt, h):
        return jnp.sum(vt * (sub == h).astype(f32), axis=0, keepdims=True)

    dt = jax.nn.softplus(dtp + dtb)
    acum = _dot_hi(causal.astype(f32), dt * (-jnp.exp(alog)))
    acum_t = acum.T
    aend = jnp.sum(acum * (sub == Q - 1).astype(f32), axis=0, keepdims=True)
    outs, hn = [], []
    for grp in range(2):
        bv, cv = _silu(bs[grp]), _silu(cs[grp])
        gmat = _dot_nt(cv, bv)
        for j in range(4):
            p = grp * 4 + j
            h0, h1 = 2 * p, 2 * p + 1
            x2 = _silu(xs[p])
            c0, c1 = col(acum, h0), col(acum, h1)
            s2 = jnp.where(lane_lo, c0, c1)
            xdt = x2 * jnp.where(lane_lo, col(dt, h0), col(dt, h1))
            l0 = jnp.where(causal, jnp.exp(jnp.minimum(c0 - row(acum_t, h0), 0.0)), 0.0)
            l1 = jnp.where(causal, jnp.exp(jnp.minimum(c1 - row(acum_t, h1), 0.0)), 0.0)
            yd = _dot(gmat * l0, jnp.where(lane_lo, xdt, 0.0)) + _dot(gmat * l1, jnp.where(lane_lo, 0.0, xdt))
            e0, e1 = col(aend, h0), col(aend, h1)
            st = _dot_tn(xdt * jnp.exp(jnp.where(lane_lo, e0, e1) - s2), bv)
            yo = jnp.exp(s2) * _dot_nt(cv, hs[p])
            hn.append(jnp.exp(jnp.where(sub_lo, e0, e1)) * hs[p] + st)
            yv = yd + yo + x2 * jnp.where(lane_lo, col(dsk, h0), col(dsk, h1))
            outs.append(yv * _silu(zs[p]))
    ms = sum(jnp.sum(o * o, axis=1, keepdims=True) for o in outs) / SSD_D_INNER
    rs = lax.rsqrt(ms + NORM_EPS)
    return [outs[p] * rs * ngs[p] for p in range(N_PAIRS)], hn


def _lane_blocks(ref, n, start=0):
    return [ref[:, (start + k) * LANES:(start + k + 1) * LANES] for k in range(n)]


def _ssd_args(z_ref, cx_ref, dt_ref, dtb_ref, alog_ref, dsk_ref, ng_ref):
    xs = _lane_blocks(cx_ref, 8)
    bs = _lane_blocks(cx_ref, 2, 8)
    cs = _lane_blocks(cx_ref, 2, 10)
    zs = _lane_blocks(z_ref, 8)
    ngs = _lane_blocks(ng_ref, 8)
    return xs, bs, cs, dt_ref[...], zs, dtb_ref[...], alog_ref[...], dsk_ref[...], ngs


DT_COLB = (EVEN_IN_PAD - LANES) // LANES


def ssd_fwd(proj, cx, dtb, alog, dsk, ng):
    B, T, _ = proj.shape
    Q = SSD_CHUNK
    nc = T // Q

    def body(z_ref, cx_ref, dt_ref, dtb_ref, alog_ref, dsk_ref, ng_ref, ya_ref, hsave_ref, h_scr):
        @pl.when(pl.program_id(1) == 0)
        def _():
            h_scr[...] = jnp.zeros_like(h_scr)

        xs, bs, cs, dtp, zs, dtb_v, alog_v, dsk_v, ngs = _ssd_args(z_ref, cx_ref, dt_ref, dtb_ref, alog_ref, dsk_ref, ng_ref)
        hs = [h_scr[p] for p in range(N_PAIRS)]
        for p in range(N_PAIRS):
            hsave_ref[p] = hs[p]
        outs, hn = _ssd_chunk(xs, bs, cs, dtp, zs, hs, dtb_v, alog_v, dsk_v, ngs)
        for p in range(N_PAIRS):
            ya_ref[:, p * LANES:(p + 1) * LANES] = outs[p].astype(bf16)
            h_scr[p] = hn[p]

    return pl.pallas_call(
        body, name="ssd_fwd", grid=(B, nc),
        in_specs=[pl.BlockSpec((None, Q, SSD_D_INNER), lambda b, i: (b, i, 0)),
                  pl.BlockSpec((None, Q, SSD_CONV_DIM), lambda b, i: (b, i, 0)),
                  pl.BlockSpec((None, Q, LANES), lambda b, i: (b, i, DT_COLB)),
                  _const_spec((1, LANES)), _const_spec((1, LANES)), _const_spec((1, LANES)),
                  _const_spec((1, SSD_D_INNER))],
        out_specs=[pl.BlockSpec((None, Q, SSD_D_INNER), lambda b, i: (b, i, 0)),
                   pl.BlockSpec((None, None, N_PAIRS, LANES, LANES), lambda b, i: (b, i, 0, 0, 0))],
        out_shape=[jax.ShapeDtypeStruct((B, T, SSD_D_INNER), bf16),
                   jax.ShapeDtypeStruct((B, nc, N_PAIRS, LANES, LANES), f32)],
        scratch_shapes=[pltpu.VMEM((N_PAIRS, LANES, LANES), f32)],
        compiler_params=_cp(),
    )(proj, cx, proj, dtb, alog, dsk, ng)


def ssd_bwd(dycat, proj, cx, hsave, dtb, alog, dsk, ng):
    B, T, _ = proj.shape
    Q = SSD_CHUNK
    nc = T // Q

    def body(dya_ref, z_ref, cx_ref, dt_ref, hsave_ref, dtb_ref, alog_ref, dsk_ref, ng_ref,
             dz_ref, dcx_ref, ddt_ref, gsm_ref, gng_ref, dh_scr):
        b, i = pl.program_id(0), pl.program_id(1)

        @pl.when((b == 0) & (i == 0))
        def _():
            gsm_ref[...] = jnp.zeros_like(gsm_ref)
            gng_ref[...] = jnp.zeros_like(gng_ref)

        @pl.when(i == 0)
        def _():
            dh_scr[...] = jnp.zeros_like(dh_scr)

        xs, bs, cs, dtp, zs, dtb_v, alog_v, dsk_v, ngs = _ssd_args(z_ref, cx_ref, dt_ref, dtb_ref, alog_ref, dsk_ref, ng_ref)
        hs = [hsave_ref[p] for p in range(N_PAIRS)]
        _, vjp = jax.vjp(_ssd_chunk, xs, bs, cs, dtp, zs, hs, dtb_v, alog_v, dsk_v, ngs)
        douts = _lane_blocks(dya_ref, 8)
        dhn = [dh_scr[p] for p in range(N_PAIRS)]
        dxs, dbs, dcs, ddtp, dzs, dhs, ddtb, dalog, ddsk, dngs = vjp((douts, dhn))
        for p in range(N_PAIRS):
            dcx_ref[:, p * LANES:(p + 1) * LANES] = dxs[p]
            dz_ref[:, p * LANES:(p + 1) * LANES] = dzs[p].astype(bf16)
            dh_scr[p] = dhs[p]
            gng_ref[:, p * LANES:(p + 1) * LANES] += dngs[p]
        for k in range(2):
            dcx_ref[:, (8 + k) * LANES:(9 + k) * LANES] = dbs[k]
            dcx_ref[:, (10 + k) * LANES:(11 + k) * LANES] = dcs[k]
        ddt_ref[...] = ddtp.astype(bf16)
        gsm_ref[0:1, :] += ddtb
        gsm_ref[1:2, :] += dalog
        gsm_ref[2:3, :] += ddsk

    rev = lambda w, cb=0: pl.BlockSpec((None, Q, w), lambda b, i: (b, nc - 1 - i, cb))
    return pl.pallas_call(
        body, name="ssd_bwd", grid=(B, nc),
        in_specs=[rev(SSD_D_INNER), rev(SSD_D_INNER), rev(SSD_CONV_DIM), rev(LANES, DT_COLB),
                  pl.BlockSpec((None, None, N_PAIRS, LANES, LANES), lambda b, i: (b, nc - 1 - i, 0, 0, 0)),
                  _const_spec((1, LANES)), _const_spec((1, LANES)), _const_spec((1, LANES)),
                  _const_spec((1, SSD_D_INNER))],
        out_specs=[rev(SSD_D_INNER), rev(SSD_CONV_DIM), rev(LANES),
                   pl.BlockSpec((3, LANES), lambda b, i: (0, 0)),
                   pl.BlockSpec((1, SSD_D_INNER), lambda b, i: (0, 0))],
        out_shape=[jax.ShapeDtypeStruct((B, T, SSD_D_INNER), bf16), jax.ShapeDtypeStruct((B, T, SSD_CONV_DIM), f32),
                   jax.ShapeDtypeStruct((B, T, LANES), bf16), jax.ShapeDtypeStruct((3, LANES), f32),
                   jax.ShapeDtypeStruct((1, SSD_D_INNER), f32)],
        scratch_shapes=[pltpu.VMEM((N_PAIRS, LANES, LANES), f32)],
        compiler_params=_cp(),
    )(dycat, proj, cx, proj, hsave, dtb, alog, dsk, ng)


def glu_fwd(proj):
    B, T, _ = proj.shape
    tm = _row_tile(T, 1024)

    def body(v_ref, g_ref, o_ref):
        o_ref[...] = v_ref[...] * jax.nn.sigmoid(g_ref[...])

    blk = lambda cb: pl.BlockSpec((None, tm, CONF_DIM), lambda b, i: (b, i, cb))
    return pl.pallas_call(body, name="glu_fwd", grid=(B, T // tm), in_specs=[blk(0), blk(1)], out_specs=blk(0),
                          out_shape=jax.ShapeDtypeStruct((B, T, CONF_DIM), f32), compiler_params=_cp())(proj, proj)


def glu_bwd(dhh, proj):
    B, T, _ = proj.shape
    tm = _row_tile(T, 1024)

    def body(d_ref, v_ref, g_ref, o_ref):
        sig = jax.nn.sigmoid(g_ref[...])
        dv = d_ref[...]
        o_ref[:, :CONF_DIM] = (dv * sig).astype(bf16)
        o_ref[:, CONF_DIM:] = (dv * v_ref[...] * sig * (1.0 - sig)).astype(bf16)

    blk = lambda cb: pl.BlockSpec((None, tm, CONF_DIM), lambda b, i: (b, i, cb))
    return pl.pallas_call(body, name="glu_bwd", grid=(B, T // tm), in_specs=[blk(0), blk(0), blk(1)],
                          out_specs=pl.BlockSpec((None, tm, 2 * CONF_DIM), lambda b, i: (b, i, 0)),
                          out_shape=jax.ShapeDtypeStruct((B, T, 2 * CONF_DIM), bf16), compiler_params=_cp())(dhh, proj, proj)


def _neg_expm1(x):
    series = x * (1.0 + x * (1.0 / 2.0) * (1.0 + x * (1.0 / 3.0) * (1.0 + x * (1.0 / 4.0) * (1.0 + x * (1.0 / 5.0)))))
    return -jnp.where(x > -0.1, series, jnp.exp(x) - 1.0)


def _lru_gates(cc, xc8, gr8, clg, clb, wa8, wx8, ba8, bx8, lam8):
    xhat, _ = _ln_stats(cc)
    yc = _silu(xhat * clg + clb)
    a8, b8, ge8 = [], [], []
    for hb in range(LRU_HEADS):
        xh = xc8[hb]
        rg = jax.nn.sigmoid(_dot(xh, wa8[hb]) + ba8[hb])
        ig = jax.nn.sigmoid(_dot(xh, wx8[hb]) + bx8[hb])
        log_a = -LRU_C * rg * jax.nn.softplus(-lam8[hb])
        a8.append(jnp.exp(log_a))
        b8.append(jnp.sqrt(_neg_expm1(2.0 * log_a)) * (ig * xh))
        ge8.append(jax.nn.gelu(gr8[hb]))
    return yc, a8, b8, ge8


def _scan_fwd(a, b, h_in):
    tm = a.shape[0]
    rows = lax.broadcasted_iota(jnp.int32, (tm, 1), 0)
    s = 1
    while s < tm:
        keep = rows >= s
        b = a * jnp.where(keep, pltpu.roll(b, s, 0), 0.0) + b
        a = a * jnp.where(keep, pltpu.roll(a, s, 0), 1.0)
        s *= 2
    return a * h_in + b


def _scan_bwd(e, d, g_in):
    tm = e.shape[0]
    rows = lax.broadcasted_iota(jnp.int32, (tm, 1), 0)
    s = 1
    while s < tm:
        keep = rows < tm - s
        d = e * jnp.where(keep, pltpu.roll(d, tm - s, 0), 0.0) + d
        e = e * jnp.where(keep, pltpu.roll(e, tm - s, 0), 1.0)
        s *= 2
    return e * g_in + d


def _lru_params(wa_ref, wx_ref, vec_ref):
    wa8 = [wa_ref[h] for h in range(LRU_HEADS)]
    wx8 = [wx_ref[h] for h in range(LRU_HEADS)]
    ba8 = [vec_ref[0:1, h * LANES:(h + 1) * LANES] for h in range(LRU_HEADS)]
    bx8 = [vec_ref[1:2, h * LANES:(h + 1) * LANES] for h in range(LRU_HEADS)]
    lam8 = [vec_ref[2:3, h * LANES:(h + 1) * LANES] for h in range(LRU_HEADS)]
    return wa8, wx8, ba8, bx8, lam8


GR_COLB = 2


def lru_fwd(cc, xc, proj, cln, wa, wx, vec):
    B, T, _ = xc.shape
    tm = _row_tile(T)

    def body(cc_ref, xc_ref, gr_ref, cln_ref, wa_ref, wx_ref, vec_ref, y_ref, hs_ref, h_scr):
        @pl.when(pl.program_id(1) == 0)
        def _():
            h_scr[...] = jnp.zeros_like(h_scr)

        yc, a8, b8, ge8 = _lru_gates(cc_ref[...], _lane_blocks(xc_ref, 8), _lane_blocks(gr_ref, 8), cln_ref[0:1, :],
                                     cln_ref[1:2, :], *_lru_params(wa_ref, wx_ref, vec_ref))
        h = _scan_fwd(jnp.concatenate(a8, axis=1), jnp.concatenate(b8, axis=1), h_scr[...])
        hs_ref[...] = h
        h_scr[...] = h[tm - 1:tm, :]
        y_ref[:, :CONF_DIM] = yc.astype(bf16)
        y_ref[:, CONF_DIM:] = (h * jnp.concatenate(ge8, axis=1)).astype(bf16)

    row = lambda w, cb=0: pl.BlockSpec((None, tm, w), lambda b, i: (b, i, cb))
    return pl.pallas_call(
        body, name="lru_fwd", grid=(B, T // tm),
        in_specs=[row(CONF_DIM), row(LRU_DIM), row(LRU_DIM, GR_COLB), _const_spec((2, CONF_DIM)),
                  _const_spec(wa.shape), _const_spec(wx.shape), _const_spec((3, LRU_DIM))],
        out_specs=[row(CONF_DIM + LRU_DIM), row(LRU_DIM)],
        out_shape=[jax.ShapeDtypeStruct((B, T, CONF_DIM + LRU_DIM), bf16), jax.ShapeDtypeStruct((B, T, LRU_DIM), f32)],
        scratch_shapes=[pltpu.VMEM((1, LRU_DIM), f32)],
        compiler_params=_cp(),
    )(cc, xc, proj, cln, wa, wx, vec)


def lru_bwd(dycat, cc, xc, proj, hs, cln, wa, wx, vec):
    B, T, _ = xc.shape
    tm = _row_tile(T)
    nt = T // tm
    r = tm // 8

    def body(dy_ref, cc_ref, xc_ref, gr_ref, hs_ref, hsh_ref, cln_ref, wa_ref, wx_ref, vec_ref,
             dcc_ref, dxc_ref, dgr_ref, dcln_ref, dwa_ref, dwx_ref, dvec_ref, g_scr, a_scr):
        b, i = pl.program_id(0), pl.program_id(1)
        it = nt - 1 - i

        @pl.when((b == 0) & (i == 0))
        def _():
            dcln_ref[...] = jnp.zeros_like(dcln_ref)
            dwa_ref[...] = jnp.zeros_like(dwa_ref)
            dwx_ref[...] = jnp.zeros_like(dwx_ref)
            dvec_ref[...] = jnp.zeros_like(dvec_ref)

        @pl.when(i == 0)
        def _():
            g_scr[...] = jnp.zeros_like(g_scr)
            a_scr[...] = jnp.zeros_like(a_scr)

        (yc, a8, b8, ge8), vjp = jax.vjp(_lru_gates, cc_ref[...], _lane_blocks(xc_ref, 8), _lane_blocks(gr_ref, 8),
                                         cln_ref[0:1, :], cln_ref[1:2, :], *_lru_params(wa_ref, wx_ref, vec_ref))
        a = jnp.concatenate(a8, axis=1)
        ge = jnp.concatenate(ge8, axis=1)
        h = hs_ref[...]
        dyd = dy_ref[:, CONF_DIM:]
        rows = lax.broadcasted_iota(jnp.int32, (tm, 1), 0)
        e = jnp.where(rows < tm - 1, pltpu.roll(a, tm - 1, 0), a_scr[...])
        g = _scan_bwd(e, dyd * ge, g_scr[...])
        h_first = jnp.where(it == 0, 0.0, hsh_ref[7:8, :])
        h_prev = jnp.where(rows >= 1, pltpu.roll(h, 1, 0), h_first)
        da = g * h_prev
        g_scr[...] = g[0:1, :]
        a_scr[...] = a[0:1, :]
        split = lambda v: [v[:, k * LANES:(k + 1) * LANES] for k in range(LRU_HEADS)]
        dcc, dxc8, dgr8, dclg, dclb, dwa8, dwx8, dba8, dbx8, dlam8 = vjp((dy_ref[:, :CONF_DIM], split(da), split(g), split(dyd * h)))
        dcc_ref[...] = dcc
        dcln_ref[0:1, :] += dclg
        dcln_ref[1:2, :] += dclb
        for k in range(LRU_HEADS):
            sl = slice(k * LANES, (k + 1) * LANES)
            dxc_ref[:, sl] = dxc8[k]
            dgr_ref[:, sl] = dgr8[k].astype(bf16)
            dwa_ref[k] += dwa8[k]
            dwx_ref[k] += dwx8[k]
            dvec_ref[0:1, sl] += dba8[k]
            dvec_ref[1:2, sl] += dbx8[k]
            dvec_ref[2:3, sl] += dlam8[k]

    rev = lambda w, cb=0: pl.BlockSpec((None, tm, w), lambda b, i: (b, nt - 1 - i, cb))
    acc = lambda shape: pl.BlockSpec(shape, lambda b, i: (0,) * len(shape))
    return pl.pallas_call(
        body, name="lru_bwd", grid=(B, nt),
        in_specs=[rev(CONF_DIM + LRU_DIM), rev(CONF_DIM), rev(LRU_DIM), rev(LRU_DIM, GR_COLB), rev(LRU_DIM),
                  pl.BlockSpec((None, 8, LRU_DIM), lambda b, i: (b, jnp.maximum((nt - 1 - i) * r - 1, 0), 0)),
                  _const_spec((2, CONF_DIM)), _const_spec(wa.shape), _const_spec(wx.shape), _const_spec((3, LRU_DIM))],
        out_specs=[rev(CONF_DIM), rev(LRU_DIM), rev(LRU_DIM), acc((2, CONF_DIM)), acc(wa.shape), acc(wx.shape),
                   acc((3, LRU_DIM))],
        out_shape=[jax.ShapeDtypeStruct((B, T, CONF_DIM), f32), jax.ShapeDtypeStruct((B, T, LRU_DIM), f32),
                   jax.ShapeDtypeStruct((B, T, LRU_DIM), bf16), jax.ShapeDtypeStruct((2, CONF_DIM), f32),
                   jax.ShapeDtypeStruct(wa.shape, f32), jax.ShapeDtypeStruct(wx.shape, f32),
                   jax.ShapeDtypeStruct((3, LRU_DIM), f32)],
        scratch_shapes=[pltpu.VMEM((1, LRU_DIM), f32), pltpu.VMEM((1, LRU_DIM), f32)],
        compiler_params=_cp(),
    )(dycat, cc, xc, proj, hs, hs, cln, wa, wx, vec)


def loss_fwd(y, target):
    B, T, D = y.shape
    tm = _row_tile(T)

    def body(y_ref, t_ref, l_ref, dy_ref):
        @pl.when((pl.program_id(0) == 0) & (pl.program_id(1) == 0))
        def _():
            l_ref[...] = jnp.zeros_like(l_ref)

        d = y_ref[...] - t_ref[...]
        dy_ref[...] = d * (1.0 / D)
        l_ref[...] += jnp.sum(jnp.sum(d * d, axis=1, keepdims=True), axis=0, keepdims=True)

    row = pl.BlockSpec((None, tm, D), lambda b, i: (b, i, 0))
    return pl.pallas_call(
        body, name="loss_fwd", grid=(B, T // tm), in_specs=[row, row],
        out_specs=[pl.BlockSpec((1, 1), lambda b, i: (0, 0)), row],
        out_shape=[jax.ShapeDtypeStruct((1, 1), f32), jax.ShapeDtypeStruct((B, T, D), f32)],
        compiler_params=_cp(),
    )(y, target)


ADA_COL_TILE = 768


def ada_fwd(c_all, w, b):
    L, D, N = w.shape
    nb = c_all.shape[0]
    tn = ADA_COL_TILE

    def body(c_ref, w_ref, b_ref, o_ref):
        o_ref[...] = _dot_hi(_silu(c_ref[...]), w_ref[...]) + b_ref[...]

    return pl.pallas_call(
        body, name="ada_fwd", grid=(L, N // tn),
        in_specs=[pl.BlockSpec((nb, D), lambda l, j: (0, 0)), pl.BlockSpec((None, D, tn), lambda l, j: (l, 0, j)),
                  pl.BlockSpec((None, 1, tn), lambda l, j: (l, 0, j))],
        out_specs=pl.BlockSpec((None, nb, tn), lambda l, j: (l, 0, j)),
        out_shape=jax.ShapeDtypeStruct((L, nb, N), f32),
        compiler_params=_cp(),
    )(c_all, w, b)


def ada_bwd(c_all, dmod_my, dmod_all):
    L, nb, N = dmod_my.shape
    D = c_all.shape[1]
    NA = dmod_all.shape[2]
    tn = ADA_COL_TILE
    nj = N // tn
    ta = NA // nj

    def body(c_ref, dm_ref, da_ref, gw_ref, gb_ref):
        gw_ref[...] = lax.dot_general(_silu(c_ref[...]), dm_ref[...], (((0,), (0,)), ((), ())),
                                      preferred_element_type=f32, precision=lax.Precision.HIGHEST)
        gb_ref[...] = jnp.sum(da_ref[...], axis=0, keepdims=True)

    return pl.pallas_call(
        body, name="ada_bwd", grid=(L, nj),
        in_specs=[pl.BlockSpec((nb, D), lambda l, j: (0, 0)), pl.BlockSpec((None, nb, tn), lambda l, j: (l, 0, j)),
                  pl.BlockSpec((None, nb, ta), lambda l, j: (l, 0, j))],
        out_specs=[pl.BlockSpec((None, D, tn), lambda l, j: (l, 0, j)), pl.BlockSpec((None, 1, ta), lambda l, j: (l, 0, j))],
        out_shape=[jax.ShapeDtypeStruct((L, D, N), f32), jax.ShapeDtypeStruct((L, 1, NA), f32)],
        compiler_params=_cp(),
    )(c_all, dmod_my, dmod_all)


def adamw(w, m, v, gs, offs, tr):
    R, C = w.shape
    ng = len(gs)
    c1 = 1.0 - ADAM_B1 ** ADAM_STEP
    c2 = 1.0 - ADAM_B2 ** ADAM_STEP

    def body(*refs):
        w_ref, m_ref, v_ref = refs[:3]
        g_refs = refs[3:3 + ng]
        d_out, m_out, v_out = refs[3 + ng:]
        g = g_refs[0][...]
        for r in g_refs[1:]:
            g = g + r[...]
        mn = ADAM_B1 * m_ref[...] + (1.0 - ADAM_B1) * g
        vn = ADAM_B2 * v_ref[...] + (1.0 - ADAM_B2) * (g * g)
        m_out[...] = mn
        v_out[...] = vn
        d_out[...] = -ADAM_LR * ((mn / c1) / (jnp.sqrt(vn / c2) + ADAM_EPS) + ADAM_WD * w_ref[...])

    blk = pl.BlockSpec((tr, C), lambda i: (i, 0))
    gspec = lambda off: pl.BlockSpec((tr, C), lambda i: (i + off // tr, 0))
    sds = jax.ShapeDtypeStruct((R, C), f32)
    return pl.pallas_call(
        body, name="adamw", grid=(pl.cdiv(R, tr),), in_specs=[blk, blk, blk] + [gspec(o) for o in offs],
        out_specs=[blk] * 3, out_shape=[sds] * 3, compiler_params=_cp(),
    )(w, m, v, *gs)


def sum_leading(a, tr):
    k, R, C = a.shape

    def body(a_ref, o_ref):
        s = a_ref[0]
        for j in range(1, k):
            s = s + a_ref[j]
        o_ref[...] = s

    return pl.pallas_call(
        body, name="sum_leading", grid=(R // tr,), in_specs=[pl.BlockSpec((k, tr, C), lambda i: (0, i, 0))],
        out_specs=pl.BlockSpec((tr, C), lambda i: (i, 0)), out_shape=jax.ShapeDtypeStruct((R, C), a.dtype),
        compiler_params=_cp(),
    )(a)


ANY = pl.BlockSpec(memory_space=pl.ANY)
CHIP_FLIPS = ((1, 0), (0, 1), (1, 1))
DEV_FLIPS = tuple((fx, fy, fc) for fx in (0, 1) for fy in (0, 1) for fc in (0, 1))[1:]


def _flip(v, f):
    return 1 - v if f else v


def _put(out, v, idx, axis=0):
    return lax.dynamic_update_slice_in_dim(out, jnp.expand_dims(v, axis) if v.ndim < out.ndim else v, idx, axis)


def allgather8(v):
    R, C = v.shape
    n = len(DEV_FLIPS)

    def body(v_ref, o_ref, send_sems, recv_sems):
        x, y, c = lax.axis_index("x"), lax.axis_index("y"), lax.axis_index("c")
        me = 4 * x + 2 * y + c
        peers = [(_flip(x, fx), _flip(y, fy), _flip(c, fc)) for fx, fy, fc in DEV_FLIPS]

        def copy(k, slot):
            return pltpu.make_async_remote_copy(src_ref=v_ref, dst_ref=o_ref.at[slot], send_sem=send_sems.at[k],
                                                recv_sem=recv_sems.at[k], device_id=peers[k], device_id_type=MESH)

        for k in range(n):
            copy(k, me).start()
        for k, (px, py, pc) in enumerate(peers):
            copy(k, 4 * px + 2 * py + pc).wait_recv()
        for k in range(n):
            copy(k, me).wait_send()

    out = pl.pallas_call(
        body, name="allgather8", in_specs=[ANY], out_specs=ANY, out_shape=jax.ShapeDtypeStruct((N_DEV, R, C), v.dtype),
        scratch_shapes=[pltpu.SemaphoreType.DMA((n,)), pltpu.SemaphoreType.DMA((n,))],
    )(v)
    return _put(out, v, 4 * lax.axis_index("x") + 2 * lax.axis_index("y") + lax.axis_index("c"))


def _half(ref_or_shape0, c):
    hsz = ref_or_shape0 // 2
    return pl.ds(c * hsz, hsz)


def _gather_steps(w_refs, o_refs, send_sems, recv_sems):
    nw, nc = len(w_refs), len(CHIP_FLIPS)
    x, y, c = lax.axis_index("x"), lax.axis_index("y"), lax.axis_index("c")
    me, sibling = 2 * x + y, (x, y, 1 - c)
    peers = [(_flip(x, fx), _flip(y, fy), c) for fx, fy in CHIP_FLIPS]
    slots = [2 * px + py for px, py, _ in peers]

    def copy(a, j, slot, half, to, own=False):
        hs = _half(w_refs[a].shape[0], half)
        return pltpu.make_async_remote_copy(src_ref=w_refs[a].at[hs] if own else o_refs[a].at[slot, hs],
                                            dst_ref=o_refs[a].at[slot, hs], send_sem=send_sems.at[j],
                                            recv_sem=recv_sems.at[j], device_id=to, device_id_type=MESH)

    first = [copy(a, a * nc + k, me, c, peers[k], own=True) for a in range(nw) for k in range(nc)]

    def start():
        for cp in first:
            cp.start()

    def finish():
        passed = []
        for a in range(nw):
            for k in range(nc):
                copy(a, a * nc + k, slots[k], c, peers[k]).wait_recv()
                passed.append(copy(a, nw * nc + a * nc + k, slots[k], c, sibling))
                passed[-1].start()
        for a in range(nw):
            for k in range(nc):
                copy(a, nw * nc + a * nc + k, slots[k], 1 - c, sibling).wait_recv()
        for cp in first + passed:
            cp.wait_send()

    return start, finish


def _gather_sems(nw):
    n = 2 * nw * len(CHIP_FLIPS)
    return [pltpu.SemaphoreType.DMA((n,)), pltpu.SemaphoreType.DMA((n,))]


def _gather_out_shapes(ws):
    return [jax.ShapeDtypeStruct((N_CHIPS,) + w.shape, w.dtype) for w in ws]


def _fill_own(outs, ws):
    if not ws:
        return []
    chip =2 * lax.axis_index("x") + lax.axis_index("y")
    return [_put(o, w, chip) for o, w in zip(outs, ws)]


def gather_weights(ws):
    nw = len(ws)

    def body(*refs):
        start, finish = _gather_steps(refs[:nw], refs[nw:2 * nw], *refs[2 * nw:])
        start()
        finish()

    outs = pl.pallas_call(
        body, name="gather_weights", in_specs=[ANY] * nw, out_specs=[ANY] * nw,
        out_shape=_gather_out_shapes(ws), scratch_shapes=_gather_sems(nw),
    )(*ws)
    return _fill_own(outs, ws)


def swap_halves(gs):
    nw = len(gs)

    def body(*refs):
        g_refs, t_refs = refs[:nw], refs[nw:2 * nw]
        send_sems, recv_sems = refs[2 * nw:]
        x, y, c = lax.axis_index("x"), lax.axis_index("y"), lax.axis_index("c")
        cps = [pltpu.make_async_remote_copy(src_ref=g_refs[a].at[_half(gs[a].shape[0], 1 - c)], dst_ref=t_refs[a],
                                            send_sem=send_sems.at[a], recv_sem=recv_sems.at[a],
                                            device_id=(x, y, 1 - c), device_id_type=MESH) for a in range(nw)]
        for cp in cps:
            cp.start()
        for cp in cps:
            cp.wait()

    return pl.pallas_call(
        body, name="swap_halves", in_specs=[ANY] * nw, out_specs=[ANY] * nw,
        out_shape=[jax.ShapeDtypeStruct((g.shape[0] // 2,) + g.shape[1:], g.dtype) for g in gs],
        scratch_shapes=[pltpu.SemaphoreType.DMA((nw,)), pltpu.SemaphoreType.DMA((nw,))],
    )(*gs)


def exchange_blocks(ps):
    nw, nc = len(ps), len(CHIP_FLIPS)

    def body(*refs):
        p_refs, r_refs = refs[:nw], refs[nw:2 * nw]
        send_sems, recv_sems = refs[2 * nw:]
        x, y, c = lax.axis_index("x"), lax.axis_index("y"), lax.axis_index("c")
        me = 2 * x + y
        peers = [(_flip(x, fx), _flip(y, fy), c) for fx, fy in CHIP_FLIPS]
        slots = [2 * px + py for px, py, _ in peers]

        def copy(a, k, src_slot, dst_slot):
            return pltpu.make_async_remote_copy(src_ref=p_refs[a].at[src_slot], dst_ref=r_refs[a].at[dst_slot],
                                                send_sem=send_sems.at[a * nc + k], recv_sem=recv_sems.at[a * nc + k],
                                                device_id=peers[k], device_id_type=MESH)

        for a in range(nw):
            for k in range(nc):
                copy(a, k, slots[k], me).start()
        for a in range(nw):
            for k in range(nc):
                copy(a, k, me, slots[k]).wait_recv()
        for a in range(nw):
            for k in range(nc):
                copy(a, k, slots[k], me).wait_send()

    outs = pl.pallas_call(
        body, name="exchange_blocks", in_specs=[ANY] * nw, out_specs=[ANY] * nw,
        out_shape=[jax.ShapeDtypeStruct(p.shape, p.dtype) for p in ps],
        scratch_shapes=[pltpu.SemaphoreType.DMA((nw * nc,)), pltpu.SemaphoreType.DMA((nw * nc,))],
    )(*ps)
    chip = 2 * lax.axis_index("x") + lax.axis_index("y")
    return [_put(o, lax.dynamic_slice_in_dim(p, chip, 1, axis=0), chip) for o, p in zip(outs, ps)]


def join_halves(ss):
    nw = len(ss)

    def body(*refs):
        s_refs, o_refs = refs[:nw], refs[nw:2 * nw]
        send_sems, recv_sems = refs[2 * nw:]
        x, y, c = lax.axis_index("x"), lax.axis_index("y"), lax.axis_index("c")

        def copy(a, half):
            hs = _half(2 * ss[a].shape[0], half)
            return pltpu.make_async_remote_copy(src_ref=s_refs[a], dst_ref=o_refs[a].at[hs], send_sem=send_sems.at[a],
                                                recv_sem=recv_sems.at[a], device_id=(x, y, 1 - c), device_id_type=MESH)

        for a in range(nw):
            copy(a, c).start()
        for a in range(nw):
            copy(a, 1 - c).wait_recv()
        for a in range(nw):
            copy(a, c).wait_send()

    outs = pl.pallas_call(
        body, name="join_halves", in_specs=[ANY] * nw, out_specs=[ANY] * nw,
        out_shape=[jax.ShapeDtypeStruct((2 * s.shape[0],) + s.shape[1:], s.dtype) for s in ss],
        scratch_shapes=[pltpu.SemaphoreType.DMA((nw,)), pltpu.SemaphoreType.DMA((nw,))],
    )(*ss)
    c = lax.axis_index("c")
    return [_put(o, s, c * s.shape[0]) for o, s in zip(outs, ss)]


def _tile_rows(a, b, itemsize=4, budget=4 * 2 ** 20):
    best = 8
    for t in range(8, a + 1, 8):
        if a % t == 0 and t * b * itemsize <= budget:
            best = t
    return best


def add_half(g, t, cidx):
    def body(c_ref, g_ref, t_ref, o_ref):
        o_ref[...] = (g_ref[...] + t_ref[...]).astype(bf16)

    if g.ndim == 5:
        n0, F, _, A, B = g.shape
        hsz, ta = n0 // 2, _tile_rows(A, B)
        nta = A // ta
        grid = (hsz, N_CHIPS, F, nta)
        in_specs = [pl.BlockSpec((None, None, None, ta, B), lambda h, k, f, i, c_ref: (c_ref[0] * hsz + h, f, k, i, 0)),
                    pl.BlockSpec((None, None, None, ta, B), lambda h, k, f, i, c_ref: (h, f, k, i, 0))]
        out_spec = pl.BlockSpec((None, None, ta, B), lambda h, k, f, i, c_ref: (k, h, f * nta + i, 0))
    else:
        n0, _, A, B = g.shape
        F, hsz, ta = 1, n0 // 2, _tile_rows(A, B)
        grid = (hsz, N_CHIPS, A // ta)
        in_specs = [pl.BlockSpec((None, None, ta, B), lambda h, k, i, c_ref: (c_ref[0] * hsz + h, k, i, 0)),
                    pl.BlockSpec((None, None, ta, B), lambda h, k, i, c_ref: (h, k, i, 0))]
        out_spec = pl.BlockSpec((None, None, ta, B), lambda h, k, i, c_ref: (k, h, i, 0))
    return pl.pallas_call(
        body, name="add_half",
        grid_spec=pltpu.PrefetchScalarGridSpec(num_scalar_prefetch=1, grid=grid, in_specs=in_specs, out_specs=out_spec),
        out_shape=jax.ShapeDtypeStruct((N_CHIPS, hsz, F * A, B), bf16),
        compiler_params=_cp(),
    )(cidx, g, t)


def sum_chips(r):
    _, h, A, B = r.shape
    ta = _tile_rows(A, B)

    def body(r_ref, o_ref):
        s = r_ref[0].astype(f32)
        for j in range(1, N_CHIPS):
            s = s + r_ref[j].astype(f32)
        o_ref[...] = s

    return pl.pallas_call(
        body, name="sum_chips", grid=(h, A // ta),
        in_specs=[pl.BlockSpec((N_CHIPS, None, ta, B), lambda hh, i: (0, hh, i, 0))],
        out_specs=pl.BlockSpec((None, ta, B), lambda hh, i: (hh, i, 0)),
        out_shape=jax.ShapeDtypeStruct((h, A, B), f32),
        compiler_params=_cp(),
    )(r)


WEIGHTS = ('ada_w', 'ada_b', 'ln_g', 'ln_b', 'ffn_w_in', 'ffn_w_out', 'ev_w_in', 'ssd_conv_w', 'ssd_conv_b',
           'ssd_dt_bias', 'ssd_a_log', 'ssd_d', 'ssd_norm_g', 'pool_w', 'pool_scale', 'ev_w_out', 'od_w_in',
           'conf_dw_w', 'conf_dw_b', 'conf_ln_g', 'conf_ln_b', 'lru_conv_w', 'lru_conv_b', 'lru_wa', 'lru_ba',
           'lru_wx', 'lru_bx', 'lru_lambda', 'od_w_out')
BIG =('ffn_w_in', 'ffn_w_out', 'ev_w_out', 'od_w_in', 'od_w_out', 'ev_w_in')
SMALL_SHARDED = ('ln_g', 'ln_b', 'ssd_conv_w', 'conf_dw_w', 'conf_dw_b', 'conf_ln_g', 'conf_ln_b', 'lru_conv_w',
                 'lru_conv_b', 'lru_ba', 'lru_bx', 'lru_lambda')
SMALL_REPLICATED = ('ssd_conv_b', 'ssd_dt_bias', 'ssd_a_log', 'ssd_d', 'ssd_norm_g', 'pool_w', 'pool_scale',
                    'lru_wa', 'lru_wx')
PACK_COLS = 1024
BIG_ROW_TILE = 256


def _pack(arrs, row_mult):
    flat = jnp.concatenate([a.reshape(-1) for a in arrs])
    rows = -(-flat.shape[0] // (PACK_COLS * row_mult)) * row_mult
    return jnp.pad(flat, (0, rows * PACK_COLS - flat.shape[0])).reshape(rows, PACK_COLS)


def _unpack(flat, shapes):
    out, off = [], 0
    for s in shapes:
        n = 1
        for d in s:
            n *= d
        out.append(flat[off:off + n].reshape(s))
        off += n
    return out


def _unshard_last(g4):
    m = jnp.moveaxis(g4, 0, -2)
    return m.reshape(m.shape[:-2] + (m.shape[-2] * m.shape[-1],))


def _pad_rows(a, rows):
    return jnp.pad(a, ((0, rows - a.shape[0]),) + ((0, 0),) * (a.ndim - 1))


def _pad_lanes(a):
    return jnp.pad(a, ((0, 0), (0, LANES - a.shape[1])))


def kernel(x, c, ada_w, ada_b, ln_g, ln_b, ffn_w_in, ffn_w_out, ev_w_in, ssd_conv_w, ssd_conv_b, ssd_dt_bias, ssd_a_log, ssd_d, ssd_norm_g, pool_w, pool_scale, ev_w_out, od_w_in, conf_dw_w, conf_dw_b, conf_ln_g, conf_ln_b, lru_conv_w, lru_conv_b, lru_wa, lru_ba, lru_wx, lru_bx, lru_lambda, od_w_out, loss_target, m_ada_w, m_ada_b, m_ln_g, m_ln_b, m_ffn_w_in, m_ffn_w_out, m_ev_w_in, m_ssd_conv_w, m_ssd_conv_b, m_ssd_dt_bias, m_ssd_a_log, m_ssd_d, m_ssd_norm_g, m_pool_w, m_pool_scale, m_ev_w_out, m_od_w_in, m_conf_dw_w, m_conf_dw_b, m_conf_ln_g, m_conf_ln_b, m_lru_conv_w, m_lru_conv_b, m_lru_wa, m_lru_ba, m_lru_wx, m_lru_bx, m_lru_lambda, m_od_w_out, v_ada_w, v_ada_b, v_ln_g, v_ln_b, v_ffn_w_in, v_ffn_w_out, v_ev_w_in, v_ssd_conv_w, v_ssd_conv_b, v_ssd_dt_bias, v_ssd_a_log, v_ssd_d, v_ssd_norm_g, v_pool_w, v_pool_scale, v_ev_w_out, v_od_w_in, v_conf_dw_w, v_conf_dw_b, v_conf_ln_g, v_conf_ln_b, v_lru_conv_w, v_lru_conv_b, v_lru_wa, v_lru_ba, v_lru_wx, v_lru_bx, v_lru_lambda, v_od_w_out):
    given = dict(locals())
    W = {n: given[n] for n in WEIGHTS}
    M = {n: given["m_" + n] for n in WEIGHTS}
    V = {n: given["v_" + n] for n in WEIGHTS}
    B, T, D = x.shape
    L = DEPTH
    chip = 2 * lax.axis_index("x") + lax.axis_index("y")
    dev = 2 * chip + lax.axis_index("c")

    g1 = allgather8(_pack([c] + [W[n] for n in SMALL_SHARDED], 8)).reshape(N_DEV, -1)
    c_all = g1[:, :B * D].reshape(N_DEV * B, D)
    per_chip = g1[0::2, B * D:]
    full = dict(zip(SMALL_SHARDED, [_unshard_last(jnp.stack(p)) for p in zip(*[
        _unpack(per_chip[k], [W[n].shape for n in SMALL_SHARDED]) for k in range(N_CHIPS)])]))
    for n in SMALL_REPLICATED:
        full[n] = W[n]

    n_ada = ada_w.shape[2]
    ada_b_cols = lax.dynamic_slice_in_dim(ada_b, chip * n_ada, n_ada, axis=1)[:, None, :]
    mod_cols = ada_fwd(c_all, ada_w, ada_b_cols)
    g2 = allgather8(mod_cols.reshape(-1, PACK_COLS))[0::2].reshape(N_CHIPS, L, N_DEV * B, n_ada)
    mod_all = jnp.moveaxis(g2, 0, 2).reshape(L, N_DEV * B, N_CHIPS * n_ada)
    mod = lax.dynamic_slice_in_dim(mod_all, dev * B, B, axis=1).reshape(L, B, N_MOD, D)

    FS = ffn_w_in.shape[3]

    def ffn_shards(l, f):
        return [ffn_w_in[l, f].astype(bf16), ffn_w_out[l, f].astype(bf16)]

    def mix_shards(l):
        w_i, w_o = (ev_w_in, ev_w_out) if l % 2 == 0 else (od_w_in, od_w_out)
        return [w_i[l // 2].astype(bf16), w_o[l // 2].astype(bf16)]

    def ffn_weights(g):
        return g[0], g[1].reshape(-1, D)

    def mix_weights(l, g):
        if l % 2 == 0:
            w = _unshard_last(g[0])
            w_i = jnp.concatenate([w[:, :2560], w[:, 2576:], w[:, 2560:2576], jnp.zeros((D, EVEN_IN_PAD - EVEN_IN), bf16)], axis=1)
        else:
            w_i = jnp.moveaxis(g[0], 0, 1).reshape(D, -1)
        return w_i, g[1].reshape(-1, D)

    saved = []
    xs = x
    next_ffn = gather_weights(ffn_shards(0, 0))
    next_mix = None
    for l in range(L):
        lg, lb = full['ln_g'][l], full['ln_b'][l]
        rec = {}
        m1, m2, m3 = mod[l][:, 0:3], mod[l][:, 3:6], mod[l][:, 6:9]
        w_in_a, w_out_a = ffn_weights(next_ffn)
        carry = ffn_shards(l, 1) + (mix_shards(l) if l == 0 else [])
        xn, h, gu, a, y, got = ffn_fwd(xs, m1, w_in_a, w_out_a, lg[0:1], lb[0:1], carry)
        w_in_b, w_out_b = ffn_weights(got[:2])
        w_in_m, w_out_m = mix_weights(l, got[2:] if l == 0 else next_mix)
        rec['ffa'] = (xs, h, gu, a, y, m1, w_in_a, w_out_a, lg[0:1])
        xs = xn
        if l % 2 == 0:
            e = l // 2
            cw = _pad_rows(full['ssd_conv_w'][e], 8)
            cb = full['ssd_conv_b'][e][None]
            dtb, alog, dsk = (_pad_lanes(full[n][e][None]) for n in ('ssd_dt_bias', 'ssd_a_log', 'ssd_d'))
            ng, pw, ps = full['ssd_norm_g'][e][None], full['pool_w'][e], full['pool_scale'][e][None]
            proj, hm = inproj_fwd(xs, m2, w_in_m)
            cx = dwconv_fwd(proj, 2, SSD_CONV_DIM, cw, cb, SSD_CONV)
            ya, hsave = ssd_fwd(proj, cx, dtb, alog, dsk, ng)
            yb = pool_fwd(proj, 5, pw, ps)
            ycat = jnp.concatenate([ya, yb], axis=-1)
            rec['mix'] = (proj, cx, hsave, cw, dtb, alog, dsk, ng, pw, ps)
        else:
            o = l // 2
            dww =_pad_rows(full['conf_dw_w'][o], 32)
            dwb = full['conf_dw_b'][o][None]
            cw = _pad_rows(full['lru_conv_w'][o], 8)
            cb = full['lru_conv_b'][o][None]
            cln = jnp.stack([full['conf_ln_g'][o], full['conf_ln_b'][o]])
            vec = jnp.stack([full['lru_ba'][o], full['lru_bx'][o], full['lru_lambda'][o]])
            wa, wx = full['lru_wa'][o], full['lru_wx'][o]
            proj, hm = inproj_fwd(xs, m2, w_in_m)
            hh = glu_fwd(proj)
            cc = dwconv_fwd(hh, 0, CONF_DIM, dww, dwb, CONF_KERNEL)
            xc = dwconv_fwd(proj, 2, LRU_DIM, cw, cb, LRU_CONV)
            ycat, hst = lru_fwd(cc, xc, proj, cln, wa, wx, vec)
            rec['mix'] = (proj, hh, cc, xc, hst, dww, cw, cln, wa, wx, vec)
        xn, ym = outproj_fwd(ycat, w_out_m, xs, m2, lg[1:2], lb[1:2])
        rec['mixio'] = (xs, hm, ycat, ym, m2, w_in_m, w_out_m, lg[1:2])
        xs = xn
        carry = ffn_shards(l + 1, 0) + mix_shards(l + 1) if l + 1 < L else []
        xn, h, gu, a, y, got = ffn_fwd(xs, m3, w_in_b, w_out_b, lg[2:3], lb[2:3], carry)
        next_ffn, next_mix = got[:2], got[2:]
        rec['ffb'] = (xs, h, gu, a, y, m3, w_in_b, w_out_b, lg[2:3])
        xs = xn
        saved.append(rec)

    sq, dxs = loss_fwd(xs, loss_target)
    loss = lax.psum(sq[0, 0], ("x", "y", "c")) * (0.5 / D)

    gpart = {n: [None] * W[n].shape[0] for n in WEIGHTS}
    gpart['ffn_w_in'] = gpart['ffn_w_out'] = None
    gpart['ln_g'] = [[None] * 3 for _ in range(L)]
    gpart['ln_b'] = [[None] * 3 for _ in range(L)]
    dmod = [None] * L

    def ffn_back(dxn, rec, l, f):
        xin, h, gu, a, y, m3_, w_in_, w_out_, lg_ = rec
        dx, dgu, dy, dm3, dln = ffn_bwd(dxn, xin, y, gu, m3_, w_in_, w_out_, lg_)
        gpart['ffn_w_in'] = wgrad_into(h, dgu, FS, (L, 2), (l, f), gpart['ffn_w_in'])
        gpart['ffn_w_out'] = wgrad_into(a, dy, D, (L, 2), (l, f), gpart['ffn_w_out'])
        gpart['ln_g'][l][2 * f] = dln[0]
        gpart['ln_b'][l][2 * f] = dln[1]
        return dx, dm3

    for l in reversed(range(L)):
        rec = saved[l]
        dxs, dm3 = ffn_back(dxs, rec['ffb'], l, 1)
        xin, hm, ycat, ym, m2, w_in_m, w_out_m, lg_ = rec['mixio']
        dxp, dycat, dy, dg2, dln = outproj_bwd(dxs, xin, ym, m2, w_out_m, lg_)
        gpart['ln_g'][l][1] = dln[0]
        gpart['ln_b'][l][1] = dln[1]
        gw_out = wgrad(ycat, dy, D)[0].reshape(N_CHIPS, -1, D)
        if l % 2 == 0:
            e = l // 2
            proj, cx, hsave, cw, dtb, alog, dsk, ng, pw, ps = rec['mix']
            dz, dcx, ddt, gsm, gng = ssd_bwd(dycat, proj, cx, hsave, dtb, alog, dsk, ng)
            dxbc, dcw, dcb = dwconv_bwd(dcx, proj, 2, SSD_CONV_DIM, cw, SSD_CONV, bf16)
            du, dpw, dps = pool_bwd(dycat, 2, proj, 5, pw, ps)
            dproj = jnp.concatenate([dz, dxbc, du, ddt], axis=-1)
            gwp = wgrad(hm, dproj, EVEN_IN_PAD)[0]
            gw = jnp.concatenate([gwp[:, :2560], gwp[:, 3072:3072 + 16], gwp[:, 2560:3072]], axis=1)
            gpart['ev_w_in'][e] = jnp.moveaxis(gw.reshape(D, N_CHIPS, -1), 1, 0)
            gpart['ev_w_out'][e] = gw_out
            gpart['ssd_conv_w'][e], gpart['ssd_conv_b'][e] = dcw[:SSD_CONV], dcb[0]
            gpart['ssd_dt_bias'][e], gpart['ssd_a_log'][e], gpart['ssd_d'][e] = (gsm[k, :SSD_HEADS] for k in range(3))
            gpart['ssd_norm_g'][e], gpart['pool_w'][e], gpart['pool_scale'][e] = gng[0], dpw, dps[0]
        else:
            o = l // 2
            proj, hh, cc, xc, hst, dww, cw, cln, wa, wx, vec = rec['mix']
            dcc, dxc, dgr, dcln, dwa, dwx, dvec = lru_bwd(dycat, cc, xc, proj, hst, cln, wa, wx, vec)
            dhh, ddw, ddb = dwconv_bwd(dcc, hh, 0, CONF_DIM, dww, CONF_KERNEL, f32)
            dvg = glu_bwd(dhh, proj)
            dxr, dcw, dcb = dwconv_bwd(dxc, proj, 2, LRU_DIM, cw, LRU_CONV, bf16)
            dproj = jnp.concatenate([dvg, dxr, dgr], axis=-1)
            gpart['od_w_in'][o] = wgrad(hm, dproj, dproj.shape[-1] // N_CHIPS)
            gpart['od_w_out'][o] = gw_out
            gpart['conf_dw_w'][o], gpart['conf_dw_b'][o] = ddw[:CONF_KERNEL], ddb[0]
            gpart['conf_ln_g'][o], gpart['conf_ln_b'][o] = dcln[0], dcln[1]
            gpart['lru_conv_w'][o], gpart['lru_conv_b'][o] = dcw[:LRU_CONV], dcb[0]
            gpart['lru_wa'][o], gpart['lru_wx'][o] = dwa, dwx
            gpart['lru_ba'][o], gpart['lru_bx'][o], gpart['lru_lambda'][o] = dvec[0], dvec[1], dvec[2]
        dxs, dm2 = inproj_bwd(dproj, w_in_m, xin, m2, dxp)
        dxs, dm1 = ffn_back(dxs, rec['ffa'], l, 0)
        dmod[l] = jnp.concatenate([dm1, dm2, dg2, dm3], axis=1)
    grad_x = dxs

    def stack(v):
        return jnp.stack([stack(u) if isinstance(u, list) else u for u in v])

    def per_chip(n):
        if W[n].ndim == 4:
            return gpart[n].reshape(W[n].shape[:2] + (N_CHIPS,) + W[n].shape[2:])
        return jnp.stack(gpart[n])

    gfull = [per_chip(n) for n in BIG]
    cidx = lax.axis_index("c").astype(jnp.int32).reshape(1)
    summed = [add_half(g, t, cidx) for g, t in zip(gfull, swap_halves(gfull))]
    reduced = join_halves([sum_chips(r) for r in exchange_blocks(summed)])

    out_g, out_d, out_m, out_v = {}, {}, {}, {}
    for n, g in zip(BIG, reduced):
        shp = W[n].shape
        as2d = lambda a: a.reshape(-1, shp[-1])
        res = adamw(as2d(W[n]), as2d(M[n]), as2d(V[n]), [as2d(g)], [0], BIG_ROW_TILE)
        out_g[n] = g.reshape(shp)
        out_d[n], out_m[n], out_v[n] = (r.reshape(shp) for r in res)

    small = SMALL_SHARDED + SMALL_REPLICATED
    dmod_flat = stack(dmod).reshape(L, B, N_MOD * D)
    g3 = allgather8(_pack([dmod_flat] + [stack(gpart[n]) for n in small], 64))
    n_dmod = L * B * N_MOD * D
    dmod_all = jnp.moveaxis(g3.reshape(N_DEV, -1)[:, :n_dmod].reshape(N_DEV, L, B, N_MOD * D), 0, 1).reshape(L, N_DEV * B, N_MOD * D)
    ssum = sum_leading(g3, 64).reshape(-1)[n_dmod:]
    gsmall = dict(zip(small, _unpack(ssum, [full[n].shape for n in small])))
    for n in SMALL_SHARDED:
        wdt = W[n].shape[-1]
        gsmall[n] = lax.dynamic_slice_in_dim(gsmall[n], chip * wdt, wdt, axis=gsmall[n].ndim - 1)
    dmod_my = lax.dynamic_slice_in_dim(dmod_all, chip * n_ada, n_ada, axis=2)
    g_ada_w, g_ada_b = ada_bwd(c_all, dmod_my, dmod_all)
    gsmall['ada_b'] = g_ada_b[:, 0, :]

    res = adamw(ada_w.reshape(-1, n_ada), M['ada_w'].reshape(-1, n_ada), V['ada_w'].reshape(-1, n_ada),
                [g_ada_w.reshape(-1, n_ada)], [0], BIG_ROW_TILE)
    out_g['ada_w'] = g_ada_w
    out_d['ada_w'], out_m['ada_w'], out_v['ada_w'] = (r.reshape(ada_w.shape) for r in res)

    names = ('ada_b',) + small
    shapes = [W[n].shape for n in names]
    res = adamw(_pack([W[n] for n in names], 64), _pack([M[n] for n in names], 64), _pack([V[n] for n in names], 64),
                [_pack([gsmall[n] for n in names], 64)], [0], 64)
    out_g.update({n: gsmall[n] for n in names})
    for dst, r in zip((out_d, out_m, out_v), res):
        dst.update(zip(names, _unpack(r.reshape(-1), shapes)))

    return (loss, grad_x, *[out_g[n] for n in WEIGHTS], *[out_d[n] for n in WEIGHTS], *[out_m[n] for n in WEIGHTS],
            *[out_v[n] for n in WEIGHTS])
```

```python
import jax
import jax.numpy as jnp
from jax import lax
from jax.experimental import pallas as pl
from jax.experimental.pallas import tpu as pltpu

f32 = jnp.float32
bf16 = jnp.bfloat16

DEPTH = 4
D_MODEL = 1024
N_MOD = 9
DN_ALPHA = (2.0 * DEPTH) ** 0.25
NORM_EPS = 1e-5
SSD_CHUNK = 128
SSD_D_INNER = 1024
SSD_CONV_DIM = 1536
SSD_HEADS = 16
POOL_WINDOWS = (2, 4, 8, 16)
POOL_DIM = 512
EVEN_IN = 3088
EVEN_IN_PAD = 3200
CONF_DIM = 512
CONF_KERNEL = 31
LRU_DIM = 1024
LRU_HEADS = 8
LRU_CONV = 4
SSD_CONV = 4
LRU_C = 8.0
ADAM_LR = 0.001
ADAM_B1 = 0.9
ADAM_B2 = 0.999
ADAM_EPS = 1e-08
ADAM_WD = 0.01
ADAM_STEP = 10

LANES = 128
VMEM_LIMIT_BYTES = 56 * 2 ** 20
COL_TILE = 512
N_CHIPS = 4
N_DEV = 8
MESH = pl.DeviceIdType.MESH


def _cp():
    return pltpu.CompilerParams(vmem_limit_bytes=VMEM_LIMIT_BYTES)


def _dot(a, b):
    return jnp.dot(a, b, preferred_element_type=f32)


def _dot_nt(a, b):
    return lax.dot_general(a, b, (((1,), (1,)), ((), ())), preferred_element_type=f32)


def _dot_tn(a, b):
    return lax.dot_general(a, b, (((0,), (0,)), ((), ())), preferred_element_type=f32)


def _dot_hi(a, b):
    return jnp.dot(a, b, preferred_element_type=f32, precision=lax.Precision.HIGHEST)


def _silu(x):
    return x * jax.nn.sigmoid(x)


def _ln_stats(z):
    mu = jnp.mean(z, axis=-1, keepdims=True)
    zc = z - mu
    var = jnp.mean(zc * zc, axis=-1, keepdims=True)
    rstd = lax.rsqrt(var + NORM_EPS)
    return zc * rstd, rstd


def _ln_bwd(dxn, xhat, rstd, lg):
    dxh = dxn * lg
    return rstd * (dxh - jnp.mean(dxh, axis=-1, keepdims=True) - xhat * jnp.mean(dxh * xhat, axis=-1, keepdims=True))


def _const_spec(shape):
    nd = len(shape)
    return pl.BlockSpec(shape, lambda *_: (0,) * nd, pipeline_mode=pl.Buffered(1))


def _row_tile(t, want=256):
    return min(want, t)


def ffn_fwd(x, mod3, w_in, w_out, lg, lb, carry=()):
    B, T, D = x.shape
    FS = w_in.shape[2]
    tm = _row_tile(T, 512)
    nt = T // tm
    nw = len(carry)

    def body(x_ref, mod_ref, win_ref, wout_ref, lg_ref, lb_ref, *rest):
        xn_ref, h_ref, gu_ref, a_ref, y_ref = rest[nw:nw + 5]
        if nw:
            start, finish = _gather_steps(rest[:nw], rest[nw + 5:2 * nw + 5], *rest[2 * nw + 5:])
            b, i = pl.program_id(0), pl.program_id(1)
            pl.when((b == 0) & (i == 0))(start)
        xv = x_ref[...]
        sh, sc, g = mod_ref[0:1, :], mod_ref[1:2, :], mod_ref[2:3, :]
        h = (xv * (1.0 + sc) + sh).astype(bf16)
        h_ref[...] = h
        acc = jnp.zeros((tm, D), f32)
        for s in range(2):
            gate = _dot(h, win_ref[s])
            up = _dot(h, win_ref[s + 2])
            gu_ref[:, s * FS:(s + 1) * FS] = gate.astype(bf16)
            gu_ref[:, (s + 2) * FS:(s + 3) * FS] = up.astype(bf16)
            a = (_silu(gate) * up).astype(bf16)
            a_ref[:, s * FS:(s + 1) * FS] = a
            acc = acc + _dot(a, wout_ref[s * FS:(s + 1) * FS, :])
        y_ref[...] = acc
        xhat, _ = _ln_stats(DN_ALPHA * xv + 0.5 * (1.0 + g) * acc)
        xn_ref[...] = xhat * lg_ref[...] + lb_ref[...]
        if nw:
            pl.when((b == B - 1) & (i == nt - 1))(finish)

    row = lambda w: pl.BlockSpec((None, tm, w), lambda b, i: (b, i, 0))
    res = pl.pallas_call(
        body, name="ffn_fwd_gather" if nw else "ffn_fwd", grid=(B, nt),
        in_specs=[row(D), pl.BlockSpec((None, 3, D), lambda b, i: (b, 0, 0)), _const_spec(w_in.shape),
                  _const_spec(w_out.shape), _const_spec((1, D)), _const_spec((1, D))] + [ANY] * nw,
        out_specs=[row(D), row(D), row(4 * FS), row(2 * FS), row(D)] + [ANY] * nw,
        out_shape=[jax.ShapeDtypeStruct((B, T, D), f32), jax.ShapeDtypeStruct((B, T, D), bf16),
                   jax.ShapeDtypeStruct((B, T, 4 * FS), bf16), jax.ShapeDtypeStruct((B, T, 2 * FS), bf16),
                   jax.ShapeDtypeStruct((B, T, D), f32)] + _gather_out_shapes(carry),
        scratch_shapes=_gather_sems(nw) if nw else [],
        compiler_params=_cp(),
    )(x, mod3, w_in, w_out, lg, lb, *carry)
    return tuple(res[:5]) + (_fill_own(res[5:], carry),)


def ffn_bwd(dxn, x, y, gu, mod3, w_in, w_out, lg):
    B, T, D = x.shape
    FS = w_in.shape[2]
    tm = _row_tile(T)

    def body(dxn_ref, x_ref, y_ref, gu_ref, mod_ref, win_ref, wout_ref, lg_ref,
             dx_ref, dgu_ref, dy_ref, dmod_ref, dln_ref):
        b, i = pl.program_id(0), pl.program_id(1)

        @pl.when((b == 0) & (i == 0))
        def _():
            dln_ref[...] = jnp.zeros_like(dln_ref)

        @pl.when(i == 0)
        def _():
            dmod_ref[...] = jnp.zeros_like(dmod_ref)

        xv, yv, dxn_v = x_ref[...], y_ref[...], dxn_ref[...]
        sc, g = mod_ref[1:2, :], mod_ref[2:3, :]
        xhat, rstd = _ln_stats(DN_ALPHA * xv + 0.5 * (1.0 + g) * yv)
        dln_ref[0:1, :] += jnp.sum(dxn_v * xhat, axis=0, keepdims=True)
        dln_ref[1:2, :] += jnp.sum(dxn_v, axis=0, keepdims=True)
        dz = _ln_bwd(dxn_v, xhat, rstd, lg_ref[...])
        dmod_ref[2:3, :] += jnp.sum(0.5 * dz * yv, axis=0, keepdims=True)
        dy = (0.5 * (1.0 + g) * dz).astype(bf16)
        dy_ref[...] = dy
        dh = jnp.zeros((tm, D), f32)
        for s in range(2):
            da = _dot_nt(dy, wout_ref[s * FS:(s + 1) * FS, :]).astype(bf16)
            gate = gu_ref[:, s * FS:(s + 1) * FS]
            up = gu_ref[:, (s + 2) * FS:(s + 3) * FS]
            sig = jax.nn.sigmoid(gate)
            dgate = da * up * (sig * (1.0 + gate * (1.0 - sig)))
            dup = da * (gate * sig)
            dgu_ref[:, s * FS:(s + 1) * FS] = dgate
            dgu_ref[:, (s + 2) * FS:(s + 3) * FS] = dup
            dh = dh + _dot_nt(dgate, win_ref[s]) + _dot_nt(dup, win_ref[s + 2])
        dx_ref[...] = DN_ALPHA * dz + dh * (1.0 + sc)
        dmod_ref[0:1, :] += jnp.sum(dh, axis=0, keepdims=True)
        dmod_ref[1:2, :] += jnp.sum(dh * xv, axis=0, keepdims=True)

    row = lambda w: pl.BlockSpec((None, tm, w), lambda b, i: (b, i, 0))
    return pl.pallas_call(
        body, name="ffn_bwd", grid=(B, T // tm),
        in_specs=[row(D), row(D), row(D), row(4 * FS), pl.BlockSpec((None, 3, D), lambda b, i: (b, 0, 0)),
                  _const_spec(w_in.shape), _const_spec(w_out.shape), _const_spec((1, D))],
        out_specs=[row(D), row(4 * FS), row(D), pl.BlockSpec((None, 3, D), lambda b, i: (b, 0, 0)),
                   pl.BlockSpec((2, D), lambda b, i: (0, 0))],
        out_shape=[jax.ShapeDtypeStruct((B, T, D), f32), jax.ShapeDtypeStruct((B, T, 4 * FS), bf16),
                   jax.ShapeDtypeStruct((B, T, D), bf16), jax.ShapeDtypeStruct((B, 3, D), f32),
                   jax.ShapeDtypeStruct((2, D), f32)],
        compiler_params=_cp(),
    )(dxn, x, y, gu, mod3, w_in, w_out, lg)


def wgrad(a, b, tn):
    B, T, K = a.shape
    N = b.shape[2]
    tr = _row_tile(T, 1024 if K * tn <= 1024 * 1536 else 512)

    def body(a_ref, b_ref, o_ref):
        @pl.when((pl.program_id(1) == 0) & (pl.program_id(2) == 0))
        def _():
            o_ref[...] = jnp.zeros_like(o_ref)

        o_ref[...] += _dot_tn(a_ref[...], b_ref[...])

    return pl.pallas_call(
        body, name="wgrad", grid=(N // tn, B, T // tr),
        in_specs=[pl.BlockSpec((None, tr, K), lambda s, b, r: (b, r, 0)),
                  pl.BlockSpec((None, tr, tn), lambda s, b, r: (b, r, s))],
        out_specs=pl.BlockSpec((None, K, tn), lambda s, b, r: (s, 0, 0)),
        out_shape=jax.ShapeDtypeStruct((N // tn, K, tn), f32),
        compiler_params=_cp(),
    )(a, b)


def wgrad_into(a, b, tn, lead, pos, buf=None):
    B, T, K = a.shape
    N = b.shape[2]
    tr = _row_tile(T, 1024 if K * tn <= 1024 * 1536 else 512)
    nl = len(lead)

    def body(pos_ref, a_ref, b_ref, *rest):
        o_ref = rest[-1]

        @pl.when((pl.program_id(1) == 0) & (pl.program_id(2) == 0))
        def _():
            o_ref[...] = jnp.zeros_like(o_ref)

        o_ref[...] += _dot_tn(a_ref[...], b_ref[...])

    return pl.pallas_call(
        body, name="wgrad_into",
        grid_spec=pltpu.PrefetchScalarGridSpec(
            num_scalar_prefetch=1, grid=(N // tn, B, T // tr),
            in_specs=[pl.BlockSpec((None, tr, K), lambda s, b, r, p: (b, r, 0)),
                      pl.BlockSpec((None, tr, tn), lambda s, b, r, p: (b, r, s))] + ([] if buf is None else [ANY]),
            out_specs=pl.BlockSpec((None,) * (nl + 1) + (K, tn),
                                   lambda s, b, r, p: tuple(p[j] for j in range(nl)) + (s, 0, 0))),
        out_shape=jax.ShapeDtypeStruct(tuple(lead) + (N // tn, K, tn), f32),
        input_output_aliases={} if buf is None else {3: 0},
        compiler_params=_cp(),
    )(jnp.asarray(pos, jnp.int32), a, b, *([] if buf is None else [buf]))


def inproj_fwd(x, mod3, w):
    B, T, D = x.shape
    N = w.shape[1]
    tm = _row_tile(T, 512)

    def body(x_ref, mod_ref, w_ref, p_ref, h_ref):
        h = (x_ref[...] * (1.0 + mod_ref[1:2, :]) + mod_ref[0:1, :]).astype(bf16)
        h_ref[...] = h
        p_ref[...] = _dot(h, w_ref[...])

    row = lambda n: pl.BlockSpec((None, tm, n), lambda b, i: (b, i, 0))
    return pl.pallas_call(
        body, name="inproj_fwd", grid=(B, T // tm),
        in_specs=[row(D), pl.BlockSpec((None, 3, D), lambda b, i: (b, 0, 0)), _const_spec(w.shape)],
        out_specs=[row(N), row(D)],
        out_shape=[jax.ShapeDtypeStruct((B, T, N), f32), jax.ShapeDtypeStruct((B, T, D), bf16)],
        compiler_params=_cp(),
    )(x, mod3, w)


def inproj_bwd(dproj, w, x, mod3, dxp):
    B, T, D = x.shape
    N = w.shape[1]
    tm = _row_tile(T, 512)

    def body(dp_ref, w_ref, x_ref, mod_ref, dxp_ref, dx_ref, dmod_ref):
        @pl.when(pl.program_id(1) == 0)
        def _():
            dmod_ref[...] = jnp.zeros_like(dmod_ref)

        dh = _dot_nt(dp_ref[...], w_ref[...])
        dx_ref[...] = dxp_ref[...] + dh * (1.0 + mod_ref[1:2, :])
        dmod_ref[0:1, :] += jnp.sum(dh, axis=0, keepdims=True)
        dmod_ref[1:2, :] += jnp.sum(dh * x_ref[...], axis=0, keepdims=True)

    row = lambda n: pl.BlockSpec((None, tm, n), lambda b, i: (b, i, 0))
    return pl.pallas_call(
        body, name="inproj_bwd", grid=(B, T // tm),
        in_specs=[row(N), _const_spec(w.shape), row(D), pl.BlockSpec((None, 3, D), lambda b, i: (b, 0, 0)), row(D)],
        out_specs=[row(D), pl.BlockSpec((None, 2, D), lambda b, i: (b, 0, 0))],
        out_shape=[jax.ShapeDtypeStruct((B, T, D), f32), jax.ShapeDtypeStruct((B, 2, D), f32)],
        compiler_params=_cp(),
    )(dproj, w, x, mod3, dxp)


def outproj_fwd(ycat, w, x, mod3, lg, lb):
    B, T, D = x.shape
    E = w.shape[0]
    tm = _row_tile(T, 512)

    def body(yc_ref, w_ref, x_ref, mod_ref, lg_ref, lb_ref, xn_ref, y_ref):
        yv = _dot(yc_ref[...], w_ref[...])
        y_ref[...] = yv
        xhat, _ = _ln_stats(DN_ALPHA * x_ref[...] + (1.0 + mod_ref[2:3, :]) * yv)
        xn_ref[...] = xhat * lg_ref[...] + lb_ref[...]

    row = lambda n: pl.BlockSpec((None, tm, n), lambda b, i: (b, i, 0))
    return pl.pallas_call(
        body, name="outproj_fwd", grid=(B, T // tm),
        in_specs=[row(E), _const_spec(w.shape), row(D), pl.BlockSpec((None, 3, D), lambda b, i: (b, 0, 0)),
                  _const_spec((1, D)), _const_spec((1, D))],
        out_specs=[row(D), row(D)],
        out_shape=[jax.ShapeDtypeStruct((B, T, D), f32), jax.ShapeDtypeStruct((B, T, D), f32)],
        compiler_params=_cp(),
    )(ycat, w, x, mod3, lg, lb)


def outproj_bwd(dxn, x, y, mod3, w, lg):
    B, T, D = x.shape
    E = w.shape[0]
    tm = _row_tile(T, 512)

    def body(dxn_ref, x_ref, y_ref, mod_ref, w_ref, lg_ref, dxp_ref, dyc_ref, dy_ref, dg_ref, dln_ref):
        b, i = pl.program_id(0), pl.program_id(1)

        @pl.when((b == 0) & (i == 0))
        def _():
            dln_ref[...] = jnp.zeros_like(dln_ref)

        @pl.when(i == 0)
        def _():
            dg_ref[...] = jnp.zeros_like(dg_ref)

        xv, yv, dxn_v = x_ref[...], y_ref[...], dxn_ref[...]
        g = mod_ref[2:3, :]
        xhat, rstd = _ln_stats(DN_ALPHA * xv + (1.0 + g) * yv)
        dln_ref[0:1, :] += jnp.sum(dxn_v * xhat, axis=0, keepdims=True)
        dln_ref[1:2, :] += jnp.sum(dxn_v, axis=0, keepdims=True)
        dz = _ln_bwd(dxn_v, xhat, rstd, lg_ref[...])
        dg_ref[...] += jnp.sum(dz * yv, axis=0, keepdims=True)
        dy = ((1.0 + g) * dz).astype(bf16)
        dy_ref[...] = dy
        dxp_ref[...] = DN_ALPHA * dz
        dyc_ref[...] = _dot_nt(dy, w_ref[...])

    row = lambda n: pl.BlockSpec((None, tm, n), lambda b, i: (b, i, 0))
    return pl.pallas_call(
        body, name="outproj_bwd", grid=(B, T // tm),
        in_specs=[row(D), row(D), row(D), pl.BlockSpec((None, 3, D), lambda b, i: (b, 0, 0)), _const_spec(w.shape),
                  _const_spec((1, D))],
        out_specs=[row(D), row(E), row(D), pl.BlockSpec((None, 1, D), lambda b, i: (b, 0, 0)),
                   pl.BlockSpec((2, D), lambda b, i: (0, 0))],
        out_shape=[jax.ShapeDtypeStruct((B, T, D), f32), jax.ShapeDtypeStruct((B, T, E), f32),
                   jax.ShapeDtypeStruct((B, T, D), bf16), jax.ShapeDtypeStruct((B, 1, D), f32),
                   jax.ShapeDtypeStruct((2, D), f32)],
        compiler_params=_cp(),
    )(dxn, x, y, mod3, w, lg)


def _halo_rows(K):
    return 8 if K <= 9 else 32


def dwconv_fwd(x, col0, C, w, b, K):
    B, T, _ = x.shape
    tc, hp = COL_TILE, _halo_rows(K)
    tm = _row_tile(T, 1024 if K <= 9 else 512)
    r = tm // hp

    def body(xh_ref, x_ref, w_ref, b_ref, o_ref):
        halo = jnp.where(pl.program_id(2) == 0, 0.0, xh_ref[...])
        xe = jnp.concatenate([halo, x_ref[...]], axis=0)
        acc = jnp.zeros((tm, tc), f32) + b_ref[...]
        for k in range(K):
            sft = K - 1 - k
            xs = xe if sft == 0 else pltpu.roll(xe, sft, 0)
            acc = acc + xs[hp:, :] * w_ref[k:k + 1, :]
        o_ref[...] = acc

    return pl.pallas_call(
        body, name=f"dwconv{K}_fwd", grid=(C // tc, B, T // tm),
        in_specs=[pl.BlockSpec((None, hp, tc), lambda j, b, i: (b, jnp.maximum(i * r - 1, 0), col0 + j)),
                  pl.BlockSpec((None, tm, tc), lambda j, b, i: (b, i, col0 + j)),
                  pl.BlockSpec((w.shape[0], tc), lambda j, b, i: (0, j)),
                  pl.BlockSpec((1, tc), lambda j, b, i: (0, j))],
        out_specs=pl.BlockSpec((None, tm, tc), lambda j, b, i: (b, i, j)),
        out_shape=jax.ShapeDtypeStruct((B, T, C), f32),
        compiler_params=_cp(),
    )(x, x, w, b)


def dwconv_bwd(dc, x, col0, C, w, K, out_dtype):
    B, T, _ = x.shape
    tc, hp = COL_TILE, _halo_rows(K)
    tm = _row_tile(T, 1024 if K <= 9 else 512)
    r = tm // hp
    nt = T // tm
    n = tm + hp
    KP = w.shape[0]

    def body(dcn_ref, dc_ref, x_ref, w_ref, dx_ref, dw_ref, db_ref):
        b, i = pl.program_id(1), pl.program_id(2)

        @pl.when((b == 0) & (i == 0))
        def _():
            dw_ref[...] = jnp.zeros_like(dw_ref)
            db_ref[...] = jnp.zeros_like(db_ref)

        dcv, xv = dc_ref[...], x_ref[...]
        de = jnp.concatenate([dcv, jnp.where(i == nt - 1, 0.0, dcn_ref[...])], axis=0)
        acc = jnp.zeros((tm, tc), f32)
        for k in range(K):
            j = K - 1 - k
            ds = (de if j == 0 else pltpu.roll(de, n - j, 0))[:tm, :]
            acc = acc + ds * w_ref[k:k + 1, :]
            dw_ref[k:k + 1, :] += jnp.sum(ds * xv, axis=0, keepdims=True)
        dx_ref[...] = acc.astype(out_dtype)
        db_ref[...] += jnp.sum(dcv, axis=0, keepdims=True)

    return pl.pallas_call(
        body, name=f"dwconv{K}_bwd", grid=(C // tc, B, nt),
        in_specs=[pl.BlockSpec((None, hp, tc), lambda j, b, i: (b, jnp.minimum((i + 1) * r, T // hp - 1), j)),
                  pl.BlockSpec((None, tm, tc), lambda j, b, i: (b, i, j)),
                  pl.BlockSpec((None, tm, tc), lambda j, b, i: (b, i, col0 + j)),
                  pl.BlockSpec((KP, tc), lambda j, b, i: (0, j))],
        out_specs=[pl.BlockSpec((None, tm, tc), lambda j, b, i: (b, i, j)),
                   pl.BlockSpec((KP, tc), lambda j, b, i: (0, j)),
                   pl.BlockSpec((1, tc), lambda j, b, i: (0, j))],
        out_shape=[jax.ShapeDtypeStruct((B, T, C), out_dtype), jax.ShapeDtypeStruct((KP, C), f32),
                   jax.ShapeDtypeStruct((1, C), f32)],
        compiler_params=_cp(),
    )(dc, dc, x, w)


POOL_HALO = 16


def _pool_windows(ue, pos, hp):
    out = []
    for g, wd in enumerate(POOL_WINDOWS):
        ug = ue[:, g * LANES:(g + 1) * LANES]
        s, span = ug, 1
        while span < wd:
            s = s + pltpu.roll(s, span, 0)
            span *= 2
        cnt = jnp.minimum(pos + 1, wd).astype(f32)
        out.append(s[hp:, :] / cnt - ug[hp:, :])
    return out


def pool_fwd(proj, colb, w, scale):
    B, T, _ = proj.shape
    hp = POOL_HALO
    tm = _row_tile(T, 512)
    r = tm // hp

    def body(uh_ref, u_ref, w_ref, sc_ref, o_ref):
        i = pl.program_id(1)
        ue = jnp.concatenate([jnp.where(i == 0, 0.0, uh_ref[...]), u_ref[...]], axis=0)
        pos = i * tm + lax.broadcasted_iota(jnp.int32, (tm, 1), 0)
        ps = _pool_windows(ue, pos, hp)
        o = jnp.concatenate([_dot(ps[g], w_ref[g]) for g in range(4)], axis=1) * sc_ref[...]
        o_ref[...] = o.astype(bf16)

    return pl.pallas_call(
        body, name="pool_fwd", grid=(B, T // tm),
        in_specs=[pl.BlockSpec((None, hp, POOL_DIM), lambda b, i: (b, jnp.maximum(i * r - 1, 0), colb)),
                  pl.BlockSpec((None, tm, POOL_DIM), lambda b, i: (b, i, colb)),
                  _const_spec(w.shape), _const_spec((1, POOL_DIM))],
        out_specs=pl.BlockSpec((None, tm, POOL_DIM), lambda b, i: (b, i, 0)),
        out_shape=jax.ShapeDtypeStruct((B, T, POOL_DIM), bf16),
        compiler_params=_cp(),
    )(proj, proj, w, scale)


def pool_bwd(dycat, dcolb, proj, colb, w, scale):
    B, T, _ = proj.shape
    hp = POOL_HALO
    tm = _row_tile(T, 512)
    r = tm // hp
    nt = T // tm
    n = tm + hp

    def body(dyn_ref, dy_ref, uh_ref, u_ref, w_ref, sc_ref, du_ref, dw_ref, dsc_ref):
        b, i = pl.program_id(0), pl.program_id(1)

        @pl.when((b == 0) & (i == 0))
        def _():
            dw_ref[...] = jnp.zeros_like(dw_ref)
            dsc_ref[...] = jnp.zeros_like(dsc_ref)

        dyv = dy_ref[...]
        dye = jnp.concatenate([dyv, jnp.where(i == nt - 1, 0.0, dyn_ref[...])], axis=0)
        ue = jnp.concatenate([jnp.where(i == 0, 0.0, uh_ref[...]), u_ref[...]], axis=0)
        pos = i * tm + lax.broadcasted_iota(jnp.int32, (tm, 1), 0)
        pos_e = i * tm + lax.broadcasted_iota(jnp.int32, (n, 1), 0)
        ps = _pool_windows(ue, pos, hp)
        dme = dye * sc_ref[...]
        dus, dscs = [], []
        for g, wd in enumerate(POOL_WINDOWS):
            sl = slice(g * LANES, (g + 1) * LANES)
            dscs.append(jnp.sum(dyv[:, sl] * _dot(ps[g], w_ref[g]), axis=0, keepdims=True))
            dw_ref[g] += _dot_tn(ps[g], dme[:tm, sl])
            dpe = _dot_nt(dme[:, sl], w_ref[g])
            s, span = dpe / jnp.minimum(pos_e + 1, wd).astype(f32), 1
            while span < wd:
                s = s + pltpu.roll(s, n - span, 0)
                span *= 2
            dus.append(s[:tm, :] - dpe[:tm, :])
        du_ref[...] = jnp.concatenate(dus, axis=1).astype(bf16)
        dsc_ref[...] += jnp.concatenate(dscs, axis=1)

    return pl.pallas_call(
        body, name="pool_bwd", grid=(B, nt),
        in_specs=[pl.BlockSpec((None, hp, POOL_DIM), lambda b, i: (b, jnp.minimum((i + 1) * r, T // hp - 1), dcolb)),
                  pl.BlockSpec((None, tm, POOL_DIM), lambda b, i: (b, i, dcolb)),
                  pl.BlockSpec((None, hp, POOL_DIM), lambda b, i: (b, jnp.maximum(i * r - 1, 0), colb)),
                  pl.BlockSpec((None, tm, POOL_DIM), lambda b, i: (b, i, colb)),
                  _const_spec(w.shape), _const_spec((1, POOL_DIM))],
        out_specs=[pl.BlockSpec((None, tm, POOL_DIM), lambda b, i: (b, i, 0)),
                   pl.BlockSpec(w.shape, lambda b, i: (0, 0, 0)),
                   pl.BlockSpec((1, POOL_DIM), lambda b, i: (0, 0))],
        out_shape=[jax.ShapeDtypeStruct((B, T, POOL_DIM), bf16), jax.ShapeDtypeStruct(w.shape, f32),
                   jax.ShapeDtypeStruct((1, POOL_DIM), f32)],
        compiler_params=_cp(),
    )(dycat, dycat, proj, proj, w, scale)


N_PAIRS = SSD_HEADS // 2


def _ssd_chunk(xs, bs, cs, dtp, zs, hs, dtb, alog, dsk, ngs):
    Q = SSD_CHUNK
    lane = lax.broadcasted_iota(jnp.int32, (1, LANES), 1)
    sub = lax.broadcasted_iota(jnp.int32, (LANES, 1), 0)
    causal = lax.broadcasted_iota(jnp.int32, (Q, Q), 0) >= lax.broadcasted_iota(jnp.int32, (Q, Q), 1)
    lane_lo, sub_lo = lane < 64, sub < 64

    def col(v, h):
        return jnp.sum(v * (lane == h).astype(f32), axis=1, keepdims=True)

    def row(vt, h):
        return jnp.sum(vt * (sub == h).astype(f32), axis=0, keepdims=True)

    dt = jax.nn.softplus(dtp + dtb)
    acum = _dot_hi(causal.astype(f32), dt * (-jnp.exp(alog)))
    acum_t = acum.T
    aend = jnp.sum(acum * (sub == Q - 1).astype(f32), axis=0, keepdims=True)
    outs, hn = [], []
    for grp in range(2):
        bv, cv = _silu(bs[grp]), _silu(cs[grp])
        gmat = _dot_nt(cv, bv)
        for j in range(4):
            p = grp * 4 + j
            h0, h1 = 2 * p, 2 * p + 1
            x2 = _silu(xs[p])
            c0, c1 = col(acum, h0), col(acum, h1)
            s2 = jnp.where(lane_lo, c0, c1)
            xdt = x2 * jnp.where(lane_lo, col(dt, h0), col(dt, h1))
            l0 = jnp.where(causal, jnp.exp(jnp.minimum(c0 - row(acum_t, h0), 0.0)), 0.0)
            l1 = jnp.where(causal, jnp.exp(jnp.minimum(c1 - row(acum_t, h1), 0.0)), 0.0)
            yd = _dot(gmat * l0, jnp.where(lane_lo, xdt, 0.0)) + _dot(gmat * l1, jnp.where(lane_lo, 0.0, xdt))
            e0, e1 = col(aend, h0), col(aend, h1)
            st = _dot_tn(xdt * jnp.exp(jnp.where(lane_lo, e0, e1) - s2), bv)
            yo = jnp.exp(s2) * _dot_nt(cv, hs[p])
            hn.append(jnp.exp(jnp.where(sub_lo, e0, e1)) * hs[p] + st)
            yv = yd + yo + x2 * jnp.where(lane_lo, col(dsk, h0), col(dsk, h1))
            outs.append(yv * _silu(zs[p]))
    ms = sum(jnp.sum(o * o, axis=1, keepdims=True) for o in outs) / SSD_D_INNER
    rs = lax.rsqrt(ms + NORM_EPS)
    return [outs[p] * rs * ngs[p] for p in range(N_PAIRS)], hn


def _lane_blocks(ref, n, start=0):
    return [ref[:, (start + k) * LANES:(start + k + 1) * LANES] for k in range(n)]


def _ssd_args(z_ref, cx_ref, dt_ref, dtb_ref, alog_ref, dsk_ref, ng_ref):
    xs = _lane_blocks(cx_ref, 8)
    bs = _lane_blocks(cx_ref, 2, 8)
    cs = _lane_blocks(cx_ref, 2, 10)
    zs = _lane_blocks(z_ref, 8)
    ngs = _lane_blocks(ng_ref, 8)
    return xs, bs, cs, dt_ref[...], zs, dtb_ref[...], alog_ref[...], dsk_ref[...], ngs


DT_COLB = (EVEN_IN_PAD - LANES) // LANES


def ssd_fwd(proj, cx, dtb, alog, dsk, ng):
    B, T, _ = proj.shape
    Q = SSD_CHUNK
    nc = T // Q

    def body(z_ref, cx_ref, dt_ref, dtb_ref, alog_ref, dsk_ref, ng_ref, ya_ref, hsave_ref, h_scr):
        @pl.when(pl.program_id(1) == 0)
        def _():
            h_scr[...] = jnp.zeros_like(h_scr)

        xs, bs, cs, dtp, zs, dtb_v, alog_v, dsk_v, ngs = _ssd_args(z_ref, cx_ref, dt_ref, dtb_ref, alog_ref, dsk_ref, ng_ref)
        hs = [h_scr[p] for p in range(N_PAIRS)]
        for p in range(N_PAIRS):
            hsave_ref[p] = hs[p]
        outs, hn = _ssd_chunk(xs, bs, cs, dtp, zs, hs, dtb_v, alog_v, dsk_v, ngs)
        for p in range(N_PAIRS):
            ya_ref[:, p * LANES:(p + 1) * LANES] = outs[p].astype(bf16)
            h_scr[p] = hn[p]

    return pl.pallas_call(
        body, name="ssd_fwd", grid=(B, nc),
        in_specs=[pl.BlockSpec((None, Q, SSD_D_INNER), lambda b, i: (b, i, 0)),
                  pl.BlockSpec((None, Q, SSD_CONV_DIM), lambda b, i: (b, i, 0)),
                  pl.BlockSpec((None, Q, LANES), lambda b, i: (b, i, DT_COLB)),
                  _const_spec((1, LANES)), _const_spec((1, LANES)), _const_spec((1, LANES)),
                  _const_spec((1, SSD_D_INNER))],
        out_specs=[pl.BlockSpec((None, Q, SSD_D_INNER), lambda b, i: (b, i, 0)),
                   pl.BlockSpec((None, None, N_PAIRS, LANES, LANES), lambda b, i: (b, i, 0, 0, 0))],
        out_shape=[jax.ShapeDtypeStruct((B, T, SSD_D_INNER), bf16),
                   jax.ShapeDtypeStruct((B, nc, N_PAIRS, LANES, LANES), f32)],
        scratch_shapes=[pltpu.VMEM((N_PAIRS, LANES, LANES), f32)],
        compiler_params=_cp(),
    )(proj, cx, proj, dtb, alog, dsk, ng)


def ssd_bwd(dycat, proj, cx, hsave, dtb, alog, dsk, ng):
    B, T, _ = proj.shape
    Q = SSD_CHUNK
    nc = T // Q

    def body(dya_ref, z_ref, cx_ref, dt_ref, hsave_ref, dtb_ref, alog_ref, dsk_ref, ng_ref,
             dz_ref, dcx_ref, ddt_ref, gsm_ref, gng_ref, dh_scr):
        b, i = pl.program_id(0), pl.program_id(1)

        @pl.when((b == 0) & (i == 0))
        def _():
            gsm_ref[...] = jnp.zeros_like(gsm_ref)
            gng_ref[...] = jnp.zeros_like(gng_ref)

        @pl.when(i == 0)
        def _():
            dh_scr[...] = jnp.zeros_like(dh_scr)

        xs, bs, cs, dtp, zs, dtb_v, alog_v, dsk_v, ngs = _ssd_args(z_ref, cx_ref, dt_ref, dtb_ref, alog_ref, dsk_ref, ng_ref)
        hs = [hsave_ref[p] for p in range(N_PAIRS)]
        _, vjp = jax.vjp(_ssd_chunk, xs, bs, cs, dtp, zs, hs, dtb_v, alog_v, dsk_v, ngs)
        douts = _lane_blocks(dya_ref, 8)
        dhn = [dh_scr[p] for p in range(N_PAIRS)]
        dxs, dbs, dcs, ddtp, dzs, dhs, ddtb, dalog, ddsk, dngs = vjp((douts, dhn))
        for p in range(N_PAIRS):
            dcx_ref[:, p * LANES:(p + 1) * LANES] = dxs[p]
            dz_ref[:, p * LANES:(p + 1) * LANES] = dzs[p].astype(bf16)
            dh_scr[p] = dhs[p]
            gng_ref[:, p * LANES:(p + 1) * LANES] += dngs[p]
        for k in range(2):
            dcx_ref[:, (8 + k) * LANES:(9 + k) * LANES] = dbs[k]
            dcx_ref[:, (10 + k) * LANES:(11 + k) * LANES] = dcs[k]
        ddt_ref[...] = ddtp.astype(bf16)
        gsm_ref[0:1, :] += ddtb
        gsm_ref[1:2, :] += dalog
        gsm_ref[2:3, :] += ddsk

    rev = lambda w, cb=0: pl.BlockSpec((None, Q, w), lambda b, i: (b, nc - 1 - i, cb))
    return pl.pallas_call(
        body, name="ssd_bwd", grid=(B, nc),
        in_specs=[rev(SSD_D_INNER), rev(SSD_D_INNER), rev(SSD_CONV_DIM), rev(LANES, DT_COLB),
                  pl.BlockSpec((None, None, N_PAIRS, LANES, LANES), lambda b, i: (b, nc - 1 - i, 0, 0, 0)),
                  _const_spec((1, LANES)), _const_spec((1, LANES)), _const_spec((1, LANES)),
                  _const_spec((1, SSD_D_INNER))],
        out_specs=[rev(SSD_D_INNER), rev(SSD_CONV_DIM), rev(LANES),
                   pl.BlockSpec((3, LANES), lambda b, i: (0, 0)),
                   pl.BlockSpec((1, SSD_D_INNER), lambda b, i: (0, 0))],
        out_shape=[jax.ShapeDtypeStruct((B, T, SSD_D_INNER), bf16), jax.ShapeDtypeStruct((B, T, SSD_CONV_DIM), f32),
                   jax.ShapeDtypeStruct((B, T, LANES), bf16), jax.ShapeDtypeStruct((3, LANES), f32),
                   jax.ShapeDtypeStruct((1, SSD_D_INNER), f32)],
        scratch_shapes=[pltpu.VMEM((N_PAIRS, LANES, LANES), f32)],
        compiler_params=_cp(),
    )(dycat, proj, cx, proj, hsave, dtb, alog, dsk, ng)


def glu_fwd(proj):
    B, T, _ = proj.shape
    tm = _row_tile(T, 1024)

    def body(v_ref, g_ref, o_ref):
        o_ref[...] = v_ref[...] * jax.nn.sigmoid(g_ref[...])

    blk = lambda cb: pl.BlockSpec((None, tm, CONF_DIM), lambda b, i: (b, i, cb))
    return pl.pallas_call(body, name="glu_fwd", grid=(B, T // tm), in_specs=[blk(0), blk(1)], out_specs=blk(0),
                          out_shape=jax.ShapeDtypeStruct((B, T, CONF_DIM), f32), compiler_params=_cp())(proj, proj)


def glu_bwd(dhh, proj):
    B, T, _ = proj.shape
    tm = _row_tile(T, 1024)

    def body(d_ref, v_ref, g_ref, o_ref):
        sig = jax.nn.sigmoid(g_ref[...])
        dv = d_ref[...]
        o_ref[:, :CONF_DIM] = (dv * sig).astype(bf16)
        o_ref[:, CONF_DIM:] = (dv * v_ref[...] * sig * (1.0 - sig)).astype(bf16)

    blk = lambda cb: pl.BlockSpec((None, tm, CONF_DIM), lambda b, i: (b, i, cb))
    return pl.pallas_call(body, name="glu_bwd", grid=(B, T // tm), in_specs=[blk(0), blk(0), blk(1)],
                          out_specs=pl.BlockSpec((None, tm, 2 * CONF_DIM), lambda b, i: (b, i, 0)),
                          out_shape=jax.ShapeDtypeStruct((B, T, 2 * CONF_DIM), bf16), compiler_params=_cp())(dhh, proj, proj)


def _neg_expm1(x):
    series = x * (1.0 + x * (1.0 / 2.0) * (1.0 + x * (1.0 / 3.0) * (1.0 + x * (1.0 / 4.0) * (1.0 + x * (1.0 / 5.0)))))
    return -jnp.where(x > -0.1, series, jnp.exp(x) - 1.0)


def _lru_gates(cc, xc8, gr8, clg, clb, wa8, wx8, ba8, bx8, lam8):
    xhat, _ = _ln_stats(cc)
    yc = _silu(xhat * clg + clb)
    a8, b8, ge8 = [], [], []
    for hb in range(LRU_HEADS):
        xh = xc8[hb]
        rg = jax.nn.sigmoid(_dot(xh, wa8[hb]) + ba8[hb])
        ig = jax.nn.sigmoid(_dot(xh, wx8[hb]) + bx8[hb])
        log_a = -LRU_C * rg * jax.nn.softplus(-lam8[hb])
        a8.append(jnp.exp(log_a))
        b8.append(jnp.sqrt(_neg_expm1(2.0 * log_a)) * (ig * xh))
        ge8.append(jax.nn.gelu(gr8[hb]))
    return yc, a8, b8, ge8


def _scan_fwd(a, b, h_in):
    tm = a.shape[0]
    rows = lax.broadcasted_iota(jnp.int32, (tm, 1), 0)
    s = 1
    while s < tm:
        keep = rows >= s
        b = a * jnp.where(keep, pltpu.roll(b, s, 0), 0.0) + b
        a = a * jnp.where(keep, pltpu.roll(a, s, 0), 1.0)
        s *= 2
    return a * h_in + b


def _scan_bwd(e, d, g_in):
    tm = e.shape[0]
    rows = lax.broadcasted_iota(jnp.int32, (tm, 1), 0)
    s = 1
    while s < tm:
        keep = rows < tm - s
        d = e * jnp.where(keep, pltpu.roll(d, tm - s, 0), 0.0) + d
        e = e * jnp.where(keep, pltpu.roll(e, tm - s, 0), 1.0)
        s *= 2
    return e * g_in + d


def _lru_params(wa_ref, wx_ref, vec_ref):
    wa8 = [wa_ref[h] for h in range(LRU_HEADS)]
    wx8 = [wx_ref[h] for h in range(LRU_HEADS)]
    ba8 = [vec_ref[0:1, h * LANES:(h + 1) * LANES] for h in range(LRU_HEADS)]
    bx8 = [vec_ref[1:2, h * LANES:(h + 1) * LANES] for h in range(LRU_HEADS)]
    lam8 = [vec_ref[2:3, h * LANES:(h + 1) * LANES] for h in range(LRU_HEADS)]
    return wa8, wx8, ba8, bx8, lam8


GR_COLB = 2


def lru_fwd(cc, xc, proj, cln, wa, wx, vec):
    B, T, _ = xc.shape
    tm = _row_tile(T)

    def body(cc_ref, xc_ref, gr_ref, cln_ref, wa_ref, wx_ref, vec_ref, y_ref, hs_ref, h_scr):
        @pl.when(pl.program_id(1) == 0)
        def _():
            h_scr[...] = jnp.zeros_like(h_scr)

        yc, a8, b8, ge8 = _lru_gates(cc_ref[...], _lane_blocks(xc_ref, 8), _lane_blocks(gr_ref, 8), cln_ref[0:1, :],
                                     cln_ref[1:2, :], *_lru_params(wa_ref, wx_ref, vec_ref))
        h = _scan_fwd(jnp.concatenate(a8, axis=1), jnp.concatenate(b8, axis=1), h_scr[...])
        hs_ref[...] = h
        h_scr[...] = h[tm - 1:tm, :]
        y_ref[:, :CONF_DIM] = yc.astype(bf16)
        y_ref[:, CONF_DIM:] = (h * jnp.concatenate(ge8, axis=1)).astype(bf16)

    row = lambda w, cb=0: pl.BlockSpec((None, tm, w), lambda b, i: (b, i, cb))
    return pl.pallas_call(
        body, name="lru_fwd", grid=(B, T // tm),
        in_specs=[row(CONF_DIM), row(LRU_DIM), row(LRU_DIM, GR_COLB), _const_spec((2, CONF_DIM)),
                  _const_spec(wa.shape), _const_spec(wx.shape), _const_spec((3, LRU_DIM))],
        out_specs=[row(CONF_DIM + LRU_DIM), row(LRU_DIM)],
        out_shape=[jax.ShapeDtypeStruct((B, T, CONF_DIM + LRU_DIM), bf16), jax.ShapeDtypeStruct((B, T, LRU_DIM), f32)],
        scratch_shapes=[pltpu.VMEM((1, LRU_DIM), f32)],
        compiler_params=_cp(),
    )(cc, xc, proj, cln, wa, wx, vec)


def lru_bwd(dycat, cc, xc, proj, hs, cln, wa, wx, vec):
    B, T, _ = xc.shape
    tm = _row_tile(T)
    nt = T // tm
    r = tm // 8

    def body(dy_ref, cc_ref, xc_ref, gr_ref, hs_ref, hsh_ref, cln_ref, wa_ref, wx_ref, vec_ref,
             dcc_ref, dxc_ref, dgr_ref, dcln_ref, dwa_ref, dwx_ref, dvec_ref, g_scr, a_scr):
        b, i = pl.program_id(0), pl.program_id(1)
        it = nt - 1 - i

        @pl.when((b == 0) & (i == 0))
        def _():
            dcln_ref[...] = jnp.zeros_like(dcln_ref)
            dwa_ref[...] = jnp.zeros_like(dwa_ref)
            dwx_ref[...] = jnp.zeros_like(dwx_ref)
            dvec_ref[...] = jnp.zeros_like(dvec_ref)

        @pl.when(i == 0)
        def _():
            g_scr[...] = jnp.zeros_like(g_scr)
            a_scr[...] = jnp.zeros_like(a_scr)

        (yc, a8, b8, ge8), vjp = jax.vjp(_lru_gates, cc_ref[...], _lane_blocks(xc_ref, 8), _lane_blocks(gr_ref, 8),
                                         cln_ref[0:1, :], cln_ref[1:2, :], *_lru_params(wa_ref, wx_ref, vec_ref))
        a = jnp.concatenate(a8, axis=1)
        ge = jnp.concatenate(ge8, axis=1)
        h = hs_ref[...]
        dyd = dy_ref[:, CONF_DIM:]
        rows = lax.broadcasted_iota(jnp.int32, (tm, 1), 0)
        e = jnp.where(rows < tm - 1, pltpu.roll(a, tm - 1, 0), a_scr[...])
        g = _scan_bwd(e, dyd * ge, g_scr[...])
        h_first = jnp.where(it == 0, 0.0, hsh_ref[7:8, :])
        h_prev = jnp.where(rows >= 1, pltpu.roll(h, 1, 0), h_first)
        da = g * h_prev
        g_scr[...] = g[0:1, :]
        a_scr[...] = a[0:1, :]
        split = lambda v: [v[:, k * LANES:(k + 1) * LANES] for k in range(LRU_HEADS)]
        dcc, dxc8, dgr8, dclg, dclb, dwa8, dwx8, dba8, dbx8, dlam8 = vjp((dy_ref[:, :CONF_DIM], split(da), split(g), split(dyd * h)))
        dcc_ref[...] = dcc
        dcln_ref[0:1, :] += dclg
        dcln_ref[1:2, :] += dclb
        for k in range(LRU_HEADS):
            sl = slice(k * LANES, (k + 1) * LANES)
            dxc_ref[:, sl] = dxc8[k]
            dgr_ref[:, sl] = dgr8[k].astype(bf16)
            dwa_ref[k] += dwa8[k]
            dwx_ref[k] += dwx8[k]
            dvec_ref[0:1, sl] += dba8[k]
            dvec_ref[1:2, sl] += dbx8[k]
            dvec_ref[2:3, sl] += dlam8[k]

    rev = lambda w, cb=0: pl.BlockSpec((None, tm, w), lambda b, i: (b, nt - 1 - i, cb))
    acc = lambda shape: pl.BlockSpec(shape, lambda b, i: (0,) * len(shape))
    return pl.pallas_call(
        body, name="lru_bwd", grid=(B, nt),
        in_specs=[rev(CONF_DIM + LRU_DIM), rev(CONF_DIM), rev(LRU_DIM), rev(LRU_DIM, GR_COLB), rev(LRU_DIM),
                  pl.BlockSpec((None, 8, LRU_DIM), lambda b, i: (b, jnp.maximum((nt - 1 - i) * r - 1, 0), 0)),
                  _const_spec((2, CONF_DIM)), _const_spec(wa.shape), _const_spec(wx.shape), _const_spec((3, LRU_DIM))],
        out_specs=[rev(CONF_DIM), rev(LRU_DIM), rev(LRU_DIM), acc((2, CONF_DIM)), acc(wa.shape), acc(wx.shape),
                   acc((3, LRU_DIM))],
        out_shape=[jax.ShapeDtypeStruct((B, T, CONF_DIM), f32), jax.ShapeDtypeStruct((B, T, LRU_DIM), f32),
                   jax.ShapeDtypeStruct((B, T, LRU_DIM), bf16), jax.ShapeDtypeStruct((2, CONF_DIM), f32),
                   jax.ShapeDtypeStruct(wa.shape, f32), jax.ShapeDtypeStruct(wx.shape, f32),
                   jax.ShapeDtypeStruct((3, LRU_DIM), f32)],
        scratch_shapes=[pltpu.VMEM((1, LRU_DIM), f32), pltpu.VMEM((1, LRU_DIM), f32)],
        compiler_params=_cp(),
    )(dycat, cc, xc, proj, hs, hs, cln, wa, wx, vec)


def loss_fwd(y, target):
    B, T, D = y.shape
    tm = _row_tile(T)

    def body(y_ref, t_ref, l_ref, dy_ref):
        @pl.when((pl.program_id(0) == 0) & (pl.program_id(1) == 0))
        def _():
            l_ref[...] = jnp.zeros_like(l_ref)

        d = y_ref[...] - t_ref[...]
        dy_ref[...] = d * (1.0 / D)
        l_ref[...] += jnp.sum(jnp.sum(d * d, axis=1, keepdims=True), axis=0, keepdims=True)

    row = pl.BlockSpec((None, tm, D), lambda b, i: (b, i, 0))
    return pl.pallas_call(
        body, name="loss_fwd", grid=(B, T // tm), in_specs=[row, row],
        out_specs=[pl.BlockSpec((1, 1), lambda b, i: (0, 0)), row],
        out_shape=[jax.ShapeDtypeStruct((1, 1), f32), jax.ShapeDtypeStruct((B, T, D), f32)],
        compiler_params=_cp(),
    )(y, target)


ADA_COL_TILE = 768


def ada_fwd(c_all, w, b):
    L, D, N = w.shape
    nb = c_all.shape[0]
    tn = ADA_COL_TILE

    def body(c_ref, w_ref, b_ref, o_ref):
        o_ref[...] = _dot_hi(_silu(c_ref[...]), w_ref[...]) + b_ref[...]

    return pl.pallas_call(
        body, name="ada_fwd", grid=(L, N // tn),
        in_specs=[pl.BlockSpec((nb, D), lambda l, j: (0, 0)), pl.BlockSpec((None, D, tn), lambda l, j: (l, 0, j)),
                  pl.BlockSpec((None, 1, tn), lambda l, j: (l, 0, j))],
        out_specs=pl.BlockSpec((None, nb, tn), lambda l, j: (l, 0, j)),
        out_shape=jax.ShapeDtypeStruct((L, nb, N), f32),
        compiler_params=_cp(),
    )(c_all, w, b)


def ada_bwd(c_all, dmod_my, dmod_all):
    L, nb, N = dmod_my.shape
    D = c_all.shape[1]
    NA = dmod_all.shape[2]
    tn = ADA_COL_TILE
    nj = N // tn
    ta = NA // nj

    def body(c_ref, dm_ref, da_ref, gw_ref, gb_ref):
        gw_ref[...] = lax.dot_general(_silu(c_ref[...]), dm_ref[...], (((0,), (0,)), ((), ())),
                                      preferred_element_type=f32, precision=lax.Precision.HIGHEST)
        gb_ref[...] = jnp.sum(da_ref[...], axis=0, keepdims=True)

    return pl.pallas_call(
        body, name="ada_bwd", grid=(L, nj),
        in_specs=[pl.BlockSpec((nb, D), lambda l, j: (0, 0)), pl.BlockSpec((None, nb, tn), lambda l, j: (l, 0, j)),
                  pl.BlockSpec((None, nb, ta), lambda l, j: (l, 0, j))],
        out_specs=[pl.BlockSpec((None, D, tn), lambda l, j: (l, 0, j)), pl.BlockSpec((None, 1, ta), lambda l, j: (l, 0, j))],
        out_shape=[jax.ShapeDtypeStruct((L, D, N), f32), jax.ShapeDtypeStruct((L, 1, NA), f32)],
        compiler_params=_cp(),
    )(c_all, dmod_my, dmod_all)


def adamw(w, m, v, gs, offs, tr):
    R, C = w.shape
    ng = len(gs)
    c1 = 1.0 - ADAM_B1 ** ADAM_STEP
    c2 = 1.0 - ADAM_B2 ** ADAM_STEP

    def body(*refs):
        w_ref, m_ref, v_ref = refs[:3]
        g_refs = refs[3:3 + ng]
        d_out, m_out, v_out = refs[3 + ng:]
        g = g_refs[0][...]
        for r in g_refs[1:]:
            g = g + r[...]
        mn = ADAM_B1 * m_ref[...] + (1.0 - ADAM_B1) * g
        vn = ADAM_B2 * v_ref[...] + (1.0 - ADAM_B2) * (g * g)
        m_out[...] = mn
        v_out[...] = vn
        d_out[...] = -ADAM_LR * ((mn / c1) / (jnp.sqrt(vn / c2) + ADAM_EPS) + ADAM_WD * w_ref[...])

    blk = pl.BlockSpec((tr, C), lambda i: (i, 0))
    gspec = lambda off: pl.BlockSpec((tr, C), lambda i: (i + off // tr, 0))
    sds = jax.ShapeDtypeStruct((R, C), f32)
    return pl.pallas_call(
        body, name="adamw", grid=(pl.cdiv(R, tr),), in_specs=[blk, blk, blk] + [gspec(o) for o in offs],
        out_specs=[blk] * 3, out_shape=[sds] * 3, compiler_params=_cp(),
    )(w, m, v, *gs)


def sum_leading(a, tr):
    k, R, C = a.shape

    def body(a_ref, o_ref):
        s = a_ref[0]
        for j in range(1, k):
            s = s + a_ref[j]
        o_ref[...] = s

    return pl.pallas_call(
        body, name="sum_leading", grid=(R // tr,), in_specs=[pl.BlockSpec((k, tr, C), lambda i: (0, i, 0))],
        out_specs=pl.BlockSpec((tr, C), lambda i: (i, 0)), out_shape=jax.ShapeDtypeStruct((R, C), a.dtype),
        compiler_params=_cp(),
    )(a)


ANY = pl.BlockSpec(memory_space=pl.ANY)
CHIP_FLIPS = ((1, 0), (0, 1), (1, 1))
DEV_FLIPS = tuple((fx, fy, fc) for fx in (0, 1) for fy in (0, 1) for fc in (0, 1))[1:]


def _flip(v, f):
    return 1 - v if f else v


def _put(out, v, idx, axis=0):
    return lax.dynamic_update_slice_in_dim(out, jnp.expand_dims(v, axis) if v.ndim < out.ndim else v, idx, axis)


def allgather8(v):
    R, C = v.shape
    n = len(DEV_FLIPS)

    def body(v_ref, o_ref, send_sems, recv_sems):
        x, y, c = lax.axis_index("x"), lax.axis_index("y"), lax.axis_index("c")
        me, sibling = (x, y, c), (x, y, 1 - c)
        chips = [(_flip(x, fx), _flip(y, fy)) for fx, fy in CHIP_FLIPS]

        def copy(k, block, to, own=False):
            px, py, pc = block
            dst = o_ref.at[4 * px + 2 * py + pc]
            return pltpu.make_async_remote_copy(src_ref=v_ref if own else dst, dst_ref=dst, send_sem=send_sems.at[k],
                                                recv_sem=recv_sems.at[k], device_id=to, device_id_type=MESH)

        first = [copy(0, me, sibling, own=True)] + [copy(1 + j, me, (*chip, c), own=True) for j, chip in enumerate(chips)]
        for cp in first:
            cp.start()
        passed = [copy(4 + j, (*chip, c), sibling) for j, chip in enumerate(chips)]
        for j, chip in enumerate(chips):
            copy(1 + j, (*chip, c), me).wait_recv()
            passed[j].start()
        copy(0, sibling, me).wait_recv()
        for j, chip in enumerate(chips):
            copy(4 + j, (*chip, 1 - c), me).wait_recv()
        for cp in first + passed:
            cp.wait_send()

    out = pl.pallas_call(
        body, name="allgather8", in_specs=[ANY], out_specs=ANY, out_shape=jax.ShapeDtypeStruct((N_DEV, R, C), v.dtype),
        scratch_shapes=[pltpu.SemaphoreType.DMA((n,)), pltpu.SemaphoreType.DMA((n,))],
    )(v)
    return _put(out, v, 4 * lax.axis_index("x") + 2 * lax.axis_index("y") + lax.axis_index("c"))


def _half(ref_or_shape0, c):
    hsz = ref_or_shape0 // 2
    return pl.ds(c * hsz, hsz)


def _gather_steps(w_refs, o_refs, send_sems, recv_sems):
    nw, nc = len(w_refs), len(CHIP_FLIPS)
    x, y, c = lax.axis_index("x"), lax.axis_index("y"), lax.axis_index("c")
    me, sibling = 2 * x + y, (x, y, 1 - c)
    peers = [(_flip(x, fx), _flip(y, fy), c) for fx, fy in CHIP_FLIPS]
    slots = [2 * px + py for px, py, _ in peers]

    def copy(a, j, slot, half, to, own=False):
        hs = _half(w_refs[a].shape[0], half)
        return pltpu.make_async_remote_copy(src_ref=w_refs[a].at[hs] if own else o_refs[a].at[slot, hs],
                                            dst_ref=o_refs[a].at[slot, hs], send_sem=send_sems.at[j],
                                            recv_sem=recv_sems.at[j], device_id=to, device_id_type=MESH)

    first = [copy(a, a * nc + k, me, c, peers[k], own=True) for a in range(nw) for k in range(nc)]

    def start():
        for cp in first:
            cp.start()

    def finish():
        passed = []
        for a in range(nw):
            for k in range(nc):
                copy(a, a * nc + k, slots[k], c, peers[k]).wait_recv()
                passed.append(copy(a, nw * nc + a * nc + k, slots[k], c, sibling))
                passed[-1].start()
        for a in range(nw):
            for k in range(nc):
                copy(a, nw * nc + a * nc + k, slots[k], 1 - c, sibling).wait_recv()
        for cp in first + passed:
            cp.wait_send()

    return start, finish


def _gather_sems(nw):
    n = 2 * nw * len(CHIP_FLIPS)
    return [pltpu.SemaphoreType.DMA((n,)), pltpu.SemaphoreType.DMA((n,))]


def _gather_out_shapes(ws):
    return [jax.ShapeDtypeStruct((N_CHIPS,) + w.shape, w.dtype) for w in ws]


def _fill_own(outs, ws):
    if not ws:
        return []
    chip =2 * lax.axis_index("x") + lax.axis_index("y")
    return [_put(o, w, chip) for o, w in zip(outs, ws)]


def gather_weights(ws):
    nw = len(ws)

    def body(*refs):
        start, finish = _gather_steps(refs[:nw], refs[nw:2 * nw], *refs[2 * nw:])
        start()
        finish()

    outs = pl.pallas_call(
        body, name="gather_weights", in_specs=[ANY] * nw, out_specs=[ANY] * nw,
        out_shape=_gather_out_shapes(ws), scratch_shapes=_gather_sems(nw),
    )(*ws)
    return _fill_own(outs, ws)


def swap_halves(gs):
    nw = len(gs)

    def body(*refs):
        g_refs, t_refs = refs[:nw], refs[nw:2 * nw]
        send_sems, recv_sems = refs[2 * nw:]
        x, y, c = lax.axis_index("x"), lax.axis_index("y"), lax.axis_index("c")
        cps = [pltpu.make_async_remote_copy(src_ref=g_refs[a].at[_half(gs[a].shape[0], 1 - c)], dst_ref=t_refs[a],
                                            send_sem=send_sems.at[a], recv_sem=recv_sems.at[a],
                                            device_id=(x, y, 1 - c), device_id_type=MESH) for a in range(nw)]
        for cp in cps:
            cp.start()
        for cp in cps:
            cp.wait()

    return pl.pallas_call(
        body, name="swap_halves", in_specs=[ANY] * nw, out_specs=[ANY] * nw,
        out_shape=[jax.ShapeDtypeStruct((g.shape[0] // 2,) + g.shape[1:], g.dtype) for g in gs],
        scratch_shapes=[pltpu.SemaphoreType.DMA((nw,)), pltpu.SemaphoreType.DMA((nw,))],
    )(*gs)


def exchange_blocks(ps):
    nw, nc = len(ps), len(CHIP_FLIPS)

    def body(*refs):
        p_refs, r_refs = refs[:nw], refs[nw:2 * nw]
        send_sems, recv_sems = refs[2 * nw:]
        x, y, c = lax.axis_index("x"), lax.axis_index("y"), lax.axis_index("c")
        me = 2 * x + y
        peers = [(_flip(x, fx), _flip(y, fy), c) for fx, fy in CHIP_FLIPS]
        slots = [2 * px + py for px, py, _ in peers]

        def copy(a, k, src_slot, dst_slot):
            return pltpu.make_async_remote_copy(src_ref=p_refs[a].at[src_slot], dst_ref=r_refs[a].at[dst_slot],
                                                send_sem=send_sems.at[a * nc + k], recv_sem=recv_sems.at[a * nc + k],
                                                device_id=peers[k], device_id_type=MESH)

        for a in range(nw):
            for k in range(nc):
                copy(a, k, slots[k], me).start()
        for a in range(nw):
            for k in range(nc):
                copy(a, k, me, slots[k]).wait_recv()
        for a in range(nw):
            for k in range(nc):
                copy(a, k, slots[k], me).wait_send()

    outs = pl.pallas_call(
        body, name="exchange_blocks", in_specs=[ANY] * nw, out_specs=[ANY] * nw,
        out_shape=[jax.ShapeDtypeStruct(p.shape, p.dtype) for p in ps],
        scratch_shapes=[pltpu.SemaphoreType.DMA((nw * nc,)), pltpu.SemaphoreType.DMA((nw * nc,))],
    )(*ps)
    chip = 2 * lax.axis_index("x") + lax.axis_index("y")
    return [_put(o, lax.dynamic_slice_in_dim(p, chip, 1, axis=0), chip) for o, p in zip(outs, ps)]


def join_halves(ss):
    nw = len(ss)

    def body(*refs):
        s_refs, o_refs = refs[:nw], refs[nw:2 * nw]
        send_sems, recv_sems = refs[2 * nw:]
        x, y, c = lax.axis_index("x"), lax.axis_index("y"), lax.axis_index("c")

        def copy(a, half):
            hs = _half(2 * ss[a].shape[0], half)
            return pltpu.make_async_remote_copy(src_ref=s_refs[a], dst_ref=o_refs[a].at[hs], send_sem=send_sems.at[a],
                                                recv_sem=recv_sems.at[a], device_id=(x, y, 1 - c), device_id_type=MESH)

        for a in range(nw):
            copy(a, c).start()
        for a in range(nw):
            copy(a, 1 - c).wait_recv()
        for a in range(nw):
            copy(a, c).wait_send()

    outs = pl.pallas_call(
        body, name="join_halves", in_specs=[ANY] * nw, out_specs=[ANY] * nw,
        out_shape=[jax.ShapeDtypeStruct((2 * s.shape[0],) + s.shape[1:], s.dtype) for s in ss],
        scratch_shapes=[pltpu.SemaphoreType.DMA((nw,)), pltpu.SemaphoreType.DMA((nw,))],
    )(*ss)
    c = lax.axis_index("c")
    return [_put(o, s, c * s.shape[0]) for o, s in zip(outs, ss)]


def _tile_rows(a, b, itemsize=4, budget=4 * 2 ** 20):
    best = 8
    for t in range(8, a + 1, 8):
        if a % t == 0 and t * b * itemsize <= budget:
            best = t
    return best


def add_half(g, t, cidx):
    def body(c_ref, g_ref, t_ref, o_ref):
        o_ref[...] = (g_ref[...] + t_ref[...]).astype(bf16)

    if g.ndim == 5:
        n0, F, _, A, B = g.shape
        hsz, ta = n0 // 2, _tile_rows(A, B)
        nta = A // ta
        grid = (hsz, N_CHIPS, F, nta)
        in_specs = [pl.BlockSpec((None, None, None, ta, B), lambda h, k, f, i, c_ref: (c_ref[0] * hsz + h, f, k, i, 0)),
                    pl.BlockSpec((None, None, None, ta, B), lambda h, k, f, i, c_ref: (h, f, k, i, 0))]
        out_spec = pl.BlockSpec((None, None, ta, B), lambda h, k, f, i, c_ref: (k, h, f * nta + i, 0))
    else:
        n0, _, A, B = g.shape
        F, hsz, ta = 1, n0 // 2, _tile_rows(A, B)
        grid = (hsz, N_CHIPS, A // ta)
        in_specs = [pl.BlockSpec((None, None, ta, B), lambda h, k, i, c_ref: (c_ref[0] * hsz + h, k, i, 0)),
                    pl.BlockSpec((None, None, ta, B), lambda h, k, i, c_ref: (h, k, i, 0))]
        out_spec = pl.BlockSpec((None, None, ta, B), lambda h, k, i, c_ref: (k, h, i, 0))
    return pl.pallas_call(
        body, name="add_half",
        grid_spec=pltpu.PrefetchScalarGridSpec(num_scalar_prefetch=1, grid=grid, in_specs=in_specs, out_specs=out_spec),
        out_shape=jax.ShapeDtypeStruct((N_CHIPS, hsz, F * A, B), bf16),
        compiler_params=_cp(),
    )(cidx, g, t)


def sum_chips(r):
    _, h, A, B = r.shape
    ta = _tile_rows(A, B)

    def body(r_ref, o_ref):
        s = r_ref[0].astype(f32)
        for j in range(1, N_CHIPS):
            s = s + r_ref[j].astype(f32)
        o_ref[...] = s

    return pl.pallas_call(
        body, name="sum_chips", grid=(h, A // ta),
        in_specs=[pl.BlockSpec((N_CHIPS, None, ta, B), lambda hh, i: (0, hh, i, 0))],
        out_specs=pl.BlockSpec((None, ta, B), lambda hh, i: (hh, i, 0)),
        out_shape=jax.ShapeDtypeStruct((h, A, B), f32),
        compiler_params=_cp(),
    )(r)


WEIGHTS = ('ada_w', 'ada_b', 'ln_g', 'ln_b', 'ffn_w_in', 'ffn_w_out', 'ev_w_in', 'ssd_conv_w', 'ssd_conv_b',
           'ssd_dt_bias', 'ssd_a_log', 'ssd_d', 'ssd_norm_g', 'pool_w', 'pool_scale', 'ev_w_out', 'od_w_in',
           'conf_dw_w', 'conf_dw_b', 'conf_ln_g', 'conf_ln_b', 'lru_conv_w', 'lru_conv_b', 'lru_wa', 'lru_ba',
           'lru_wx', 'lru_bx', 'lru_lambda', 'od_w_out')
BIG =('ffn_w_in', 'ffn_w_out', 'ev_w_out', 'od_w_in', 'od_w_out', 'ev_w_in')
SMALL_SHARDED = ('ln_g', 'ln_b', 'ssd_conv_w', 'conf_dw_w', 'conf_dw_b', 'conf_ln_g', 'conf_ln_b', 'lru_conv_w',
                 'lru_conv_b', 'lru_ba', 'lru_bx', 'lru_lambda')
SMALL_REPLICATED = ('ssd_conv_b', 'ssd_dt_bias', 'ssd_a_log', 'ssd_d', 'ssd_norm_g', 'pool_w', 'pool_scale',
                    'lru_wa', 'lru_wx')
PACK_COLS = 1024
BIG_ROW_TILE = 256


def _pack(arrs, row_mult):
    flat = jnp.concatenate([a.reshape(-1) for a in arrs])
    rows = -(-flat.shape[0] // (PACK_COLS * row_mult)) * row_mult
    return jnp.pad(flat, (0, rows * PACK_COLS - flat.shape[0])).reshape(rows, PACK_COLS)


def _unpack(flat, shapes):
    out, off = [], 0
    for s in shapes:
        n = 1
        for d in s:
            n *= d
        out.append(flat[off:off + n].reshape(s))
        off += n
    return out


def _unshard_last(g4):
    m = jnp.moveaxis(g4, 0, -2)
    return m.reshape(m.shape[:-2] + (m.shape[-2] * m.shape[-1],))


def _pad_rows(a, rows):
    return jnp.pad(a, ((0, rows - a.shape[0]),) + ((0, 0),) * (a.ndim - 1))


def _pad_lanes(a):
    return jnp.pad(a, ((0, 0), (0, LANES - a.shape[1])))


def kernel(x, c, ada_w, ada_b, ln_g, ln_b, ffn_w_in, ffn_w_out, ev_w_in, ssd_conv_w, ssd_conv_b, ssd_dt_bias, ssd_a_log, ssd_d, ssd_norm_g, pool_w, pool_scale, ev_w_out, od_w_in, conf_dw_w, conf_dw_b, conf_ln_g, conf_ln_b, lru_conv_w, lru_conv_b, lru_wa, lru_ba, lru_wx, lru_bx, lru_lambda, od_w_out, loss_target, m_ada_w, m_ada_b, m_ln_g, m_ln_b, m_ffn_w_in, m_ffn_w_out, m_ev_w_in, m_ssd_conv_w, m_ssd_conv_b, m_ssd_dt_bias, m_ssd_a_log, m_ssd_d, m_ssd_norm_g, m_pool_w, m_pool_scale, m_ev_w_out, m_od_w_in, m_conf_dw_w, m_conf_dw_b, m_conf_ln_g, m_conf_ln_b, m_lru_conv_w, m_lru_conv_b, m_lru_wa, m_lru_ba, m_lru_wx, m_lru_bx, m_lru_lambda, m_od_w_out, v_ada_w, v_ada_b, v_ln_g, v_ln_b, v_ffn_w_in, v_ffn_w_out, v_ev_w_in, v_ssd_conv_w, v_ssd_conv_b, v_ssd_dt_bias, v_ssd_a_log, v_ssd_d, v_ssd_norm_g, v_pool_w, v_pool_scale, v_ev_w_out, v_od_w_in, v_conf_dw_w, v_conf_dw_b, v_conf_ln_g, v_conf_ln_b, v_lru_conv_w, v_lru_conv_b, v_lru_wa, v_lru_ba, v_lru_wx, v_lru_bx, v_lru_lambda, v_od_w_out):
    given = dict(locals())
    W = {n: given[n] for n in WEIGHTS}
    M = {n: given["m_" + n] for n in WEIGHTS}
    V = {n: given["v_" + n] for n in WEIGHTS}
    B, T, D = x.shape
    L = DEPTH
    chip = 2 * lax.axis_index("x") + lax.axis_index("y")
    dev = 2 * chip + lax.axis_index("c")

    g1 = allgather8(_pack([c] + [W[n] for n in SMALL_SHARDED], 8)).reshape(N_DEV, -1)
    c_all = g1[:, :B * D].reshape(N_DEV * B, D)
    per_chip = g1[0::2, B * D:]
    full = dict(zip(SMALL_SHARDED, [_unshard_last(jnp.stack(p)) for p in zip(*[
        _unpack(per_chip[k], [W[n].shape for n in SMALL_SHARDED]) for k in range(N_CHIPS)])]))
    for n in SMALL_REPLICATED:
        full[n] = W[n]

    n_ada = ada_w.shape[2]
    ada_b_cols = lax.dynamic_slice_in_dim(ada_b, chip * n_ada, n_ada, axis=1)[:, None, :]
    mod_cols = ada_fwd(c_all, ada_w, ada_b_cols)
    g2 = allgather8(mod_cols.reshape(-1, PACK_COLS))[0::2].reshape(N_CHIPS, L, N_DEV * B, n_ada)
    mod_all = jnp.moveaxis(g2, 0, 2).reshape(L, N_DEV * B, N_CHIPS * n_ada)
    mod = lax.dynamic_slice_in_dim(mod_all, dev * B, B, axis=1).reshape(L, B, N_MOD, D)

    FS = ffn_w_in.shape[3]

    def ffn_shards(l, f):
        return [ffn_w_in[l, f].astype(bf16), ffn_w_out[l, f].astype(bf16)]

    def mix_shards(l):
        w_i, w_o = (ev_w_in, ev_w_out) if l % 2 == 0 else (od_w_in, od_w_out)
        return [w_i[l // 2].astype(bf16), w_o[l // 2].astype(bf16)]

    def ffn_weights(g):
        return g[0], g[1].reshape(-1, D)

    def mix_weights(l, g):
        if l % 2 == 0:
            w = _unshard_last(g[0])
            w_i = jnp.concatenate([w[:, :2560], w[:, 2576:], w[:, 2560:2576], jnp.zeros((D, EVEN_IN_PAD - EVEN_IN), bf16)], axis=1)
        else:
            w_i = jnp.moveaxis(g[0], 0, 1).reshape(D, -1)
        return w_i, g[1].reshape(-1, D)

    saved = []
    xs = x
    next_ffn = gather_weights(ffn_shards(0, 0))
    next_mix = None
    for l in range(L):
        lg, lb = full['ln_g'][l], full['ln_b'][l]
        rec = {}
        m1, m2, m3 = mod[l][:, 0:3], mod[l][:, 3:6], mod[l][:, 6:9]
        w_in_a, w_out_a = ffn_weights(next_ffn)
        carry = ffn_shards(l, 1) + (mix_shards(l) if l == 0 else [])
        xn, h, gu, a, y, got = ffn_fwd(xs, m1, w_in_a, w_out_a, lg[0:1], lb[0:1], carry)
        w_in_b, w_out_b = ffn_weights(got[:2])
        w_in_m, w_out_m = mix_weights(l, got[2:] if l == 0 else next_mix)
        rec['ffa'] = (xs, h, gu, a, y, m1, w_in_a, w_out_a, lg[0:1])
        xs = xn
        if l % 2 == 0:
            e = l // 2
            cw = _pad_rows(full['ssd_conv_w'][e], 8)
            cb = full['ssd_conv_b'][e][None]
            dtb, alog, dsk = (_pad_lanes(full[n][e][None]) for n in ('ssd_dt_bias', 'ssd_a_log', 'ssd_d'))
            ng, pw, ps = full['ssd_norm_g'][e][None], full['pool_w'][e], full['pool_scale'][e][None]
            proj, hm = inproj_fwd(xs, m2, w_in_m)
            cx = dwconv_fwd(proj, 2, SSD_CONV_DIM, cw, cb, SSD_CONV)
            ya, hsave = ssd_fwd(proj, cx, dtb, alog, dsk, ng)
            yb = pool_fwd(proj, 5, pw, ps)
            ycat = jnp.concatenate([ya, yb], axis=-1)
            rec['mix'] = (proj, cx, hsave, cw, dtb, alog, dsk, ng, pw, ps)
        else:
            o = l // 2
            dww =_pad_rows(full['conf_dw_w'][o], 32)
            dwb = full['conf_dw_b'][o][None]
            cw = _pad_rows(full['lru_conv_w'][o], 8)
            cb = full['lru_conv_b'][o][None]
            cln = jnp.stack([full['conf_ln_g'][o], full['conf_ln_b'][o]])
            vec = jnp.stack([full['lru_ba'][o], full['lru_bx'][o], full['lru_lambda'][o]])
            wa, wx = full['lru_wa'][o], full['lru_wx'][o]
            proj, hm = inproj_fwd(xs, m2, w_in_m)
            hh = glu_fwd(proj)
            cc = dwconv_fwd(hh, 0, CONF_DIM, dww, dwb, CONF_KERNEL)
            xc = dwconv_fwd(proj, 2, LRU_DIM, cw, cb, LRU_CONV)
            ycat, hst = lru_fwd(cc, xc, proj, cln, wa, wx, vec)
            rec['mix'] = (proj, hh, cc, xc, hst, dww, cw, cln, wa, wx, vec)
        xn, ym = outproj_fwd(ycat, w_out_m, xs, m2, lg[1:2], lb[1:2])
        rec['mixio'] = (xs, hm, ycat, ym, m2, w_in_m, w_out_m, lg[1:2])
        xs = xn
        carry = ffn_shards(l + 1, 0) + mix_shards(l + 1) if l + 1 < L else []
        xn, h, gu, a, y, got = ffn_fwd(xs, m3, w_in_b, w_out_b, lg[2:3], lb[2:3], carry)
        next_ffn, next_mix = got[:2], got[2:]
        rec['ffb'] = (xs, h, gu, a, y, m3, w_in_b, w_out_b, lg[2:3])
        xs = xn
        saved.append(rec)

    sq, dxs = loss_fwd(xs, loss_target)
    loss = lax.psum(sq[0, 0], ("x", "y", "c")) * (0.5 / D)

    gpart = {n: [None] * W[n].shape[0] for n in WEIGHTS}
    gpart['ffn_w_in'] = gpart['ffn_w_out'] = None
    gpart['ln_g'] = [[None] * 3 for _ in range(L)]
    gpart['ln_b'] = [[None] * 3 for _ in range(L)]
    dmod = [None] * L

    def ffn_back(dxn, rec, l, f):
        xin, h, gu, a, y, m3_, w_in_, w_out_, lg_ = rec
        dx, dgu, dy, dm3, dln = ffn_bwd(dxn, xin, y, gu, m3_, w_in_, w_out_, lg_)
        gpart['ffn_w_in'] = wgrad_into(h, dgu, FS, (L, 2), (l, f), gpart['ffn_w_in'])
        gpart['ffn_w_out'] = wgrad_into(a, dy, D, (L, 2), (l, f), gpart['ffn_w_out'])
        gpart['ln_g'][l][2 * f] = dln[0]
        gpart['ln_b'][l][2 * f] = dln[1]
        return dx, dm3

    for l in reversed(range(L)):
        rec = saved[l]
        dxs, dm3 = ffn_back(dxs, rec['ffb'], l, 1)
        xin, hm, ycat, ym, m2, w_in_m, w_out_m, lg_ = rec['mixio']
        dxp, dycat, dy, dg2, dln = outproj_bwd(dxs, xin, ym, m2, w_out_m, lg_)
        gpart['ln_g'][l][1] = dln[0]
        gpart['ln_b'][l][1] = dln[1]
        gw_out = wgrad(ycat, dy, D)[0].reshape(N_CHIPS, -1, D)
        if l % 2 == 0:
            e = l // 2
            proj, cx, hsave, cw, dtb, alog, dsk, ng, pw, ps = rec['mix']
            dz, dcx, ddt, gsm, gng = ssd_bwd(dycat, proj, cx, hsave, dtb, alog, dsk, ng)
            dxbc, dcw, dcb = dwconv_bwd(dcx, proj, 2, SSD_CONV_DIM, cw, SSD_CONV, bf16)
            du, dpw, dps = pool_bwd(dycat, 2, proj, 5, pw, ps)
            dproj = jnp.concatenate([dz, dxbc, du, ddt], axis=-1)
            gwp = wgrad(hm, dproj, EVEN_IN_PAD)[0]
            gw = jnp.concatenate([gwp[:, :2560], gwp[:, 3072:3072 + 16], gwp[:, 2560:3072]], axis=1)
            gpart['ev_w_in'][e] = jnp.moveaxis(gw.reshape(D, N_CHIPS, -1), 1, 0)
            gpart['ev_w_out'][e] = gw_out
            gpart['ssd_conv_w'][e], gpart['ssd_conv_b'][e] = dcw[:SSD_CONV], dcb[0]
            gpart['ssd_dt_bias'][e], gpart['ssd_a_log'][e], gpart['ssd_d'][e] = (gsm[k, :SSD_HEADS] for k in range(3))
            gpart['ssd_norm_g'][e], gpart['pool_w'][e], gpart['pool_scale'][e] = gng[0], dpw, dps[0]
        else:
            o = l // 2
            proj, hh, cc, xc, hst, dww, cw, cln, wa, wx, vec = rec['mix']
            dcc, dxc, dgr, dcln, dwa, dwx, dvec = lru_bwd(dycat, cc, xc, proj, hst, cln, wa, wx, vec)
            dhh, ddw, ddb = dwconv_bwd(dcc, hh, 0, CONF_DIM, dww, CONF_KERNEL, f32)
            dvg = glu_bwd(dhh, proj)
            dxr, dcw, dcb = dwconv_bwd(dxc, proj, 2, LRU_DIM, cw, LRU_CONV, bf16)
            dproj = jnp.concatenate([dvg, dxr, dgr], axis=-1)
            gpart['od_w_in'][o] = wgrad(hm, dproj, dproj.shape[-1] // N_CHIPS)
            gpart['od_w_out'][o] = gw_out
            gpart['conf_dw_w'][o], gpart['conf_dw_b'][o] = ddw[:CONF_KERNEL], ddb[0]
            gpart['conf_ln_g'][o], gpart['conf_ln_b'][o] = dcln[0], dcln[1]
            gpart['lru_conv_w'][o], gpart['lru_conv_b'][o] = dcw[:LRU_CONV], dcb[0]
            gpart['lru_wa'][o], gpart['lru_wx'][o] = dwa, dwx
            gpart['lru_ba'][o], gpart['lru_bx'][o], gpart['lru_lambda'][o] = dvec[0], dvec[1], dvec[2]
        dxs, dm2 = inproj_bwd(dproj, w_in_m, xin, m2, dxp)
        dxs, dm1 = ffn_back(dxs, rec['ffa'], l, 0)
        dmod[l] = jnp.concatenate([dm1, dm2, dg2, dm3], axis=1)
    grad_x = dxs

    def stack(v):
        return jnp.stack([stack(u) if isinstance(u, list) else u for u in v])

    def per_chip(n):
        if W[n].ndim == 4:
            return gpart[n].reshape(W[n].shape[:2] + (N_CHIPS,) + W[n].shape[2:])
        return jnp.stack(gpart[n])

    gfull = [per_chip(n) for n in BIG]
    cidx = lax.axis_index("c").astype(jnp.int32).reshape(1)
    summed = [add_half(g, t, cidx) for g, t in zip(gfull, swap_halves(gfull))]
    reduced = join_halves([sum_chips(r) for r in exchange_blocks(summed)])

    out_g, out_d, out_m, out_v = {}, {}, {}, {}
    for n, g in zip(BIG, reduced):
        shp = W[n].shape
        as2d = lambda a: a.reshape(-1, shp[-1])
        res = adamw(as2d(W[n]), as2d(M[n]), as2d(V[n]), [as2d(g)], [0], BIG_ROW_TILE)
        out_g[n] = g.reshape(shp)
        out_d[n], out_m[n], out_v[n] = (r.reshape(shp) for r in res)

    small = SMALL_SHARDED + SMALL_REPLICATED
    dmod_flat = stack(dmod).reshape(L, B, N_MOD * D)
    g3 = allgather8(_pack([dmod_flat] + [stack(gpart[n]) for n in small], 64))
    n_dmod = L * B * N_MOD * D
    dmod_all = jnp.moveaxis(g3.reshape(N_DEV, -1)[:, :n_dmod].reshape(N_DEV, L, B, N_MOD * D), 0, 1).reshape(L, N_DEV * B, N_MOD * D)
    ssum = sum_leading(g3, 64).reshape(-1)[n_dmod:]
    gsmall = dict(zip(small, _unpack(ssum, [full[n].shape for n in small])))
    for n in SMALL_SHARDED:
        wdt = W[n].shape[-1]
        gsmall[n] = lax.dynamic_slice_in_dim(gsmall[n], chip * wdt, wdt, axis=gsmall[n].ndim - 1)
    dmod_my = lax.dynamic_slice_in_dim(dmod_all, chip * n_ada, n_ada, axis=2)
    g_ada_w, g_ada_b = ada_bwd(c_all, dmod_my, dmod_all)
    gsmall['ada_b'] = g_ada_b[:, 0, :]

    res = adamw(ada_w.reshape(-1, n_ada), M['ada_w'].reshape(-1, n_ada), V['ada_w'].reshape(-1, n_ada),
                [g_ada_w.reshape(-1, n_ada)], [0], BIG_ROW_TILE)
    out_g['ada_w'] = g_ada_w
    out_d['ada_w'], out_m['ada_w'], out_v['ada_w'] = (r.reshape(ada_w.shape) for r in res)

    names = ('ada_b',) + small
    shapes = [W[n].shape for n in names]
    res = adamw(_pack([W[n] for n in names], 64), _pack([M[n] for n in names], 64), _pack([V[n] for n in names], 64),
                [_pack([gsmall[n] for n in names], 64)], [0], 64)
    out_g.update({n: gsmall[n] for n in names})
    for dst, r in zip((out_d, out_m, out_v), res):
        dst.update(zip(names, _unpack(r.reshape(-1), shapes)))

    return (loss, grad_x, *[out_g[n] for n in WEIGHTS], *[out_d[n] for n in WEIGHTS], *[out_m[n] for n in WEIGHTS],
            *[out_v[n] for n in WEIGHTS])
```

```python
import jax
import jax.numpy as jnp
from jax import lax
from jax.experimental import pallas as pl
from jax.experimental.pallas import tpu as pltpu

f32 = jnp.float32
bf16 = jnp.bfloat16

DEPTH = 4
D_MODEL = 1024
N_MOD = 9
DN_ALPHA = (2.0 * DEPTH) ** 0.25
NORM_EPS = 1e-5
SSD_CHUNK = 128
SSD_D_INNER = 1024
SSD_CONV_DIM = 1536
SSD_HEADS = 16
POOL_WINDOWS = (2, 4, 8, 16)
POOL_DIM = 512
EVEN_IN = 3088
EVEN_IN_PAD = 3200
CONF_DIM = 512
CONF_KERNEL = 31
LRU_DIM = 1024
LRU_HEADS = 8
LRU_CONV = 4
SSD_CONV = 4
LRU_C = 8.0
ADAM_LR = 0.001
ADAM_B1 = 0.9
ADAM_B2 = 0.999
ADAM_EPS = 1e-08
ADAM_WD = 0.01
ADAM_STEP = 10

SSD_ZX = SSD_D_INNER + SSD_CONV_DIM
LANES = 128
VMEM_LIMIT_BYTES = 56 * 2 ** 20
COL_TILE = 512
XBC_COLB = SSD_D_INNER // COL_TILE
XR_COLB = 2 * CONF_DIM // COL_TILE
POOL_COLB = SSD_ZX // POOL_DIM
POOL_DY_COLB = SSD_D_INNER // POOL_DIM
N_CHIPS = 4
N_DEV = 8
MESH = pl.DeviceIdType.MESH


def _cp():
    return pltpu.CompilerParams(vmem_limit_bytes=VMEM_LIMIT_BYTES)


def _dot(a, b):
    return jnp.dot(a, b, preferred_element_type=f32)


def _dot_nt(a, b):
    return lax.dot_general(a, b, (((1,), (1,)), ((), ())), preferred_element_type=f32)


def _dot_tn(a, b):
    return lax.dot_general(a, b, (((0,), (0,)), ((), ())), preferred_element_type=f32)


def _dot_hi(a, b):
    return jnp.dot(a, b, preferred_element_type=f32, precision=lax.Precision.HIGHEST)


def _silu(x):
    return x * jax.nn.sigmoid(x)


def _ln_stats(z):
    mu = jnp.mean(z, axis=-1, keepdims=True)
    zc = z - mu
    var = jnp.mean(zc * zc, axis=-1, keepdims=True)
    rstd = lax.rsqrt(var + NORM_EPS)
    return zc * rstd, rstd


def _ln_bwd(dxn, xhat, rstd, lg):
    dxh = dxn * lg
    return rstd * (dxh - jnp.mean(dxh, axis=-1, keepdims=True) - xhat * jnp.mean(dxh * xhat, axis=-1, keepdims=True))


def _const_spec(shape):
    nd = len(shape)
    return pl.BlockSpec(shape, lambda *_: (0,) * nd, pipeline_mode=pl.Buffered(1))


def _row_tile(t, want=256):
    return min(want, t)


def ffn_fwd(x, mod3, w_in, w_out, lg, lb, carry=()):
    B, T, D = x.shape
    FS = w_in.shape[2]
    tm = _row_tile(T, 512)
    nt = T // tm
    nw = len(carry)

    def body(x_ref, mod_ref, win_ref, wout_ref, lg_ref, lb_ref, *rest):
        xn_ref, h_ref, gu_ref, a_ref, y_ref = rest[nw:nw + 5]
        if nw:
            start, finish = _gather_steps(rest[:nw], rest[nw + 5:2 * nw + 5], *rest[2 * nw + 5:])
            b, i = pl.program_id(0), pl.program_id(1)
            pl.when((b == 0) & (i == 0))(start)
        xv = x_ref[...]
        sh, sc, g = mod_ref[0:1, :], mod_ref[1:2, :], mod_ref[2:3, :]
        h = (xv * (1.0 + sc) + sh).astype(bf16)
        h_ref[...] = h
        acc = jnp.zeros((tm, D), f32)
        for s in range(2):
            gate = _dot(h, win_ref[s])
            up = _dot(h, win_ref[s + 2])
            gu_ref[:, s * FS:(s + 1) * FS] = gate.astype(bf16)
            gu_ref[:, (s + 2) * FS:(s + 3) * FS] = up.astype(bf16)
            a = (_silu(gate) * up).astype(bf16)
            a_ref[:, s * FS:(s + 1) * FS] = a
            acc = acc + _dot(a, wout_ref[s * FS:(s + 1) * FS, :])
        y_ref[...] = acc
        xhat, _ = _ln_stats(DN_ALPHA * xv + 0.5 * (1.0 + g) * acc)
        xn_ref[...] = xhat * lg_ref[...] + lb_ref[...]
        if nw:
            pl.when((b == B - 1) & (i == nt - 1))(finish)

    row = lambda w: pl.BlockSpec((None, tm, w), lambda b, i: (b, i, 0))
    res = pl.pallas_call(
        body, name="ffn_fwd_gather" if nw else "ffn_fwd", grid=(B, nt),
        in_specs=[row(D), pl.BlockSpec((None, 3, D), lambda b, i: (b, 0, 0)), _const_spec(w_in.shape),
                  _const_spec(w_out.shape), _const_spec((1, D)), _const_spec((1, D))] + [ANY] * nw,
        out_specs=[row(D), row(D), row(4 * FS), row(2 * FS), row(D)] + [ANY] * nw,
        out_shape=[jax.ShapeDtypeStruct((B, T, D), f32), jax.ShapeDtypeStruct((B, T, D), bf16),
                   jax.ShapeDtypeStruct((B, T, 4 * FS), bf16), jax.ShapeDtypeStruct((B, T, 2 * FS), bf16),
                   jax.ShapeDtypeStruct((B, T, D), f32)] + _gather_out_shapes(carry),
        scratch_shapes=_gather_sems(nw) if nw else [],
        compiler_params=_cp(),
    )(x, mod3, w_in, w_out, lg, lb, *carry)
    return tuple(res[:5]) + (_fill_own(res[5:], carry),)


def ffn_bwd(dxn, x, y, gu, mod3, w_in, w_out, lg):
    B, T, D = x.shape
    FS = w_in.shape[2]
    tm = _row_tile(T)

    def body(dxn_ref, x_ref, y_ref, gu_ref, mod_ref, win_ref, wout_ref, lg_ref,
             dx_ref, dgu_ref, dy_ref, dmod_ref, dln_ref):
        b, i = pl.program_id(0), pl.program_id(1)

        @pl.when((b == 0) & (i == 0))
        def _():
            dln_ref[...] = jnp.zeros_like(dln_ref)

        @pl.when(i == 0)
        def _():
            dmod_ref[...] = jnp.zeros_like(dmod_ref)

        xv, yv, dxn_v = x_ref[...], y_ref[...], dxn_ref[...]
        sc, g = mod_ref[1:2, :], mod_ref[2:3, :]
        xhat, rstd = _ln_stats(DN_ALPHA * xv + 0.5 * (1.0 + g) * yv)
        dln_ref[0:1, :] += jnp.sum(dxn_v * xhat, axis=0, keepdims=True)
        dln_ref[1:2, :] += jnp.sum(dxn_v, axis=0, keepdims=True)
        dz = _ln_bwd(dxn_v, xhat, rstd, lg_ref[...])
        dmod_ref[2:3, :] += jnp.sum(0.5 * dz * yv, axis=0, keepdims=True)
        dy = (0.5 * (1.0 + g) * dz).astype(bf16)
        dy_ref[...] = dy
        dh = jnp.zeros((tm, D), f32)
        for s in range(2):
            da = _dot_nt(dy, wout_ref[s * FS:(s + 1) * FS, :]).astype(bf16)
            gate = gu_ref[:, s * FS:(s + 1) * FS]
            up = gu_ref[:, (s + 2) * FS:(s + 3) * FS]
            sig = jax.nn.sigmoid(gate)
            dgate = da * up * (sig * (1.0 + gate * (1.0 - sig)))
            dup = da * (gate * sig)
            dgu_ref[:, s * FS:(s + 1) * FS] = dgate
            dgu_ref[:, (s + 2) * FS:(s + 3) * FS] = dup
            dh = dh + _dot_nt(dgate, win_ref[s]) + _dot_nt(dup, win_ref[s + 2])
        dx_ref[...] = DN_ALPHA * dz + dh * (1.0 + sc)
        dmod_ref[0:1, :] += jnp.sum(dh, axis=0, keepdims=True)
        dmod_ref[1:2, :] += jnp.sum(dh * xv, axis=0, keepdims=True)

    row = lambda w: pl.BlockSpec((None, tm, w), lambda b, i: (b, i, 0))
    return pl.pallas_call(
        body, name="ffn_bwd", grid=(B, T // tm),
        in_specs=[row(D), row(D), row(D), row(4 * FS), pl.BlockSpec((None, 3, D), lambda b, i: (b, 0, 0)),
                  _const_spec(w_in.shape), _const_spec(w_out.shape), _const_spec((1, D))],
        out_specs=[row(D), row(4 * FS), row(D), pl.BlockSpec((None, 3, D), lambda b, i: (b, 0, 0)),
                   pl.BlockSpec((2, D), lambda b, i: (0, 0))],
        out_shape=[jax.ShapeDtypeStruct((B, T, D), f32), jax.ShapeDtypeStruct((B, T, 4 * FS), bf16),
                   jax.ShapeDtypeStruct((B, T, D), bf16), jax.ShapeDtypeStruct((B, 3, D), f32),
                   jax.ShapeDtypeStruct((2, D), f32)],
        compiler_params=_cp(),
    )(dxn, x, y, gu, mod3, w_in, w_out, lg)


def wgrad(a, b, tn):
    B, T, K = a.shape
    N = b.shape[2]
    tr = _row_tile(T, 1024 if K * tn <= 1024 * 1536 else 512)

    def body(a_ref, b_ref, o_ref):
        @pl.when((pl.program_id(1) == 0) & (pl.program_id(2) == 0))
        def _():
            o_ref[...] = jnp.zeros_like(o_ref)

        o_ref[...] += _dot_tn(a_ref[...], b_ref[...])

    return pl.pallas_call(
        body, name="wgrad", grid=(N // tn, B, T // tr),
        in_specs=[pl.BlockSpec((None, tr, K), lambda s, b, r: (b, r, 0)),
                  pl.BlockSpec((None, tr, tn), lambda s, b, r: (b, r, s))],
        out_specs=pl.BlockSpec((None, K, tn), lambda s, b, r: (s, 0, 0)),
        out_shape=jax.ShapeDtypeStruct((N // tn, K, tn), f32),
        compiler_params=_cp(),
    )(a, b)


def wgrad_into(a, b, tn, lead, pos, buf=None):
    B, T, K = a.shape
    N = b.shape[2]
    tr = _row_tile(T, 1024 if K * tn <= 1024 * 1536 else 512)
    nl = len(lead)

    def body(pos_ref, a_ref, b_ref, *rest):
        o_ref = rest[-1]

        @pl.when((pl.program_id(1) == 0) & (pl.program_id(2) == 0))
        def _():
            o_ref[...] = jnp.zeros_like(o_ref)

        o_ref[...] += _dot_tn(a_ref[...], b_ref[...])

    return pl.pallas_call(
        body, name="wgrad_into",
        grid_spec=pltpu.PrefetchScalarGridSpec(
            num_scalar_prefetch=1, grid=(N // tn, B, T // tr),
            in_specs=[pl.BlockSpec((None, tr, K), lambda s, b, r, p: (b, r, 0)),
                      pl.BlockSpec((None, tr, tn), lambda s, b, r, p: (b, r, s))] + ([] if buf is None else [ANY]),
            out_specs=pl.BlockSpec((None,) * (nl + 1) + (K, tn),
                                   lambda s, b, r, p: tuple(p[j] for j in range(nl)) + (s, 0, 0))),
        out_shape=jax.ShapeDtypeStruct(tuple(lead) + (N // tn, K, tn), f32),
        input_output_aliases={} if buf is None else {3: 0},
        compiler_params=_cp(),
    )(jnp.asarray(pos, jnp.int32), a, b, *([] if buf is None else [buf]))


def inproj_fwd(x, mod3, w):
    B, T, D = x.shape
    N = w.shape[1]
    tm = _row_tile(T, 512)

    def body(x_ref, mod_ref, w_ref, p_ref, h_ref):
        h = (x_ref[...] * (1.0 + mod_ref[1:2, :]) + mod_ref[0:1, :]).astype(bf16)
        h_ref[...] = h
        p_ref[...] = _dot(h, w_ref[...])

    row = lambda n: pl.BlockSpec((None, tm, n), lambda b, i: (b, i, 0))
    return pl.pallas_call(
        body, name="inproj_fwd", grid=(B, T // tm),
        in_specs=[row(D), pl.BlockSpec((None, 3, D), lambda b, i: (b, 0, 0)), _const_spec(w.shape)],
        out_specs=[row(N), row(D)],
        out_shape=[jax.ShapeDtypeStruct((B, T, N), f32), jax.ShapeDtypeStruct((B, T, D), bf16)],
        compiler_params=_cp(),
    )(x, mod3, w)


def inproj_bwd(dproj, w, x, mod3, dxp):
    B, T, D = x.shape
    N = w.shape[1]
    tm = _row_tile(T, 512)

    def body(dp_ref, w_ref, x_ref, mod_ref, dxp_ref, dx_ref, dmod_ref):
        @pl.when(pl.program_id(1) == 0)
        def _():
            dmod_ref[...] = jnp.zeros_like(dmod_ref)

        dh = _dot_nt(dp_ref[...], w_ref[...])
        dx_ref[...] = dxp_ref[...] + dh * (1.0 + mod_ref[1:2, :])
        dmod_ref[0:1, :] += jnp.sum(dh, axis=0, keepdims=True)
        dmod_ref[1:2, :] += jnp.sum(dh * x_ref[...], axis=0, keepdims=True)

    row = lambda n: pl.BlockSpec((None, tm, n), lambda b, i: (b, i, 0))
    return pl.pallas_call(
        body, name="inproj_bwd", grid=(B, T // tm),
        in_specs=[row(N), _const_spec(w.shape), row(D), pl.BlockSpec((None, 3, D), lambda b, i: (b, 0, 0)), row(D)],
        out_specs=[row(D), pl.BlockSpec((None, 2, D), lambda b, i: (b, 0, 0))],
        out_shape=[jax.ShapeDtypeStruct((B, T, D), f32), jax.ShapeDtypeStruct((B, 2, D), f32)],
        compiler_params=_cp(),
    )(dproj, w, x, mod3, dxp)


def outproj_fwd(ycat, w, x, mod3, lg, lb):
    B, T, D = x.shape
    E = w.shape[0]
    tm = _row_tile(T, 512)

    def body(yc_ref, w_ref, x_ref, mod_ref, lg_ref, lb_ref, xn_ref, y_ref):
        yv = _dot(yc_ref[...], w_ref[...])
        y_ref[...] = yv
        xhat, _ = _ln_stats(DN_ALPHA * x_ref[...] + (1.0 + mod_ref[2:3, :]) * yv)
        xn_ref[...] = xhat * lg_ref[...] + lb_ref[...]

    row = lambda n: pl.BlockSpec((None, tm, n), lambda b, i: (b, i, 0))
    return pl.pallas_call(
        body, name="outproj_fwd", grid=(B, T // tm),
        in_specs=[row(E), _const_spec(w.shape), row(D), pl.BlockSpec((None, 3, D), lambda b, i: (b, 0, 0)),
                  _const_spec((1, D)), _const_spec((1, D))],
        out_specs=[row(D), row(D)],
        out_shape=[jax.ShapeDtypeStruct((B, T, D), f32), jax.ShapeDtypeStruct((B, T, D), f32)],
        compiler_params=_cp(),
    )(ycat, w, x, mod3, lg, lb)


def outproj_bwd(dxn, x, y, mod3, w, lg):
    B, T, D = x.shape
    E = w.shape[0]
    tm = _row_tile(T, 512)

    def body(dxn_ref, x_ref, y_ref, mod_ref, w_ref, lg_ref, dxp_ref, dyc_ref, dy_ref, dg_ref, dln_ref):
        b, i = pl.program_id(0), pl.program_id(1)

        @pl.when((b == 0) & (i == 0))
        def _():
            dln_ref[...] = jnp.zeros_like(dln_ref)

        @pl.when(i == 0)
        def _():
            dg_ref[...] = jnp.zeros_like(dg_ref)

        xv, yv, dxn_v = x_ref[...], y_ref[...], dxn_ref[...]
        g = mod_ref[2:3, :]
        xhat, rstd = _ln_stats(DN_ALPHA * xv + (1.0 + g) * yv)
        dln_ref[0:1, :] += jnp.sum(dxn_v * xhat, axis=0, keepdims=True)
        dln_ref[1:2, :] += jnp.sum(dxn_v, axis=0, keepdims=True)
        dz = _ln_bwd(dxn_v, xhat, rstd, lg_ref[...])
        dg_ref[...] += jnp.sum(dz * yv, axis=0, keepdims=True)
        dy = ((1.0 + g) * dz).astype(bf16)
        dy_ref[...] = dy
        dxp_ref[...] = DN_ALPHA * dz
        dyc_ref[...] = _dot_nt(dy, w_ref[...])

    row = lambda n: pl.BlockSpec((None, tm, n), lambda b, i: (b, i, 0))
    return pl.pallas_call(
        body, name="outproj_bwd", grid=(B, T // tm),
        in_specs=[row(D), row(D), row(D), pl.BlockSpec((None, 3, D), lambda b, i: (b, 0, 0)), _const_spec(w.shape),
                  _const_spec((1, D))],
        out_specs=[row(D), row(E), row(D), pl.BlockSpec((None, 1, D), lambda b, i: (b, 0, 0)),
                   pl.BlockSpec((2, D), lambda b, i: (0, 0))],
        out_shape=[jax.ShapeDtypeStruct((B, T, D), f32), jax.ShapeDtypeStruct((B, T, E), f32),
                   jax.ShapeDtypeStruct((B, T, D), bf16), jax.ShapeDtypeStruct((B, 1, D), f32),
                   jax.ShapeDtypeStruct((2, D), f32)],
        compiler_params=_cp(),
    )(dxn, x, y, mod3, w, lg)


def _halo_rows(K):
    return 8 if K <= 9 else 32


def dwconv_fwd(x, col0, C, w, b, K):
    B, T, _ = x.shape
    tc, hp = COL_TILE, _halo_rows(K)
    tm = _row_tile(T, 1024 if K <= 9 else 512)
    r = tm // hp

    def body(xh_ref, x_ref, w_ref, b_ref, o_ref):
        halo = jnp.where(pl.program_id(2) == 0, 0.0, xh_ref[...])
        xe = jnp.concatenate([halo, x_ref[...]], axis=0)
        acc = jnp.zeros((tm, tc), f32) + b_ref[...]
        for k in range(K):
            sft = K - 1 - k
            xs = xe if sft == 0 else pltpu.roll(xe, sft, 0)
            acc = acc + xs[hp:, :] * w_ref[k:k + 1, :]
        o_ref[...] = acc

    return pl.pallas_call(
        body, name=f"dwconv{K}_fwd", grid=(C // tc, B, T // tm),
        in_specs=[pl.BlockSpec((None, hp, tc), lambda j, b, i: (b, jnp.maximum(i * r - 1, 0), col0 + j)),
                  pl.BlockSpec((None, tm, tc), lambda j, b, i: (b, i, col0 + j)),
                  pl.BlockSpec((w.shape[0], tc), lambda j, b, i: (0, j)),
                  pl.BlockSpec((1, tc), lambda j, b, i: (0, j))],
        out_specs=pl.BlockSpec((None, tm, tc), lambda j, b, i: (b, i, j)),
        out_shape=jax.ShapeDtypeStruct((B, T, C), f32),
        compiler_params=_cp(),
    )(x, x, w, b)


def dwconv_bwd(dc, x, col0, C, w, K, out_dtype):
    B, T, _ = x.shape
    tc, hp = COL_TILE, _halo_rows(K)
    tm = _row_tile(T, 1024 if K <= 9 else 512)
    r = tm // hp
    nt = T // tm
    n = tm + hp
    KP = w.shape[0]

    def body(dcn_ref, dc_ref, x_ref, w_ref, dx_ref, dw_ref, db_ref):
        b, i = pl.program_id(1), pl.program_id(2)

        @pl.when((b == 0) & (i == 0))
        def _():
            dw_ref[...] = jnp.zeros_like(dw_ref)
            db_ref[...] = jnp.zeros_like(db_ref)

        dcv, xv = dc_ref[...], x_ref[...]
        de = jnp.concatenate([dcv, jnp.where(i == nt - 1, 0.0, dcn_ref[...])], axis=0)
        acc = jnp.zeros((tm, tc), f32)
        for k in range(K):
            j = K - 1 - k
            ds = (de if j == 0 else pltpu.roll(de, n - j, 0))[:tm, :]
            acc = acc + ds * w_ref[k:k + 1, :]
            dw_ref[k:k + 1, :] += jnp.sum(ds * xv, axis=0, keepdims=True)
        dx_ref[...] = acc.astype(out_dtype)
        db_ref[...] += jnp.sum(dcv, axis=0, keepdims=True)

    return pl.pallas_call(
        body, name=f"dwconv{K}_bwd", grid=(C // tc, B, nt),
        in_specs=[pl.BlockSpec((None, hp, tc), lambda j, b, i: (b, jnp.minimum((i + 1) * r, T // hp - 1), j)),
                  pl.BlockSpec((None, tm, tc), lambda j, b, i: (b, i, j)),
                  pl.BlockSpec((None, tm, tc), lambda j, b, i: (b, i, col0 + j)),
                  pl.BlockSpec((KP, tc), lambda j, b, i: (0, j))],
        out_specs=[pl.BlockSpec((None, tm, tc), lambda j, b, i: (b, i, j)),
                   pl.BlockSpec((KP, tc), lambda j, b, i: (0, j)),
                   pl.BlockSpec((1, tc), lambda j, b, i: (0, j))],
        out_shape=[jax.ShapeDtypeStruct((B, T, C), out_dtype), jax.ShapeDtypeStruct((KP, C), f32),
                   jax.ShapeDtypeStruct((1, C), f32)],
        compiler_params=_cp(),
    )(dc, dc, x, w)


POOL_HALO = 16


def _pool_windows(ue, pos, hp):
    out = []
    for g, wd in enumerate(POOL_WINDOWS):
        ug = ue[:, g * LANES:(g + 1) * LANES]
        s, span = ug, 1
        while span < wd:
            s = s + pltpu.roll(s, span, 0)
            span *= 2
        cnt = jnp.minimum(pos + 1, wd).astype(f32)
        out.append(s[hp:, :] / cnt - ug[hp:, :])
    return out


def pool_fwd(proj, colb, w, scale, ybuf):
    B, T, _ = proj.shape
    hp = POOL_HALO
    tm = _row_tile(T, 512)
    r = tm // hp

    def body(uh_ref, u_ref, w_ref, sc_ref, ybuf_ref, o_ref):
        i = pl.program_id(1)
        ue = jnp.concatenate([jnp.where(i == 0, 0.0, uh_ref[...]), u_ref[...]], axis=0)
        pos = i * tm + lax.broadcasted_iota(jnp.int32, (tm, 1), 0)
        ps = _pool_windows(ue, pos, hp)
        o = jnp.concatenate([_dot(ps[g], w_ref[g]) for g in range(4)], axis=1) * sc_ref[...]
        o_ref[...] = o.astype(bf16)

    return pl.pallas_call(
        body, name="pool_fwd", grid=(B, T // tm),
        in_specs=[pl.BlockSpec((None, hp, POOL_DIM), lambda b, i: (b, jnp.maximum(i * r - 1, 0), colb)),
                  pl.BlockSpec((None, tm, POOL_DIM), lambda b, i: (b, i, colb)),
                  _const_spec(w.shape), _const_spec((1, POOL_DIM)), pl.BlockSpec(memory_space=pl.ANY)],
        out_specs=pl.BlockSpec((None, tm, POOL_DIM), lambda b, i: (b, i, ybuf.shape[2] // POOL_DIM - 1)),
        out_shape=jax.ShapeDtypeStruct(ybuf.shape, bf16),
        input_output_aliases={4: 0},
        compiler_params=_cp(),
    )(proj, proj, w, scale, ybuf)


def pool_bwd(dycat, dcolb, proj, colb, w, scale):
    B, T, _ = proj.shape
    hp = POOL_HALO
    tm = _row_tile(T, 512)
    r = tm // hp
    nt = T // tm
    n = tm + hp

    def body(dyn_ref, dy_ref, uh_ref, u_ref, w_ref, sc_ref, du_ref, dw_ref, dsc_ref):
        b, i = pl.program_id(0), pl.program_id(1)

        @pl.when((b == 0) & (i == 0))
        def _():
            dw_ref[...] = jnp.zeros_like(dw_ref)
            dsc_ref[...] = jnp.zeros_like(dsc_ref)

        dyv = dy_ref[...]
        dye = jnp.concatenate([dyv, jnp.where(i == nt - 1, 0.0, dyn_ref[...])], axis=0)
        ue = jnp.concatenate([jnp.where(i == 0, 0.0, uh_ref[...]), u_ref[...]], axis=0)
        pos = i * tm + lax.broadcasted_iota(jnp.int32, (tm, 1), 0)
        pos_e = i * tm + lax.broadcasted_iota(jnp.int32, (n, 1), 0)
        ps = _pool_windows(ue, pos, hp)
        dme = dye * sc_ref[...]
        dus, dscs = [], []
        for g, wd in enumerate(POOL_WINDOWS):
            sl = slice(g * LANES, (g + 1) * LANES)
            dscs.append(jnp.sum(dyv[:, sl] * _dot(ps[g], w_ref[g]), axis=0, keepdims=True))
            dw_ref[g] += _dot_tn(ps[g], dme[:tm, sl])
            dpe = _dot_nt(dme[:, sl], w_ref[g])
            s, span = dpe / jnp.minimum(pos_e + 1, wd).astype(f32), 1
            while span < wd:
                s = s + pltpu.roll(s, n - span, 0)
                span *= 2
            dus.append(s[:tm, :] - dpe[:tm, :])
        du_ref[...] = jnp.concatenate(dus, axis=1).astype(bf16)
        dsc_ref[...] += jnp.concatenate(dscs, axis=1)

    return pl.pallas_call(
        body, name="pool_bwd", grid=(B, nt),
        in_specs=[pl.BlockSpec((None, hp, POOL_DIM), lambda b, i: (b, jnp.minimum((i + 1) * r, T // hp - 1), dcolb)),
                  pl.BlockSpec((None, tm, POOL_DIM), lambda b, i: (b, i, dcolb)),
                  pl.BlockSpec((None, hp, POOL_DIM), lambda b, i: (b, jnp.maximum(i * r - 1, 0), colb)),
                  pl.BlockSpec((None, tm, POOL_DIM), lambda b, i: (b, i, colb)),
                  _const_spec(w.shape), _const_spec((1, POOL_DIM))],
        out_specs=[pl.BlockSpec((None, tm, POOL_DIM), lambda b, i: (b, i, 0)),
                   pl.BlockSpec(w.shape, lambda b, i: (0, 0, 0)),
                   pl.BlockSpec((1, POOL_DIM), lambda b, i: (0, 0))],
        out_shape=[jax.ShapeDtypeStruct((B, T, POOL_DIM), bf16), jax.ShapeDtypeStruct(w.shape, f32),
                   jax.ShapeDtypeStruct((1, POOL_DIM), f32)],
        compiler_params=_cp(),
    )(dycat, dycat, proj, proj, w, scale)


N_PAIRS = SSD_HEADS // 2


def _ssd_chunk(xs, bs, cs, dtp, zs, hs, dtb, alog, dsk, ngs):
    Q = SSD_CHUNK
    lane = lax.broadcasted_iota(jnp.int32, (1, LANES), 1)
    sub = lax.broadcasted_iota(jnp.int32, (LANES, 1), 0)
    causal = lax.broadcasted_iota(jnp.int32, (Q, Q), 0) >= lax.broadcasted_iota(jnp.int32, (Q, Q), 1)
    lane_lo, sub_lo = lane < 64, sub < 64

    def col(v, h):
        return jnp.sum(v * (lane == h).astype(f32), axis=1, keepdims=True)

    def row(vt, h):
        return jnp.sum(vt * (sub == h).astype(f32), axis=0, keepdims=True)

    dt = jax.nn.softplus(dtp + dtb)
    acum = _dot_hi(causal.astype(f32), dt * (-jnp.exp(alog)))
    acum_t = acum.T
    aend = jnp.sum(acum * (sub == Q - 1).astype(f32), axis=0, keepdims=True)
    outs, hn = [], []
    for grp in range(2):
        bv, cv = _silu(bs[grp]), _silu(cs[grp])
        gmat = _dot_nt(cv, bv)
        for j in range(4):
            p = grp * 4 + j
            h0, h1 = 2 * p, 2 * p + 1
            x2 = _silu(xs[p])
            c0, c1 = col(acum, h0), col(acum, h1)
            s2 = jnp.where(lane_lo, c0, c1)
            xdt = x2 * jnp.where(lane_lo, col(dt, h0), col(dt, h1))
            l0 = jnp.where(causal, jnp.exp(jnp.minimum(c0 - row(acum_t, h0), 0.0)), 0.0)
            l1 = jnp.where(causal, jnp.exp(jnp.minimum(c1 - row(acum_t, h1), 0.0)), 0.0)
            yd = _dot(gmat * l0, jnp.where(lane_lo, xdt, 0.0)) + _dot(gmat * l1, jnp.where(lane_lo, 0.0, xdt))
            e0, e1 = col(aend, h0), col(aend, h1)
            st = _dot_tn(xdt * jnp.exp(jnp.where(lane_lo, e0, e1) - s2), bv)
            yo = jnp.exp(s2) * _dot_nt(cv, hs[p])
            hn.append(jnp.exp(jnp.where(sub_lo, e0, e1)) * hs[p] + st)
            yv = yd + yo + x2 * jnp.where(lane_lo, col(dsk, h0), col(dsk, h1))
            outs.append(yv * _silu(zs[p]))
    ms = sum(jnp.sum(o * o, axis=1, keepdims=True) for o in outs) / SSD_D_INNER
    rs = lax.rsqrt(ms + NORM_EPS)
    return [outs[p] * rs * ngs[p] for p in range(N_PAIRS)], hn


def _lane_blocks(ref, n, start=0):
    return [ref[:, (start + k) * LANES:(start + k + 1) * LANES] for k in range(n)]


def _ssd_args(z_ref, cx_ref, dt_ref, dtb_ref, alog_ref, dsk_ref, ng_ref):
    xs = _lane_blocks(cx_ref, 8)
    bs = _lane_blocks(cx_ref, 2, 8)
    cs = _lane_blocks(cx_ref, 2, 10)
    zs = _lane_blocks(z_ref, 8)
    ngs = _lane_blocks(ng_ref, 8)
    return xs, bs, cs, dt_ref[...], zs, dtb_ref[...], alog_ref[...], dsk_ref[...], ngs


DT_COLB = (EVEN_IN_PAD - LANES) // LANES


def ssd_fwd(proj, cx, dtb, alog, dsk, ng):
    B, T, _ = proj.shape
    Q = SSD_CHUNK
    nc = T // Q

    def body(z_ref, cx_ref, dt_ref, dtb_ref, alog_ref, dsk_ref, ng_ref, ya_ref, hsave_ref, h_scr):
        @pl.when(pl.program_id(1) == 0)
        def _():
            h_scr[...] = jnp.zeros_like(h_scr)

        xs, bs, cs, dtp, zs, dtb_v, alog_v, dsk_v, ngs = _ssd_args(z_ref, cx_ref, dt_ref, dtb_ref, alog_ref, dsk_ref, ng_ref)
        hs = [h_scr[p] for p in range(N_PAIRS)]
        for p in range(N_PAIRS):
            hsave_ref[p] = hs[p]
        outs, hn = _ssd_chunk(xs, bs, cs, dtp, zs, hs, dtb_v, alog_v, dsk_v, ngs)
        for p in range(N_PAIRS):
            ya_ref[:, p * LANES:(p + 1) * LANES] = outs[p].astype(bf16)
            h_scr[p] = hn[p]

    return pl.pallas_call(
        body, name="ssd_fwd", grid=(B, nc),
        in_specs=[pl.BlockSpec((None, Q, SSD_D_INNER), lambda b, i: (b, i, 0)),
                  pl.BlockSpec((None, Q, SSD_CONV_DIM), lambda b, i: (b, i, 0)),
                  pl.BlockSpec((None, Q, LANES), lambda b, i: (b, i, DT_COLB)),
                  _const_spec((1, LANES)), _const_spec((1, LANES)), _const_spec((1, LANES)),
                  _const_spec((1, SSD_D_INNER))],
        out_specs=[pl.BlockSpec((None, Q, SSD_D_INNER), lambda b, i: (b, i, 0)),
                   pl.BlockSpec((None, None, N_PAIRS, LANES, LANES), lambda b, i: (b, i, 0, 0, 0))],
        out_shape=[jax.ShapeDtypeStruct((B, T, SSD_D_INNER + POOL_DIM), bf16),
                   jax.ShapeDtypeStruct((B, nc, N_PAIRS, LANES, LANES), f32)],
        scratch_shapes=[pltpu.VMEM((N_PAIRS, LANES, LANES), f32)],
        compiler_params=_cp(),
    )(proj, cx, proj, dtb, alog, dsk, ng)


def ssd_bwd(dycat, proj, cx, hsave, dtb, alog, dsk, ng):
    B, T, _ = proj.shape
    Q = SSD_CHUNK
    nc = T // Q

    def body(dya_ref, z_ref, cx_ref, dt_ref, hsave_ref, dtb_ref, alog_ref, dsk_ref, ng_ref,
             dz_ref, dcx_ref, ddt_ref, gsm_ref, gng_ref, dh_scr):
        b, i = pl.program_id(0), pl.program_id(1)

        @pl.when((b == 0) & (i == 0))
        def _():
            gsm_ref[...] = jnp.zeros_like(gsm_ref)
            gng_ref[...] = jnp.zeros_like(gng_ref)

        @pl.when(i == 0)
        def _():
            dh_scr[...] = jnp.zeros_like(dh_scr)

        xs, bs, cs, dtp, zs, dtb_v, alog_v, dsk_v, ngs = _ssd_args(z_ref, cx_ref, dt_ref, dtb_ref, alog_ref, dsk_ref, ng_ref)
        hs = [hsave_ref[p] for p in range(N_PAIRS)]
        _, vjp = jax.vjp(_ssd_chunk, xs, bs, cs, dtp, zs, hs, dtb_v, alog_v, dsk_v, ngs)
        douts = _lane_blocks(dya_ref, 8)
        dhn = [dh_scr[p] for p in range(N_PAIRS)]
        dxs, dbs, dcs, ddtp, dzs, dhs, ddtb, dalog, ddsk, dngs = vjp((douts, dhn))
        for p in range(N_PAIRS):
            dcx_ref[:, p * LANES:(p + 1) * LANES] = dxs[p]
            dz_ref[:, p * LANES:(p + 1) * LANES] = dzs[p].astype(bf16)
            dh_scr[p] = dhs[p]
            gng_ref[:, p * LANES:(p + 1) * LANES] += dngs[p]
        for k in range(2):
            dcx_ref[:, (8 + k) * LANES:(9 + k) * LANES] = dbs[k]
            dcx_ref[:, (10 + k) * LANES:(11 + k) * LANES] = dcs[k]
        ddt_ref[...] = ddtp.astype(bf16)
        gsm_ref[0:1, :] += ddtb
        gsm_ref[1:2, :] += dalog
        gsm_ref[2:3, :] += ddsk

    rev = lambda w, cb=0: pl.BlockSpec((None, Q, w), lambda b, i: (b, nc - 1 - i, cb))
    return pl.pallas_call(
        body, name="ssd_bwd", grid=(B, nc),
        in_specs=[rev(SSD_D_INNER), rev(SSD_D_INNER), rev(SSD_CONV_DIM), rev(LANES, DT_COLB),
                  pl.BlockSpec((None, None, N_PAIRS, LANES, LANES), lambda b, i: (b, nc - 1 - i, 0, 0, 0)),
                  _const_spec((1, LANES)), _const_spec((1, LANES)), _const_spec((1, LANES)),
                  _const_spec((1, SSD_D_INNER))],
        out_specs=[rev(SSD_D_INNER), rev(SSD_CONV_DIM), rev(LANES),
                   pl.BlockSpec((3, LANES), lambda b, i: (0, 0)),
                   pl.BlockSpec((1, SSD_D_INNER), lambda b, i: (0, 0))],
        out_shape=[jax.ShapeDtypeStruct((B, T, SSD_D_INNER), bf16), jax.ShapeDtypeStruct((B, T, SSD_CONV_DIM), f32),
                   jax.ShapeDtypeStruct((B, T, LANES), bf16), jax.ShapeDtypeStruct((3, LANES), f32),
                   jax.ShapeDtypeStruct((1, SSD_D_INNER), f32)],
        scratch_shapes=[pltpu.VMEM((N_PAIRS, LANES, LANES), f32)],
        compiler_params=_cp(),
    )(dycat, proj, cx, proj, hsave, dtb, alog, dsk, ng)


def glu_fwd(proj):
    B, T, _ = proj.shape
    tm = _row_tile(T, 1024)

    def body(v_ref, g_ref, o_ref):
        o_ref[...] = v_ref[...] * jax.nn.sigmoid(g_ref[...])

    blk = lambda cb: pl.BlockSpec((None, tm, CONF_DIM), lambda b, i: (b, i, cb))
    return pl.pallas_call(body, name="glu_fwd", grid=(B, T // tm), in_specs=[blk(0), blk(1)], out_specs=blk(0),
                          out_shape=jax.ShapeDtypeStruct((B, T, CONF_DIM), f32), compiler_params=_cp())(proj, proj)


def glu_bwd(dhh, proj):
    B, T, _ = proj.shape
    tm = _row_tile(T, 1024)

    def body(d_ref, v_ref, g_ref, o_ref):
        sig = jax.nn.sigmoid(g_ref[...])
        dv = d_ref[...]
        o_ref[:, :CONF_DIM] = (dv * sig).astype(bf16)
        o_ref[:, CONF_DIM:] = (dv * v_ref[...] * sig * (1.0 - sig)).astype(bf16)

    blk = lambda cb: pl.BlockSpec((None, tm, CONF_DIM), lambda b, i: (b, i, cb))
    return pl.pallas_call(body, name="glu_bwd", grid=(B, T // tm), in_specs=[blk(0), blk(0), blk(1)],
                          out_specs=pl.BlockSpec((None, tm, 2 * CONF_DIM), lambda b, i: (b, i, 0)),
                          out_shape=jax.ShapeDtypeStruct((B, T, 2 * CONF_DIM), bf16), compiler_params=_cp())(dhh, proj, proj)


def _neg_expm1(x):
    series = x * (1.0 + x * (1.0 / 2.0) * (1.0 + x * (1.0 / 3.0) * (1.0 + x * (1.0 / 4.0) * (1.0 + x * (1.0 / 5.0)))))
    return -jnp.where(x > -0.1, series, jnp.exp(x) - 1.0)


def _lru_gates(cc, xc8, gr8, clg, clb, wa8, wx8, ba8, bx8, lam8):
    xhat, _ = _ln_stats(cc)
    yc = _silu(xhat * clg + clb)
    a8, b8, ge8 = [], [], []
    for hb in range(LRU_HEADS):
        xh = xc8[hb]
        rg = jax.nn.sigmoid(_dot(xh, wa8[hb]) + ba8[hb])
        ig = jax.nn.sigmoid(_dot(xh, wx8[hb]) + bx8[hb])
        log_a = -LRU_C * rg * jax.nn.softplus(-lam8[hb])
        a8.append(jnp.exp(log_a))
        b8.append(jnp.sqrt(_neg_expm1(2.0 * log_a)) * (ig * xh))
        ge8.append(jax.nn.gelu(gr8[hb]))
    return yc, a8, b8, ge8


def _scan_fwd(a, b, h_in):
    tm = a.shape[0]
    rows = lax.broadcasted_iota(jnp.int32, (tm, 1), 0)
    s = 1
    while s < tm:
        keep = rows >= s
        b = a * jnp.where(keep, pltpu.roll(b, s, 0), 0.0) + b
        a = a * jnp.where(keep, pltpu.roll(a, s, 0), 1.0)
        s *= 2
    return a * h_in + b


def _scan_bwd(e, d, g_in):
    tm = e.shape[0]
    rows = lax.broadcasted_iota(jnp.int32, (tm, 1), 0)
    s = 1
    while s < tm:
        keep = rows < tm - s
        d = e * jnp.where(keep, pltpu.roll(d, tm - s, 0), 0.0) + d
        e = e * jnp.where(keep, pltpu.roll(e, tm - s, 0), 1.0)
        s *= 2
    return e * g_in + d


def _lru_params(wa_ref, wx_ref, vec_ref):
    wa8 = [wa_ref[h] for h in range(LRU_HEADS)]
    wx8 = [wx_ref[h] for h in range(LRU_HEADS)]
    ba8 = [vec_ref[0:1, h * LANES:(h + 1) * LANES] for h in range(LRU_HEADS)]
    bx8 = [vec_ref[1:2, h * LANES:(h + 1) * LANES] for h in range(LRU_HEADS)]
    lam8 = [vec_ref[2:3, h * LANES:(h + 1) * LANES] for h in range(LRU_HEADS)]
    return wa8, wx8, ba8, bx8, lam8


GR_COLB = 2


def lru_fwd(cc, xc, proj, cln, wa, wx, vec):
    B, T, _ = xc.shape
    tm = _row_tile(T)

    def body(cc_ref, xc_ref, gr_ref, cln_ref, wa_ref, wx_ref, vec_ref, y_ref, hs_ref, h_scr):
        @pl.when(pl.program_id(1) == 0)
        def _():
            h_scr[...] = jnp.zeros_like(h_scr)

        yc, a8, b8, ge8 = _lru_gates(cc_ref[...], _lane_blocks(xc_ref, 8), _lane_blocks(gr_ref, 8), cln_ref[0:1, :],
                                     cln_ref[1:2, :], *_lru_params(wa_ref, wx_ref, vec_ref))
        h = _scan_fwd(jnp.concatenate(a8, axis=1), jnp.concatenate(b8, axis=1), h_scr[...])
        hs_ref[...] = h
        h_scr[...] = h[tm - 1:tm, :]
        y_ref[:, :CONF_DIM] = yc.astype(bf16)
        y_ref[:, CONF_DIM:] = (h * jnp.concatenate(ge8, axis=1)).astype(bf16)

    row = lambda w, cb=0: pl.BlockSpec((None, tm, w), lambda b, i: (b, i, cb))
    return pl.pallas_call(
        body, name="lru_fwd", grid=(B, T // tm),
        in_specs=[row(CONF_DIM), row(LRU_DIM), row(LRU_DIM, GR_COLB), _const_spec((2, CONF_DIM)),
                  _const_spec(wa.shape), _const_spec(wx.shape), _const_spec((3, LRU_DIM))],
        out_specs=[row(CONF_DIM + LRU_DIM), row(LRU_DIM)],
        out_shape=[jax.ShapeDtypeStruct((B, T, CONF_DIM + LRU_DIM), bf16), jax.ShapeDtypeStruct((B, T, LRU_DIM), f32)],
        scratch_shapes=[pltpu.VMEM((1, LRU_DIM), f32)],
        compiler_params=_cp(),
    )(cc, xc, proj, cln, wa, wx, vec)


def lru_bwd(dycat, cc, xc, proj, hs, cln, wa, wx, vec):
    B, T, _ = xc.shape
    tm = _row_tile(T)
    nt = T // tm
    r = tm // 8

    def body(dy_ref, cc_ref, xc_ref, gr_ref, hs_ref, hsh_ref, cln_ref, wa_ref, wx_ref, vec_ref,
             dcc_ref, dxc_ref, dgr_ref, dcln_ref, dwa_ref, dwx_ref, dvec_ref, g_scr, a_scr):
        b, i = pl.program_id(0), pl.program_id(1)
        it = nt - 1 - i

        @pl.when((b == 0) & (i == 0))
        def _():
            dcln_ref[...] = jnp.zeros_like(dcln_ref)
            dwa_ref[...] = jnp.zeros_like(dwa_ref)
            dwx_ref[...] = jnp.zeros_like(dwx_ref)
            dvec_ref[...] = jnp.zeros_like(dvec_ref)

        @pl.when(i == 0)
        def _():
            g_scr[...] = jnp.zeros_like(g_scr)
            a_scr[...] = jnp.zeros_like(a_scr)

        (yc, a8, b8, ge8), vjp = jax.vjp(_lru_gates, cc_ref[...], _lane_blocks(xc_ref, 8), _lane_blocks(gr_ref, 8),
                                         cln_ref[0:1, :], cln_ref[1:2, :], *_lru_params(wa_ref, wx_ref, vec_ref))
        a = jnp.concatenate(a8, axis=1)
        ge = jnp.concatenate(ge8, axis=1)
        h = hs_ref[...]
        dyd = dy_ref[:, CONF_DIM:]
        rows = lax.broadcasted_iota(jnp.int32, (tm, 1), 0)
        e = jnp.where(rows < tm - 1, pltpu.roll(a, tm - 1, 0), a_scr[...])
        g = _scan_bwd(e, dyd * ge, g_scr[...])
        h_first = jnp.where(it == 0, 0.0, hsh_ref[7:8, :])
        h_prev = jnp.where(rows >= 1, pltpu.roll(h, 1, 0), h_first)
        da = g * h_prev
        g_scr[...] = g[0:1, :]
        a_scr[...] = a[0:1, :]
        split = lambda v: [v[:, k * LANES:(k + 1) * LANES] for k in range(LRU_HEADS)]
        dcc, dxc8, dgr8, dclg, dclb, dwa8, dwx8, dba8, dbx8, dlam8 = vjp((dy_ref[:, :CONF_DIM], split(da), split(g), split(dyd * h)))
        dcc_ref[...] = dcc
        dcln_ref[0:1, :] += dclg
        dcln_ref[1:2, :] += dclb
        for k in range(LRU_HEADS):
            sl = slice(k * LANES, (k + 1) * LANES)
            dxc_ref[:, sl] = dxc8[k]
            dgr_ref[:, sl] = dgr8[k].astype(bf16)
            dwa_ref[k] += dwa8[k]
            dwx_ref[k] += dwx8[k]
            dvec_ref[0:1, sl] += dba8[k]
            dvec_ref[1:2, sl] += dbx8[k]
            dvec_ref[2:3, sl] += dlam8[k]

    rev = lambda w, cb=0: pl.BlockSpec((None, tm, w), lambda b, i: (b, nt - 1 - i, cb))
    acc = lambda shape: pl.BlockSpec(shape, lambda b, i: (0,) * len(shape))
    return pl.pallas_call(
        body, name="lru_bwd", grid=(B, nt),
        in_specs=[rev(CONF_DIM + LRU_DIM), rev(CONF_DIM), rev(LRU_DIM), rev(LRU_DIM, GR_COLB), rev(LRU_DIM),
                  pl.BlockSpec((None, 8, LRU_DIM), lambda b, i: (b, jnp.maximum((nt - 1 - i) * r - 1, 0), 0)),
                  _const_spec((2, CONF_DIM)), _const_spec(wa.shape), _const_spec(wx.shape), _const_spec((3, LRU_DIM))],
        out_specs=[rev(CONF_DIM), rev(LRU_DIM), rev(LRU_DIM), acc((2, CONF_DIM)), acc(wa.shape), acc(wx.shape),
                   acc((3, LRU_DIM))],
        out_shape=[jax.ShapeDtypeStruct((B, T, CONF_DIM), f32), jax.ShapeDtypeStruct((B, T, LRU_DIM), f32),
                   jax.ShapeDtypeStruct((B, T, LRU_DIM), bf16), jax.ShapeDtypeStruct((2, CONF_DIM), f32),
                   jax.ShapeDtypeStruct(wa.shape, f32), jax.ShapeDtypeStruct(wx.shape, f32),
                   jax.ShapeDtypeStruct((3, LRU_DIM), f32)],
        scratch_shapes=[pltpu.VMEM((1, LRU_DIM), f32), pltpu.VMEM((1, LRU_DIM), f32)],
        compiler_params=_cp(),
    )(dycat, cc, xc, proj, hs, hs, cln, wa, wx, vec)


def loss_fwd(y, target):
    B, T, D = y.shape
    tm = _row_tile(T)

    def body(y_ref, t_ref, l_ref, dy_ref):
        @pl.when((pl.program_id(0) == 0) & (pl.program_id(1) == 0))
        def _():
            l_ref[...] = jnp.zeros_like(l_ref)

        d = y_ref[...] - t_ref[...]
        dy_ref[...] = d * (1.0 / D)
        l_ref[...] += jnp.sum(jnp.sum(d * d, axis=1, keepdims=True), axis=0, keepdims=True)

    row = pl.BlockSpec((None, tm, D), lambda b, i: (b, i, 0))
    return pl.pallas_call(
        body, name="loss_fwd", grid=(B, T // tm), in_specs=[row, row],
        out_specs=[pl.BlockSpec((1, 1), lambda b, i: (0, 0)), row],
        out_shape=[jax.ShapeDtypeStruct((1, 1), f32), jax.ShapeDtypeStruct((B, T, D), f32)],
        compiler_params=_cp(),
    )(y, target)


ADA_COL_TILE = 768


def ada_fwd(c_all, w, b):
    L, D, N = w.shape
    nb = c_all.shape[0]
    tn = ADA_COL_TILE

    def body(c_ref, w_ref, b_ref, o_ref):
        o_ref[...] = _dot_hi(_silu(c_ref[...]), w_ref[...]) + b_ref[...]

    return pl.pallas_call(
        body, name="ada_fwd", grid=(L, N // tn),
        in_specs=[pl.BlockSpec((nb, D), lambda l, j: (0, 0)), pl.BlockSpec((None, D, tn), lambda l, j: (l, 0, j)),
                  pl.BlockSpec((None, 1, tn), lambda l, j: (l, 0, j))],
        out_specs=pl.BlockSpec((None, nb, tn), lambda l, j: (l, 0, j)),
        out_shape=jax.ShapeDtypeStruct((L, nb, N), f32),
        compiler_params=_cp(),
    )(c_all, w, b)


def ada_bwd(c_all, dmod_my, dmod_all):
    L, nb, N = dmod_my.shape
    D = c_all.shape[1]
    NA = dmod_all.shape[2]
    tn = ADA_COL_TILE
    nj = N // tn
    ta = NA // nj

    def body(c_ref, dm_ref, da_ref, gw_ref, gb_ref):
        gw_ref[...] = lax.dot_general(_silu(c_ref[...]), dm_ref[...], (((0,), (0,)), ((), ())),
                                      preferred_element_type=f32, precision=lax.Precision.HIGHEST)
        gb_ref[...] = jnp.sum(da_ref[...], axis=0, keepdims=True)

    return pl.pallas_call(
        body, name="ada_bwd", grid=(L, nj),
        in_specs=[pl.BlockSpec((nb, D), lambda l, j: (0, 0)), pl.BlockSpec((None, nb, tn), lambda l, j: (l, 0, j)),
                  pl.BlockSpec((None, nb, ta), lambda l, j: (l, 0, j))],
        out_specs=[pl.BlockSpec((None, D, tn), lambda l, j: (l, 0, j)), pl.BlockSpec((None, 1, ta), lambda l, j: (l, 0, j))],
        out_shape=[jax.ShapeDtypeStruct((L, D, N), f32), jax.ShapeDtypeStruct((L, 1, NA), f32)],
        compiler_params=_cp(),
    )(c_all, dmod_my, dmod_all)


def adamw(w, m, v, gs, offs, tr):
    R, C = w.shape
    ng = len(gs)
    c1 = 1.0 - ADAM_B1 ** ADAM_STEP
    c2 = 1.0 - ADAM_B2 ** ADAM_STEP

    def body(*refs):
        w_ref, m_ref, v_ref = refs[:3]
        g_refs = refs[3:3 + ng]
        d_out, m_out, v_out = refs[3 + ng:]
        g = g_refs[0][...]
        for r in g_refs[1:]:
            g = g + r[...]
        mn = ADAM_B1 * m_ref[...] + (1.0 - ADAM_B1) * g
        vn = ADAM_B2 * v_ref[...] + (1.0 - ADAM_B2) * (g * g)
        m_out[...] = mn
        v_out[...] = vn
        d_out[...] = -ADAM_LR * ((mn / c1) / (jnp.sqrt(vn / c2) + ADAM_EPS) + ADAM_WD * w_ref[...])

    blk = pl.BlockSpec((tr, C), lambda i: (i, 0))
    gspec = lambda off: pl.BlockSpec((tr, C), lambda i: (i + off // tr, 0))
    sds = jax.ShapeDtypeStruct((R, C), f32)
    return pl.pallas_call(
        body, name="adamw", grid=(pl.cdiv(R, tr),), in_specs=[blk, blk, blk] + [gspec(o) for o in offs],
        out_specs=[blk] * 3, out_shape=[sds] * 3, compiler_params=_cp(),
    )(w, m, v, *gs)


def sum_leading(a, tr):
    k, R, C = a.shape

    def body(a_ref, o_ref):
        s = a_ref[0]
        for j in range(1, k):
            s = s + a_ref[j]
        o_ref[...] = s

    return pl.pallas_call(
        body, name="sum_leading", grid=(R // tr,), in_specs=[pl.BlockSpec((k, tr, C), lambda i: (0, i, 0))],
        out_specs=pl.BlockSpec((tr, C), lambda i: (i, 0)), out_shape=jax.ShapeDtypeStruct((R, C), a.dtype),
        compiler_params=_cp(),
    )(a)


ANY = pl.BlockSpec(memory_space=pl.ANY)
CHIP_FLIPS = ((1, 0), (0, 1), (1, 1))
DEV_FLIPS = tuple((fx, fy, fc) for fx in (0, 1) for fy in (0, 1) for fc in (0, 1))[1:]


def _flip(v, f):
    return 1 - v if f else v


def _put(out, v, idx, axis=0):
    return lax.dynamic_update_slice_in_dim(out, jnp.expand_dims(v, axis) if v.ndim < out.ndim else v, idx, axis)


def allgather8(v):
    R, C = v.shape
    n = len(DEV_FLIPS)

    def body(v_ref, o_ref, send_sems, recv_sems):
        x, y, c = lax.axis_index("x"), lax.axis_index("y"), lax.axis_index("c")
        me, sibling = (x, y, c), (x, y, 1 - c)
        chips = [(_flip(x, fx), _flip(y, fy)) for fx, fy in CHIP_FLIPS]

        def copy(k, block, to, own=False):
            px, py, pc = block
            dst = o_ref.at[4 * px + 2 * py + pc]
            return pltpu.make_async_remote_copy(src_ref=v_ref if own else dst, dst_ref=dst, send_sem=send_sems.at[k],
                                                recv_sem=recv_sems.at[k], device_id=to, device_id_type=MESH)

        first = [copy(0, me, sibling, own=True)] + [copy(1 + j, me, (*chip, c), own=True) for j, chip in enumerate(chips)]
        for cp in first:
            cp.start()
        passed = [copy(4 + j, (*chip, c), sibling) for j, chip in enumerate(chips)]
        for j, chip in enumerate(chips):
            copy(1 + j, (*chip, c), me).wait_recv()
            passed[j].start()
        copy(0, sibling, me).wait_recv()
        for j, chip in enumerate(chips):
            copy(4 + j, (*chip, 1 - c), me).wait_recv()
        for cp in first + passed:
            cp.wait_send()

    out = pl.pallas_call(
        body, name="allgather8", in_specs=[ANY], out_specs=ANY, out_shape=jax.ShapeDtypeStruct((N_DEV, R, C), v.dtype),
        scratch_shapes=[pltpu.SemaphoreType.DMA((n,)), pltpu.SemaphoreType.DMA((n,))],
    )(v)
    return _put(out, v, 4 * lax.axis_index("x") + 2 * lax.axis_index("y") + lax.axis_index("c"))


def _half(ref_or_shape0, c):
    hsz = ref_or_shape0 // 2
    return pl.ds(c * hsz, hsz)


def _gather_steps(w_refs, o_refs, send_sems, recv_sems):
    nw, nc = len(w_refs), len(CHIP_FLIPS)
    x, y, c = lax.axis_index("x"), lax.axis_index("y"), lax.axis_index("c")
    me, sibling = 2 * x + y, (x, y, 1 - c)
    peers = [(_flip(x, fx), _flip(y, fy), c) for fx, fy in CHIP_FLIPS]
    slots = [2 * px + py for px, py, _ in peers]

    def copy(a, j, slot, half, to, own=False):
        hs = _half(w_refs[a].shape[0], half)
        return pltpu.make_async_remote_copy(src_ref=w_refs[a].at[hs] if own else o_refs[a].at[slot, hs],
                                            dst_ref=o_refs[a].at[slot, hs], send_sem=send_sems.at[j],
                                            recv_sem=recv_sems.at[j], device_id=to, device_id_type=MESH)

    first = [copy(a, a * nc + k, me, c, peers[k], own=True) for a in range(nw) for k in range(nc)]

    def start():
        for cp in first:
            cp.start()

    def finish():
        passed = []
        for a in range(nw):
            for k in range(nc):
                copy(a, a * nc + k, slots[k], c, peers[k]).wait_recv()
                passed.append(copy(a, nw * nc + a * nc + k, slots[k], c, sibling))
                passed[-1].start()
        for a in range(nw):
            for k in range(nc):
                copy(a, nw * nc + a * nc + k, slots[k], 1 - c, sibling).wait_recv()
        for cp in first + passed:
            cp.wait_send()

    return start, finish


def _gather_sems(nw):
    n = 2 * nw * len(CHIP_FLIPS)
    return [pltpu.SemaphoreType.DMA((n,)), pltpu.SemaphoreType.DMA((n,))]


def _gather_out_shapes(ws):
    return [jax.ShapeDtypeStruct((N_CHIPS,) + w.shape, w.dtype) for w in ws]


def _fill_own(outs, ws):
    if not ws:
        return []
    chip =2 * lax.axis_index("x") + lax.axis_index("y")
    return [_put(o, w, chip) for o, w in zip(outs, ws)]


def gather_weights(ws):
    nw = len(ws)

    def body(*refs):
        start, finish = _gather_steps(refs[:nw], refs[nw:2 * nw], *refs[2 * nw:])
        start()
        finish()

    outs = pl.pallas_call(
        body, name="gather_weights", in_specs=[ANY] * nw, out_specs=[ANY] * nw,
        out_shape=_gather_out_shapes(ws), scratch_shapes=_gather_sems(nw),
    )(*ws)
    return _fill_own(outs, ws)


def swap_halves(gs):
    nw = len(gs)

    def body(*refs):
        g_refs, t_refs = refs[:nw], refs[nw:2 * nw]
        send_sems, recv_sems = refs[2 * nw:]
        x, y, c = lax.axis_index("x"), lax.axis_index("y"), lax.axis_index("c")
        cps = [pltpu.make_async_remote_copy(src_ref=g_refs[a].at[_half(gs[a].shape[0], 1 - c)], dst_ref=t_refs[a],
                                            send_sem=send_sems.at[a], recv_sem=recv_sems.at[a],
                                            device_id=(x, y, 1 - c), device_id_type=MESH) for a in range(nw)]
        for cp in cps:
            cp.start()
        for cp in cps:
            cp.wait()

    return pl.pallas_call(
        body, name="swap_halves", in_specs=[ANY] * nw, out_specs=[ANY] * nw,
        out_shape=[jax.ShapeDtypeStruct((g.shape[0] // 2,) + g.shape[1:], g.dtype) for g in gs],
        scratch_shapes=[pltpu.SemaphoreType.DMA((nw,)), pltpu.SemaphoreType.DMA((nw,))],
    )(*gs)


def exchange_blocks(ps):
    nw, nc = len(ps), len(CHIP_FLIPS)

    def body(*refs):
        p_refs, r_refs = refs[:nw], refs[nw:2 * nw]
        send_sems, recv_sems = refs[2 * nw:]
        x, y, c = lax.axis_index("x"), lax.axis_index("y"), lax.axis_index("c")
        me = 2 * x + y
        peers = [(_flip(x, fx), _flip(y, fy), c) for fx, fy in CHIP_FLIPS]
        slots = [2 * px + py for px, py, _ in peers]

        def copy(a, k, src_slot, dst_slot):
            return pltpu.make_async_remote_copy(src_ref=p_refs[a].at[src_slot], dst_ref=r_refs[a].at[dst_slot],
                                                send_sem=send_sems.at[a * nc + k], recv_sem=recv_sems.at[a * nc + k],
                                                device_id=peers[k], device_id_type=MESH)

        for a in range(nw):
            for k in range(nc):
                copy(a, k, slots[k], me).start()
        for a in range(nw):
            for k in range(nc):
                copy(a, k, me, slots[k]).wait_recv()
        for a in range(nw):
            for k in range(nc):
                copy(a, k, slots[k], me).wait_send()

    outs = pl.pallas_call(
        body, name="exchange_blocks", in_specs=[ANY] * nw, out_specs=[ANY] * nw,
        out_shape=[jax.ShapeDtypeStruct(p.shape, p.dtype) for p in ps],
        scratch_shapes=[pltpu.SemaphoreType.DMA((nw * nc,)), pltpu.SemaphoreType.DMA((nw * nc,))],
    )(*ps)
    chip = 2 * lax.axis_index("x") + lax.axis_index("y")
    return [_put(o, lax.dynamic_slice_in_dim(p, chip, 1, axis=0), chip) for o, p in zip(outs, ps)]


def join_halves(ss):
    nw = len(ss)

    def body(*refs):
        s_refs, o_refs = refs[:nw], refs[nw:2 * nw]
        send_sems, recv_sems = refs[2 * nw:]
        x, y, c = lax.axis_index("x"), lax.axis_index("y"), lax.axis_index("c")

        def copy(a, half):
            hs = _half(2 * ss[a].shape[0], half)
            return pltpu.make_async_remote_copy(src_ref=s_refs[a], dst_ref=o_refs[a].at[hs], send_sem=send_sems.at[a],
                                                recv_sem=recv_sems.at[a], device_id=(x, y, 1 - c), device_id_type=MESH)

        for a in range(nw):
            copy(a, c).start()
        for a in range(nw):
            copy(a, 1 - c).wait_recv()
        for a in range(nw):
            copy(a, c).wait_send()

    outs = pl.pallas_call(
        body, name="join_halves", in_specs=[ANY] * nw, out_specs=[ANY] * nw,
        out_shape=[jax.ShapeDtypeStruct((2 * s.shape[0],) + s.shape[1:], s.dtype) for s in ss],
        scratch_shapes=[pltpu.SemaphoreType.DMA((nw,)), pltpu.SemaphoreType.DMA((nw,))],
    )(*ss)
    c = lax.axis_index("c")
    return [_put(o, s, c * s.shape[0]) for o, s in zip(outs, ss)]


def _tile_rows(a, b, itemsize=4, budget=4 * 2 ** 20):
    best = 8
    for t in range(8, a + 1, 8):
        if a % t == 0 and t * b * itemsize <= budget:
            best = t
    return best


def add_half(g, t, cidx):
    def body(c_ref, g_ref, t_ref, o_ref):
        o_ref[...] = (g_ref[...] + t_ref[...]).astype(bf16)

    if g.ndim == 5:
        n0, F, _, A, B = g.shape
        hsz, ta = n0 // 2, _tile_rows(A, B)
        nta = A // ta
        grid = (hsz, N_CHIPS, F, nta)
        in_specs = [pl.BlockSpec((None, None, None, ta, B), lambda h, k, f, i, c_ref: (c_ref[0] * hsz + h, f, k, i, 0)),
                    pl.BlockSpec((None, None, None, ta, B), lambda h, k, f, i, c_ref: (h, f, k, i, 0))]
        out_spec = pl.BlockSpec((None, None, ta, B), lambda h, k, f, i, c_ref: (k, h, f * nta + i, 0))
    else:
        n0, _, A, B = g.shape
        F, hsz, ta = 1, n0 // 2, _tile_rows(A, B)
        grid = (hsz, N_CHIPS, A // ta)
        in_specs = [pl.BlockSpec((None, None, ta, B), lambda h, k, i, c_ref: (c_ref[0] * hsz + h, k, i, 0)),
                    pl.BlockSpec((None, None, ta, B), lambda h, k, i, c_ref: (h, k, i, 0))]
        out_spec = pl.BlockSpec((None, None, ta, B), lambda h, k, i, c_ref: (k, h, i, 0))
    return pl.pallas_call(
        body, name="add_half",
        grid_spec=pltpu.PrefetchScalarGridSpec(num_scalar_prefetch=1, grid=grid, in_specs=in_specs, out_specs=out_spec),
        out_shape=jax.ShapeDtypeStruct((N_CHIPS, hsz, F * A, B), bf16),
        compiler_params=_cp(),
    )(cidx, g, t)


def sum_chips(r):
    _, h, A, B = r.shape
    ta = _tile_rows(A, B)

    def body(r_ref, o_ref):
        s = r_ref[0].astype(f32)
        for j in range(1, N_CHIPS):
            s = s + r_ref[j].astype(f32)
        o_ref[...] = s

    return pl.pallas_call(
        body, name="sum_chips", grid=(h, A // ta),
        in_specs=[pl.BlockSpec((N_CHIPS, None, ta, B), lambda hh, i: (0, hh, i, 0))],
        out_specs=pl.BlockSpec((None, ta, B), lambda hh, i: (hh, i, 0)),
        out_shape=jax.ShapeDtypeStruct((h, A, B), f32),
        compiler_params=_cp(),
    )(r)


WEIGHTS = ('ada_w', 'ada_b', 'ln_g', 'ln_b', 'ffn_w_in', 'ffn_w_out', 'ev_w_in', 'ssd_conv_w', 'ssd_conv_b',
           'ssd_dt_bias', 'ssd_a_log', 'ssd_d', 'ssd_norm_g', 'pool_w', 'pool_scale', 'ev_w_out', 'od_w_in',
           'conf_dw_w', 'conf_dw_b', 'conf_ln_g', 'conf_ln_b', 'lru_conv_w', 'lru_conv_b', 'lru_wa', 'lru_ba',
           'lru_wx', 'lru_bx', 'lru_lambda', 'od_w_out')
BIG =('ffn_w_in', 'ffn_w_out', 'ev_w_out', 'od_w_in', 'od_w_out', 'ev_w_in')
SMALL_SHARDED = ('ln_g', 'ln_b', 'ssd_conv_w', 'conf_dw_w', 'conf_dw_b', 'conf_ln_g', 'conf_ln_b', 'lru_conv_w',
                 'lru_conv_b', 'lru_ba', 'lru_bx', 'lru_lambda')
SMALL_REPLICATED = ('ssd_conv_b', 'ssd_dt_bias', 'ssd_a_log', 'ssd_d', 'ssd_norm_g', 'pool_w', 'pool_scale',
                    'lru_wa', 'lru_wx')
PACK_COLS = 1024
BIG_ROW_TILE = 256


def _pack(arrs, row_mult):
    flat = jnp.concatenate([a.reshape(-1) for a in arrs])
    rows = -(-flat.shape[0] // (PACK_COLS * row_mult)) * row_mult
    return jnp.pad(flat, (0, rows * PACK_COLS - flat.shape[0])).reshape(rows, PACK_COLS)


def _unpack(flat, shapes):
    out, off = [], 0
    for s in shapes:
        n = 1
        for d in s:
            n *= d
        out.append(flat[off:off + n].reshape(s))
        off += n
    return out


def _unshard_last(g4):
    m = jnp.moveaxis(g4, 0, -2)
    return m.reshape(m.shape[:-2] + (m.shape[-2] * m.shape[-1],))


def _pad_rows(a, rows):
    return jnp.pad(a, ((0, rows - a.shape[0]),) + ((0, 0),) * (a.ndim - 1))


def _pad_lanes(a):
    return jnp.pad(a, ((0, 0), (0, LANES - a.shape[1])))


def kernel(x, c, ada_w, ada_b, ln_g, ln_b, ffn_w_in, ffn_w_out, ev_w_in, ssd_conv_w, ssd_conv_b, ssd_dt_bias, ssd_a_log, ssd_d, ssd_norm_g, pool_w, pool_scale, ev_w_out, od_w_in, conf_dw_w, conf_dw_b, conf_ln_g, conf_ln_b, lru_conv_w, lru_conv_b, lru_wa, lru_ba, lru_wx, lru_bx, lru_lambda, od_w_out, loss_target, m_ada_w, m_ada_b, m_ln_g, m_ln_b, m_ffn_w_in, m_ffn_w_out, m_ev_w_in, m_ssd_conv_w, m_ssd_conv_b, m_ssd_dt_bias, m_ssd_a_log, m_ssd_d, m_ssd_norm_g, m_pool_w, m_pool_scale, m_ev_w_out, m_od_w_in, m_conf_dw_w, m_conf_dw_b, m_conf_ln_g, m_conf_ln_b, m_lru_conv_w, m_lru_conv_b, m_lru_wa, m_lru_ba, m_lru_wx, m_lru_bx, m_lru_lambda, m_od_w_out, v_ada_w, v_ada_b, v_ln_g, v_ln_b, v_ffn_w_in, v_ffn_w_out, v_ev_w_in, v_ssd_conv_w, v_ssd_conv_b, v_ssd_dt_bias, v_ssd_a_log, v_ssd_d, v_ssd_norm_g, v_pool_w, v_pool_scale, v_ev_w_out, v_od_w_in, v_conf_dw_w, v_conf_dw_b, v_conf_ln_g, v_conf_ln_b, v_lru_conv_w, v_lru_conv_b, v_lru_wa, v_lru_ba, v_lru_wx, v_lru_bx, v_lru_lambda, v_od_w_out):
    given = dict(locals())
    W = {n: given[n] for n in WEIGHTS}
    M = {n: given["m_" + n] for n in WEIGHTS}
    V = {n: given["v_" + n] for n in WEIGHTS}
    B, T, D = x.shape
    L = DEPTH
    chip = 2 * lax.axis_index("x") + lax.axis_index("y")
    dev = 2 * chip + lax.axis_index("c")

    g1 = allgather8(_pack([c] + [W[n] for n in SMALL_SHARDED], 8)).reshape(N_DEV, -1)
    c_all = g1[:, :B * D].reshape(N_DEV * B, D)
    per_chip = g1[0::2, B * D:]
    full = dict(zip(SMALL_SHARDED, [_unshard_last(jnp.stack(p)) for p in zip(*[
        _unpack(per_chip[k], [W[n].shape for n in SMALL_SHARDED]) for k in range(N_CHIPS)])]))
    for n in SMALL_REPLICATED:
        full[n] = W[n]

    n_ada = ada_w.shape[2]
    ada_b_cols = lax.dynamic_slice_in_dim(ada_b, chip * n_ada, n_ada, axis=1)[:, None, :]
    mod_cols = ada_fwd(c_all, ada_w, ada_b_cols)
    g2 = allgather8(mod_cols.reshape(-1, PACK_COLS))[0::2].reshape(N_CHIPS, L, N_DEV * B, n_ada)
    mod_all = jnp.moveaxis(g2, 0, 2).reshape(L, N_DEV * B, N_CHIPS * n_ada)
    mod = lax.dynamic_slice_in_dim(mod_all, dev * B, B, axis=1).reshape(L, B, N_MOD, D)

    FS = ffn_w_in.shape[3]

    def ffn_shards(l, f):
        return [ffn_w_in[l, f].astype(bf16), ffn_w_out[l, f].astype(bf16)]

    def mix_shards(l):
        w_i, w_o = (ev_w_in, ev_w_out) if l % 2 == 0 else (od_w_in, od_w_out)
        return [w_i[l // 2].astype(bf16), w_o[l // 2].astype(bf16)]

    def ffn_weights(g):
        return g[0], g[1].reshape(-1, D)

    def mix_weights(l, g):
        if l % 2 == 0:
            w = _unshard_last(g[0])
            w_i = jnp.concatenate([w[:, :SSD_ZX], w[:, SSD_ZX + SSD_HEADS:], w[:, SSD_ZX:SSD_ZX + SSD_HEADS],
                                   jnp.zeros((D, EVEN_IN_PAD - EVEN_IN), bf16)], axis=1)
        else:
            w_i = jnp.moveaxis(g[0], 0, 1).reshape(D, -1)
        return w_i, g[1].reshape(-1, D)

    saved = []
    xs = x
    next_ffn = gather_weights(ffn_shards(0, 0))
    next_mix = None
    for l in range(L):
        lg, lb = full['ln_g'][l], full['ln_b'][l]
        rec = {}
        m1, m2, m3 = mod[l][:, 0:3], mod[l][:, 3:6], mod[l][:, 6:9]
        w_in_a, w_out_a = ffn_weights(next_ffn)
        carry = ffn_shards(l, 1) + (mix_shards(l) if l == 0 else [])
        xn, h, gu, a, y, got = ffn_fwd(xs, m1, w_in_a, w_out_a, lg[0:1], lb[0:1], carry)
        w_in_b, w_out_b = ffn_weights(got[:2])
        w_in_m, w_out_m = mix_weights(l, got[2:] if l == 0 else next_mix)
        rec['ffa'] = (xs, h, gu, a, y, m1, w_in_a, w_out_a, lg[0:1])
        xs = xn
        if l % 2 == 0:
            e = l // 2
            cw = _pad_rows(full['ssd_conv_w'][e], 8)
            cb = full['ssd_conv_b'][e][None]
            dtb, alog, dsk = (_pad_lanes(full[n][e][None]) for n in ('ssd_dt_bias', 'ssd_a_log', 'ssd_d'))
            ng, pw, ps = full['ssd_norm_g'][e][None], full['pool_w'][e], full['pool_scale'][e][None]
            proj, hm = inproj_fwd(xs, m2, w_in_m)
            cx = dwconv_fwd(proj, XBC_COLB, SSD_CONV_DIM, cw, cb, SSD_CONV)
            ya, hsave = ssd_fwd(proj, cx, dtb, alog, dsk, ng)
            ycat = pool_fwd(proj, POOL_COLB, pw, ps, ya)
            rec['mix'] = (proj, cx, hsave, cw, dtb, alog, dsk, ng, pw, ps)
        else:
            o = l // 2
            dww =_pad_rows(full['conf_dw_w'][o], 32)
            dwb = full['conf_dw_b'][o][None]
            cw = _pad_rows(full['lru_conv_w'][o], 8)
            cb = full['lru_conv_b'][o][None]
            cln = jnp.stack([full['conf_ln_g'][o], full['conf_ln_b'][o]])
            vec = jnp.stack([full['lru_ba'][o], full['lru_bx'][o], full['lru_lambda'][o]])
            wa, wx = full['lru_wa'][o], full['lru_wx'][o]
            proj, hm = inproj_fwd(xs, m2, w_in_m)
            hh = glu_fwd(proj)
            cc = dwconv_fwd(hh, 0, CONF_DIM, dww, dwb, CONF_KERNEL)
            xc = dwconv_fwd(proj, XR_COLB, LRU_DIM, cw, cb, LRU_CONV)
            ycat, hst = lru_fwd(cc, xc, proj, cln, wa, wx, vec)
            rec['mix'] = (proj, hh, cc, xc, hst, dww, cw, cln, wa, wx, vec)
        xn, ym = outproj_fwd(ycat, w_out_m, xs, m2, lg[1:2], lb[1:2])
        rec['mixio'] = (xs, hm, ycat, ym, m2, w_in_m, w_out_m, lg[1:2])
        xs = xn
        carry = ffn_shards(l + 1, 0) + mix_shards(l + 1) if l + 1 < L else []
        xn, h, gu, a, y, got = ffn_fwd(xs, m3, w_in_b, w_out_b, lg[2:3], lb[2:3], carry)
        next_ffn, next_mix = got[:2], got[2:]
        rec['ffb'] = (xs, h, gu, a, y, m3, w_in_b, w_out_b, lg[2:3])
        xs = xn
        saved.append(rec)

    sq, dxs = loss_fwd(xs, loss_target)
    loss = lax.psum(sq[0, 0], ("x", "y", "c")) * (0.5 / D)

    gpart = {n: [None] * W[n].shape[0] for n in WEIGHTS}
    for n in ('ffn_w_in', 'ffn_w_out', 'ev_w_out', 'od_w_in', 'od_w_out'):
        gpart[n] = None
    gpart['ln_g'] = [[None] * 3 for _ in range(L)]
    gpart['ln_b'] = [[None] * 3 for _ in range(L)]
    dmod = [None] * L

    def ffn_back(dxn, rec, l, f):
        xin, h, gu, a, y, m3_, w_in_, w_out_, lg_ = rec
        dx, dgu, dy, dm3, dln = ffn_bwd(dxn, xin, y, gu, m3_, w_in_, w_out_, lg_)
        gpart['ffn_w_in'] = wgrad_into(h, dgu, FS, (L, 2), (l, f), gpart['ffn_w_in'])
        gpart['ffn_w_out'] = wgrad_into(a, dy, D, (L, 2), (l, f), gpart['ffn_w_out'])
        gpart['ln_g'][l][2 * f] = dln[0]
        gpart['ln_b'][l][2 * f] = dln[1]
        return dx, dm3

    for l in reversed(range(L)):
        rec = saved[l]
        dxs, dm3 = ffn_back(dxs, rec['ffb'], l, 1)
        xin, hm, ycat, ym, m2, w_in_m, w_out_m, lg_ = rec['mixio']
        dxp, dycat, dy, dg2, dln = outproj_bwd(dxs, xin, ym, m2, w_out_m, lg_)
        gpart['ln_g'][l][1] = dln[0]
        gpart['ln_b'][l][1] = dln[1]
        n_out = 'ev_w_out' if l % 2 == 0 else 'od_w_out'
        gpart[n_out] = wgrad_into(ycat, dy, D, (L // 2,), (l // 2,), gpart[n_out])
        if l % 2 == 0:
            e = l // 2
            proj, cx, hsave, cw, dtb, alog, dsk, ng, pw, ps = rec['mix']
            dz, dcx, ddt, gsm, gng = ssd_bwd(dycat, proj, cx, hsave, dtb, alog, dsk, ng)
            dxbc, dcw, dcb = dwconv_bwd(dcx, proj, XBC_COLB, SSD_CONV_DIM, cw, SSD_CONV, bf16)
            du, dpw, dps = pool_bwd(dycat, POOL_DY_COLB, proj, POOL_COLB, pw, ps)
            dproj = jnp.concatenate([dz, dxbc, du, ddt], axis=-1)
            gwp = wgrad(hm, dproj, EVEN_IN_PAD)[0]
            dt0 = SSD_ZX + POOL_DIM
            gw = jnp.concatenate([gwp[:, :SSD_ZX], gwp[:, dt0:dt0 + SSD_HEADS], gwp[:, SSD_ZX:dt0]], axis=1)
            gpart['ev_w_in'][e] = jnp.moveaxis(gw.reshape(D, N_CHIPS, -1), 1, 0)
            gpart['ssd_conv_w'][e], gpart['ssd_conv_b'][e] = dcw[:SSD_CONV], dcb[0]
            gpart['ssd_dt_bias'][e], gpart['ssd_a_log'][e], gpart['ssd_d'][e] = (gsm[k, :SSD_HEADS] for k in range(3))
            gpart['ssd_norm_g'][e], gpart['pool_w'][e], gpart['pool_scale'][e] = gng[0], dpw, dps[0]
        else:
            o = l // 2
            proj, hh, cc, xc, hst, dww, cw, cln, wa, wx, vec = rec['mix']
            dcc, dxc, dgr, dcln, dwa, dwx, dvec = lru_bwd(dycat, cc, xc, proj, hst, cln, wa, wx, vec)
            dhh, ddw, ddb = dwconv_bwd(dcc, hh, 0, CONF_DIM, dww, CONF_KERNEL, f32)
            dvg = glu_bwd(dhh, proj)
            dxr, dcw, dcb = dwconv_bwd(dxc, proj, XR_COLB, LRU_DIM, cw, LRU_CONV, bf16)
            dproj = jnp.concatenate([dvg, dxr, dgr], axis=-1)
            gpart['od_w_in'] = wgrad_into(hm, dproj, dproj.shape[-1] // N_CHIPS, (L // 2,), (o,), gpart['od_w_in'])
            gpart['conf_dw_w'][o], gpart['conf_dw_b'][o] = ddw[:CONF_KERNEL], ddb[0]
            gpart['conf_ln_g'][o], gpart['conf_ln_b'][o] = dcln[0], dcln[1]
            gpart['lru_conv_w'][o], gpart['lru_conv_b'][o] = dcw[:LRU_CONV], dcb[0]
            gpart['lru_wa'][o], gpart['lru_wx'][o] = dwa, dwx
            gpart['lru_ba'][o], gpart['lru_bx'][o], gpart['lru_lambda'][o] = dvec[0], dvec[1], dvec[2]
        dxs, dm2 = inproj_bwd(dproj, w_in_m, xin, m2, dxp)
        dxs, dm1 = ffn_back(dxs, rec['ffa'], l, 0)
        dmod[l] = jnp.concatenate([dm1, dm2, dg2, dm3], axis=1)
    grad_x = dxs

    def stack(v):
        return jnp.stack([stack(u) if isinstance(u, list) else u for u in v])

    def per_chip(n):
        if isinstance(gpart[n], list):
            return jnp.stack(gpart[n])
        lead = W[n].ndim - 2
        return gpart[n].reshape(W[n].shape[:lead] + (N_CHIPS,) + W[n].shape[lead:])

    gfull = [per_chip(n) for n in BIG]
    cidx = lax.axis_index("c").astype(jnp.int32).reshape(1)
    summed = [add_half(g, t, cidx) for g, t in zip(gfull, swap_halves(gfull))]
    reduced = join_halves([sum_chips(r) for r in exchange_blocks(summed)])

    out_g, out_d, out_m, out_v = {}, {}, {}, {}
    for n, g in zip(BIG, reduced):
        shp = W[n].shape
        as2d = lambda a: a.reshape(-1, shp[-1])
        res = adamw(as2d(W[n]), as2d(M[n]), as2d(V[n]), [as2d(g)], [0], BIG_ROW_TILE)
        out_g[n] = g.reshape(shp)
        out_d[n], out_m[n], out_v[n] = (r.reshape(shp) for r in res)

    small = SMALL_SHARDED + SMALL_REPLICATED
    dmod_flat = stack(dmod).reshape(L, B, N_MOD * D)
    g3 = allgather8(_pack([dmod_flat] + [stack(gpart[n]) for n in small], 64))
    n_dmod = L * B * N_MOD * D
    dmod_all = jnp.moveaxis(g3.reshape(N_DEV, -1)[:, :n_dmod].reshape(N_DEV, L, B, N_MOD * D), 0, 1).reshape(L, N_DEV * B, N_MOD * D)
    ssum = sum_leading(g3, 64).reshape(-1)[n_dmod:]
    gsmall = dict(zip(small, _unpack(ssum, [full[n].shape for n in small])))
    for n in SMALL_SHARDED:
        wdt = W[n].shape[-1]
        gsmall[n] = lax.dynamic_slice_in_dim(gsmall[n], chip * wdt, wdt, axis=gsmall[n].ndim - 1)
    dmod_my = lax.dynamic_slice_in_dim(dmod_all, chip * n_ada, n_ada, axis=2)
    g_ada_w, g_ada_b = ada_bwd(c_all, dmod_my, dmod_all)
    gsmall['ada_b'] = g_ada_b[:, 0, :]

    res = adamw(ada_w.reshape(-1, n_ada), M['ada_w'].reshape(-1, n_ada), V['ada_w'].reshape(-1, n_ada),
                [g_ada_w.reshape(-1, n_ada)], [0], BIG_ROW_TILE)
    out_g['ada_w'] = g_ada_w
    out_d['ada_w'], out_m['ada_w'], out_v['ada_w'] = (r.reshape(ada_w.shape) for r in res)

    names = ('ada_b',) + small
    shapes = [W[n].shape for n in names]
    res = adamw(_pack([W[n] for n in names], 64), _pack([M[n] for n in names], 64), _pack([V[n] for n in names], 64),
                [_pack([gsmall[n] for n in names], 64)], [0], 64)
    out_g.update({n: gsmall[n] for n in names})
    for dst, r in zip((out_d, out_m, out_v), res):
        dst.update(zip(names, _unpack(r.reshape(-1), shapes)))

    return (loss, grad_x, *[out_g[n] for n in WEIGHTS], *[out_d[n] for n in WEIGHTS], *[out_m[n] for n in WEIGHTS],
            *[out_v[n] for n in WEIGHTS])
```

```python
import jax
import jax.numpy as jnp
from jax import lax
from jax.experimental import pallas as pl
from jax.experimental.pallas import tpu as pltpu

f32 = jnp.float32
bf16 = jnp.bfloat16

DEPTH = 4
D_MODEL = 1024
N_MOD = 9
DN_ALPHA = (2.0 * DEPTH) ** 0.25
NORM_EPS = 1e-5
SSD_CHUNK = 128
SSD_D_INNER = 1024
SSD_CONV_DIM = 1536
SSD_HEADS = 16
POOL_WINDOWS = (2, 4, 8, 16)
POOL_DIM = 512
EVEN_IN = 3088
EVEN_IN_PAD = 3200
CONF_DIM = 512
CONF_KERNEL = 31
LRU_DIM = 1024
LRU_HEADS = 8
LRU_CONV = 4
SSD_CONV = 4
LRU_C = 8.0
ADAM_LR = 0.001
ADAM_B1 = 0.9
ADAM_B2 = 0.999
ADAM_EPS = 1e-08
ADAM_WD = 0.01
ADAM_STEP = 10

SSD_ZX = SSD_D_INNER + SSD_CONV_DIM
LANES = 128
VMEM_LIMIT_BYTES = 56 * 2 ** 20
COL_TILE = 512
XBC_COLB = SSD_D_INNER // COL_TILE
XR_COLB = 2 * CONF_DIM // COL_TILE
POOL_COLB = SSD_ZX // POOL_DIM
POOL_DY_COLB = SSD_D_INNER // POOL_DIM
N_CHIPS = 4
N_DEV = 8
MESH = pl.DeviceIdType.MESH


def _cp():
    return pltpu.CompilerParams(vmem_limit_bytes=VMEM_LIMIT_BYTES)


def _dot(a, b):
    return jnp.dot(a, b, preferred_element_type=f32)


def _dot_nt(a, b):
    return lax.dot_general(a, b, (((1,), (1,)), ((), ())), preferred_element_type=f32)


def _dot_tn(a, b):
    return lax.dot_general(a, b, (((0,), (0,)), ((), ())), preferred_element_type=f32)


def _dot_hi(a, b):
    return jnp.dot(a, b, preferred_element_type=f32, precision=lax.Precision.HIGHEST)


def _silu(x):
    return x * jax.nn.sigmoid(x)


def _ln_stats(z):
    mu = jnp.mean(z, axis=-1, keepdims=True)
    zc = z - mu
    var = jnp.mean(zc * zc, axis=-1, keepdims=True)
    rstd = lax.rsqrt(var + NORM_EPS)
    return zc * rstd, rstd


def _ln_bwd(dxn, xhat, rstd, lg):
    dxh = dxn * lg
    return rstd * (dxh - jnp.mean(dxh, axis=-1, keepdims=True) - xhat * jnp.mean(dxh * xhat, axis=-1, keepdims=True))


def _const_spec(shape):
    nd = len(shape)
    return pl.BlockSpec(shape, lambda *_: (0,) * nd, pipeline_mode=pl.Buffered(1))


def _row_tile(t, want=256):
    return min(want, t)


def ffn_fwd(x, mod3, w_in, w_out, lg, lb, carry=()):
    B, T, D = x.shape
    FS = w_in.shape[2]
    tm = _row_tile(T, 512)
    nt = T // tm
    nw = len(carry)

    def body(x_ref, mod_ref, win_ref, wout_ref, lg_ref, lb_ref, *rest):
        xn_ref, h_ref, gu_ref, a_ref, y_ref = rest[nw:nw + 5]
        if nw:
            start, forward, finish = _gather_steps(rest[:nw], rest[nw + 5:2 * nw + 5], *rest[2 * nw + 5:])
            b, i = pl.program_id(0), pl.program_id(1)
            pl.when((b == 0) & (i == 0))(start)
        xv = x_ref[...]
        sh, sc, g = mod_ref[0:1, :], mod_ref[1:2, :], mod_ref[2:3, :]
        h = (xv * (1.0 + sc) + sh).astype(bf16)
        h_ref[...] = h
        acc = jnp.zeros((tm, D), f32)
        for s in range(2):
            gate = _dot(h, win_ref[s])
            up = _dot(h, win_ref[s + 2])
            gu_ref[:, s * FS:(s + 1) * FS] = gate.astype(bf16)
            gu_ref[:, (s + 2) * FS:(s + 3) * FS] = up.astype(bf16)
            a = (_silu(gate) * up).astype(bf16)
            a_ref[:, s * FS:(s + 1) * FS] = a
            acc = acc + _dot(a, wout_ref[s * FS:(s + 1) * FS, :])
        y_ref[...] = acc
        xhat, _ = _ln_stats(DN_ALPHA * xv + 0.5 * (1.0 + g) * acc)
        xn_ref[...] = xhat * lg_ref[...] + lb_ref[...]
        if nw:
            pl.when(b * nt + i == (3 * B * nt) // 4)(forward)
            pl.when((b == B - 1) & (i == nt - 1))(finish)

    row = lambda w: pl.BlockSpec((None, tm, w), lambda b, i: (b, i, 0))
    res = pl.pallas_call(
        body, name="ffn_fwd_gather" if nw else "ffn_fwd", grid=(B, nt),
        in_specs=[row(D), pl.BlockSpec((None, 3, D), lambda b, i: (b, 0, 0)), _const_spec(w_in.shape),
                  _const_spec(w_out.shape), _const_spec((1, D)), _const_spec((1, D))] + [ANY] * nw,
        out_specs=[row(D), row(D), row(4 * FS), row(2 * FS), row(D)] + [ANY] * nw,
        out_shape=[jax.ShapeDtypeStruct((B, T, D), f32), jax.ShapeDtypeStruct((B, T, D), bf16),
                   jax.ShapeDtypeStruct((B, T, 4 * FS), bf16), jax.ShapeDtypeStruct((B, T, 2 * FS), bf16),
                   jax.ShapeDtypeStruct((B, T, D), f32)] + _gather_out_shapes(carry),
        scratch_shapes=_gather_sems(nw) if nw else [],
        compiler_params=_cp(),
    )(x, mod3, w_in, w_out, lg, lb, *carry)
    return tuple(res[:5]) + (_fill_own(res[5:], carry),)


def ffn_bwd(dxn, x, y, gu, mod3, w_in, w_out, lg):
    B, T, D = x.shape
    FS = w_in.shape[2]
    tm = _row_tile(T)

    def body(dxn_ref, x_ref, y_ref, gu_ref, mod_ref, win_ref, wout_ref, lg_ref,
             dx_ref, dgu_ref, dy_ref, dmod_ref, dln_ref):
        b, i = pl.program_id(0), pl.program_id(1)

        @pl.when((b == 0) & (i == 0))
        def _():
            dln_ref[...] = jnp.zeros_like(dln_ref)

        @pl.when(i == 0)
        def _():
            dmod_ref[...] = jnp.zeros_like(dmod_ref)

        xv, yv, dxn_v = x_ref[...], y_ref[...], dxn_ref[...]
        sc, g = mod_ref[1:2, :], mod_ref[2:3, :]
        xhat, rstd = _ln_stats(DN_ALPHA * xv + 0.5 * (1.0 + g) * yv)
        dln_ref[0:1, :] += jnp.sum(dxn_v * xhat, axis=0, keepdims=True)
        dln_ref[1:2, :] += jnp.sum(dxn_v, axis=0, keepdims=True)
        dz = _ln_bwd(dxn_v, xhat, rstd, lg_ref[...])
        dmod_ref[2:3, :] += jnp.sum(0.5 * dz * yv, axis=0, keepdims=True)
        dy = (0.5 * (1.0 + g) * dz).astype(bf16)
        dy_ref[...] = dy
        dh = jnp.zeros((tm, D), f32)
        for s in range(2):
            da = _dot_nt(dy, wout_ref[s * FS:(s + 1) * FS, :]).astype(bf16)
            gate = gu_ref[:, s * FS:(s + 1) * FS]
            up = gu_ref[:, (s + 2) * FS:(s + 3) * FS]
            sig = jax.nn.sigmoid(gate)
            dgate = da * up * (sig * (1.0 + gate * (1.0 - sig)))
            dup = da * (gate * sig)
            dgu_ref[:, s * FS:(s + 1) * FS] = dgate
            dgu_ref[:, (s + 2) * FS:(s + 3) * FS] = dup
            dh = dh + _dot_nt(dgate, win_ref[s]) + _dot_nt(dup, win_ref[s + 2])
        dx_ref[...] = DN_ALPHA * dz + dh * (1.0 + sc)
        dmod_ref[0:1, :] += jnp.sum(dh, axis=0, keepdims=True)
        dmod_ref[1:2, :] += jnp.sum(dh * xv, axis=0, keepdims=True)

    row = lambda w: pl.BlockSpec((None, tm, w), lambda b, i: (b, i, 0))
    return pl.pallas_call(
        body, name="ffn_bwd", grid=(B, T // tm),
        in_specs=[row(D), row(D), row(D), row(4 * FS), pl.BlockSpec((None, 3, D), lambda b, i: (b, 0, 0)),
                  _const_spec(w_in.shape), _const_spec(w_out.shape), _const_spec((1, D))],
        out_specs=[row(D), row(4 * FS), row(D), pl.BlockSpec((None, 3, D), lambda b, i: (b, 0, 0)),
                   pl.BlockSpec((2, D), lambda b, i: (0, 0))],
        out_shape=[jax.ShapeDtypeStruct((B, T, D), f32), jax.ShapeDtypeStruct((B, T, 4 * FS), bf16),
                   jax.ShapeDtypeStruct((B, T, D), bf16), jax.ShapeDtypeStruct((B, 3, D), f32),
                   jax.ShapeDtypeStruct((2, D), f32)],
        compiler_params=_cp(),
    )(dxn, x, y, gu, mod3, w_in, w_out, lg)


def wgrad(a, b, tn):
    B, T, K = a.shape
    N = b.shape[2]
    tr = _row_tile(T, 1024 if K * tn <= 1024 * 1536 else 512)

    def body(a_ref, b_ref, o_ref):
        @pl.when((pl.program_id(1) == 0) & (pl.program_id(2) == 0))
        def _():
            o_ref[...] = jnp.zeros_like(o_ref)

        o_ref[...] += _dot_tn(a_ref[...], b_ref[...])

    return pl.pallas_call(
        body, name="wgrad", grid=(N // tn, B, T // tr),
        in_specs=[pl.BlockSpec((None, tr, K), lambda s, b, r: (b, r, 0)),
                  pl.BlockSpec((None, tr, tn), lambda s, b, r: (b, r, s))],
        out_specs=pl.BlockSpec((None, K, tn), lambda s, b, r: (s, 0, 0)),
        out_shape=jax.ShapeDtypeStruct((N // tn, K, tn), f32),
        compiler_params=_cp(),
    )(a, b)


def wgrad_into(a, b, tn, lead, pos, buf=None):
    B, T, K = a.shape
    N = b.shape[2]
    tr = _row_tile(T, 1024 if K * tn <= 1024 * 1536 else 512)
    nl = len(lead)

    def body(pos_ref, a_ref, b_ref, *rest):
        o_ref = rest[-1]

        @pl.when((pl.program_id(1) == 0) & (pl.program_id(2) == 0))
        def _():
            o_ref[...] = jnp.zeros_like(o_ref)

        o_ref[...] += _dot_tn(a_ref[...], b_ref[...])

    return pl.pallas_call(
        body, name="wgrad_into",
        grid_spec=pltpu.PrefetchScalarGridSpec(
            num_scalar_prefetch=1, grid=(N // tn, B, T // tr),
            in_specs=[pl.BlockSpec((None, tr, K), lambda s, b, r, p: (b, r, 0)),
                      pl.BlockSpec((None, tr, tn), lambda s, b, r, p: (b, r, s))] + ([] if buf is None else [ANY]),
            out_specs=pl.BlockSpec((None,) * (nl + 1) + (K, tn),
                                   lambda s, b, r, p: tuple(p[j] for j in range(nl)) + (s, 0, 0))),
        out_shape=jax.ShapeDtypeStruct(tuple(lead) + (N // tn, K, tn), f32),
        input_output_aliases={} if buf is None else {3: 0},
        compiler_params=_cp(),
    )(jnp.asarray(pos, jnp.int32), a, b, *([] if buf is None else [buf]))


def inproj_fwd(x, mod3, w):
    B, T, D = x.shape
    N = w.shape[1]
    tm = _row_tile(T, 512)

    def body(x_ref, mod_ref, w_ref, p_ref, h_ref):
        h = (x_ref[...] * (1.0 + mod_ref[1:2, :]) + mod_ref[0:1, :]).astype(bf16)
        h_ref[...] = h
        p_ref[...] = _dot(h, w_ref[...])

    row = lambda n: pl.BlockSpec((None, tm, n), lambda b, i: (b, i, 0))
    return pl.pallas_call(
        body, name="inproj_fwd", grid=(B, T // tm),
        in_specs=[row(D), pl.BlockSpec((None, 3, D), lambda b, i: (b, 0, 0)), _const_spec(w.shape)],
        out_specs=[row(N), row(D)],
        out_shape=[jax.ShapeDtypeStruct((B, T, N), f32), jax.ShapeDtypeStruct((B, T, D), bf16)],
        compiler_params=_cp(),
    )(x, mod3, w)


def inproj_bwd(dproj, w, x, mod3, dxp):
    B, T, D = x.shape
    N = w.shape[1]
    tm = _row_tile(T, 512)

    def body(dp_ref, w_ref, x_ref, mod_ref, dxp_ref, dx_ref, dmod_ref):
        @pl.when(pl.program_id(1) == 0)
        def _():
            dmod_ref[...] = jnp.zeros_like(dmod_ref)

        dh = _dot_nt(dp_ref[...], w_ref[...])
        dx_ref[...] = dxp_ref[...] + dh * (1.0 + mod_ref[1:2, :])
        dmod_ref[0:1, :] += jnp.sum(dh, axis=0, keepdims=True)
        dmod_ref[1:2, :] += jnp.sum(dh * x_ref[...], axis=0, keepdims=True)

    row = lambda n: pl.BlockSpec((None, tm, n), lambda b, i: (b, i, 0))
    return pl.pallas_call(
        body, name="inproj_bwd", grid=(B, T // tm),
        in_specs=[row(N), _const_spec(w.shape), row(D), pl.BlockSpec((None, 3, D), lambda b, i: (b, 0, 0)), row(D)],
        out_specs=[row(D), pl.BlockSpec((None, 2, D), lambda b, i: (b, 0, 0))],
        out_shape=[jax.ShapeDtypeStruct((B, T, D), f32), jax.ShapeDtypeStruct((B, 2, D), f32)],
        compiler_params=_cp(),
    )(dproj, w, x, mod3, dxp)


def outproj_fwd(ycat, w, x, mod3, lg, lb):
    B, T, D = x.shape
    E = w.shape[0]
    tm = _row_tile(T, 512)

    def body(yc_ref, w_ref, x_ref, mod_ref, lg_ref, lb_ref, xn_ref, y_ref):
        yv = _dot(yc_ref[...], w_ref[...])
        y_ref[...] = yv
        xhat, _ = _ln_stats(DN_ALPHA * x_ref[...] + (1.0 + mod_ref[2:3, :]) * yv)
        xn_ref[...] = xhat * lg_ref[...] + lb_ref[...]

    row = lambda n: pl.BlockSpec((None, tm, n), lambda b, i: (b, i, 0))
    return pl.pallas_call(
        body, name="outproj_fwd", grid=(B, T // tm),
        in_specs=[row(E), _const_spec(w.shape), row(D), pl.BlockSpec((None, 3, D), lambda b, i: (b, 0, 0)),
                  _const_spec((1, D)), _const_spec((1, D))],
        out_specs=[row(D), row(D)],
        out_shape=[jax.ShapeDtypeStruct((B, T, D), f32), jax.ShapeDtypeStruct((B, T, D), f32)],
        compiler_params=_cp(),
    )(ycat, w, x, mod3, lg, lb)


def outproj_bwd(dxn, x, y, mod3, w, lg):
    B, T, D = x.shape
    E = w.shape[0]
    tm = _row_tile(T, 512)

    def body(dxn_ref, x_ref, y_ref, mod_ref, w_ref, lg_ref, dxp_ref, dyc_ref, dy_ref, dg_ref, dln_ref):
        b, i = pl.program_id(0), pl.program_id(1)

        @pl.when((b == 0) & (i == 0))
        def _():
            dln_ref[...] = jnp.zeros_like(dln_ref)

        @pl.when(i == 0)
        def _():
            dg_ref[...] = jnp.zeros_like(dg_ref)

        xv, yv, dxn_v = x_ref[...], y_ref[...], dxn_ref[...]
        g = mod_ref[2:3, :]
        xhat, rstd = _ln_stats(DN_ALPHA * xv + (1.0 + g) * yv)
        dln_ref[0:1, :] += jnp.sum(dxn_v * xhat, axis=0, keepdims=True)
        dln_ref[1:2, :] += jnp.sum(dxn_v, axis=0, keepdims=True)
        dz = _ln_bwd(dxn_v, xhat, rstd, lg_ref[...])
        dg_ref[...] += jnp.sum(dz * yv, axis=0, keepdims=True)
        dy = ((1.0 + g) * dz).astype(bf16)
        dy_ref[...] = dy
        dxp_ref[...] = DN_ALPHA * dz
        dyc_ref[...] = _dot_nt(dy, w_ref[...])

    row = lambda n: pl.BlockSpec((None, tm, n), lambda b, i: (b, i, 0))
    return pl.pallas_call(
        body, name="outproj_bwd", grid=(B, T // tm),
        in_specs=[row(D), row(D), row(D), pl.BlockSpec((None, 3, D), lambda b, i: (b, 0, 0)), _const_spec(w.shape),
                  _const_spec((1, D))],
        out_specs=[row(D), row(E), row(D), pl.BlockSpec((None, 1, D), lambda b, i: (b, 0, 0)),
                   pl.BlockSpec((2, D), lambda b, i: (0, 0))],
        out_shape=[jax.ShapeDtypeStruct((B, T, D), f32), jax.ShapeDtypeStruct((B, T, E), f32),
                   jax.ShapeDtypeStruct((B, T, D), bf16), jax.ShapeDtypeStruct((B, 1, D), f32),
                   jax.ShapeDtypeStruct((2, D), f32)],
        compiler_params=_cp(),
    )(dxn, x, y, mod3, w, lg)


def _halo_rows(K):
    return 8 if K <= 9 else 32


def dwconv_fwd(x, col0, C, w, b, K):
    B, T, _ = x.shape
    tc, hp = COL_TILE, _halo_rows(K)
    tm = _row_tile(T, 1024 if K <= 9 else 512)
    r = tm // hp

    def body(xh_ref, x_ref, w_ref, b_ref, o_ref):
        halo = jnp.where(pl.program_id(2) == 0, 0.0, xh_ref[...])
        xe = jnp.concatenate([halo, x_ref[...]], axis=0)
        acc = jnp.zeros((tm, tc), f32) + b_ref[...]
        for k in range(K):
            sft = K - 1 - k
            xs = xe if sft == 0 else pltpu.roll(xe, sft, 0)
            acc = acc + xs[hp:, :] * w_ref[k:k + 1, :]
        o_ref[...] = acc

    return pl.pallas_call(
        body, name=f"dwconv{K}_fwd", grid=(C // tc, B, T // tm),
        in_specs=[pl.BlockSpec((None, hp, tc), lambda j, b, i: (b, jnp.maximum(i * r - 1, 0), col0 + j)),
                  pl.BlockSpec((None, tm, tc), lambda j, b, i: (b, i, col0 + j)),
                  pl.BlockSpec((w.shape[0], tc), lambda j, b, i: (0, j)),
                  pl.BlockSpec((1, tc), lambda j, b, i: (0, j))],
        out_specs=pl.BlockSpec((None, tm, tc), lambda j, b, i: (b, i, j)),
        out_shape=jax.ShapeDtypeStruct((B, T, C), f32),
        compiler_params=_cp(),
    )(x, x, w, b)


def dwconv_bwd(dc, x, col0, C, w, K, out_dtype):
    B, T, _ = x.shape
    tc, hp = COL_TILE, _halo_rows(K)
    tm = _row_tile(T, 1024 if K <= 9 else 512)
    r = tm // hp
    nt = T // tm
    n = tm + hp
    KP = w.shape[0]

    def body(dcn_ref, dc_ref, x_ref, w_ref, dx_ref, dw_ref, db_ref):
        b, i = pl.program_id(1), pl.program_id(2)

        @pl.when((b == 0) & (i == 0))
        def _():
            dw_ref[...] = jnp.zeros_like(dw_ref)
            db_ref[...] = jnp.zeros_like(db_ref)

        dcv, xv = dc_ref[...], x_ref[...]
        de = jnp.concatenate([dcv, jnp.where(i == nt - 1, 0.0, dcn_ref[...])], axis=0)
        acc = jnp.zeros((tm, tc), f32)
        for k in range(K):
            j = K - 1 - k
            ds = (de if j == 0 else pltpu.roll(de, n - j, 0))[:tm, :]
            acc = acc + ds * w_ref[k:k + 1, :]
            dw_ref[k:k + 1, :] += jnp.sum(ds * xv, axis=0, keepdims=True)
        dx_ref[...] = acc.astype(out_dtype)
        db_ref[...] += jnp.sum(dcv, axis=0, keepdims=True)

    return pl.pallas_call(
        body, name=f"dwconv{K}_bwd", grid=(C // tc, B, nt),
        in_specs=[pl.BlockSpec((None, hp, tc), lambda j, b, i: (b, jnp.minimum((i + 1) * r, T // hp - 1), j)),
                  pl.BlockSpec((None, tm, tc), lambda j, b, i: (b, i, j)),
                  pl.BlockSpec((None, tm, tc), lambda j, b, i: (b, i, col0 + j)),
                  pl.BlockSpec((KP, tc), lambda j, b, i: (0, j))],
        out_specs=[pl.BlockSpec((None, tm, tc), lambda j, b, i: (b, i, j)),
                   pl.BlockSpec((KP, tc), lambda j, b, i: (0, j)),
                   pl.BlockSpec((1, tc), lambda j, b, i: (0, j))],
        out_shape=[jax.ShapeDtypeStruct((B, T, C), out_dtype), jax.ShapeDtypeStruct((KP, C), f32),
                   jax.ShapeDtypeStruct((1, C), f32)],
        compiler_params=_cp(),
    )(dc, dc, x, w)


POOL_HALO = 16


def _pool_windows(ue, pos, hp):
    out = []
    for g, wd in enumerate(POOL_WINDOWS):
        ug = ue[:, g * LANES:(g + 1) * LANES]
        s, span = ug, 1
        while span < wd:
            s = s + pltpu.roll(s, span, 0)
            span *= 2
        cnt = jnp.minimum(pos + 1, wd).astype(f32)
        out.append(s[hp:, :] / cnt - ug[hp:, :])
    return out


def pool_fwd(proj, colb, w, scale, ybuf):
    B, T, _ = proj.shape
    hp = POOL_HALO
    tm = _row_tile(T, 512)
    r = tm // hp

    def body(uh_ref, u_ref, w_ref, sc_ref, ybuf_ref, o_ref):
        i = pl.program_id(1)
        ue = jnp.concatenate([jnp.where(i == 0, 0.0, uh_ref[...]), u_ref[...]], axis=0)
        pos = i * tm + lax.broadcasted_iota(jnp.int32, (tm, 1), 0)
        ps = _pool_windows(ue, pos, hp)
        o = jnp.concatenate([_dot(ps[g], w_ref[g]) for g in range(4)], axis=1) * sc_ref[...]
        o_ref[...] = o.astype(bf16)

    return pl.pallas_call(
        body, name="pool_fwd", grid=(B, T // tm),
        in_specs=[pl.BlockSpec((None, hp, POOL_DIM), lambda b, i: (b, jnp.maximum(i * r - 1, 0), colb)),
                  pl.BlockSpec((None, tm, POOL_DIM), lambda b, i: (b, i, colb)),
                  _const_spec(w.shape), _const_spec((1, POOL_DIM)), pl.BlockSpec(memory_space=pl.ANY)],
        out_specs=pl.BlockSpec((None, tm, POOL_DIM), lambda b, i: (b, i, ybuf.shape[2] // POOL_DIM - 1)),
        out_shape=jax.ShapeDtypeStruct(ybuf.shape, bf16),
        input_output_aliases={4: 0},
        compiler_params=_cp(),
    )(proj, proj, w, scale, ybuf)


def pool_bwd(dycat, dcolb, proj, colb, w, scale):
    B, T, _ = proj.shape
    hp = POOL_HALO
    tm = _row_tile(T, 512)
    r = tm // hp
    nt = T // tm
    n = tm + hp

    def body(dyn_ref, dy_ref, uh_ref, u_ref, w_ref, sc_ref, du_ref, dw_ref, dsc_ref):
        b, i = pl.program_id(0), pl.program_id(1)

        @pl.when((b == 0) & (i == 0))
        def _():
            dw_ref[...] = jnp.zeros_like(dw_ref)
            dsc_ref[...] = jnp.zeros_like(dsc_ref)

        dyv = dy_ref[...]
        dye = jnp.concatenate([dyv, jnp.where(i == nt - 1, 0.0, dyn_ref[...])], axis=0)
        ue = jnp.concatenate([jnp.where(i == 0, 0.0, uh_ref[...]), u_ref[...]], axis=0)
        pos = i * tm + lax.broadcasted_iota(jnp.int32, (tm, 1), 0)
        pos_e = i * tm + lax.broadcasted_iota(jnp.int32, (n, 1), 0)
        ps = _pool_windows(ue, pos, hp)
        dme = dye * sc_ref[...]
        dus, dscs = [], []
        for g, wd in enumerate(POOL_WINDOWS):
            sl = slice(g * LANES, (g + 1) * LANES)
            dscs.append(jnp.sum(dyv[:, sl] * _dot(ps[g], w_ref[g]), axis=0, keepdims=True))
            dw_ref[g] += _dot_tn(ps[g], dme[:tm, sl])
            dpe = _dot_nt(dme[:, sl], w_ref[g])
            s, span = dpe / jnp.minimum(pos_e + 1, wd).astype(f32), 1
            while span < wd:
                s = s + pltpu.roll(s, n - span, 0)
                span *= 2
            dus.append(s[:tm, :] - dpe[:tm, :])
        du_ref[...] = jnp.concatenate(dus, axis=1).astype(bf16)
        dsc_ref[...] += jnp.concatenate(dscs, axis=1)

    return pl.pallas_call(
        body, name="pool_bwd", grid=(B, nt),
        in_specs=[pl.BlockSpec((None, hp, POOL_DIM), lambda b, i: (b, jnp.minimum((i + 1) * r, T // hp - 1), dcolb)),
                  pl.BlockSpec((None, tm, POOL_DIM), lambda b, i: (b, i, dcolb)),
                  pl.BlockSpec((None, hp, POOL_DIM), lambda b, i: (b, jnp.maximum(i * r - 1, 0), colb)),
                  pl.BlockSpec((None, tm, POOL_DIM), lambda b, i: (b, i, colb)),
                  _const_spec(w.shape), _const_spec((1, POOL_DIM))],
        out_specs=[pl.BlockSpec((None, tm, POOL_DIM), lambda b, i: (b, i, 0)),
                   pl.BlockSpec(w.shape, lambda b, i: (0, 0, 0)),
                   pl.BlockSpec((1, POOL_DIM), lambda b, i: (0, 0))],
        out_shape=[jax.ShapeDtypeStruct((B, T, POOL_DIM), bf16), jax.ShapeDtypeStruct(w.shape, f32),
                   jax.ShapeDtypeStruct((1, POOL_DIM), f32)],
        compiler_params=_cp(),
    )(dycat, dycat, proj, proj, w, scale)


N_PAIRS = SSD_HEADS // 2


def _ssd_chunk(xs, bs, cs, dtp, zs, hs, dtb, alog, dsk, ngs):
    Q = SSD_CHUNK
    lane = lax.broadcasted_iota(jnp.int32, (1, LANES), 1)
    sub = lax.broadcasted_iota(jnp.int32, (LANES, 1), 0)
    causal = lax.broadcasted_iota(jnp.int32, (Q, Q), 0) >= lax.broadcasted_iota(jnp.int32, (Q, Q), 1)
    lane_lo, sub_lo = lane < 64, sub < 64

    def col(v, h):
        return jnp.sum(v * (lane == h).astype(f32), axis=1, keepdims=True)

    def row(vt, h):
        return jnp.sum(vt * (sub == h).astype(f32), axis=0, keepdims=True)

    dt = jax.nn.softplus(dtp + dtb)
    acum = _dot_hi(causal.astype(f32), dt * (-jnp.exp(alog)))
    acum_t = acum.T
    aend = jnp.sum(acum * (sub == Q - 1).astype(f32), axis=0, keepdims=True)
    outs, hn = [], []
    for grp in range(2):
        bv, cv = _silu(bs[grp]), _silu(cs[grp])
        gmat = _dot_nt(cv, bv)
        for j in range(4):
            p = grp * 4 + j
            h0, h1 = 2 * p, 2 * p + 1
            x2 = _silu(xs[p])
            c0, c1 = col(acum, h0), col(acum, h1)
            s2 = jnp.where(lane_lo, c0, c1)
            xdt = x2 * jnp.where(lane_lo, col(dt, h0), col(dt, h1))
            l0 = jnp.where(causal, jnp.exp(jnp.minimum(c0 - row(acum_t, h0), 0.0)), 0.0)
            l1 = jnp.where(causal, jnp.exp(jnp.minimum(c1 - row(acum_t, h1), 0.0)), 0.0)
            yd = _dot(gmat * l0, jnp.where(lane_lo, xdt, 0.0)) + _dot(gmat * l1, jnp.where(lane_lo, 0.0, xdt))
            e0, e1 = col(aend, h0), col(aend, h1)
            st = _dot_tn(xdt * jnp.exp(jnp.where(lane_lo, e0, e1) - s2), bv)
            yo = jnp.exp(s2) * _dot_nt(cv, hs[p])
            hn.append(jnp.exp(jnp.where(sub_lo, e0, e1)) * hs[p] + st)
            yv = yd + yo + x2 * jnp.where(lane_lo, col(dsk, h0), col(dsk, h1))
            outs.append(yv * _silu(zs[p]))
    ms = sum(jnp.sum(o * o, axis=1, keepdims=True) for o in outs) / SSD_D_INNER
    rs = lax.rsqrt(ms + NORM_EPS)
    return [outs[p] * rs * ngs[p] for p in range(N_PAIRS)], hn


def _lane_blocks(ref, n, start=0):
    return [ref[:, (start + k) * LANES:(start + k + 1) * LANES] for k in range(n)]


def _ssd_args(z_ref, cx_ref, dt_ref, dtb_ref, alog_ref, dsk_ref, ng_ref):
    xs = _lane_blocks(cx_ref, 8)
    bs = _lane_blocks(cx_ref, 2, 8)
    cs = _lane_blocks(cx_ref, 2, 10)
    zs = _lane_blocks(z_ref, 8)
    ngs = _lane_blocks(ng_ref, 8)
    return xs, bs, cs, dt_ref[...], zs, dtb_ref[...], alog_ref[...], dsk_ref[...], ngs


DT_COLB = (EVEN_IN_PAD - LANES) // LANES


def ssd_fwd(proj, cx, dtb, alog, dsk, ng):
    B, T, _ = proj.shape
    Q = SSD_CHUNK
    nc = T // Q

    def body(z_ref, cx_ref, dt_ref, dtb_ref, alog_ref, dsk_ref, ng_ref, ya_ref, hsave_ref, h_scr):
        @pl.when(pl.program_id(1) == 0)
        def _():
            h_scr[...] = jnp.zeros_like(h_scr)

        xs, bs, cs, dtp, zs, dtb_v, alog_v, dsk_v, ngs = _ssd_args(z_ref, cx_ref, dt_ref, dtb_ref, alog_ref, dsk_ref, ng_ref)
        hs = [h_scr[p] for p in range(N_PAIRS)]
        for p in range(N_PAIRS):
            hsave_ref[p] = hs[p]
        outs, hn = _ssd_chunk(xs, bs, cs, dtp, zs, hs, dtb_v, alog_v, dsk_v, ngs)
        for p in range(N_PAIRS):
            ya_ref[:, p * LANES:(p + 1) * LANES] = outs[p].astype(bf16)
            h_scr[p] = hn[p]

    return pl.pallas_call(
        body, name="ssd_fwd", grid=(B, nc),
        in_specs=[pl.BlockSpec((None, Q, SSD_D_INNER), lambda b, i: (b, i, 0)),
                  pl.BlockSpec((None, Q, SSD_CONV_DIM), lambda b, i: (b, i, 0)),
                  pl.BlockSpec((None, Q, LANES), lambda b, i: (b, i, DT_COLB)),
                  _const_spec((1, LANES)), _const_spec((1, LANES)), _const_spec((1, LANES)),
                  _const_spec((1, SSD_D_INNER))],
        out_specs=[pl.BlockSpec((None, Q, SSD_D_INNER), lambda b, i: (b, i, 0)),
                   pl.BlockSpec((None, None, N_PAIRS, LANES, LANES), lambda b, i: (b, i, 0, 0, 0))],
        out_shape=[jax.ShapeDtypeStruct((B, T, SSD_D_INNER + POOL_DIM), bf16),
                   jax.ShapeDtypeStruct((B, nc, N_PAIRS, LANES, LANES), f32)],
        scratch_shapes=[pltpu.VMEM((N_PAIRS, LANES, LANES), f32)],
        compiler_params=_cp(),
    )(proj, cx, proj, dtb, alog, dsk, ng)


def ssd_bwd(dycat, proj, cx, hsave, dtb, alog, dsk, ng):
    B, T, _ = proj.shape
    Q = SSD_CHUNK
    nc = T // Q

    def body(dya_ref, z_ref, cx_ref, dt_ref, hsave_ref, dtb_ref, alog_ref, dsk_ref, ng_ref,
             dz_ref, dcx_ref, ddt_ref, gsm_ref, gng_ref, dh_scr):
        b, i = pl.program_id(0), pl.program_id(1)

        @pl.when((b == 0) & (i == 0))
        def _():
            gsm_ref[...] = jnp.zeros_like(gsm_ref)
            gng_ref[...] = jnp.zeros_like(gng_ref)

        @pl.when(i == 0)
        def _():
            dh_scr[...] = jnp.zeros_like(dh_scr)

        xs, bs, cs, dtp, zs, dtb_v, alog_v, dsk_v, ngs = _ssd_args(z_ref, cx_ref, dt_ref, dtb_ref, alog_ref, dsk_ref, ng_ref)
        hs = [hsave_ref[p] for p in range(N_PAIRS)]
        _, vjp = jax.vjp(_ssd_chunk, xs, bs, cs, dtp, zs, hs, dtb_v, alog_v, dsk_v, ngs)
        douts = _lane_blocks(dya_ref, 8)
        dhn = [dh_scr[p] for p in range(N_PAIRS)]
        dxs, dbs, dcs, ddtp, dzs, dhs, ddtb, dalog, ddsk, dngs = vjp((douts, dhn))
        for p in range(N_PAIRS):
            dcx_ref[:, p * LANES:(p + 1) * LANES] = dxs[p]
            dz_ref[:, p * LANES:(p + 1) * LANES] = dzs[p].astype(bf16)
            dh_scr[p] = dhs[p]
            gng_ref[:, p * LANES:(p + 1) * LANES] += dngs[p]
        for k in range(2):
            dcx_ref[:, (8 + k) * LANES:(9 + k) * LANES] = dbs[k]
            dcx_ref[:, (10 + k) * LANES:(11 + k) * LANES] = dcs[k]
        ddt_ref[...] = ddtp.astype(bf16)
        gsm_ref[0:1, :] += ddtb
        gsm_ref[1:2, :] += dalog
        gsm_ref[2:3, :] += ddsk

    rev = lambda w, cb=0: pl.BlockSpec((None, Q, w), lambda b, i: (b, nc - 1 - i, cb))
    return pl.pallas_call(
        body, name="ssd_bwd", grid=(B, nc),
        in_specs=[rev(SSD_D_INNER), rev(SSD_D_INNER), rev(SSD_CONV_DIM), rev(LANES, DT_COLB),
                  pl.BlockSpec((None, None, N_PAIRS, LANES, LANES), lambda b, i: (b, nc - 1 - i, 0, 0, 0)),
                  _const_spec((1, LANES)), _const_spec((1, LANES)), _const_spec((1, LANES)),
                  _const_spec((1, SSD_D_INNER))],
        out_specs=[rev(SSD_D_INNER), rev(SSD_CONV_DIM), rev(LANES),
                   pl.BlockSpec((3, LANES), lambda b, i: (0, 0)),
                   pl.BlockSpec((1, SSD_D_INNER), lambda b, i: (0, 0))],
        out_shape=[jax.ShapeDtypeStruct((B, T, SSD_D_INNER), bf16), jax.ShapeDtypeStruct((B, T, SSD_CONV_DIM), f32),
                   jax.ShapeDtypeStruct((B, T, LANES), bf16), jax.ShapeDtypeStruct((3, LANES), f32),
                   jax.ShapeDtypeStruct((1, SSD_D_INNER), f32)],
        scratch_shapes=[pltpu.VMEM((N_PAIRS, LANES, LANES), f32)],
        compiler_params=_cp(),
    )(dycat, proj, cx, proj, hsave, dtb, alog, dsk, ng)


def glu_fwd(proj):
    B, T, _ = proj.shape
    tm = _row_tile(T, 1024)

    def body(v_ref, g_ref, o_ref):
        o_ref[...] = v_ref[...] * jax.nn.sigmoid(g_ref[...])

    blk = lambda cb: pl.BlockSpec((None, tm, CONF_DIM), lambda b, i: (b, i, cb))
    return pl.pallas_call(body, name="glu_fwd", grid=(B, T // tm), in_specs=[blk(0), blk(1)], out_specs=blk(0),
                          out_shape=jax.ShapeDtypeStruct((B, T, CONF_DIM), f32), compiler_params=_cp())(proj, proj)


def glu_bwd(dhh, proj):
    B, T, _ = proj.shape
    tm = _row_tile(T, 1024)

    def body(d_ref, v_ref, g_ref, o_ref):
        sig = jax.nn.sigmoid(g_ref[...])
        dv = d_ref[...]
        o_ref[:, :CONF_DIM] = (dv * sig).astype(bf16)
        o_ref[:, CONF_DIM:] = (dv * v_ref[...] * sig * (1.0 - sig)).astype(bf16)

    blk = lambda cb: pl.BlockSpec((None, tm, CONF_DIM), lambda b, i: (b, i, cb))
    return pl.pallas_call(body, name="glu_bwd", grid=(B, T // tm), in_specs=[blk(0), blk(0), blk(1)],
                          out_specs=pl.BlockSpec((None, tm, 2 * CONF_DIM), lambda b, i: (b, i, 0)),
                          out_shape=jax.ShapeDtypeStruct((B, T, 2 * CONF_DIM), bf16), compiler_params=_cp())(dhh, proj, proj)


def _neg_expm1(x):
    series = x * (1.0 + x * (1.0 / 2.0) * (1.0 + x * (1.0 / 3.0) * (1.0 + x * (1.0 / 4.0) * (1.0 + x * (1.0 / 5.0)))))
    return -jnp.where(x > -0.1, series, jnp.exp(x) - 1.0)


def _lru_gates(cc, xc8, gr8, clg, clb, wa8, wx8, ba8, bx8, lam8):
    xhat, _ = _ln_stats(cc)
    yc = _silu(xhat * clg + clb)
    a8, b8, ge8 = [], [], []
    for hb in range(LRU_HEADS):
        xh = xc8[hb]
        rg = jax.nn.sigmoid(_dot(xh, wa8[hb]) + ba8[hb])
        ig = jax.nn.sigmoid(_dot(xh, wx8[hb]) + bx8[hb])
        log_a = -LRU_C * rg * jax.nn.softplus(-lam8[hb])
        a8.append(jnp.exp(log_a))
        b8.append(jnp.sqrt(_neg_expm1(2.0 * log_a)) * (ig * xh))
        ge8.append(jax.nn.gelu(gr8[hb]))
    return yc, a8, b8, ge8


def _scan_fwd(a, b, h_in):
    tm = a.shape[0]
    rows = lax.broadcasted_iota(jnp.int32, (tm, 1), 0)
    s = 1
    while s < tm:
        keep = rows >= s
        b = a * jnp.where(keep, pltpu.roll(b, s, 0), 0.0) + b
        a = a * jnp.where(keep, pltpu.roll(a, s, 0), 1.0)
        s *= 2
    return a * h_in + b


def _scan_bwd(e, d, g_in):
    tm = e.shape[0]
    rows = lax.broadcasted_iota(jnp.int32, (tm, 1), 0)
    s = 1
    while s < tm:
        keep = rows < tm - s
        d = e * jnp.where(keep, pltpu.roll(d, tm - s, 0), 0.0) + d
        e = e * jnp.where(keep, pltpu.roll(e, tm - s, 0), 1.0)
        s *= 2
    return e * g_in + d


def _lru_params(wa_ref, wx_ref, vec_ref):
    wa8 = [wa_ref[h] for h in range(LRU_HEADS)]
    wx8 = [wx_ref[h] for h in range(LRU_HEADS)]
    ba8 = [vec_ref[0:1, h * LANES:(h + 1) * LANES] for h in range(LRU_HEADS)]
    bx8 = [vec_ref[1:2, h * LANES:(h + 1) * LANES] for h in range(LRU_HEADS)]
    lam8 = [vec_ref[2:3, h * LANES:(h + 1) * LANES] for h in range(LRU_HEADS)]
    return wa8, wx8, ba8, bx8, lam8


GR_COLB = 2


def lru_fwd(cc, xc, proj, cln, wa, wx, vec):
    B, T, _ = xc.shape
    tm = _row_tile(T)

    def body(cc_ref, xc_ref, gr_ref, cln_ref, wa_ref, wx_ref, vec_ref, y_ref, hs_ref, h_scr):
        @pl.when(pl.program_id(1) == 0)
        def _():
            h_scr[...] = jnp.zeros_like(h_scr)

        yc, a8, b8, ge8 = _lru_gates(cc_ref[...], _lane_blocks(xc_ref, 8), _lane_blocks(gr_ref, 8), cln_ref[0:1, :],
                                     cln_ref[1:2, :], *_lru_params(wa_ref, wx_ref, vec_ref))
        h = _scan_fwd(jnp.concatenate(a8, axis=1), jnp.concatenate(b8, axis=1), h_scr[...])
        hs_ref[...] = h
        h_scr[...] = h[tm - 1:tm, :]
        y_ref[:, :CONF_DIM] = yc.astype(bf16)
        y_ref[:, CONF_DIM:] = (h * jnp.concatenate(ge8, axis=1)).astype(bf16)

    row = lambda w, cb=0: pl.BlockSpec((None, tm, w), lambda b, i: (b, i, cb))
    return pl.pallas_call(
        body, name="lru_fwd", grid=(B, T // tm),
        in_specs=[row(CONF_DIM), row(LRU_DIM), row(LRU_DIM, GR_COLB), _const_spec((2, CONF_DIM)),
                  _const_spec(wa.shape), _const_spec(wx.shape), _const_spec((3, LRU_DIM))],
        out_specs=[row(CONF_DIM + LRU_DIM), row(LRU_DIM)],
        out_shape=[jax.ShapeDtypeStruct((B, T, CONF_DIM + LRU_DIM), bf16), jax.ShapeDtypeStruct((B, T, LRU_DIM), f32)],
        scratch_shapes=[pltpu.VMEM((1, LRU_DIM), f32)],
        compiler_params=_cp(),
    )(cc, xc, proj, cln, wa, wx, vec)


def lru_bwd(dycat, cc, xc, proj, hs, cln, wa, wx, vec):
    B, T, _ = xc.shape
    tm = _row_tile(T)
    nt = T // tm
    r = tm // 8

    def body(dy_ref, cc_ref, xc_ref, gr_ref, hs_ref, hsh_ref, cln_ref, wa_ref, wx_ref, vec_ref,
             dcc_ref, dxc_ref, dgr_ref, dcln_ref, dwa_ref, dwx_ref, dvec_ref, g_scr, a_scr):
        b, i = pl.program_id(0), pl.program_id(1)
        it = nt - 1 - i

        @pl.when((b == 0) & (i == 0))
        def _():
            dcln_ref[...] = jnp.zeros_like(dcln_ref)
            dwa_ref[...] = jnp.zeros_like(dwa_ref)
            dwx_ref[...] = jnp.zeros_like(dwx_ref)
            dvec_ref[...] = jnp.zeros_like(dvec_ref)

        @pl.when(i == 0)
        def _():
            g_scr[...] = jnp.zeros_like(g_scr)
            a_scr[...] = jnp.zeros_like(a_scr)

        (yc, a8, b8, ge8), vjp = jax.vjp(_lru_gates, cc_ref[...], _lane_blocks(xc_ref, 8), _lane_blocks(gr_ref, 8),
                                         cln_ref[0:1, :], cln_ref[1:2, :], *_lru_params(wa_ref, wx_ref, vec_ref))
        a = jnp.concatenate(a8, axis=1)
        ge = jnp.concatenate(ge8, axis=1)
        h = hs_ref[...]
        dyd = dy_ref[:, CONF_DIM:]
        rows = lax.broadcasted_iota(jnp.int32, (tm, 1), 0)
        e = jnp.where(rows < tm - 1, pltpu.roll(a, tm - 1, 0), a_scr[...])
        g = _scan_bwd(e, dyd * ge, g_scr[...])
        h_first = jnp.where(it == 0, 0.0, hsh_ref[7:8, :])
        h_prev = jnp.where(rows >= 1, pltpu.roll(h, 1, 0), h_first)
        da = g * h_prev
        g_scr[...] = g[0:1, :]
        a_scr[...] = a[0:1, :]
        split = lambda v: [v[:, k * LANES:(k + 1) * LANES] for k in range(LRU_HEADS)]
        dcc, dxc8, dgr8, dclg, dclb, dwa8, dwx8, dba8, dbx8, dlam8 = vjp((dy_ref[:, :CONF_DIM], split(da), split(g), split(dyd * h)))
        dcc_ref[...] = dcc
        dcln_ref[0:1, :] += dclg
        dcln_ref[1:2, :] += dclb
        for k in range(LRU_HEADS):
            sl = slice(k * LANES, (k + 1) * LANES)
            dxc_ref[:, sl] = dxc8[k]
            dgr_ref[:, sl] = dgr8[k].astype(bf16)
            dwa_ref[k] += dwa8[k]
            dwx_ref[k] += dwx8[k]
            dvec_ref[0:1, sl] += dba8[k]
            dvec_ref[1:2, sl] += dbx8[k]
            dvec_ref[2:3, sl] += dlam8[k]

    rev = lambda w, cb=0: pl.BlockSpec((None, tm, w), lambda b, i: (b, nt - 1 - i, cb))
    acc = lambda shape: pl.BlockSpec(shape, lambda b, i: (0,) * len(shape))
    return pl.pallas_call(
        body, name="lru_bwd", grid=(B, nt),
        in_specs=[rev(CONF_DIM + LRU_DIM), rev(CONF_DIM), rev(LRU_DIM), rev(LRU_DIM, GR_COLB), rev(LRU_DIM),
                  pl.BlockSpec((None, 8, LRU_DIM), lambda b, i: (b, jnp.maximum((nt - 1 - i) * r - 1, 0), 0)),
                  _const_spec((2, CONF_DIM)), _const_spec(wa.shape), _const_spec(wx.shape), _const_spec((3, LRU_DIM))],
        out_specs=[rev(CONF_DIM), rev(LRU_DIM), rev(LRU_DIM), acc((2, CONF_DIM)), acc(wa.shape), acc(wx.shape),
                   acc((3, LRU_DIM))],
        out_shape=[jax.ShapeDtypeStruct((B, T, CONF_DIM), f32), jax.ShapeDtypeStruct((B, T, LRU_DIM), f32),
                   jax.ShapeDtypeStruct((B, T, LRU_DIM), bf16), jax.ShapeDtypeStruct((2, CONF_DIM), f32),
                   jax.ShapeDtypeStruct(wa.shape, f32), jax.ShapeDtypeStruct(wx.shape, f32),
                   jax.ShapeDtypeStruct((3, LRU_DIM), f32)],
        scratch_shapes=[pltpu.VMEM((1, LRU_DIM), f32), pltpu.VMEM((1, LRU_DIM), f32)],
        compiler_params=_cp(),
    )(dycat, cc, xc, proj, hs, hs, cln, wa, wx, vec)


def loss_fwd(y, target):
    B, T, D = y.shape
    tm = _row_tile(T)

    def body(y_ref, t_ref, l_ref, dy_ref):
        @pl.when((pl.program_id(0) == 0) & (pl.program_id(1) == 0))
        def _():
            l_ref[...] = jnp.zeros_like(l_ref)

        d = y_ref[...] - t_ref[...]
        dy_ref[...] = d * (1.0 / D)
        l_ref[...] += jnp.sum(jnp.sum(d * d, axis=1, keepdims=True), axis=0, keepdims=True)

    row = pl.BlockSpec((None, tm, D), lambda b, i: (b, i, 0))
    return pl.pallas_call(
        body, name="loss_fwd", grid=(B, T // tm), in_specs=[row, row],
        out_specs=[pl.BlockSpec((1, 1), lambda b, i: (0, 0)), row],
        out_shape=[jax.ShapeDtypeStruct((1, 1), f32), jax.ShapeDtypeStruct((B, T, D), f32)],
        compiler_params=_cp(),
    )(y, target)


ADA_COL_TILE = 768


def ada_fwd(c_all, w, b):
    L, D, N = w.shape
    nb = c_all.shape[0]
    tn = ADA_COL_TILE

    def body(c_ref, w_ref, b_ref, o_ref):
        o_ref[...] = _dot_hi(_silu(c_ref[...]), w_ref[...]) + b_ref[...]

    return pl.pallas_call(
        body, name="ada_fwd", grid=(L, N // tn),
        in_specs=[pl.BlockSpec((nb, D), lambda l, j: (0, 0)), pl.BlockSpec((None, D, tn), lambda l, j: (l, 0, j)),
                  pl.BlockSpec((None, 1, tn), lambda l, j: (l, 0, j))],
        out_specs=pl.BlockSpec((None, nb, tn), lambda l, j: (l, 0, j)),
        out_shape=jax.ShapeDtypeStruct((L, nb, N), f32),
        compiler_params=_cp(),
    )(c_all, w, b)


def ada_bwd(c_all, dmod_my, dmod_all):
    L, nb, N = dmod_my.shape
    D = c_all.shape[1]
    NA = dmod_all.shape[2]
    tn = ADA_COL_TILE
    nj = N // tn
    ta = NA // nj

    def body(c_ref, dm_ref, da_ref, gw_ref, gb_ref):
        gw_ref[...] = lax.dot_general(_silu(c_ref[...]), dm_ref[...], (((0,), (0,)), ((), ())),
                                      preferred_element_type=f32, precision=lax.Precision.HIGHEST)
        gb_ref[...] = jnp.sum(da_ref[...], axis=0, keepdims=True)

    return pl.pallas_call(
        body, name="ada_bwd", grid=(L, nj),
        in_specs=[pl.BlockSpec((nb, D), lambda l, j: (0, 0)), pl.BlockSpec((None, nb, tn), lambda l, j: (l, 0, j)),
                  pl.BlockSpec((None, nb, ta), lambda l, j: (l, 0, j))],
        out_specs=[pl.BlockSpec((None, D, tn), lambda l, j: (l, 0, j)), pl.BlockSpec((None, 1, ta), lambda l, j: (l, 0, j))],
        out_shape=[jax.ShapeDtypeStruct((L, D, N), f32), jax.ShapeDtypeStruct((L, 1, NA), f32)],
        compiler_params=_cp(),
    )(c_all, dmod_my, dmod_all)


def adamw(w, m, v, gs, offs, tr):
    R, C = w.shape
    ng = len(gs)
    c1 = 1.0 - ADAM_B1 ** ADAM_STEP
    c2 = 1.0 - ADAM_B2 ** ADAM_STEP

    def body(*refs):
        w_ref, m_ref, v_ref = refs[:3]
        g_refs = refs[3:3 + ng]
        d_out, m_out, v_out = refs[3 + ng:]
        g = g_refs[0][...]
        for r in g_refs[1:]:
            g = g + r[...]
        mn = ADAM_B1 * m_ref[...] + (1.0 - ADAM_B1) * g
        vn = ADAM_B2 * v_ref[...] + (1.0 - ADAM_B2) * (g * g)
        m_out[...] = mn
        v_out[...] = vn
        d_out[...] = -ADAM_LR * ((mn / c1) / (jnp.sqrt(vn / c2) + ADAM_EPS) + ADAM_WD * w_ref[...])

    blk = pl.BlockSpec((tr, C), lambda i: (i, 0))
    gspec = lambda off: pl.BlockSpec((tr, C), lambda i: (i + off // tr, 0))
    sds = jax.ShapeDtypeStruct((R, C), f32)
    return pl.pallas_call(
        body, name="adamw", grid=(pl.cdiv(R, tr),), in_specs=[blk, blk, blk] + [gspec(o) for o in offs],
        out_specs=[blk] * 3, out_shape=[sds] * 3, compiler_params=_cp(),
    )(w, m, v, *gs)


def sum_leading(a, tr):
    k, R, C = a.shape

    def body(a_ref, o_ref):
        s = a_ref[0]
        for j in range(1, k):
            s = s + a_ref[j]
        o_ref[...] = s

    return pl.pallas_call(
        body, name="sum_leading", grid=(R // tr,), in_specs=[pl.BlockSpec((k, tr, C), lambda i: (0, i, 0))],
        out_specs=pl.BlockSpec((tr, C), lambda i: (i, 0)), out_shape=jax.ShapeDtypeStruct((R, C), a.dtype),
        compiler_params=_cp(),
    )(a)


ANY = pl.BlockSpec(memory_space=pl.ANY)
CHIP_FLIPS = ((1, 0), (0, 1), (1, 1))
DEV_FLIPS = tuple((fx, fy, fc) for fx in (0, 1) for fy in (0, 1) for fc in (0, 1))[1:]


def _flip(v, f):
    return 1 - v if f else v


def _put(out, v, idx, axis=0):
    return lax.dynamic_update_slice_in_dim(out, jnp.expand_dims(v, axis) if v.ndim < out.ndim else v, idx, axis)


def allgather8(v):
    R, C = v.shape
    n = len(DEV_FLIPS)

    def body(v_ref, o_ref, send_sems, recv_sems):
        x, y, c = lax.axis_index("x"), lax.axis_index("y"), lax.axis_index("c")
        me, sibling = (x, y, c), (x, y, 1 - c)
        chips = [(_flip(x, fx), _flip(y, fy)) for fx, fy in CHIP_FLIPS]

        def copy(k, block, to, own=False):
            px, py, pc = block
            dst = o_ref.at[4 * px + 2 * py + pc]
            return pltpu.make_async_remote_copy(src_ref=v_ref if own else dst, dst_ref=dst, send_sem=send_sems.at[k],
                                                recv_sem=recv_sems.at[k], device_id=to, device_id_type=MESH)

        first = [copy(0, me, sibling, own=True)] + [copy(1 + j, me, (*chip, c), own=True) for j, chip in enumerate(chips)]
        for cp in first:
            cp.start()
        passed = [copy(4 + j, (*chip, c), sibling) for j, chip in enumerate(chips)]
        for j, chip in enumerate(chips):
            copy(1 + j, (*chip, c), me).wait_recv()
            passed[j].start()
        copy(0, sibling, me).wait_recv()
        for j, chip in enumerate(chips):
            copy(4 + j, (*chip, 1 - c), me).wait_recv()
        for cp in first + passed:
            cp.wait_send()

    out = pl.pallas_call(
        body, name="allgather8", in_specs=[ANY], out_specs=ANY, out_shape=jax.ShapeDtypeStruct((N_DEV, R, C), v.dtype),
        scratch_shapes=[pltpu.SemaphoreType.DMA((n,)), pltpu.SemaphoreType.DMA((n,))],
    )(v)
    return _put(out, v, 4 * lax.axis_index("x") + 2 * lax.axis_index("y") + lax.axis_index("c"))


def _half(ref_or_shape0, c):
    hsz = ref_or_shape0 // 2
    return pl.ds(c * hsz, hsz)


def _gather_steps(w_refs, o_refs, send_sems, recv_sems):
    nw, nc = len(w_refs), len(CHIP_FLIPS)
    x, y, c = lax.axis_index("x"), lax.axis_index("y"), lax.axis_index("c")
    me, sibling = 2 * x + y, (x, y, 1 - c)
    peers = [(_flip(x, fx), _flip(y, fy), c) for fx, fy in CHIP_FLIPS]
    slots = [2 * px + py for px, py, _ in peers]

    def copy(a, j, slot, half, to, own=False):
        hs = _half(w_refs[a].shape[0], half)
        return pltpu.make_async_remote_copy(src_ref=w_refs[a].at[hs] if own else o_refs[a].at[slot, hs],
                                            dst_ref=o_refs[a].at[slot, hs], send_sem=send_sems.at[j],
                                            recv_sem=recv_sems.at[j], device_id=to, device_id_type=MESH)

    first = [copy(a, a * nc + k, me, c, peers[k], own=True) for a in range(nw) for k in range(nc)]

    def start():
        for cp in first:
            cp.start()

    passed = [copy(a, nw * nc + a * nc + k, slots[k], c, sibling) for a in range(nw) for k in range(nc)]

    def forward():
        for a in range(nw):
            for k in range(nc):
                copy(a, a * nc + k, slots[k], c, peers[k]).wait_recv()
                passed[a * nc + k].start()

    def finish():
        for a in range(nw):
            for k in range(nc):
                copy(a, nw * nc + a * nc + k, slots[k], 1 - c, sibling).wait_recv()
        for cp in first + passed:
            cp.wait_send()

    return start, forward, finish


def _gather_sems(nw):
    n = 2 * nw * len(CHIP_FLIPS)
    return [pltpu.SemaphoreType.DMA((n,)), pltpu.SemaphoreType.DMA((n,))]


def _gather_out_shapes(ws):
    return [jax.ShapeDtypeStruct((N_CHIPS,) + w.shape, w.dtype) for w in ws]


def _fill_own(outs, ws):
    if not ws:
        return []
    chip =2 * lax.axis_index("x") + lax.axis_index("y")
    return [_put(o, w, chip) for o, w in zip(outs, ws)]


def gather_weights(ws):
    nw = len(ws)

    def body(*refs):
        start, forward, finish = _gather_steps(refs[:nw], refs[nw:2 * nw], *refs[2 * nw:])
        start()
        forward()
        finish()

    outs = pl.pallas_call(
        body, name="gather_weights", in_specs=[ANY] * nw, out_specs=[ANY] * nw,
        out_shape=_gather_out_shapes(ws), scratch_shapes=_gather_sems(nw),
    )(*ws)
    return _fill_own(outs, ws)


def swap_halves(gs):
    nw = len(gs)

    def body(*refs):
        g_refs, t_refs = refs[:nw], refs[nw:2 * nw]
        send_sems, recv_sems = refs[2 * nw:]
        x, y, c = lax.axis_index("x"), lax.axis_index("y"), lax.axis_index("c")
        cps = [pltpu.make_async_remote_copy(src_ref=g_refs[a].at[_half(gs[a].shape[0], 1 - c)], dst_ref=t_refs[a],
                                            send_sem=send_sems.at[a], recv_sem=recv_sems.at[a],
                                            device_id=(x, y, 1 - c), device_id_type=MESH) for a in range(nw)]
        for cp in cps:
            cp.start()
        for cp in cps:
            cp.wait()

    return pl.pallas_call(
        body, name="swap_halves", in_specs=[ANY] * nw, out_specs=[ANY] * nw,
        out_shape=[jax.ShapeDtypeStruct((g.shape[0] // 2,) + g.shape[1:], g.dtype) for g in gs],
        scratch_shapes=[pltpu.SemaphoreType.DMA((nw,)), pltpu.SemaphoreType.DMA((nw,))],
    )(*gs)


def exchange_blocks(ps):
    nw, nc = len(ps), len(CHIP_FLIPS)

    def body(*refs):
        p_refs, r_refs = refs[:nw], refs[nw:2 * nw]
        send_sems, recv_sems = refs[2 * nw:]
        x, y, c = lax.axis_index("x"), lax.axis_index("y"), lax.axis_index("c")
        me = 2 * x + y
        peers = [(_flip(x, fx), _flip(y, fy), c) for fx, fy in CHIP_FLIPS]
        slots = [2 * px + py for px, py, _ in peers]

        def copy(a, k, src_slot, dst_slot):
            return pltpu.make_async_remote_copy(src_ref=p_refs[a].at[src_slot], dst_ref=r_refs[a].at[dst_slot],
                                                send_sem=send_sems.at[a * nc + k], recv_sem=recv_sems.at[a * nc + k],
                                                device_id=peers[k], device_id_type=MESH)

        for a in range(nw):
            for k in range(nc):
                copy(a, k, slots[k], me).start()
        for a in range(nw):
            for k in range(nc):
                copy(a, k, me, slots[k]).wait_recv()
        for a in range(nw):
            for k in range(nc):
                copy(a, k, slots[k], me).wait_send()

    outs = pl.pallas_call(
        body, name="exchange_blocks", in_specs=[ANY] * nw, out_specs=[ANY] * nw,
        out_shape=[jax.ShapeDtypeStruct(p.shape, p.dtype) for p in ps],
        scratch_shapes=[pltpu.SemaphoreType.DMA((nw * nc,)), pltpu.SemaphoreType.DMA((nw * nc,))],
    )(*ps)
    chip = 2 * lax.axis_index("x") + lax.axis_index("y")
    return [_put(o, lax.dynamic_slice_in_dim(p, chip, 1, axis=0), chip) for o, p in zip(outs, ps)]


def join_halves(ss):
    nw = len(ss)

    def body(*refs):
        s_refs, o_refs = refs[:nw], refs[nw:2 * nw]
        send_sems, recv_sems = refs[2 * nw:]
        x, y, c = lax.axis_index("x"), lax.axis_index("y"), lax.axis_index("c")

        def copy(a, half):
            hs = _half(2 * ss[a].shape[0], half)
            return pltpu.make_async_remote_copy(src_ref=s_refs[a], dst_ref=o_refs[a].at[hs], send_sem=send_sems.at[a],
                                                recv_sem=recv_sems.at[a], device_id=(x, y, 1 - c), device_id_type=MESH)

        for a in range(nw):
            copy(a, c).start()
        for a in range(nw):
            copy(a, 1 - c).wait_recv()
        for a in range(nw):
            copy(a, c).wait_send()

    outs = pl.pallas_call(
        body, name="join_halves", in_specs=[ANY] * nw, out_specs=[ANY] * nw,
        out_shape=[jax.ShapeDtypeStruct((2 * s.shape[0],) + s.shape[1:], s.dtype) for s in ss],
        scratch_shapes=[pltpu.SemaphoreType.DMA((nw,)), pltpu.SemaphoreType.DMA((nw,))],
    )(*ss)
    c = lax.axis_index("c")
    return [_put(o, s, c * s.shape[0]) for o, s in zip(outs, ss)]


def _tile_rows(a, b, itemsize=4, budget=4 * 2 ** 20):
    best = 8
    for t in range(8, a + 1, 8):
        if a % t == 0 and t * b * itemsize <= budget:
            best = t
    return best


def add_half(g, t, cidx):
    def body(c_ref, g_ref, t_ref, o_ref):
        o_ref[...] = (g_ref[...] + t_ref[...]).astype(bf16)

    if g.ndim == 5:
        n0, F, _, A, B = g.shape
        hsz, ta = n0 // 2, _tile_rows(A, B)
        nta = A // ta
        grid = (hsz, N_CHIPS, F, nta)
        in_specs = [pl.BlockSpec((None, None, None, ta, B), lambda h, k, f, i, c_ref: (c_ref[0] * hsz + h, f, k, i, 0)),
                    pl.BlockSpec((None, None, None, ta, B), lambda h, k, f, i, c_ref: (h, f, k, i, 0))]
        out_spec = pl.BlockSpec((None, None, ta, B), lambda h, k, f, i, c_ref: (k, h, f * nta + i, 0))
    else:
        n0, _, A, B = g.shape
        F, hsz, ta = 1, n0 // 2, _tile_rows(A, B)
        grid = (hsz, N_CHIPS, A // ta)
        in_specs = [pl.BlockSpec((None, None, ta, B), lambda h, k, i, c_ref: (c_ref[0] * hsz + h, k, i, 0)),
                    pl.BlockSpec((None, None, ta, B), lambda h, k, i, c_ref: (h, k, i, 0))]
        out_spec = pl.BlockSpec((None, None, ta, B), lambda h, k, i, c_ref: (k, h, i, 0))
    return pl.pallas_call(
        body, name="add_half",
        grid_spec=pltpu.PrefetchScalarGridSpec(num_scalar_prefetch=1, grid=grid, in_specs=in_specs, out_specs=out_spec),
        out_shape=jax.ShapeDtypeStruct((N_CHIPS, hsz, F * A, B), bf16),
        compiler_params=_cp(),
    )(cidx, g, t)


def sum_chips(r):
    _, h, A, B = r.shape
    ta = _tile_rows(A, B)

    def body(r_ref, o_ref):
        s = r_ref[0].astype(f32)
        for j in range(1, N_CHIPS):
            s = s + r_ref[j].astype(f32)
        o_ref[...] = s

    return pl.pallas_call(
        body, name="sum_chips", grid=(h, A // ta),
        in_specs=[pl.BlockSpec((N_CHIPS, None, ta, B), lambda hh, i: (0, hh, i, 0))],
        out_specs=pl.BlockSpec((None, ta, B), lambda hh, i: (hh, i, 0)),
        out_shape=jax.ShapeDtypeStruct((h, A, B), f32),
        compiler_params=_cp(),
    )(r)


WEIGHTS = ('ada_w', 'ada_b', 'ln_g', 'ln_b', 'ffn_w_in', 'ffn_w_out', 'ev_w_in', 'ssd_conv_w', 'ssd_conv_b',
           'ssd_dt_bias', 'ssd_a_log', 'ssd_d', 'ssd_norm_g', 'pool_w', 'pool_scale', 'ev_w_out', 'od_w_in',
           'conf_dw_w', 'conf_dw_b', 'conf_ln_g', 'conf_ln_b', 'lru_conv_w', 'lru_conv_b', 'lru_wa', 'lru_ba',
           'lru_wx', 'lru_bx', 'lru_lambda', 'od_w_out')
BIG =('ffn_w_in', 'ffn_w_out', 'ev_w_out', 'od_w_in', 'od_w_out', 'ev_w_in')
SMALL_SHARDED = ('ln_g', 'ln_b', 'ssd_conv_w', 'conf_dw_w', 'conf_dw_b', 'conf_ln_g', 'conf_ln_b', 'lru_conv_w',
                 'lru_conv_b', 'lru_ba', 'lru_bx', 'lru_lambda')
SMALL_REPLICATED = ('ssd_conv_b', 'ssd_dt_bias', 'ssd_a_log', 'ssd_d', 'ssd_norm_g', 'pool_w', 'pool_scale',
                    'lru_wa', 'lru_wx')
PACK_COLS = 1024
BIG_ROW_TILE = 256


def _pack(arrs, row_mult):
    flat = jnp.concatenate([a.reshape(-1) for a in arrs])
    rows = -(-flat.shape[0] // (PACK_COLS * row_mult)) * row_mult
    return jnp.pad(flat, (0, rows * PACK_COLS - flat.shape[0])).reshape(rows, PACK_COLS)


def _unpack(flat, shapes):
    out, off = [], 0
    for s in shapes:
        n = 1
        for d in s:
            n *= d
        out.append(flat[off:off + n].reshape(s))
        off += n
    return out


def _unshard_last(g4):
    m = jnp.moveaxis(g4, 0, -2)
    return m.reshape(m.shape[:-2] + (m.shape[-2] * m.shape[-1],))


def _pad_rows(a, rows):
    return jnp.pad(a, ((0, rows - a.shape[0]),) + ((0, 0),) * (a.ndim - 1))


def _pad_lanes(a):
    return jnp.pad(a, ((0, 0), (0, LANES - a.shape[1])))


def kernel(x, c, ada_w, ada_b, ln_g, ln_b, ffn_w_in, ffn_w_out, ev_w_in, ssd_conv_w, ssd_conv_b, ssd_dt_bias, ssd_a_log, ssd_d, ssd_norm_g, pool_w, pool_scale, ev_w_out, od_w_in, conf_dw_w, conf_dw_b, conf_ln_g, conf_ln_b, lru_conv_w, lru_conv_b, lru_wa, lru_ba, lru_wx, lru_bx, lru_lambda, od_w_out, loss_target, m_ada_w, m_ada_b, m_ln_g, m_ln_b, m_ffn_w_in, m_ffn_w_out, m_ev_w_in, m_ssd_conv_w, m_ssd_conv_b, m_ssd_dt_bias, m_ssd_a_log, m_ssd_d, m_ssd_norm_g, m_pool_w, m_pool_scale, m_ev_w_out, m_od_w_in, m_conf_dw_w, m_conf_dw_b, m_conf_ln_g, m_conf_ln_b, m_lru_conv_w, m_lru_conv_b, m_lru_wa, m_lru_ba, m_lru_wx, m_lru_bx, m_lru_lambda, m_od_w_out, v_ada_w, v_ada_b, v_ln_g, v_ln_b, v_ffn_w_in, v_ffn_w_out, v_ev_w_in, v_ssd_conv_w, v_ssd_conv_b, v_ssd_dt_bias, v_ssd_a_log, v_ssd_d, v_ssd_norm_g, v_pool_w, v_pool_scale, v_ev_w_out, v_od_w_in, v_conf_dw_w, v_conf_dw_b, v_conf_ln_g, v_conf_ln_b, v_lru_conv_w, v_lru_conv_b, v_lru_wa, v_lru_ba, v_lru_wx, v_lru_bx, v_lru_lambda, v_od_w_out):
    given = dict(locals())
    W = {n: given[n] for n in WEIGHTS}
    M = {n: given["m_" + n] for n in WEIGHTS}
    V = {n: given["v_" + n] for n in WEIGHTS}
    B, T, D = x.shape
    L = DEPTH
    chip = 2 * lax.axis_index("x") + lax.axis_index("y")
    dev = 2 * chip + lax.axis_index("c")

    g1 = allgather8(_pack([c] + [W[n] for n in SMALL_SHARDED], 8)).reshape(N_DEV, -1)
    c_all = g1[:, :B * D].reshape(N_DEV * B, D)
    per_chip = g1[0::2, B * D:]
    full = dict(zip(SMALL_SHARDED, [_unshard_last(jnp.stack(p)) for p in zip(*[
        _unpack(per_chip[k], [W[n].shape for n in SMALL_SHARDED]) for k in range(N_CHIPS)])]))
    for n in SMALL_REPLICATED:
        full[n] = W[n]

    n_ada = ada_w.shape[2]
    ada_b_cols = lax.dynamic_slice_in_dim(ada_b, chip * n_ada, n_ada, axis=1)[:, None, :]
    mod_cols = ada_fwd(c_all, ada_w, ada_b_cols)
    g2 = allgather8(mod_cols.reshape(-1, PACK_COLS))[0::2].reshape(N_CHIPS, L, N_DEV * B, n_ada)
    mod_all = jnp.moveaxis(g2, 0, 2).reshape(L, N_DEV * B, N_CHIPS * n_ada)
    mod = lax.dynamic_slice_in_dim(mod_all, dev * B, B, axis=1).reshape(L, B, N_MOD, D)

    FS = ffn_w_in.shape[3]

    def ffn_shards(l, f):
        return [ffn_w_in[l, f].astype(bf16), ffn_w_out[l, f].astype(bf16)]

    def mix_shards(l):
        w_i, w_o = (ev_w_in, ev_w_out) if l % 2 == 0 else (od_w_in, od_w_out)
        return [w_i[l // 2].astype(bf16), w_o[l // 2].astype(bf16)]

    def ffn_weights(g):
        return g[0], g[1].reshape(-1, D)

    def mix_weights(l, g):
        if l % 2 == 0:
            w = _unshard_last(g[0])
            w_i = jnp.concatenate([w[:, :SSD_ZX], w[:, SSD_ZX + SSD_HEADS:], w[:, SSD_ZX:SSD_ZX + SSD_HEADS],
                                   jnp.zeros((D, EVEN_IN_PAD - EVEN_IN), bf16)], axis=1)
        else:
            w_i = jnp.moveaxis(g[0], 0, 1).reshape(D, -1)
        return w_i, g[1].reshape(-1, D)

    saved = []
    xs = x
    next_ffn = gather_weights(ffn_shards(0, 0))
    next_mix = None
    for l in range(L):
        lg, lb = full['ln_g'][l], full['ln_b'][l]
        rec = {}
        m1, m2, m3 = mod[l][:, 0:3], mod[l][:, 3:6], mod[l][:, 6:9]
        w_in_a, w_out_a = ffn_weights(next_ffn)
        carry = ffn_shards(l, 1) + (mix_shards(l) if l == 0 else [])
        xn, h, gu, a, y, got = ffn_fwd(xs, m1, w_in_a, w_out_a, lg[0:1], lb[0:1], carry)
        w_in_b, w_out_b = ffn_weights(got[:2])
        w_in_m, w_out_m = mix_weights(l, got[2:] if l == 0 else next_mix)
        rec['ffa'] = (xs, h, gu, a, y, m1, w_in_a, w_out_a, lg[0:1])
        xs = xn
        if l % 2 == 0:
            e = l // 2
            cw = _pad_rows(full['ssd_conv_w'][e], 8)
            cb = full['ssd_conv_b'][e][None]
            dtb, alog, dsk = (_pad_lanes(full[n][e][None]) for n in ('ssd_dt_bias', 'ssd_a_log', 'ssd_d'))
            ng, pw, ps = full['ssd_norm_g'][e][None], full['pool_w'][e], full['pool_scale'][e][None]
            proj, hm = inproj_fwd(xs, m2, w_in_m)
            cx = dwconv_fwd(proj, XBC_COLB, SSD_CONV_DIM, cw, cb, SSD_CONV)
            ya, hsave = ssd_fwd(proj, cx, dtb, alog, dsk, ng)
            ycat = pool_fwd(proj, POOL_COLB, pw, ps, ya)
            rec['mix'] = (proj, cx, hsave, cw, dtb, alog, dsk, ng, pw, ps)
        else:
            o = l // 2
            dww =_pad_rows(full['conf_dw_w'][o], 32)
            dwb = full['conf_dw_b'][o][None]
            cw = _pad_rows(full['lru_conv_w'][o], 8)
            cb = full['lru_conv_b'][o][None]
            cln = jnp.stack([full['conf_ln_g'][o], full['conf_ln_b'][o]])
            vec = jnp.stack([full['lru_ba'][o], full['lru_bx'][o], full['lru_lambda'][o]])
            wa, wx = full['lru_wa'][o], full['lru_wx'][o]
            proj, hm = inproj_fwd(xs, m2, w_in_m)
            hh = glu_fwd(proj)
            cc = dwconv_fwd(hh, 0, CONF_DIM, dww, dwb, CONF_KERNEL)
            xc = dwconv_fwd(proj, XR_COLB, LRU_DIM, cw, cb, LRU_CONV)
            ycat, hst = lru_fwd(cc, xc, proj, cln, wa, wx, vec)
            rec['mix'] = (proj, hh, cc, xc, hst, dww, cw, cln, wa, wx, vec)
        xn, ym = outproj_fwd(ycat, w_out_m, xs, m2, lg[1:2], lb[1:2])
        rec['mixio'] = (xs, hm, ycat, ym, m2, w_in_m, w_out_m, lg[1:2])
        xs = xn
        carry = ffn_shards(l + 1, 0) + mix_shards(l + 1) if l + 1 < L else []
        xn, h, gu, a, y, got = ffn_fwd(xs, m3, w_in_b, w_out_b, lg[2:3], lb[2:3], carry)
        next_ffn, next_mix = got[:2], got[2:]
        rec['ffb'] = (xs, h, gu, a, y, m3, w_in_b, w_out_b, lg[2:3])
        xs = xn
        saved.append(rec)

    sq, dxs = loss_fwd(xs, loss_target)
    loss = lax.psum(sq[0, 0], ("x", "y", "c")) * (0.5 / D)

    gpart = {n: [None] * W[n].shape[0] for n in WEIGHTS}
    for n in ('ffn_w_in', 'ffn_w_out', 'ev_w_out', 'od_w_in', 'od_w_out'):
        gpart[n] = None
    gpart['ln_g'] = [[None] * 3 for _ in range(L)]
    gpart['ln_b'] = [[None] * 3 for _ in range(L)]
    dmod = [None] * L

    def ffn_back(dxn, rec, l, f):
        xin, h, gu, a, y, m3_, w_in_, w_out_, lg_ = rec
        dx, dgu, dy, dm3, dln = ffn_bwd(dxn, xin, y, gu, m3_, w_in_, w_out_, lg_)
        gpart['ffn_w_in'] = wgrad_into(h, dgu, FS, (L, 2), (l, f), gpart['ffn_w_in'])
        gpart['ffn_w_out'] = wgrad_into(a, dy, D, (L, 2), (l, f), gpart['ffn_w_out'])
        gpart['ln_g'][l][2 * f] = dln[0]
        gpart['ln_b'][l][2 * f] = dln[1]
        return dx, dm3

    for l in reversed(range(L)):
        rec = saved[l]
        dxs, dm3 = ffn_back(dxs, rec['ffb'], l, 1)
        xin, hm, ycat, ym, m2, w_in_m, w_out_m, lg_ = rec['mixio']
        dxp, dycat, dy, dg2, dln = outproj_bwd(dxs, xin, ym, m2, w_out_m, lg_)
        gpart['ln_g'][l][1] = dln[0]
        gpart['ln_b'][l][1] = dln[1]
        n_out = 'ev_w_out' if l % 2 == 0 else 'od_w_out'
        gpart[n_out] = wgrad_into(ycat, dy, D, (L // 2,), (l // 2,), gpart[n_out])
        if l % 2 == 0:
            e = l // 2
            proj, cx, hsave, cw, dtb, alog, dsk, ng, pw, ps = rec['mix']
            dz, dcx, ddt, gsm, gng = ssd_bwd(dycat, proj, cx, hsave, dtb, alog, dsk, ng)
            dxbc, dcw, dcb = dwconv_bwd(dcx, proj, XBC_COLB, SSD_CONV_DIM, cw, SSD_CONV, bf16)
            du, dpw, dps = pool_bwd(dycat, POOL_DY_COLB, proj, POOL_COLB, pw, ps)
            dproj = jnp.concatenate([dz, dxbc, du, ddt], axis=-1)
            gwp = wgrad(hm, dproj, EVEN_IN_PAD)[0]
            dt0 = SSD_ZX + POOL_DIM
            gw = jnp.concatenate([gwp[:, :SSD_ZX], gwp[:, dt0:dt0 + SSD_HEADS], gwp[:, SSD_ZX:dt0]], axis=1)
            gpart['ev_w_in'][e] = jnp.moveaxis(gw.reshape(D, N_CHIPS, -1), 1, 0)
            gpart['ssd_conv_w'][e], gpart['ssd_conv_b'][e] = dcw[:SSD_CONV], dcb[0]
            gpart['ssd_dt_bias'][e], gpart['ssd_a_log'][e], gpart['ssd_d'][e] = (gsm[k, :SSD_HEADS] for k in range(3))
            gpart['ssd_norm_g'][e], gpart['pool_w'][e], gpart['pool_scale'][e] = gng[0], dpw, dps[0]
        else:
            o = l // 2
            proj, hh, cc, xc, hst, dww, cw, cln, wa, wx, vec = rec['mix']
            dcc, dxc, dgr, dcln, dwa, dwx, dvec = lru_bwd(dycat, cc, xc, proj, hst, cln, wa, wx, vec)
            dhh, ddw, ddb = dwconv_bwd(dcc, hh, 0, CONF_DIM, dww, CONF_KERNEL, f32)
            dvg = glu_bwd(dhh, proj)
            dxr, dcw, dcb = dwconv_bwd(dxc, proj, XR_COLB, LRU_DIM, cw, LRU_CONV, bf16)
            dproj = jnp.concatenate([dvg, dxr, dgr], axis=-1)
            gpart['od_w_in'] = wgrad_into(hm, dproj, dproj.shape[-1] // N_CHIPS, (L // 2,), (o,), gpart['od_w_in'])
            gpart['conf_dw_w'][o], gpart['conf_dw_b'][o] = ddw[:CONF_KERNEL], ddb[0]
            gpart['conf_ln_g'][o], gpart['conf_ln_b'][o] = dcln[0], dcln[1]
            gpart['lru_conv_w'][o], gpart['lru_conv_b'][o] = dcw[:LRU_CONV], dcb[0]
            gpart['lru_wa'][o], gpart['lru_wx'][o] = dwa, dwx
            gpart['lru_ba'][o], gpart['lru_bx'][o], gpart['lru_lambda'][o] = dvec[0], dvec[1], dvec[2]
        dxs, dm2 = inproj_bwd(dproj, w_in_m, xin, m2, dxp)
        dxs, dm1 = ffn_back(dxs, rec['ffa'], l, 0)
        dmod[l] = jnp.concatenate([dm1, dm2, dg2, dm3], axis=1)
    grad_x = dxs

    def stack(v):
        return jnp.stack([stack(u) if isinstance(u, list) else u for u in v])

    def per_chip(n):
        if isinstance(gpart[n], list):
            return jnp.stack(gpart[n])
        lead = W[n].ndim - 2
        return gpart[n].reshape(W[n].shape[:lead] + (N_CHIPS,) + W[n].shape[lead:])

    gfull = [per_chip(n) for n in BIG]
    cidx = lax.axis_index("c").astype(jnp.int32).reshape(1)
    summed = [add_half(g, t, cidx) for g, t in zip(gfull, swap_halves(gfull))]
    reduced = join_halves([sum_chips(r) for r in exchange_blocks(summed)])

    out_g, out_d, out_m, out_v = {}, {}, {}, {}
    for n, g in zip(BIG, reduced):
        shp = W[n].shape
        as2d = lambda a: a.reshape(-1, shp[-1])
        res = adamw(as2d(W[n]), as2d(M[n]), as2d(V[n]), [as2d(g)], [0], BIG_ROW_TILE)
        out_g[n] = g.reshape(shp)
        out_d[n], out_m[n], out_v[n] = (r.reshape(shp) for r in res)

    small = SMALL_SHARDED + SMALL_REPLICATED
    dmod_flat = stack(dmod).reshape(L, B, N_MOD * D)
    g3 = allgather8(_pack([dmod_flat] + [stack(gpart[n]) for n in small], 64))
    n_dmod = L * B * N_MOD * D
    dmod_all = jnp.moveaxis(g3.reshape(N_DEV, -1)[:, :n_dmod].reshape(N_DEV, L, B, N_MOD * D), 0, 1).reshape(L, N_DEV * B, N_MOD * D)
    ssum = sum_leading(g3, 64).reshape(-1)[n_dmod:]
    gsmall = dict(zip(small, _unpack(ssum, [full[n].shape for n in small])))
    for n in SMALL_SHARDED:
        wdt = W[n].shape[-1]
        gsmall[n] = lax.dynamic_slice_in_dim(gsmall[n], chip * wdt, wdt, axis=gsmall[n].ndim - 1)
    dmod_my = lax.dynamic_slice_in_dim(dmod_all, chip * n_ada, n_ada, axis=2)
    g_ada_w, g_ada_b = ada_bwd(c_all, dmod_my, dmod_all)
    gsmall['ada_b'] = g_ada_b[:, 0, :]

    res = adamw(ada_w.reshape(-1, n_ada), M['ada_w'].reshape(-1, n_ada), V['ada_w'].reshape(-1, n_ada),
                [g_ada_w.reshape(-1, n_ada)], [0], BIG_ROW_TILE)
    out_g['ada_w'] = g_ada_w
    out_d['ada_w'], out_m['ada_w'], out_v['ada_w'] = (r.reshape(ada_w.shape) for r in res)

    names = ('ada_b',) + small
    shapes = [W[n].shape for n in names]
    res = adamw(_pack([W[n] for n in names], 64), _pack([M[n] for n in names], 64), _pack([V[n] for n in names], 64),
                [_pack([gsmall[n] for n in names], 64)], [0], 64)
    out_g.update({n: gsmall[n] for n in names})
    for dst, r in zip((out_d, out_m, out_v), res):
        dst.update(zip(names, _unpack(r.reshape(-1), shapes)))

    return (loss, grad_x, *[out_g[n] for n in WEIGHTS], *[out_d[n] for n in WEIGHTS], *[out_m[n] for n in WEIGHTS],
            *[out_v[n] for n in WEIGHTS])
```

```python
import jax
import jax.numpy as jnp
from jax import lax
from jax.experimental import pallas as pl
from jax.experimental.pallas import tpu as pltpu

f32 = jnp.float32
bf16 = jnp.bfloat16

DEPTH = 4
D_MODEL = 1024
N_MOD = 9
DN_ALPHA = (2.0 * DEPTH) ** 0.25
NORM_EPS = 1e-5
SSD_CHUNK = 128
SSD_D_INNER = 1024
SSD_CONV_DIM = 1536
SSD_HEADS = 16
POOL_WINDOWS = (2, 4, 8, 16)
POOL_DIM = 512
EVEN_IN = 3088
EVEN_IN_PAD = 3200
CONF_DIM = 512
CONF_KERNEL = 31
LRU_DIM = 1024
LRU_HEADS = 8
LRU_CONV = 4
SSD_CONV = 4
LRU_C = 8.0
ADAM_LR = 0.001
ADAM_B1 = 0.9
ADAM_B2 = 0.999
ADAM_EPS = 1e-08
ADAM_WD = 0.01
ADAM_STEP = 10

SSD_ZX = SSD_D_INNER + SSD_CONV_DIM
LANES = 128
VMEM_LIMIT_BYTES = 56 * 2 ** 20
COL_TILE = 512
XBC_COLB = SSD_D_INNER // COL_TILE
XR_COLB = 2 * CONF_DIM // COL_TILE
POOL_COLB = SSD_ZX // POOL_DIM
POOL_DY_COLB = SSD_D_INNER // POOL_DIM
N_CHIPS = 4
N_DEV = 8
MESH = pl.DeviceIdType.MESH


def _cp():
    return pltpu.CompilerParams(vmem_limit_bytes=VMEM_LIMIT_BYTES)


def _dot(a, b):
    return jnp.dot(a, b, preferred_element_type=f32)


def _dot_nt(a, b):
    return lax.dot_general(a, b, (((1,), (1,)), ((), ())), preferred_element_type=f32)


def _dot_tn(a, b):
    return lax.dot_general(a, b, (((0,), (0,)), ((), ())), preferred_element_type=f32)


def _dot_hi(a, b):
    return jnp.dot(a, b, preferred_element_type=f32, precision=lax.Precision.HIGHEST)


def _silu(x):
    return x * jax.nn.sigmoid(x)


def _ln_stats(z):
    mu = jnp.mean(z, axis=-1, keepdims=True)
    zc = z - mu
    var = jnp.mean(zc * zc, axis=-1, keepdims=True)
    rstd = lax.rsqrt(var + NORM_EPS)
    return zc * rstd, rstd


def _ln_bwd(dxn, xhat, rstd, lg):
    dxh = dxn * lg
    return rstd * (dxh - jnp.mean(dxh, axis=-1, keepdims=True) - xhat * jnp.mean(dxh * xhat, axis=-1, keepdims=True))


def _const_spec(shape):
    nd = len(shape)
    return pl.BlockSpec(shape, lambda *_: (0,) * nd, pipeline_mode=pl.Buffered(1))


def _row_tile(t, want=256):
    return min(want, t)


def ffn_fwd(x, mod3, w_in, w_out, lg, lb, carry=()):
    B, T, D = x.shape
    FS = w_in.shape[2]
    tm = _row_tile(T, 512)
    nt = T // tm
    nw = len(carry)

    def body(x_ref, mod_ref, win_ref, wout_ref, lg_ref, lb_ref, *rest):
        xn_ref, h_ref, gu_ref, a_ref, y_ref = rest[nw:nw + 5]
        if nw:
            start, forward, finish = _gather_steps(rest[:nw], rest[nw + 5:2 * nw + 5], *rest[2 * nw + 5:])
            b, i = pl.program_id(0), pl.program_id(1)
            pl.when((b == 0) & (i == 0))(start)
        xv = x_ref[...]
        sh, sc, g = mod_ref[0:1, :], mod_ref[1:2, :], mod_ref[2:3, :]
        h = (xv * (1.0 + sc) + sh).astype(bf16)
        h_ref[...] = h
        acc = jnp.zeros((tm, D), f32)
        for s in range(2):
            gate = _dot(h, win_ref[s])
            up = _dot(h, win_ref[s + 2])
            gu_ref[:, s * FS:(s + 1) * FS] = gate.astype(bf16)
            gu_ref[:, (s + 2) * FS:(s + 3) * FS] = up.astype(bf16)
            a = (_silu(gate) * up).astype(bf16)
            a_ref[:, s * FS:(s + 1) * FS] = a
            acc = acc + _dot(a, wout_ref[s * FS:(s + 1) * FS, :])
        y_ref[...] = acc
        xhat, _ = _ln_stats(DN_ALPHA * xv + 0.5 * (1.0 + g) * acc)
        xn_ref[...] = xhat * lg_ref[...] + lb_ref[...]
        if nw:
            pl.when(b * nt + i == (3 * B * nt) // 4)(forward)
            pl.when((b == B - 1) & (i == nt - 1))(finish)

    row = lambda w: pl.BlockSpec((None, tm, w), lambda b, i: (b, i, 0))
    res = pl.pallas_call(
        body, name="ffn_fwd_gather" if nw else "ffn_fwd", grid=(B, nt),
        in_specs=[row(D), pl.BlockSpec((None, 3, D), lambda b, i: (b, 0, 0)), _const_spec(w_in.shape),
                  _const_spec(w_out.shape), _const_spec((1, D)), _const_spec((1, D))] + [ANY] * nw,
        out_specs=[row(D), row(D), row(4 * FS), row(2 * FS), row(D)] + [ANY] * nw,
        out_shape=[jax.ShapeDtypeStruct((B, T, D), f32), jax.ShapeDtypeStruct((B, T, D), bf16),
                   jax.ShapeDtypeStruct((B, T, 4 * FS), bf16), jax.ShapeDtypeStruct((B, T, 2 * FS), bf16),
                   jax.ShapeDtypeStruct((B, T, D), f32)] + _gather_out_shapes(carry),
        scratch_shapes=_gather_sems(nw) if nw else [],
        compiler_params=_cp(),
    )(x, mod3, w_in, w_out, lg, lb, *carry)
    return tuple(res[:5]) + (_fill_own(res[5:], carry),)


def ffn_bwd(dxn, x, y, gu, mod3, w_in, w_out, lg):
    B, T, D = x.shape
    FS = w_in.shape[2]
    tm = _row_tile(T)

    def body(dxn_ref, x_ref, y_ref, gu_ref, mod_ref, win_ref, wout_ref, lg_ref,
             dx_ref, dgu_ref, dy_ref, dmod_ref, dln_ref):
        b, i = pl.program_id(0), pl.program_id(1)

        @pl.when((b == 0) & (i == 0))
        def _():
            dln_ref[...] = jnp.zeros_like(dln_ref)

        @pl.when(i == 0)
        def _():
            dmod_ref[...] = jnp.zeros_like(dmod_ref)

        xv, yv, dxn_v = x_ref[...], y_ref[...], dxn_ref[...]
        sc, g = mod_ref[1:2, :], mod_ref[2:3, :]
        xhat, rstd = _ln_stats(DN_ALPHA * xv + 0.5 * (1.0 + g) * yv)
        dln_ref[0:1, :] += jnp.sum(dxn_v * xhat, axis=0, keepdims=True)
        dln_ref[1:2, :] += jnp.sum(dxn_v, axis=0, keepdims=True)
        dz = _ln_bwd(dxn_v, xhat, rstd, lg_ref[...])
        dmod_ref[2:3, :] += jnp.sum(0.5 * dz * yv, axis=0, keepdims=True)
        dy = (0.5 * (1.0 + g) * dz).astype(bf16)
        dy_ref[...] = dy
        dh = jnp.zeros((tm, D), f32)
        for s in range(2):
            da = _dot_nt(dy, wout_ref[s * FS:(s + 1) * FS, :]).astype(bf16)
            gate = gu_ref[:, s * FS:(s + 1) * FS]
            up = gu_ref[:, (s + 2) * FS:(s + 3) * FS]
            sig = jax.nn.sigmoid(gate)
            dgate = da * up * (sig * (1.0 + gate * (1.0 - sig)))
            dup = da * (gate * sig)
            dgu_ref[:, s * FS:(s + 1) * FS] = dgate
            dgu_ref[:, (s + 2) * FS:(s + 3) * FS] = dup
            dh = dh + _dot_nt(dgate, win_ref[s]) + _dot_nt(dup, win_ref[s + 2])
        dx_ref[...] = DN_ALPHA * dz + dh * (1.0 + sc)
        dmod_ref[0:1, :] += jnp.sum(dh, axis=0, keepdims=True)
        dmod_ref[1:2, :] += jnp.sum(dh * xv, axis=0, keepdims=True)

    row = lambda w: pl.BlockSpec((None, tm, w), lambda b, i: (b, i, 0))
    return pl.pallas_call(
        body, name="ffn_bwd", grid=(B, T // tm),
        in_specs=[row(D), row(D), row(D), row(4 * FS), pl.BlockSpec((None, 3, D), lambda b, i: (b, 0, 0)),
                  _const_spec(w_in.shape), _const_spec(w_out.shape), _const_spec((1, D))],
        out_specs=[row(D), row(4 * FS), row(D), pl.BlockSpec((None, 3, D), lambda b, i: (b, 0, 0)),
                   pl.BlockSpec((2, D), lambda b, i: (0, 0))],
        out_shape=[jax.ShapeDtypeStruct((B, T, D), f32), jax.ShapeDtypeStruct((B, T, 4 * FS), bf16),
                   jax.ShapeDtypeStruct((B, T, D), bf16), jax.ShapeDtypeStruct((B, 3, D), f32),
                   jax.ShapeDtypeStruct((2, D), f32)],
        compiler_params=_cp(),
    )(dxn, x, y, gu, mod3, w_in, w_out, lg)


def wgrad(a, b, tn):
    B, T, K = a.shape
    N = b.shape[2]
    tr = _row_tile(T, 1024 if K * tn <= 1024 * 1536 else 512)

    def body(a_ref, b_ref, o_ref):
        @pl.when((pl.program_id(1) == 0) & (pl.program_id(2) == 0))
        def _():
            o_ref[...] = jnp.zeros_like(o_ref)

        o_ref[...] += _dot_tn(a_ref[...], b_ref[...])

    return pl.pallas_call(
        body, name="wgrad", grid=(N // tn, B, T // tr),
        in_specs=[pl.BlockSpec((None, tr, K), lambda s, b, r: (b, r, 0)),
                  pl.BlockSpec((None, tr, tn), lambda s, b, r: (b, r, s))],
        out_specs=pl.BlockSpec((None, K, tn), lambda s, b, r: (s, 0, 0)),
        out_shape=jax.ShapeDtypeStruct((N // tn, K, tn), f32),
        compiler_params=_cp(),
    )(a, b)


def wgrad_into(a, b, tn, lead, pos, buf=None):
    B, T, K = a.shape
    N = b.shape[2]
    tr = _row_tile(T, 1024 if K * tn <= 1024 * 1536 else 512)
    nl = len(lead)

    def body(pos_ref, a_ref, b_ref, *rest):
        o_ref = rest[-1]

        @pl.when((pl.program_id(1) == 0) & (pl.program_id(2) == 0))
        def _():
            o_ref[...] = jnp.zeros_like(o_ref)

        o_ref[...] += _dot_tn(a_ref[...], b_ref[...])

    return pl.pallas_call(
        body, name="wgrad_into",
        grid_spec=pltpu.PrefetchScalarGridSpec(
            num_scalar_prefetch=1, grid=(N // tn, B, T // tr),
            in_specs=[pl.BlockSpec((None, tr, K), lambda s, b, r, p: (b, r, 0)),
                      pl.BlockSpec((None, tr, tn), lambda s, b, r, p: (b, r, s))] + ([] if buf is None else [ANY]),
            out_specs=pl.BlockSpec((None,) * (nl + 1) + (K, tn),
                                   lambda s, b, r, p: tuple(p[j] for j in range(nl)) + (s, 0, 0))),
        out_shape=jax.ShapeDtypeStruct(tuple(lead) + (N // tn, K, tn), f32),
        input_output_aliases={} if buf is None else {3: 0},
        compiler_params=_cp(),
    )(jnp.asarray(pos, jnp.int32), a, b, *([] if buf is None else [buf]))


def inproj_fwd(x, mod3, w):
    B, T, D = x.shape
    N = w.shape[1]
    tm = _row_tile(T, 512)

    def body(x_ref, mod_ref, w_ref, p_ref, h_ref):
        h = (x_ref[...] * (1.0 + mod_ref[1:2, :]) + mod_ref[0:1, :]).astype(bf16)
        h_ref[...] = h
        p_ref[...] = _dot(h, w_ref[...])

    row = lambda n: pl.BlockSpec((None, tm, n), lambda b, i: (b, i, 0))
    return pl.pallas_call(
        body, name="inproj_fwd", grid=(B, T // tm),
        in_specs=[row(D), pl.BlockSpec((None, 3, D), lambda b, i: (b, 0, 0)), _const_spec(w.shape)],
        out_specs=[row(N), row(D)],
        out_shape=[jax.ShapeDtypeStruct((B, T, N), f32), jax.ShapeDtypeStruct((B, T, D), bf16)],
        compiler_params=_cp(),
    )(x, mod3, w)


def inproj_bwd(dproj, w, x, mod3, dxp):
    B, T, D = x.shape
    N = w.shape[1]
    tm = _row_tile(T, 512)

    def body(dp_ref, w_ref, x_ref, mod_ref, dxp_ref, dx_ref, dmod_ref):
        @pl.when(pl.program_id(1) == 0)
        def _():
            dmod_ref[...] = jnp.zeros_like(dmod_ref)

        dh = _dot_nt(dp_ref[...], w_ref[...])
        dx_ref[...] = dxp_ref[...] + dh * (1.0 + mod_ref[1:2, :])
        dmod_ref[0:1, :] += jnp.sum(dh, axis=0, keepdims=True)
        dmod_ref[1:2, :] += jnp.sum(dh * x_ref[...], axis=0, keepdims=True)

    row = lambda n: pl.BlockSpec((None, tm, n), lambda b, i: (b, i, 0))
    return pl.pallas_call(
        body, name="inproj_bwd", grid=(B, T // tm),
        in_specs=[row(N), _const_spec(w.shape), row(D), pl.BlockSpec((None, 3, D), lambda b, i: (b, 0, 0)), row(D)],
        out_specs=[row(D), pl.BlockSpec((None, 2, D), lambda b, i: (b, 0, 0))],
        out_shape=[jax.ShapeDtypeStruct((B, T, D), f32), jax.ShapeDtypeStruct((B, 2, D), f32)],
        compiler_params=_cp(),
    )(dproj, w, x, mod3, dxp)


def outproj_fwd(ycat, w, x, mod3, lg, lb):
    B, T, D = x.shape
    E = w.shape[0]
    tm = _row_tile(T, 512)

    def body(yc_ref, w_ref, x_ref, mod_ref, lg_ref, lb_ref, xn_ref, y_ref):
        yv = _dot(yc_ref[...], w_ref[...])
        y_ref[...] = yv
        xhat, _ = _ln_stats(DN_ALPHA * x_ref[...] + (1.0 + mod_ref[2:3, :]) * yv)
        xn_ref[...] = xhat * lg_ref[...] + lb_ref[...]

    row = lambda n: pl.BlockSpec((None, tm, n), lambda b, i: (b, i, 0))
    return pl.pallas_call(
        body, name="outproj_fwd", grid=(B, T // tm),
        in_specs=[row(E), _const_spec(w.shape), row(D), pl.BlockSpec((None, 3, D), lambda b, i: (b, 0, 0)),
                  _const_spec((1, D)), _const_spec((1, D))],
        out_specs=[row(D), row(D)],
        out_shape=[jax.ShapeDtypeStruct((B, T, D), f32), jax.ShapeDtypeStruct((B, T, D), f32)],
        compiler_params=_cp(),
    )(ycat, w, x, mod3, lg, lb)


def outproj_bwd(dxn, x, y, mod3, w, lg):
    B, T, D = x.shape
    E = w.shape[0]
    tm = _row_tile(T, 512)

    def body(dxn_ref, x_ref, y_ref, mod_ref, w_ref, lg_ref, dxp_ref, dyc_ref, dy_ref, dg_ref, dln_ref):
        b, i = pl.program_id(0), pl.program_id(1)

        @pl.when((b == 0) & (i == 0))
        def _():
            dln_ref[...] = jnp.zeros_like(dln_ref)

        @pl.when(i == 0)
        def _():
            dg_ref[...] = jnp.zeros_like(dg_ref)

        xv, yv, dxn_v = x_ref[...], y_ref[...], dxn_ref[...]
        g = mod_ref[2:3, :]
        xhat, rstd = _ln_stats(DN_ALPHA * xv + (1.0 + g) * yv)
        dln_ref[0:1, :] += jnp.sum(dxn_v * xhat, axis=0, keepdims=True)
        dln_ref[1:2, :] += jnp.sum(dxn_v, axis=0, keepdims=True)
        dz = _ln_bwd(dxn_v, xhat, rstd, lg_ref[...])
        dg_ref[...] += jnp.sum(dz * yv, axis=0, keepdims=True)
        dy = ((1.0 + g) * dz).astype(bf16)
        dy_ref[...] = dy
        dxp_ref[...] = DN_ALPHA * dz
        dyc_ref[...] = _dot_nt(dy, w_ref[...])

    row = lambda n: pl.BlockSpec((None, tm, n), lambda b, i: (b, i, 0))
    return pl.pallas_call(
        body, name="outproj_bwd", grid=(B, T // tm),
        in_specs=[row(D), row(D), row(D), pl.BlockSpec((None, 3, D), lambda b, i: (b, 0, 0)), _const_spec(w.shape),
                  _const_spec((1, D))],
        out_specs=[row(D), row(E), row(D), pl.BlockSpec((None, 1, D), lambda b, i: (b, 0, 0)),
                   pl.BlockSpec((2, D), lambda b, i: (0, 0))],
        out_shape=[jax.ShapeDtypeStruct((B, T, D), f32), jax.ShapeDtypeStruct((B, T, E), f32),
                   jax.ShapeDtypeStruct((B, T, D), bf16), jax.ShapeDtypeStruct((B, 1, D), f32),
                   jax.ShapeDtypeStruct((2, D), f32)],
        compiler_params=_cp(),
    )(dxn, x, y, mod3, w, lg)


def _halo_rows(K):
    return 8 if K <= 9 else 32


def dwconv_fwd(x, col0, C, w, b, K):
    B, T, _ = x.shape
    tc, hp = COL_TILE, _halo_rows(K)
    tm = _row_tile(T, 1024 if K <= 9 else 512)
    r = tm // hp

    def body(xh_ref, x_ref, w_ref, b_ref, o_ref):
        halo = jnp.where(pl.program_id(2) == 0, 0.0, xh_ref[...])
        xe = jnp.concatenate([halo, x_ref[...]], axis=0)
        acc = jnp.zeros((tm, tc), f32) + b_ref[...]
        for k in range(K):
            sft = K - 1 - k
            xs = xe if sft == 0 else pltpu.roll(xe, sft, 0)
            acc = acc + xs[hp:, :] * w_ref[k:k + 1, :]
        o_ref[...] = acc

    return pl.pallas_call(
        body, name=f"dwconv{K}_fwd", grid=(C // tc, B, T // tm),
        in_specs=[pl.BlockSpec((None, hp, tc), lambda j, b, i: (b, jnp.maximum(i * r - 1, 0), col0 + j)),
                  pl.BlockSpec((None, tm, tc), lambda j, b, i: (b, i, col0 + j)),
                  pl.BlockSpec((w.shape[0], tc), lambda j, b, i: (0, j)),
                  pl.BlockSpec((1, tc), lambda j, b, i: (0, j))],
        out_specs=pl.BlockSpec((None, tm, tc), lambda j, b, i: (b, i, j)),
        out_shape=jax.ShapeDtypeStruct((B, T, C), f32),
        compiler_params=_cp(),
    )(x, x, w, b)


def dwconv_bwd(dc, x, col0, C, w, K, out_dtype):
    B, T, _ = x.shape
    tc, hp = COL_TILE, _halo_rows(K)
    tm = _row_tile(T, 1024 if K <= 9 else 512)
    r = tm // hp
    nt = T // tm
    n = tm + hp
    KP = w.shape[0]

    def body(dcn_ref, dc_ref, x_ref, w_ref, dx_ref, dw_ref, db_ref):
        b, i = pl.program_id(1), pl.program_id(2)

        @pl.when((b == 0) & (i == 0))
        def _():
            dw_ref[...] = jnp.zeros_like(dw_ref)
            db_ref[...] = jnp.zeros_like(db_ref)

        dcv, xv = dc_ref[...], x_ref[...]
        de = jnp.concatenate([dcv, jnp.where(i == nt - 1, 0.0, dcn_ref[...])], axis=0)
        acc = jnp.zeros((tm, tc), f32)
        for k in range(K):
            j = K - 1 - k
            ds = (de if j == 0 else pltpu.roll(de, n - j, 0))[:tm, :]
            acc = acc + ds * w_ref[k:k + 1, :]
            dw_ref[k:k + 1, :] += jnp.sum(ds * xv, axis=0, keepdims=True)
        dx_ref[...] = acc.astype(out_dtype)
        db_ref[...] += jnp.sum(dcv, axis=0, keepdims=True)

    return pl.pallas_call(
        body, name=f"dwconv{K}_bwd", grid=(C // tc, B, nt),
        in_specs=[pl.BlockSpec((None, hp, tc), lambda j, b, i: (b, jnp.minimum((i + 1) * r, T // hp - 1), j)),
                  pl.BlockSpec((None, tm, tc), lambda j, b, i: (b, i, j)),
                  pl.BlockSpec((None, tm, tc), lambda j, b, i: (b, i, col0 + j)),
                  pl.BlockSpec((KP, tc), lambda j, b, i: (0, j))],
        out_specs=[pl.BlockSpec((None, tm, tc), lambda j, b, i: (b, i, j)),
                   pl.BlockSpec((KP, tc), lambda j, b, i: (0, j)),
                   pl.BlockSpec((1, tc), lambda j, b, i: (0, j))],
        out_shape=[jax.ShapeDtypeStruct((B, T, C), out_dtype), jax.ShapeDtypeStruct((KP, C), f32),
                   jax.ShapeDtypeStruct((1, C), f32)],
        compiler_params=_cp(),
    )(dc, dc, x, w)


POOL_HALO = 16


def _pool_windows(ue, pos, hp):
    out = []
    for g, wd in enumerate(POOL_WINDOWS):
        ug = ue[:, g * LANES:(g + 1) * LANES]
        s, span = ug, 1
        while span < wd:
            s = s + pltpu.roll(s, span, 0)
            span *= 2
        cnt = jnp.minimum(pos + 1, wd).astype(f32)
        out.append(s[hp:, :] / cnt - ug[hp:, :])
    return out


def pool_fwd(proj, colb, w, scale, ybuf):
    B, T, _ = proj.shape
    hp = POOL_HALO
    tm = _row_tile(T, 512)
    r = tm // hp

    def body(uh_ref, u_ref, w_ref, sc_ref, ybuf_ref, o_ref):
        i = pl.program_id(1)
        ue = jnp.concatenate([jnp.where(i == 0, 0.0, uh_ref[...]), u_ref[...]], axis=0)
        pos = i * tm + lax.broadcasted_iota(jnp.int32, (tm, 1), 0)
        ps = _pool_windows(ue, pos, hp)
        o = jnp.concatenate([_dot(ps[g], w_ref[g]) for g in range(4)], axis=1) * sc_ref[...]
        o_ref[...] = o.astype(bf16)

    return pl.pallas_call(
        body, name="pool_fwd", grid=(B, T // tm),
        in_specs=[pl.BlockSpec((None, hp, POOL_DIM), lambda b, i: (b, jnp.maximum(i * r - 1, 0), colb)),
                  pl.BlockSpec((None, tm, POOL_DIM), lambda b, i: (b, i, colb)),
                  _const_spec(w.shape), _const_spec((1, POOL_DIM)), pl.BlockSpec(memory_space=pl.ANY)],
        out_specs=pl.BlockSpec((None, tm, POOL_DIM), lambda b, i: (b, i, ybuf.shape[2] // POOL_DIM - 1)),
        out_shape=jax.ShapeDtypeStruct(ybuf.shape, bf16),
        input_output_aliases={4: 0},
        compiler_params=_cp(),
    )(proj, proj, w, scale, ybuf)


def pool_bwd(dycat, dcolb, proj, colb, w, scale):
    B, T, _ = proj.shape
    hp = POOL_HALO
    tm = _row_tile(T, 512)
    r = tm // hp
    nt = T // tm
    n = tm + hp

    def body(dyn_ref, dy_ref, uh_ref, u_ref, w_ref, sc_ref, du_ref, dw_ref, dsc_ref):
        b, i = pl.program_id(0), pl.program_id(1)

        @pl.when((b == 0) & (i == 0))
        def _():
            dw_ref[...] = jnp.zeros_like(dw_ref)
            dsc_ref[...] = jnp.zeros_like(dsc_ref)

        dyv = dy_ref[...]
        dye = jnp.concatenate([dyv, jnp.where(i == nt - 1, 0.0, dyn_ref[...])], axis=0)
        ue = jnp.concatenate([jnp.where(i == 0, 0.0, uh_ref[...]), u_ref[...]], axis=0)
        pos = i * tm + lax.broadcasted_iota(jnp.int32, (tm, 1), 0)
        pos_e = i * tm + lax.broadcasted_iota(jnp.int32, (n, 1), 0)
        ps = _pool_windows(ue, pos, hp)
        dme = dye * sc_ref[...]
        dus, dscs = [], []
        for g, wd in enumerate(POOL_WINDOWS):
            sl = slice(g * LANES, (g + 1) * LANES)
            dscs.append(jnp.sum(dyv[:, sl] * _dot(ps[g], w_ref[g]), axis=0, keepdims=True))
            dw_ref[g] += _dot_tn(ps[g], dme[:tm, sl])
            dpe = _dot_nt(dme[:, sl], w_ref[g])
            s, span = dpe / jnp.minimum(pos_e + 1, wd).astype(f32), 1
            while span < wd:
                s = s + pltpu.roll(s, n - span, 0)
                span *= 2
            dus.append(s[:tm, :] - dpe[:tm, :])
        du_ref[...] = jnp.concatenate(dus, axis=1).astype(bf16)
        dsc_ref[...] += jnp.concatenate(dscs, axis=1)

    return pl.pallas_call(
        body, name="pool_bwd", grid=(B, nt),
        in_specs=[pl.BlockSpec((None, hp, POOL_DIM), lambda b, i: (b, jnp.minimum((i + 1) * r, T // hp - 1), dcolb)),
                  pl.BlockSpec((None, tm, POOL_DIM), lambda b, i: (b, i, dcolb)),
                  pl.BlockSpec((None, hp, POOL_DIM), lambda b, i: (b, jnp.maximum(i * r - 1, 0), colb)),
                  pl.BlockSpec((None, tm, POOL_DIM), lambda b, i: (b, i, colb)),
                  _const_spec(w.shape), _const_spec((1, POOL_DIM))],
        out_specs=[pl.BlockSpec((None, tm, POOL_DIM), lambda b, i: (b, i, 0)),
                   pl.BlockSpec(w.shape, lambda b, i: (0, 0, 0)),
                   pl.BlockSpec((1, POOL_DIM), lambda b, i: (0, 0))],
        out_shape=[jax.ShapeDtypeStruct((B, T, POOL_DIM), bf16), jax.ShapeDtypeStruct(w.shape, f32),
                   jax.ShapeDtypeStruct((1, POOL_DIM), f32)],
        compiler_params=_cp(),
    )(dycat, dycat, proj, proj, w, scale)


N_PAIRS = SSD_HEADS // 2


def _ssd_chunk(xs, bs, cs, dtp, zs, hs, dtb, alog, dsk, ngs):
    Q = SSD_CHUNK
    lane = lax.broadcasted_iota(jnp.int32, (1, LANES), 1)
    sub = lax.broadcasted_iota(jnp.int32, (LANES, 1), 0)
    causal = lax.broadcasted_iota(jnp.int32, (Q, Q), 0) >= lax.broadcasted_iota(jnp.int32, (Q, Q), 1)
    lane_lo, sub_lo = lane < 64, sub < 64

    def col(v, h):
        return jnp.sum(v * (lane == h).astype(f32), axis=1, keepdims=True)

    def row(vt, h):
        return jnp.sum(vt * (sub == h).astype(f32), axis=0, keepdims=True)

    dt = jax.nn.softplus(dtp + dtb)
    acum = _dot_hi(causal.astype(f32), dt * (-jnp.exp(alog)))
    acum_t = acum.T
    aend = jnp.sum(acum * (sub == Q - 1).astype(f32), axis=0, keepdims=True)
    outs, hn = [], []
    for grp in range(2):
        bv, cv = _silu(bs[grp]), _silu(cs[grp])
        gmat = _dot_nt(cv, bv)
        for j in range(4):
            p = grp * 4 + j
            h0, h1 = 2 * p, 2 * p + 1
            x2 = _silu(xs[p])
            c0, c1 = col(acum, h0), col(acum, h1)
            s2 = jnp.where(lane_lo, c0, c1)
            xdt = x2 * jnp.where(lane_lo, col(dt, h0), col(dt, h1))
            l0 = jnp.where(causal, jnp.exp(jnp.minimum(c0 - row(acum_t, h0), 0.0)), 0.0)
            l1 = jnp.where(causal, jnp.exp(jnp.minimum(c1 - row(acum_t, h1), 0.0)), 0.0)
            yd = _dot(gmat * l0, jnp.where(lane_lo, xdt, 0.0)) + _dot(gmat * l1, jnp.where(lane_lo, 0.0, xdt))
            e0, e1 = col(aend, h0), col(aend, h1)
            st = _dot_tn(xdt * jnp.exp(jnp.where(lane_lo, e0, e1) - s2), bv)
            yo = jnp.exp(s2) * _dot_nt(cv, hs[p])
            hn.append(jnp.exp(jnp.where(sub_lo, e0, e1)) * hs[p] + st)
            yv = yd + yo + x2 * jnp.where(lane_lo, col(dsk, h0), col(dsk, h1))
            outs.append(yv * _silu(zs[p]))
    ms = sum(jnp.sum(o * o, axis=1, keepdims=True) for o in outs) / SSD_D_INNER
    rs = lax.rsqrt(ms + NORM_EPS)
    return [outs[p] * rs * ngs[p] for p in range(N_PAIRS)], hn


def _lane_blocks(ref, n, start=0):
    return [ref[:, (start + k) * LANES:(start + k + 1) * LANES] for k in range(n)]


def _ssd_args(z_ref, cx_ref, dt_ref, dtb_ref, alog_ref, dsk_ref, ng_ref):
    xs = _lane_blocks(cx_ref, 8)
    bs = _lane_blocks(cx_ref, 2, 8)
    cs = _lane_blocks(cx_ref, 2, 10)
    zs = _lane_blocks(z_ref, 8)
    ngs = _lane_blocks(ng_ref, 8)
    return xs, bs, cs, dt_ref[...], zs, dtb_ref[...], alog_ref[...], dsk_ref[...], ngs


DT_COLB = (EVEN_IN_PAD - LANES) // LANES


def ssd_fwd(proj, cx, dtb, alog, dsk, ng):
    B, T, _ = proj.shape
    Q = SSD_CHUNK
    nc = T // Q

    def body(z_ref, cx_ref, dt_ref, dtb_ref, alog_ref, dsk_ref, ng_ref, ya_ref, hsave_ref, h_scr):
        @pl.when(pl.program_id(1) == 0)
        def _():
            h_scr[...] = jnp.zeros_like(h_scr)

        xs, bs, cs, dtp, zs, dtb_v, alog_v, dsk_v, ngs = _ssd_args(z_ref, cx_ref, dt_ref, dtb_ref, alog_ref, dsk_ref, ng_ref)
        hs = [h_scr[p] for p in range(N_PAIRS)]
        for p in range(N_PAIRS):
            hsave_ref[p] = hs[p]
        outs, hn = _ssd_chunk(xs, bs, cs, dtp, zs, hs, dtb_v, alog_v, dsk_v, ngs)
        for p in range(N_PAIRS):
            ya_ref[:, p * LANES:(p + 1) * LANES] = outs[p].astype(bf16)
            h_scr[p] = hn[p]

    return pl.pallas_call(
        body, name="ssd_fwd", grid=(B, nc),
        in_specs=[pl.BlockSpec((None, Q, SSD_D_INNER), lambda b, i: (b, i, 0)),
                  pl.BlockSpec((None, Q, SSD_CONV_DIM), lambda b, i: (b, i, 0)),
                  pl.BlockSpec((None, Q, LANES), lambda b, i: (b, i, DT_COLB)),
                  _const_spec((1, LANES)), _const_spec((1, LANES)), _const_spec((1, LANES)),
                  _const_spec((1, SSD_D_INNER))],
        out_specs=[pl.BlockSpec((None, Q, SSD_D_INNER), lambda b, i: (b, i, 0)),
                   pl.BlockSpec((None, None, N_PAIRS, LANES, LANES), lambda b, i: (b, i, 0, 0, 0))],
        out_shape=[jax.ShapeDtypeStruct((B, T, SSD_D_INNER + POOL_DIM), bf16),
                   jax.ShapeDtypeStruct((B, nc, N_PAIRS, LANES, LANES), f32)],
        scratch_shapes=[pltpu.VMEM((N_PAIRS, LANES, LANES), f32)],
        compiler_params=_cp(),
    )(proj, cx, proj, dtb, alog, dsk, ng)


def ssd_bwd(dycat, proj, cx, hsave, dtb, alog, dsk, ng):
    B, T, _ = proj.shape
    Q = SSD_CHUNK
    nc = T // Q

    def body(dya_ref, z_ref, cx_ref, dt_ref, hsave_ref, dtb_ref, alog_ref, dsk_ref, ng_ref,
             dz_ref, dcx_ref, ddt_ref, gsm_ref, gng_ref, dh_scr):
        b, i = pl.program_id(0), pl.program_id(1)

        @pl.when((b == 0) & (i == 0))
        def _():
            gsm_ref[...] = jnp.zeros_like(gsm_ref)
            gng_ref[...] = jnp.zeros_like(gng_ref)

        @pl.when(i == 0)
        def _():
            dh_scr[...] = jnp.zeros_like(dh_scr)

        xs, bs, cs, dtp, zs, dtb_v, alog_v, dsk_v, ngs = _ssd_args(z_ref, cx_ref, dt_ref, dtb_ref, alog_ref, dsk_ref, ng_ref)
        hs = [hsave_ref[p] for p in range(N_PAIRS)]
        _, vjp = jax.vjp(_ssd_chunk, xs, bs, cs, dtp, zs, hs, dtb_v, alog_v, dsk_v, ngs)
        douts = _lane_blocks(dya_ref, 8)
        dhn = [dh_scr[p] for p in range(N_PAIRS)]
        dxs, dbs, dcs, ddtp, dzs, dhs, ddtb, dalog, ddsk, dngs = vjp((douts, dhn))
        for p in range(N_PAIRS):
            dcx_ref[:, p * LANES:(p + 1) * LANES] = dxs[p]
            dz_ref[:, p * LANES:(p + 1) * LANES] = dzs[p].astype(bf16)
            dh_scr[p] = dhs[p]
            gng_ref[:, p * LANES:(p + 1) * LANES] += dngs[p]
        for k in range(2):
            dcx_ref[:, (8 + k) * LANES:(9 + k) * LANES] = dbs[k]
            dcx_ref[:, (10 + k) * LANES:(11 + k) * LANES] = dcs[k]
        ddt_ref[...] = ddtp.astype(bf16)
        gsm_ref[0:1, :] += ddtb
        gsm_ref[1:2, :] += dalog
        gsm_ref[2:3, :] += ddsk

    rev = lambda w, cb=0: pl.BlockSpec((None, Q, w), lambda b, i: (b, nc - 1 - i, cb))
    return pl.pallas_call(
        body, name="ssd_bwd", grid=(B, nc),
        in_specs=[rev(SSD_D_INNER), rev(SSD_D_INNER), rev(SSD_CONV_DIM), rev(LANES, DT_COLB),
                  pl.BlockSpec((None, None, N_PAIRS, LANES, LANES), lambda b, i: (b, nc - 1 - i, 0, 0, 0)),
                  _const_spec((1, LANES)), _const_spec((1, LANES)), _const_spec((1, LANES)),
                  _const_spec((1, SSD_D_INNER))],
        out_specs=[rev(SSD_D_INNER), rev(SSD_CONV_DIM), rev(LANES),
                   pl.BlockSpec((3, LANES), lambda b, i: (0, 0)),
                   pl.BlockSpec((1, SSD_D_INNER), lambda b, i: (0, 0))],
        out_shape=[jax.ShapeDtypeStruct((B, T, SSD_D_INNER), bf16), jax.ShapeDtypeStruct((B, T, SSD_CONV_DIM), f32),
                   jax.ShapeDtypeStruct((B, T, LANES), bf16), jax.ShapeDtypeStruct((3, LANES), f32),
                   jax.ShapeDtypeStruct((1, SSD_D_INNER), f32)],
        scratch_shapes=[pltpu.VMEM((N_PAIRS, LANES, LANES), f32)],
        compiler_params=_cp(),
    )(dycat, proj, cx, proj, hsave, dtb, alog, dsk, ng)


def glu_fwd(proj):
    B, T, _ = proj.shape
    tm = _row_tile(T, 1024)

    def body(v_ref, g_ref, o_ref):
        o_ref[...] = v_ref[...] * jax.nn.sigmoid(g_ref[...])

    blk = lambda cb: pl.BlockSpec((None, tm, CONF_DIM), lambda b, i: (b, i, cb))
    return pl.pallas_call(body, name="glu_fwd", grid=(B, T // tm), in_specs=[blk(0), blk(1)], out_specs=blk(0),
                          out_shape=jax.ShapeDtypeStruct((B, T, CONF_DIM), f32), compiler_params=_cp())(proj, proj)


def glu_bwd(dhh, proj):
    B, T, _ = proj.shape
    tm = _row_tile(T, 1024)

    def body(d_ref, v_ref, g_ref, o_ref):
        sig = jax.nn.sigmoid(g_ref[...])
        dv = d_ref[...]
        o_ref[:, :CONF_DIM] = (dv * sig).astype(bf16)
        o_ref[:, CONF_DIM:] = (dv * v_ref[...] * sig * (1.0 - sig)).astype(bf16)

    blk = lambda cb: pl.BlockSpec((None, tm, CONF_DIM), lambda b, i: (b, i, cb))
    return pl.pallas_call(body, name="glu_bwd", grid=(B, T // tm), in_specs=[blk(0), blk(0), blk(1)],
                          out_specs=pl.BlockSpec((None, tm, 2 * CONF_DIM), lambda b, i: (b, i, 0)),
                          out_shape=jax.ShapeDtypeStruct((B, T, 2 * CONF_DIM), bf16), compiler_params=_cp())(dhh, proj, proj)


def _neg_expm1(x):
    series = x * (1.0 + x * (1.0 / 2.0) * (1.0 + x * (1.0 / 3.0) * (1.0 + x * (1.0 / 4.0) * (1.0 + x * (1.0 / 5.0)))))
    return -jnp.where(x > -0.1, series, jnp.exp(x) - 1.0)


def _lru_gates(cc, xc8, gr8, clg, clb, wa8, wx8, ba8, bx8, lam8):
    xhat, _ = _ln_stats(cc)
    yc = _silu(xhat * clg + clb)
    a8, b8, ge8 = [], [], []
    for hb in range(LRU_HEADS):
        xh = xc8[hb]
        rg = jax.nn.sigmoid(_dot(xh, wa8[hb]) + ba8[hb])
        ig = jax.nn.sigmoid(_dot(xh, wx8[hb]) + bx8[hb])
        log_a = -LRU_C * rg * jax.nn.softplus(-lam8[hb])
        a8.append(jnp.exp(log_a))
        b8.append(jnp.sqrt(_neg_expm1(2.0 * log_a)) * (ig * xh))
        ge8.append(jax.nn.gelu(gr8[hb]))
    return yc, a8, b8, ge8


def _scan_fwd(a, b, h_in):
    tm = a.shape[0]
    rows = lax.broadcasted_iota(jnp.int32, (tm, 1), 0)
    s = 1
    while s < tm:
        keep = rows >= s
        b = a * jnp.where(keep, pltpu.roll(b, s, 0), 0.0) + b
        a = a * jnp.where(keep, pltpu.roll(a, s, 0), 1.0)
        s *= 2
    return a * h_in + b


def _scan_bwd(e, d, g_in):
    tm = e.shape[0]
    rows = lax.broadcasted_iota(jnp.int32, (tm, 1), 0)
    s = 1
    while s < tm:
        keep = rows < tm - s
        d = e * jnp.where(keep, pltpu.roll(d, tm - s, 0), 0.0) + d
        e = e * jnp.where(keep, pltpu.roll(e, tm - s, 0), 1.0)
        s *= 2
    return e * g_in + d


def _lru_params(wa_ref, wx_ref, vec_ref):
    wa8 = [wa_ref[h] for h in range(LRU_HEADS)]
    wx8 = [wx_ref[h] for h in range(LRU_HEADS)]
    ba8 = [vec_ref[0:1, h * LANES:(h + 1) * LANES] for h in range(LRU_HEADS)]
    bx8 = [vec_ref[1:2, h * LANES:(h + 1) * LANES] for h in range(LRU_HEADS)]
    lam8 = [vec_ref[2:3, h * LANES:(h + 1) * LANES] for h in range(LRU_HEADS)]
    return wa8, wx8, ba8, bx8, lam8


GR_COLB = 2


def lru_fwd(cc, xc, proj, cln, wa, wx, vec):
    B, T, _ = xc.shape
    tm = _row_tile(T)

    def body(cc_ref, xc_ref, gr_ref, cln_ref, wa_ref, wx_ref, vec_ref, y_ref, hs_ref, h_scr):
        @pl.when(pl.program_id(1) == 0)
        def _():
            h_scr[...] = jnp.zeros_like(h_scr)

        yc, a8, b8, ge8 = _lru_gates(cc_ref[...], _lane_blocks(xc_ref, 8), _lane_blocks(gr_ref, 8), cln_ref[0:1, :],
                                     cln_ref[1:2, :], *_lru_params(wa_ref, wx_ref, vec_ref))
        h = _scan_fwd(jnp.concatenate(a8, axis=1), jnp.concatenate(b8, axis=1), h_scr[...])
        hs_ref[...] = h
        h_scr[...] = h[tm - 1:tm, :]
        y_ref[:, :CONF_DIM] = yc.astype(bf16)
        y_ref[:, CONF_DIM:] = (h * jnp.concatenate(ge8, axis=1)).astype(bf16)

    row = lambda w, cb=0: pl.BlockSpec((None, tm, w), lambda b, i: (b, i, cb))
    return pl.pallas_call(
        body, name="lru_fwd", grid=(B, T // tm),
        in_specs=[row(CONF_DIM), row(LRU_DIM), row(LRU_DIM, GR_COLB), _const_spec((2, CONF_DIM)),
                  _const_spec(wa.shape), _const_spec(wx.shape), _const_spec((3, LRU_DIM))],
        out_specs=[row(CONF_DIM + LRU_DIM), row(LRU_DIM)],
        out_shape=[jax.ShapeDtypeStruct((B, T, CONF_DIM + LRU_DIM), bf16), jax.ShapeDtypeStruct((B, T, LRU_DIM), f32)],
        scratch_shapes=[pltpu.VMEM((1, LRU_DIM), f32)],
        compiler_params=_cp(),
    )(cc, xc, proj, cln, wa, wx, vec)


def lru_bwd(dycat, cc, xc, proj, hs, cln, wa, wx, vec):
    B, T, _ = xc.shape
    tm = _row_tile(T)
    nt = T // tm
    r = tm // 8

    def body(dy_ref, cc_ref, xc_ref, gr_ref, hs_ref, hsh_ref, cln_ref, wa_ref, wx_ref, vec_ref,
             dcc_ref, dxc_ref, dgr_ref, dcln_ref, dwa_ref, dwx_ref, dvec_ref, g_scr, a_scr):
        b, i = pl.program_id(0), pl.program_id(1)
        it = nt - 1 - i

        @pl.when((b == 0) & (i == 0))
        def _():
            dcln_ref[...] = jnp.zeros_like(dcln_ref)
            dwa_ref[...] = jnp.zeros_like(dwa_ref)
            dwx_ref[...] = jnp.zeros_like(dwx_ref)
            dvec_ref[...] = jnp.zeros_like(dvec_ref)

        @pl.when(i == 0)
        def _():
            g_scr[...] = jnp.zeros_like(g_scr)
            a_scr[...] = jnp.zeros_like(a_scr)

        (yc, a8, b8, ge8), vjp = jax.vjp(_lru_gates, cc_ref[...], _lane_blocks(xc_ref, 8), _lane_blocks(gr_ref, 8),
                                         cln_ref[0:1, :], cln_ref[1:2, :], *_lru_params(wa_ref, wx_ref, vec_ref))
        a = jnp.concatenate(a8, axis=1)
        ge = jnp.concatenate(ge8, axis=1)
        h = hs_ref[...]
        dyd = dy_ref[:, CONF_DIM:]
        rows = lax.broadcasted_iota(jnp.int32, (tm, 1), 0)
        e = jnp.where(rows < tm - 1, pltpu.roll(a, tm - 1, 0), a_scr[...])
        g = _scan_bwd(e, dyd * ge, g_scr[...])
        h_first = jnp.where(it == 0, 0.0, hsh_ref[7:8, :])
        h_prev = jnp.where(rows >= 1, pltpu.roll(h, 1, 0), h_first)
        da = g * h_prev
        g_scr[...] = g[0:1, :]
        a_scr[...] = a[0:1, :]
        split = lambda v: [v[:, k * LANES:(k + 1) * LANES] for k in range(LRU_HEADS)]
        dcc, dxc8, dgr8, dclg, dclb, dwa8, dwx8, dba8, dbx8, dlam8 = vjp((dy_ref[:, :CONF_DIM], split(da), split(g), split(dyd * h)))
        dcc_ref[...] = dcc
        dcln_ref[0:1, :] += dclg
        dcln_ref[1:2, :] += dclb
        for k in range(LRU_HEADS):
            sl = slice(k * LANES, (k + 1) * LANES)
            dxc_ref[:, sl] = dxc8[k]
            dgr_ref[:, sl] = dgr8[k].astype(bf16)
            dwa_ref[k] += dwa8[k]
            dwx_ref[k] += dwx8[k]
            dvec_ref[0:1, sl] += dba8[k]
            dvec_ref[1:2, sl] += dbx8[k]
            dvec_ref[2:3, sl] += dlam8[k]

    rev = lambda w, cb=0: pl.BlockSpec((None, tm, w), lambda b, i: (b, nt - 1 - i, cb))
    acc = lambda shape: pl.BlockSpec(shape, lambda b, i: (0,) * len(shape))
    return pl.pallas_call(
        body, name="lru_bwd", grid=(B, nt),
        in_specs=[rev(CONF_DIM + LRU_DIM), rev(CONF_DIM), rev(LRU_DIM), rev(LRU_DIM, GR_COLB), rev(LRU_DIM),
                  pl.BlockSpec((None, 8, LRU_DIM), lambda b, i: (b, jnp.maximum((nt - 1 - i) * r - 1, 0), 0)),
                  _const_spec((2, CONF_DIM)), _const_spec(wa.shape), _const_spec(wx.shape), _const_spec((3, LRU_DIM))],
        out_specs=[rev(CONF_DIM), rev(LRU_DIM), rev(LRU_DIM), acc((2, CONF_DIM)), acc(wa.shape), acc(wx.shape),
                   acc((3, LRU_DIM))],
        out_shape=[jax.ShapeDtypeStruct((B, T, CONF_DIM), f32), jax.ShapeDtypeStruct((B, T, LRU_DIM), f32),
                   jax.ShapeDtypeStruct((B, T, LRU_DIM), bf16), jax.ShapeDtypeStruct((2, CONF_DIM), f32),
                   jax.ShapeDtypeStruct(wa.shape, f32), jax.ShapeDtypeStruct(wx.shape, f32),
                   jax.ShapeDtypeStruct((3, LRU_DIM), f32)],
        scratch_shapes=[pltpu.VMEM((1, LRU_DIM), f32), pltpu.VMEM((1, LRU_DIM), f32)],
        compiler_params=_cp(),
    )(dycat, cc, xc, proj, hs, hs, cln, wa, wx, vec)


def loss_fwd(y, target):
    B, T, D = y.shape
    tm = _row_tile(T)

    def body(y_ref, t_ref, l_ref, dy_ref):
        @pl.when((pl.program_id(0) == 0) & (pl.program_id(1) == 0))
        def _():
            l_ref[...] = jnp.zeros_like(l_ref)

        d = y_ref[...] - t_ref[...]
        dy_ref[...] = d * (1.0 / D)
        l_ref[...] += jnp.sum(jnp.sum(d * d, axis=1, keepdims=True), axis=0, keepdims=True)

    row = pl.BlockSpec((None, tm, D), lambda b, i: (b, i, 0))
    return pl.pallas_call(
        body, name="loss_fwd", grid=(B, T // tm), in_specs=[row, row],
        out_specs=[pl.BlockSpec((1, 1), lambda b, i: (0, 0)), row],
        out_shape=[jax.ShapeDtypeStruct((1, 1), f32), jax.ShapeDtypeStruct((B, T, D), f32)],
        compiler_params=_cp(),
    )(y, target)


ADA_COL_TILE = 768


def ada_fwd(c_all, w, b):
    L, D, N = w.shape
    nb = c_all.shape[0]
    tn = ADA_COL_TILE

    def body(c_ref, w_ref, b_ref, o_ref):
        o_ref[...] = _dot_hi(_silu(c_ref[...]), w_ref[...]) + b_ref[...]

    return pl.pallas_call(
        body, name="ada_fwd", grid=(L, N // tn),
        in_specs=[pl.BlockSpec((nb, D), lambda l, j: (0, 0)), pl.BlockSpec((None, D, tn), lambda l, j: (l, 0, j)),
                  pl.BlockSpec((None, 1, tn), lambda l, j: (l, 0, j))],
        out_specs=pl.BlockSpec((None, nb, tn), lambda l, j: (l, 0, j)),
        out_shape=jax.ShapeDtypeStruct((L, nb, N), f32),
        compiler_params=_cp(),
    )(c_all, w, b)


def ada_bwd(c_all, dmod_my, dmod_all):
    L, nb, N = dmod_my.shape
    D = c_all.shape[1]
    NA = dmod_all.shape[2]
    tn = ADA_COL_TILE
    nj = N // tn
    ta = NA // nj

    def body(c_ref, dm_ref, da_ref, gw_ref, gb_ref):
        gw_ref[...] = lax.dot_general(_silu(c_ref[...]), dm_ref[...], (((0,), (0,)), ((), ())),
                                      preferred_element_type=f32, precision=lax.Precision.HIGHEST)
        gb_ref[...] = jnp.sum(da_ref[...], axis=0, keepdims=True)

    return pl.pallas_call(
        body, name="ada_bwd", grid=(L, nj),
        in_specs=[pl.BlockSpec((nb, D), lambda l, j: (0, 0)), pl.BlockSpec((None, nb, tn), lambda l, j: (l, 0, j)),
                  pl.BlockSpec((None, nb, ta), lambda l, j: (l, 0, j))],
        out_specs=[pl.BlockSpec((None, D, tn), lambda l, j: (l, 0, j)), pl.BlockSpec((None, 1, ta), lambda l, j: (l, 0, j))],
        out_shape=[jax.ShapeDtypeStruct((L, D, N), f32), jax.ShapeDtypeStruct((L, 1, NA), f32)],
        compiler_params=_cp(),
    )(c_all, dmod_my, dmod_all)


def adamw(w, m, v, gs, offs, tr):
    R, C = w.shape
    ng = len(gs)
    c1 = 1.0 - ADAM_B1 ** ADAM_STEP
    c2 = 1.0 - ADAM_B2 ** ADAM_STEP

    def body(*refs):
        w_ref, m_ref, v_ref = refs[:3]
        g_refs = refs[3:3 + ng]
        d_out, m_out, v_out = refs[3 + ng:]
        g = g_refs[0][...]
        for r in g_refs[1:]:
            g = g + r[...]
        mn = ADAM_B1 * m_ref[...] + (1.0 - ADAM_B1) * g
        vn = ADAM_B2 * v_ref[...] + (1.0 - ADAM_B2) * (g * g)
        m_out[...] = mn
        v_out[...] = vn
        d_out[...] = -ADAM_LR * ((mn / c1) / (jnp.sqrt(vn / c2) + ADAM_EPS) + ADAM_WD * w_ref[...])

    blk = pl.BlockSpec((tr, C), lambda i: (i, 0))
    gspec = lambda off: pl.BlockSpec((tr, C), lambda i: (i + off // tr, 0))
    sds = jax.ShapeDtypeStruct((R, C), f32)
    return pl.pallas_call(
        body, name="adamw", grid=(pl.cdiv(R, tr),), in_specs=[blk, blk, blk] + [gspec(o) for o in offs],
        out_specs=[blk] * 3, out_shape=[sds] * 3, compiler_params=_cp(),
    )(w, m, v, *gs)


def sum_leading(a, tr):
    k, R, C = a.shape

    def body(a_ref, o_ref):
        s = a_ref[0]
        for j in range(1, k):
            s = s + a_ref[j]
        o_ref[...] = s

    return pl.pallas_call(
        body, name="sum_leading", grid=(R // tr,), in_specs=[pl.BlockSpec((k, tr, C), lambda i: (0, i, 0))],
        out_specs=pl.BlockSpec((tr, C), lambda i: (i, 0)), out_shape=jax.ShapeDtypeStruct((R, C), a.dtype),
        compiler_params=_cp(),
    )(a)


ANY = pl.BlockSpec(memory_space=pl.ANY)
CHIP_FLIPS = ((1, 1), (1, 0), (0, 1))
DEV_FLIPS = tuple((fx, fy, fc) for fx in (0, 1) for fy in (0, 1) for fc in (0, 1))[1:]


def _flip(v, f):
    return 1 - v if f else v


def _put(out, v, idx, axis=0):
    return lax.dynamic_update_slice_in_dim(out, jnp.expand_dims(v, axis) if v.ndim < out.ndim else v, idx, axis)


def allgather8(v):
    R, C = v.shape
    n = len(DEV_FLIPS)

    def body(v_ref, o_ref, send_sems, recv_sems):
        x, y, c = lax.axis_index("x"), lax.axis_index("y"), lax.axis_index("c")
        me, sibling = (x, y, c), (x, y, 1 - c)
        chips = [(_flip(x, fx), _flip(y, fy)) for fx, fy in CHIP_FLIPS]

        def copy(k, block, to, own=False):
            px, py, pc = block
            dst = o_ref.at[4 * px + 2 * py + pc]
            return pltpu.make_async_remote_copy(src_ref=v_ref if own else dst, dst_ref=dst, send_sem=send_sems.at[k],
                                                recv_sem=recv_sems.at[k], device_id=to, device_id_type=MESH)

        first = [copy(0, me, sibling, own=True)] + [copy(1 + j, me, (*chip, c), own=True) for j, chip in enumerate(chips)]
        for cp in first:
            cp.start()
        passed = [copy(4 + j, (*chip, c), sibling) for j, chip in enumerate(chips)]
        for j, chip in enumerate(chips):
            copy(1 + j, (*chip, c), me).wait_recv()
            passed[j].start()
        copy(0, sibling, me).wait_recv()
        for j, chip in enumerate(chips):
            copy(4 + j, (*chip, 1 - c), me).wait_recv()
        for cp in first + passed:
            cp.wait_send()

    out = pl.pallas_call(
        body, name="allgather8", in_specs=[ANY], out_specs=ANY, out_shape=jax.ShapeDtypeStruct((N_DEV, R, C), v.dtype),
        scratch_shapes=[pltpu.SemaphoreType.DMA((n,)), pltpu.SemaphoreType.DMA((n,))],
    )(v)
    return _put(out, v, 4 * lax.axis_index("x") + 2 * lax.axis_index("y") + lax.axis_index("c"))


def _half(ref_or_shape0, c):
    hsz = ref_or_shape0 // 2
    return pl.ds(c * hsz, hsz)


def _gather_steps(w_refs, o_refs, send_sems, recv_sems):
    nw, nc = len(w_refs), len(CHIP_FLIPS)
    x, y, c = lax.axis_index("x"), lax.axis_index("y"), lax.axis_index("c")
    me, sibling = 2 * x + y, (x, y, 1 - c)
    peers = [(_flip(x, fx), _flip(y, fy), c) for fx, fy in CHIP_FLIPS]
    slots = [2 * px + py for px, py, _ in peers]

    def copy(a, j, slot, half, to, own=False):
        hs = _half(w_refs[a].shape[0], half)
        return pltpu.make_async_remote_copy(src_ref=w_refs[a].at[hs] if own else o_refs[a].at[slot, hs],
                                            dst_ref=o_refs[a].at[slot, hs], send_sem=send_sems.at[j],
                                            recv_sem=recv_sems.at[j], device_id=to, device_id_type=MESH)

    first = [copy(a, a * nc + k, me, c, peers[k], own=True) for a in range(nw) for k in range(nc)]

    def start():
        for cp in first:
            cp.start()

    passed = [copy(a, nw * nc + a * nc + k, slots[k], c, sibling) for a in range(nw) for k in range(nc)]

    def forward():
        for a in range(nw):
            for k in range(nc):
                copy(a, a * nc + k, slots[k], c, peers[k]).wait_recv()
                passed[a * nc + k].start()

    def finish():
        for a in range(nw):
            for k in range(nc):
                copy(a, nw * nc + a * nc + k, slots[k], 1 - c, sibling).wait_recv()
        for cp in first + passed:
            cp.wait_send()

    return start, forward, finish


def _gather_sems(nw):
    n = 2 * nw * len(CHIP_FLIPS)
    return [pltpu.SemaphoreType.DMA((n,)), pltpu.SemaphoreType.DMA((n,))]


def _gather_out_shapes(ws):
    return [jax.ShapeDtypeStruct((N_CHIPS,) + w.shape, w.dtype) for w in ws]


def _fill_own(outs, ws):
    if not ws:
        return []
    chip =2 * lax.axis_index("x") + lax.axis_index("y")
    return [_put(o, w, chip) for o, w in zip(outs, ws)]


def gather_weights(ws):
    nw = len(ws)

    def body(*refs):
        start, forward, finish = _gather_steps(refs[:nw], refs[nw:2 * nw], *refs[2 * nw:])
        start()
        forward()
        finish()

    outs = pl.pallas_call(
        body, name="gather_weights", in_specs=[ANY] * nw, out_specs=[ANY] * nw,
        out_shape=_gather_out_shapes(ws), scratch_shapes=_gather_sems(nw),
    )(*ws)
    return _fill_own(outs, ws)


def swap_halves(gs):
    nw = len(gs)

    def body(*refs):
        g_refs, t_refs = refs[:nw], refs[nw:2 * nw]
        send_sems, recv_sems = refs[2 * nw:]
        x, y, c = lax.axis_index("x"), lax.axis_index("y"), lax.axis_index("c")
        cps = [pltpu.make_async_remote_copy(src_ref=g_refs[a].at[_half(gs[a].shape[0], 1 - c)], dst_ref=t_refs[a],
                                            send_sem=send_sems.at[a], recv_sem=recv_sems.at[a],
                                            device_id=(x, y, 1 - c), device_id_type=MESH) for a in range(nw)]
        for cp in cps:
            cp.start()
        for cp in cps:
            cp.wait()

    return pl.pallas_call(
        body, name="swap_halves", in_specs=[ANY] * nw, out_specs=[ANY] * nw,
        out_shape=[jax.ShapeDtypeStruct((g.shape[0] // 2,) + g.shape[1:], g.dtype) for g in gs],
        scratch_shapes=[pltpu.SemaphoreType.DMA((nw,)), pltpu.SemaphoreType.DMA((nw,))],
    )(*gs)


def exchange_blocks(ps):
    nw, nc = len(ps), len(CHIP_FLIPS)

    def body(*refs):
        p_refs, r_refs = refs[:nw], refs[nw:2 * nw]
        send_sems, recv_sems = refs[2 * nw:]
        x, y, c = lax.axis_index("x"), lax.axis_index("y"), lax.axis_index("c")
        me = 2 * x + y
        peers = [(_flip(x, fx), _flip(y, fy), c) for fx, fy in CHIP_FLIPS]
        slots = [2 * px + py for px, py, _ in peers]

        def copy(a, k, src_slot, dst_slot):
            return pltpu.make_async_remote_copy(src_ref=p_refs[a].at[src_slot], dst_ref=r_refs[a].at[dst_slot],
                                                send_sem=send_sems.at[a * nc + k], recv_sem=recv_sems.at[a * nc + k],
                                                device_id=peers[k], device_id_type=MESH)

        for a in range(nw):
            for k in range(nc):
                copy(a, k, slots[k], me).start()
        for a in range(nw):
            for k in range(nc):
                copy(a, k, me, slots[k]).wait_recv()
        for a in range(nw):
            for k in range(nc):
                copy(a, k, slots[k], me).wait_send()

    outs = pl.pallas_call(
        body, name="exchange_blocks", in_specs=[ANY] * nw, out_specs=[ANY] * nw,
        out_shape=[jax.ShapeDtypeStruct(p.shape, p.dtype) for p in ps],
        scratch_shapes=[pltpu.SemaphoreType.DMA((nw * nc,)), pltpu.SemaphoreType.DMA((nw * nc,))],
    )(*ps)
    chip = 2 * lax.axis_index("x") + lax.axis_index("y")
    return [_put(o, lax.dynamic_slice_in_dim(p, chip, 1, axis=0), chip) for o, p in zip(outs, ps)]


def join_halves(ss):
    nw = len(ss)

    def body(*refs):
        s_refs, o_refs = refs[:nw], refs[nw:2 * nw]
        send_sems, recv_sems = refs[2 * nw:]
        x, y, c = lax.axis_index("x"), lax.axis_index("y"), lax.axis_index("c")

        def copy(a, half):
            hs = _half(2 * ss[a].shape[0], half)
            return pltpu.make_async_remote_copy(src_ref=s_refs[a], dst_ref=o_refs[a].at[hs], send_sem=send_sems.at[a],
                                                recv_sem=recv_sems.at[a], device_id=(x, y, 1 - c), device_id_type=MESH)

        for a in range(nw):
            copy(a, c).start()
        for a in range(nw):
            copy(a, 1 - c).wait_recv()
        for a in range(nw):
            copy(a, c).wait_send()

    outs = pl.pallas_call(
        body, name="join_halves", in_specs=[ANY] * nw, out_specs=[ANY] * nw,
        out_shape=[jax.ShapeDtypeStruct((2 * s.shape[0],) + s.shape[1:], s.dtype) for s in ss],
        scratch_shapes=[pltpu.SemaphoreType.DMA((nw,)), pltpu.SemaphoreType.DMA((nw,))],
    )(*ss)
    c = lax.axis_index("c")
    return [_put(o, s, c * s.shape[0]) for o, s in zip(outs, ss)]


def _tile_rows(a, b, itemsize=4, budget=4 * 2 ** 20):
    best = 8
    for t in range(8, a + 1, 8):
        if a % t == 0 and t * b * itemsize <= budget:
            best = t
    return best


def add_half(g, t, cidx):
    def body(c_ref, g_ref, t_ref, o_ref):
        o_ref[...] = (g_ref[...] + t_ref[...]).astype(bf16)

    if g.ndim == 5:
        n0, F, _, A, B = g.shape
        hsz, ta = n0 // 2, _tile_rows(A, B)
        nta = A // ta
        grid = (hsz, N_CHIPS, F, nta)
        in_specs = [pl.BlockSpec((None, None, None, ta, B), lambda h, k, f, i, c_ref: (c_ref[0] * hsz + h, f, k, i, 0)),
                    pl.BlockSpec((None, None, None, ta, B), lambda h, k, f, i, c_ref: (h, f, k, i, 0))]
        out_spec = pl.BlockSpec((None, None, ta, B), lambda h, k, f, i, c_ref: (k, h, f * nta + i, 0))
    else:
        n0, _, A, B = g.shape
        F, hsz, ta = 1, n0 // 2, _tile_rows(A, B)
        grid = (hsz, N_CHIPS, A // ta)
        in_specs = [pl.BlockSpec((None, None, ta, B), lambda h, k, i, c_ref: (c_ref[0] * hsz + h, k, i, 0)),
                    pl.BlockSpec((None, None, ta, B), lambda h, k, i, c_ref: (h, k, i, 0))]
        out_spec = pl.BlockSpec((None, None, ta, B), lambda h, k, i, c_ref: (k, h, i, 0))
    return pl.pallas_call(
        body, name="add_half",
        grid_spec=pltpu.PrefetchScalarGridSpec(num_scalar_prefetch=1, grid=grid, in_specs=in_specs, out_specs=out_spec),
        out_shape=jax.ShapeDtypeStruct((N_CHIPS, hsz, F * A, B), bf16),
        compiler_params=_cp(),
    )(cidx, g, t)


def sum_chips(r):
    _, h, A, B = r.shape
    ta = _tile_rows(A, B)

    def body(r_ref, o_ref):
        s = r_ref[0].astype(f32)
        for j in range(1, N_CHIPS):
            s = s + r_ref[j].astype(f32)
        o_ref[...] = s

    return pl.pallas_call(
        body, name="sum_chips", grid=(h, A // ta),
        in_specs=[pl.BlockSpec((N_CHIPS, None, ta, B), lambda hh, i: (0, hh, i, 0))],
        out_specs=pl.BlockSpec((None, ta, B), lambda hh, i: (hh, i, 0)),
        out_shape=jax.ShapeDtypeStruct((h, A, B), f32),
        compiler_params=_cp(),
    )(r)


WEIGHTS = ('ada_w', 'ada_b', 'ln_g', 'ln_b', 'ffn_w_in', 'ffn_w_out', 'ev_w_in', 'ssd_conv_w', 'ssd_conv_b',
           'ssd_dt_bias', 'ssd_a_log', 'ssd_d', 'ssd_norm_g', 'pool_w', 'pool_scale', 'ev_w_out', 'od_w_in',
           'conf_dw_w', 'conf_dw_b', 'conf_ln_g', 'conf_ln_b', 'lru_conv_w', 'lru_conv_b', 'lru_wa', 'lru_ba',
           'lru_wx', 'lru_bx', 'lru_lambda', 'od_w_out')
BIG =('ffn_w_in', 'ffn_w_out', 'ev_w_out', 'od_w_in', 'od_w_out', 'ev_w_in')
SMALL_SHARDED = ('ln_g', 'ln_b', 'ssd_conv_w', 'conf_dw_w', 'conf_dw_b', 'conf_ln_g', 'conf_ln_b', 'lru_conv_w',
                 'lru_conv_b', 'lru_ba', 'lru_bx', 'lru_lambda')
SMALL_REPLICATED = ('ssd_conv_b', 'ssd_dt_bias', 'ssd_a_log', 'ssd_d', 'ssd_norm_g', 'pool_w', 'pool_scale',
                    'lru_wa', 'lru_wx')
PACK_COLS = 1024
BIG_ROW_TILE = 256


def _pack(arrs, row_mult):
    flat = jnp.concatenate([a.reshape(-1) for a in arrs])
    rows = -(-flat.shape[0] // (PACK_COLS * row_mult)) * row_mult
    return jnp.pad(flat, (0, rows * PACK_COLS - flat.shape[0])).reshape(rows, PACK_COLS)


def _unpack(flat, shapes):
    out, off = [], 0
    for s in shapes:
        n = 1
        for d in s:
            n *= d
        out.append(flat[off:off + n].reshape(s))
        off += n
    return out


def _unshard_last(g4):
    m = jnp.moveaxis(g4, 0, -2)
    return m.reshape(m.shape[:-2] + (m.shape[-2] * m.shape[-1],))


def _pad_rows(a, rows):
    return jnp.pad(a, ((0, rows - a.shape[0]),) + ((0, 0),) * (a.ndim - 1))


def _pad_lanes(a):
    return jnp.pad(a, ((0, 0), (0, LANES - a.shape[1])))


def kernel(x, c, ada_w, ada_b, ln_g, ln_b, ffn_w_in, ffn_w_out, ev_w_in, ssd_conv_w, ssd_conv_b, ssd_dt_bias, ssd_a_log, ssd_d, ssd_norm_g, pool_w, pool_scale, ev_w_out, od_w_in, conf_dw_w, conf_dw_b, conf_ln_g, conf_ln_b, lru_conv_w, lru_conv_b, lru_wa, lru_ba, lru_wx, lru_bx, lru_lambda, od_w_out, loss_target, m_ada_w, m_ada_b, m_ln_g, m_ln_b, m_ffn_w_in, m_ffn_w_out, m_ev_w_in, m_ssd_conv_w, m_ssd_conv_b, m_ssd_dt_bias, m_ssd_a_log, m_ssd_d, m_ssd_norm_g, m_pool_w, m_pool_scale, m_ev_w_out, m_od_w_in, m_conf_dw_w, m_conf_dw_b, m_conf_ln_g, m_conf_ln_b, m_lru_conv_w, m_lru_conv_b, m_lru_wa, m_lru_ba, m_lru_wx, m_lru_bx, m_lru_lambda, m_od_w_out, v_ada_w, v_ada_b, v_ln_g, v_ln_b, v_ffn_w_in, v_ffn_w_out, v_ev_w_in, v_ssd_conv_w, v_ssd_conv_b, v_ssd_dt_bias, v_ssd_a_log, v_ssd_d, v_ssd_norm_g, v_pool_w, v_pool_scale, v_ev_w_out, v_od_w_in, v_conf_dw_w, v_conf_dw_b, v_conf_ln_g, v_conf_ln_b, v_lru_conv_w, v_lru_conv_b, v_lru_wa, v_lru_ba, v_lru_wx, v_lru_bx, v_lru_lambda, v_od_w_out):
    given = dict(locals())
    W = {n: given[n] for n in WEIGHTS}
    M = {n: given["m_" + n] for n in WEIGHTS}
    V = {n: given["v_" + n] for n in WEIGHTS}
    B, T, D = x.shape
    L = DEPTH
    chip = 2 * lax.axis_index("x") + lax.axis_index("y")
    dev = 2 * chip + lax.axis_index("c")

    g1 = allgather8(_pack([c] + [W[n] for n in SMALL_SHARDED], 8)).reshape(N_DEV, -1)
    c_all = g1[:, :B * D].reshape(N_DEV * B, D)
    per_chip = g1[0::2, B * D:]
    full = dict(zip(SMALL_SHARDED, [_unshard_last(jnp.stack(p)) for p in zip(*[
        _unpack(per_chip[k], [W[n].shape for n in SMALL_SHARDED]) for k in range(N_CHIPS)])]))
    for n in SMALL_REPLICATED:
        full[n] = W[n]

    n_ada = ada_w.shape[2]
    ada_b_cols = lax.dynamic_slice_in_dim(ada_b, chip * n_ada, n_ada, axis=1)[:, None, :]
    mod_cols = ada_fwd(c_all, ada_w, ada_b_cols)
    g2 = allgather8(mod_cols.reshape(-1, PACK_COLS))[0::2].reshape(N_CHIPS, L, N_DEV * B, n_ada)
    mod_all = jnp.moveaxis(g2, 0, 2).reshape(L, N_DEV * B, N_CHIPS * n_ada)
    mod = lax.dynamic_slice_in_dim(mod_all, dev * B, B, axis=1).reshape(L, B, N_MOD, D)

    FS = ffn_w_in.shape[3]

    def ffn_shards(l, f):
        return [ffn_w_in[l, f].astype(bf16), ffn_w_out[l, f].astype(bf16)]

    def mix_shards(l):
        w_i, w_o = (ev_w_in, ev_w_out) if l % 2 == 0 else (od_w_in, od_w_out)
        return [w_i[l // 2].astype(bf16), w_o[l // 2].astype(bf16)]

    def ffn_weights(g):
        return g[0], g[1].reshape(-1, D)

    def mix_weights(l, g):
        if l % 2 == 0:
            w = _unshard_last(g[0])
            w_i = jnp.concatenate([w[:, :SSD_ZX], w[:, SSD_ZX + SSD_HEADS:], w[:, SSD_ZX:SSD_ZX + SSD_HEADS],
                                   jnp.zeros((D, EVEN_IN_PAD - EVEN_IN), bf16)], axis=1)
        else:
            w_i = jnp.moveaxis(g[0], 0, 1).reshape(D, -1)
        return w_i, g[1].reshape(-1, D)

    saved = []
    xs = x
    next_ffn = gather_weights(ffn_shards(0, 0))
    next_mix = None
    for l in range(L):
        lg, lb = full['ln_g'][l], full['ln_b'][l]
        rec = {}
        m1, m2, m3 = mod[l][:, 0:3], mod[l][:, 3:6], mod[l][:, 6:9]
        w_in_a, w_out_a = ffn_weights(next_ffn)
        carry = ffn_shards(l, 1) + (mix_shards(l) if l == 0 else [])
        xn, h, gu, a, y, got = ffn_fwd(xs, m1, w_in_a, w_out_a, lg[0:1], lb[0:1], carry)
        w_in_b, w_out_b = ffn_weights(got[:2])
        w_in_m, w_out_m = mix_weights(l, got[2:] if l == 0 else next_mix)
        rec['ffa'] = (xs, h, gu, a, y, m1, w_in_a, w_out_a, lg[0:1])
        xs = xn
        if l % 2 == 0:
            e = l // 2
            cw = _pad_rows(full['ssd_conv_w'][e], 8)
            cb = full['ssd_conv_b'][e][None]
            dtb, alog, dsk = (_pad_lanes(full[n][e][None]) for n in ('ssd_dt_bias', 'ssd_a_log', 'ssd_d'))
            ng, pw, ps = full['ssd_norm_g'][e][None], full['pool_w'][e], full['pool_scale'][e][None]
            proj, hm = inproj_fwd(xs, m2, w_in_m)
            cx = dwconv_fwd(proj, XBC_COLB, SSD_CONV_DIM, cw, cb, SSD_CONV)
            ya, hsave = ssd_fwd(proj, cx, dtb, alog, dsk, ng)
            ycat = pool_fwd(proj, POOL_COLB, pw, ps, ya)
            rec['mix'] = (proj, cx, hsave, cw, dtb, alog, dsk, ng, pw, ps)
        else:
            o = l // 2
            dww =_pad_rows(full['conf_dw_w'][o], 32)
            dwb = full['conf_dw_b'][o][None]
            cw = _pad_rows(full['lru_conv_w'][o], 8)
            cb = full['lru_conv_b'][o][None]
            cln = jnp.stack([full['conf_ln_g'][o], full['conf_ln_b'][o]])
            vec = jnp.stack([full['lru_ba'][o], full['lru_bx'][o], full['lru_lambda'][o]])
            wa, wx = full['lru_wa'][o], full['lru_wx'][o]
            proj, hm = inproj_fwd(xs, m2, w_in_m)
            hh = glu_fwd(proj)
            cc = dwconv_fwd(hh, 0, CONF_DIM, dww, dwb, CONF_KERNEL)
            xc = dwconv_fwd(proj, XR_COLB, LRU_DIM, cw, cb, LRU_CONV)
            ycat, hst = lru_fwd(cc, xc, proj, cln, wa, wx, vec)
            rec['mix'] = (proj, hh, cc, xc, hst, dww, cw, cln, wa, wx, vec)
        xn, ym = outproj_fwd(ycat, w_out_m, xs, m2, lg[1:2], lb[1:2])
        rec['mixio'] = (xs, hm, ycat, ym, m2, w_in_m, w_out_m, lg[1:2])
        xs = xn
        carry = ffn_shards(l + 1, 0) + mix_shards(l + 1) if l + 1 < L else []
        xn, h, gu, a, y, got = ffn_fwd(xs, m3, w_in_b, w_out_b, lg[2:3], lb[2:3], carry)
        next_ffn, next_mix = got[:2], got[2:]
        rec['ffb'] = (xs, h, gu, a, y, m3, w_in_b, w_out_b, lg[2:3])
        xs = xn
        saved.append(rec)

    sq, dxs = loss_fwd(xs, loss_target)
    loss = lax.psum(sq[0, 0], ("x", "y", "c")) * (0.5 / D)

    gpart = {n: [None] * W[n].shape[0] for n in WEIGHTS}
    for n in ('ffn_w_in', 'ffn_w_out', 'ev_w_out', 'od_w_in', 'od_w_out'):
        gpart[n] = None
    gpart['ln_g'] = [[None] * 3 for _ in range(L)]
    gpart['ln_b'] = [[None] * 3 for _ in range(L)]
    dmod = [None] * L

    def ffn_back(dxn, rec, l, f):
        xin, h, gu, a, y, m3_, w_in_, w_out_, lg_ = rec
        dx, dgu, dy, dm3, dln = ffn_bwd(dxn, xin, y, gu, m3_, w_in_, w_out_, lg_)
        gpart['ffn_w_in'] = wgrad_into(h, dgu, FS, (L, 2), (l, f), gpart['ffn_w_in'])
        gpart['ffn_w_out'] = wgrad_into(a, dy, D, (L, 2), (l, f), gpart['ffn_w_out'])
        gpart['ln_g'][l][2 * f] = dln[0]
        gpart['ln_b'][l][2 * f] = dln[1]
        return dx, dm3

    for l in reversed(range(L)):
        rec = saved[l]
        dxs, dm3 = ffn_back(dxs, rec['ffb'], l, 1)
        xin, hm, ycat, ym, m2, w_in_m, w_out_m, lg_ = rec['mixio']
        dxp, dycat, dy, dg2, dln = outproj_bwd(dxs, xin, ym, m2, w_out_m, lg_)
        gpart['ln_g'][l][1] = dln[0]
        gpart['ln_b'][l][1] = dln[1]
        n_out = 'ev_w_out' if l % 2 == 0 else 'od_w_out'
        gpart[n_out] = wgrad_into(ycat, dy, D, (L // 2,), (l // 2,), gpart[n_out])
        if l % 2 == 0:
            e = l // 2
            proj, cx, hsave, cw, dtb, alog, dsk, ng, pw, ps = rec['mix']
            dz, dcx, ddt, gsm, gng = ssd_bwd(dycat, proj, cx, hsave, dtb, alog, dsk, ng)
            dxbc, dcw, dcb = dwconv_bwd(dcx, proj, XBC_COLB, SSD_CONV_DIM, cw, SSD_CONV, bf16)
            du, dpw, dps = pool_bwd(dycat, POOL_DY_COLB, proj, POOL_COLB, pw, ps)
            dproj = jnp.concatenate([dz, dxbc, du, ddt], axis=-1)
            gwp = wgrad(hm, dproj, EVEN_IN_PAD)[0]
            dt0 = SSD_ZX + POOL_DIM
            gw = jnp.concatenate([gwp[:, :SSD_ZX], gwp[:, dt0:dt0 + SSD_HEADS], gwp[:, SSD_ZX:dt0]], axis=1)
            gpart['ev_w_in'][e] = jnp.moveaxis(gw.reshape(D, N_CHIPS, -1), 1, 0)
            gpart['ssd_conv_w'][e], gpart['ssd_conv_b'][e] = dcw[:SSD_CONV], dcb[0]
            gpart['ssd_dt_bias'][e], gpart['ssd_a_log'][e], gpart['ssd_d'][e] = (gsm[k, :SSD_HEADS] for k in range(3))
            gpart['ssd_norm_g'][e], gpart['pool_w'][e], gpart['pool_scale'][e] = gng[0], dpw, dps[0]
        else:
            o = l // 2
            proj, hh, cc, xc, hst, dww, cw, cln, wa, wx, vec = rec['mix']
            dcc, dxc, dgr, dcln, dwa, dwx, dvec = lru_bwd(dycat, cc, xc, proj, hst, cln, wa, wx, vec)
            dhh, ddw, ddb = dwconv_bwd(dcc, hh, 0, CONF_DIM, dww, CONF_KERNEL, f32)
            dvg = glu_bwd(dhh, proj)
            dxr, dcw, dcb = dwconv_bwd(dxc, proj, XR_COLB, LRU_DIM, cw, LRU_CONV, bf16)
            dproj = jnp.concatenate([dvg, dxr, dgr], axis=-1)
            gpart['od_w_in'] = wgrad_into(hm, dproj, dproj.shape[-1] // N_CHIPS, (L // 2,), (o,), gpart['od_w_in'])
            gpart['conf_dw_w'][o], gpart['conf_dw_b'][o] = ddw[:CONF_KERNEL], ddb[0]
            gpart['conf_ln_g'][o], gpart['conf_ln_b'][o] = dcln[0], dcln[1]
            gpart['lru_conv_w'][o], gpart['lru_conv_b'][o] = dcw[:LRU_CONV], dcb[0]
            gpart['lru_wa'][o], gpart['lru_wx'][o] = dwa, dwx
            gpart['lru_ba'][o], gpart['lru_bx'][o], gpart['lru_lambda'][o] = dvec[0], dvec[1], dvec[2]
        dxs, dm2 = inproj_bwd(dproj, w_in_m, xin, m2, dxp)
        dxs, dm1 = ffn_back(dxs, rec['ffa'], l, 0)
        dmod[l] = jnp.concatenate([dm1, dm2, dg2, dm3], axis=1)
    grad_x = dxs

    def stack(v):
        return jnp.stack([stack(u) if isinstance(u, list) else u for u in v])

    def per_chip(n):
        if isinstance(gpart[n], list):
            return jnp.stack(gpart[n])
        lead = W[n].ndim - 2
        return gpart[n].reshape(W[n].shape[:lead] + (N_CHIPS,) + W[n].shape[lead:])

    gfull = [per_chip(n) for n in BIG]
    cidx = lax.axis_index("c").astype(jnp.int32).reshape(1)
    summed = [add_half(g, t, cidx) for g, t in zip(gfull, swap_halves(gfull))]
    reduced = join_halves([sum_chips(r) for r in exchange_blocks(summed)])

    out_g, out_d, out_m, out_v = {}, {}, {}, {}
    for n, g in zip(BIG, reduced):
        shp = W[n].shape
        as2d = lambda a: a.reshape(-1, shp[-1])
        res = adamw(as2d(W[n]), as2d(M[n]), as2d(V[n]), [as2d(g)], [0], BIG_ROW_TILE)
        out_g[n] = g.reshape(shp)
        out_d[n], out_m[n], out_v[n] = (r.reshape(shp) for r in res)

    small = SMALL_SHARDED + SMALL_REPLICATED
    dmod_flat = stack(dmod).reshape(L, B, N_MOD * D)
    g3 = allgather8(_pack([dmod_flat] + [stack(gpart[n]) for n in small], 64))
    n_dmod = L * B * N_MOD * D
    dmod_all = jnp.moveaxis(g3.reshape(N_DEV, -1)[:, :n_dmod].reshape(N_DEV, L, B, N_MOD * D), 0, 1).reshape(L, N_DEV * B, N_MOD * D)
    ssum = sum_leading(g3, 64).reshape(-1)[n_dmod:]
    gsmall = dict(zip(small, _unpack(ssum, [full[n].shape for n in small])))
    for n in SMALL_SHARDED:
        wdt = W[n].shape[-1]
        gsmall[n] = lax.dynamic_slice_in_dim(gsmall[n], chip * wdt, wdt, axis=gsmall[n].ndim - 1)
    dmod_my = lax.dynamic_slice_in_dim(dmod_all, chip * n_ada, n_ada, axis=2)
    g_ada_w, g_ada_b = ada_bwd(c_all, dmod_my, dmod_all)
    gsmall['ada_b'] = g_ada_b[:, 0, :]

    res = adamw(ada_w.reshape(-1, n_ada), M['ada_w'].reshape(-1, n_ada), V['ada_w'].reshape(-1, n_ada),
                [g_ada_w.reshape(-1, n_ada)], [0], BIG_ROW_TILE)
    out_g['ada_w'] = g_ada_w
    out_d['ada_w'], out_m['ada_w'], out_v['ada_w'] = (r.reshape(ada_w.shape) for r in res)

    names = ('ada_b',) + small
    shapes = [W[n].shape for n in names]
    res = adamw(_pack([W[n] for n in names], 64), _pack([M[n] for n in names], 64), _pack([V[n] for n in names], 64),
                [_pack([gsmall[n] for n in names], 64)], [0], 64)
    out_g.update({n: gsmall[n] for n in names})
    for dst, r in zip((out_d, out_m, out_v), res):
        dst.update(zip(names, _unpack(r.reshape(-1), shapes)))

    return (loss, grad_x, *[out_g[n] for n in WEIGHTS], *[out_d[n] for n in WEIGHTS], *[out_m[n] for n in WEIGHTS],
            *[out_v[n] for n in WEIGHTS])
```
